```python
import math, functools
import jax, jax.numpy as jnp
from jax import lax
import numpy as np

D_MODEL = 1024
BATCH = 4
SEQ = 8192
DEPTH = 1
DEC_BATCH = 32
DEC_SEQ = 1
PAST_LEN = 16384
PAGE_SIZE = 128

MIX_WIDTH = D_MODEL
CONV_WIDTH = MIX_WIDTH // 2
ATT_WIDTH = MIX_WIDTH - CONV_WIDTH
CONV_K = 3
N_HEADS = 8
HEAD_DIM = ATT_WIDTH // N_HEADS
N_KV_HEADS = 2
GROUP = N_HEADS // N_KV_HEADS
KV_WIDTH = N_KV_HEADS * HEAD_DIM
CMP_LEN = 32
CMP_STRIDE = 16
CMP_HIDDEN = HEAD_DIM
SLC_BLOCK = 64
RATIO = SLC_BLOCK // CMP_STRIDE
N_SEL = 16
WINDOW = 512
Q_BLOCK = 128
N_BUCKETS = 32
REL_MAX_DIST = 128
N_EXPERTS = 32
TOP_K = 4
D_FF = D_MODEL
SWIGLU_LIMIT = 7.0
SWIGLU_ALPHA = 1.702
MOE_BLOCK = 128
EPS = 1e-6
NEG_INF = -1e30
FORCE_SCORE = 1e4
IN_COLS = 3 * CONV_WIDTH + ATT_WIDTH + 3 * 2 * KV_WIDTH + 3 * N_HEADS

kernel_name = 'hymba_conv_nsa_moe_adaln_step'


def rms_norm(x, g):
    xf = x.astype(jnp.float32)
    y = xf * lax.rsqrt(jnp.mean(xf * xf, axis=-1, keepdims=True) + EPS)
    return (y * g.astype(jnp.float32)).astype(x.dtype)


def rel_bucket(dist):
    n = jnp.maximum(dist, 0)
    max_exact = N_BUCKETS // 2
    large = max_exact + (jnp.log(jnp.maximum(n, 1).astype(jnp.float32) / max_exact)
                         / math.log(REL_MAX_DIST / max_exact) * (N_BUCKETS - max_exact)).astype(jnp.int32)
    return jnp.where(n < max_exact, n, jnp.minimum(large, N_BUCKETS - 1))


def masked_softmax(logits, mask):
    p = jax.nn.softmax(jnp.where(mask, logits.astype(jnp.float32), NEG_INF), axis=-1)
    return jnp.where(jnp.any(mask, axis=-1, keepdims=True), p, 0.0)


def split_in(u):
    B, T = u.shape[0], u.shape[1]
    widths = (CONV_WIDTH, CONV_WIDTH, CONV_WIDTH, ATT_WIDTH, 2 * KV_WIDTH, 2 * KV_WIDTH, 2 * KV_WIDTH, 3 * N_HEADS)
    offs = np.concatenate([[0], np.cumsum(widths)]).tolist()
    b_g, c_g, h_c, q, kvc, kvs, kvw, g = [u[..., offs[i]:offs[i + 1]] for i in range(8)]
    kv_shape = (B, T, 2, N_KV_HEADS, HEAD_DIM)
    return (b_g, c_g, h_c, q.reshape(B, T, N_HEADS, HEAD_DIM), kvc.reshape(kv_shape), kvs.reshape(kv_shape),
            kvw.reshape(kv_shape), jax.nn.sigmoid(g.astype(jnp.float32)).reshape(B, T, N_HEADS, 3))


def short_conv(b_g, c_g, h_c, conv_w, prev):
    T = h_c.shape[1]
    v = c_g * h_c
    vp = jnp.concatenate([prev.astype(v.dtype), v], axis=1)
    y = sum(conv_w[k] * vp[:, k:k + T] for k in range(CONV_K))
    return b_g * y, vp[:, T:]


def compress(kv, cmp_pe, cmp_w1, cmp_w2):
    B, L = kv.shape[0], kv.shape[1]
    n_cmp = (L - CMP_LEN) // CMP_STRIDE + 1
    rows = (n_cmp + 1) * CMP_STRIDE
    f = CMP_STRIDE * HEAD_DIM
    ch = kv[:, :rows].reshape(B, n_cmp + 1, CMP_STRIDE, 2, N_KV_HEADS, HEAD_DIM)
    ch = ch.transpose(0, 1, 3, 4, 2, 5).reshape(B, n_cmp + 1, 2, N_KV_HEADS, f)
    pe_term = jnp.einsum('cf,cfh->ch', cmp_pe.reshape(2, CMP_LEN * HEAD_DIM), cmp_w1)
    pre = (jnp.einsum('bnckf,cfh->bnckh', ch[:, :-1], cmp_w1[:, :f])
           + jnp.einsum('bnckf,cfh->bnckh', ch[:, 1:], cmp_w1[:, f:])
           + pe_term[None, None, :, None, :])
    out = jnp.einsum('bnckh,chd->bnckd', jax.nn.gelu(pre), cmp_w2)
    pos_end = jnp.arange(n_cmp) * CMP_STRIDE + CMP_LEN - 1
    return out, pos_end


def fetch_blocks(kb, idx):
    B, Q, H, K = idx.shape
    bi = jnp.arange(B)[:, None, None, None]
    ki = jnp.arange(H)[None, None, :, None]
    rows = kb[bi, ki, idx]
    pos = idx[..., None] * SLC_BLOCK + jnp.arange(SLC_BLOCK)
    return rows.reshape(B, Q, H, K * SLC_BLOCK, 2, HEAD_DIM), pos.reshape(B, Q, H, K * SLC_BLOCK)


def fetch_paged(pool, page_table, kv_new, idx):
    B, Q, H, K = idx.shape
    n_pages = page_table.shape[1]
    past_len = n_pages * PAGE_SIZE
    t_new = kv_new.shape[1]
    pos = idx[..., None] * SLC_BLOCK + jnp.arange(SLC_BLOCK)
    bi = jnp.arange(B)[:, None, None, None, None]
    ki = jnp.arange(H)[None, None, :, None, None]
    phys = page_table[bi, jnp.minimum(pos // PAGE_SIZE, n_pages - 1)]
    from_pool = pool[phys, pos % PAGE_SIZE, :, ki]
    from_new = kv_new[bi, jnp.clip(pos - past_len, 0, t_new - 1), :, ki]
    rows = jnp.where((pos < past_len)[..., None, None], from_pool.astype(kv_new.dtype), from_new)
    return rows.reshape(B, Q, H, K * SLC_BLOCK, 2, HEAD_DIM), pos.reshape(B, Q, H, K * SLC_BLOCK)


def nsa_attend(q, g, t_q, kvc, pos_c, n_slc, fetch, kvw, pos_w, table):
    B, Q = q.shape[0], q.shape[1]
    qg = q.reshape(B, Q, N_KV_HEADS, GROUP, HEAD_DIM) * (HEAD_DIM ** -0.5)
    tbl = table.reshape(N_BUCKETS, N_KV_HEADS, GROUP)
    dist_c = t_q[:, None] - pos_c[None, :]
    lc = jnp.einsum('bqkgd,bnkd->bkgqn', qg, kvc[:, :, 0]) + tbl[rel_bucket(dist_c)].transpose(2, 3, 0, 1)
    pc = masked_softmax(lc, dist_c >= 0)
    o_c = jnp.einsum('bkgqn,bnkd->bqkgd', pc, kvc[:, :, 1])
    imp = pc.sum(axis=2)
    imp = jnp.pad(imp, ((0, 0), (0, 0), (0, 0), (0, RATIO * n_slc - imp.shape[-1])))
    a = imp.reshape(B, N_KV_HEADS, Q, n_slc, RATIO)
    last = a[..., RATIO - 1]
    score = 2.0 * a[..., :RATIO - 1].sum(-1) + last + jnp.pad(last, ((0, 0), (0, 0), (0, 0), (1, 0)))[..., :n_slc]
    cur = t_q // SLC_BLOCK
    blk = jnp.arange(n_slc)[None, :]
    valid = blk <= cur[:, None]
    forced = (blk == 0) | (blk == cur[:, None]) | (blk == cur[:, None] - 1)
    score = jnp.where(forced, FORCE_SCORE, jnp.where(valid, score, NEG_INF))
    _, idx = lax.top_k(score, min(N_SEL, n_slc))
    kvs, pos_s = fetch(idx.transpose(0, 2, 1, 3))
    dist_s = t_q[None, :, None, None] - pos_s
    bias_s = tbl[rel_bucket(dist_s), jnp.arange(N_KV_HEADS)[None, None, :, None]].transpose(0, 1, 2, 4, 3)
    ls = jnp.einsum('bqkgd,bqksd->bqkgs', qg, kvs[..., 0, :]) + bias_s
    ps = masked_softmax(ls, (dist_s >= 0)[:, :, :, None, :])
    o_s = jnp.einsum('bqkgs,bqksd->bqkgd', ps, kvs[..., 1, :])
    dist_w = t_q[:, None] - pos_w[None, :]
    mask_w = (dist_w >= 0) & (dist_w < WINDOW) & (pos_w[None, :] >= 0)
    lw = jnp.einsum('bqkgd,bwkd->bkgqw', qg, kvw[:, :, 0]) + tbl[rel_bucket(dist_w)].transpose(2, 3, 0, 1)
    pw = masked_softmax(lw, mask_w)
    o_w = jnp.einsum('bkgqw,bwkd->bqkgd', pw, kvw[:, :, 1])
    gg = g.reshape(B, Q, N_KV_HEADS, GROUP, 3)
    o = gg[..., 0:1] * o_c + gg[..., 1:2] * o_s + gg[..., 2:3] * o_w
    return o.reshape(B, Q, ATT_WIDTH).astype(q.dtype)


def nsa_prompt(q, g, kvc, kvs, kvw, cmp_pe, cmp_w1, cmp_w2, table):
    B, T = q.shape[0], q.shape[1]
    kvc_blk, pos_c = compress(kvc, cmp_pe, cmp_w1, cmp_w2)
    n_slc = -(-T // SLC_BLOCK)
    kvs_pad = jnp.pad(kvs, ((0, 0), (0, n_slc * SLC_BLOCK - T), (0, 0), (0, 0), (0, 0)))
    kb = kvs_pad.reshape(B, n_slc, SLC_BLOCK, 2, N_KV_HEADS, HEAD_DIM).transpose(0, 4, 1, 2, 3, 5)
    fetch = functools.partial(fetch_blocks, kb)
    kvw_pad = jnp.pad(kvw, ((0, 0), (WINDOW, 0), (0, 0), (0, 0), (0, 0)))

    def q_block(i):
        q0 = i * Q_BLOCK
        qb = lax.dynamic_slice_in_dim(q, q0, Q_BLOCK, axis=1)
        gb = lax.dynamic_slice_in_dim(g, q0, Q_BLOCK, axis=1)
        wb = lax.dynamic_slice_in_dim(kvw_pad, q0, WINDOW + Q_BLOCK, axis=1)
        t_q = q0 + jnp.arange(Q_BLOCK)
        pos_w = q0 - WINDOW + jnp.arange(WINDOW + Q_BLOCK)
        return nsa_attend(qb, gb, t_q, kvc_blk, pos_c, n_slc, fetch, wb, pos_w, table)

    out = lax.map(q_block, jnp.arange(T // Q_BLOCK))
    return out.transpose(1, 0, 2, 3).reshape(B, T, ATT_WIDTH)


def merge_groups(conv_out, att, gn_conv, gn_att):
    return jnp.concatenate([rms_norm(conv_out, gn_conv), rms_norm(att, gn_att)], axis=-1)


def prompt_mixer(h, w_in, conv_w, cmp_pe, cmp_w1, cmp_w2, gn_conv, gn_att, table):
    B, T, _ = h.shape
    b_g, c_g, h_c, q, kvc, kvs, kvw, g = split_in(h @ w_in)
    conv_out, conv_state = short_conv(b_g, c_g, h_c, conv_w, jnp.zeros((B, CONV_K - 1, CONV_WIDTH), h.dtype))
    att = nsa_prompt(q, g, kvc, kvs, kvw, cmp_pe, cmp_w1, cmp_w2, table)
    page_shape = (B, T // PAGE_SIZE, PAGE_SIZE, 2, N_KV_HEADS, HEAD_DIM)
    w_keep = min(WINDOW, T)
    state = (kvc.reshape(page_shape), kvs.reshape(page_shape), kvw[:, T - w_keep:], conv_state)
    return merge_groups(conv_out, att, gn_conv, gn_att), state


def sample_mixer(h, pool_cmp, pool_slc, win_buf, conv_buf, page_table, w_in, conv_w, cmp_pe, cmp_w1, cmp_w2,
                 gn_conv, gn_att, table):
    B, T, _ = h.shape
    past_len = page_table.shape[1] * PAGE_SIZE
    b_g, c_g, h_c, q, kvc, kvs, kvw, g = split_in(h @ w_in)
    conv_out, conv_new = short_conv(b_g, c_g, h_c, conv_w, conv_buf)
    past_cmp = pool_cmp[page_table].reshape(B, past_len, 2, N_KV_HEADS, HEAD_DIM)
    kvc_blk, pos_c = compress(jnp.concatenate([past_cmp.astype(kvc.dtype), kvc], axis=1), cmp_pe, cmp_w1, cmp_w2)
    n_slc = -(-(past_len + T) // SLC_BLOCK)
    fetch = functools.partial(fetch_paged, pool_slc, page_table, kvs)
    w_buf = win_buf.shape[1]
    kvw_all = jnp.concatenate([win_buf.astype(kvw.dtype), kvw], axis=1)
    pos_w = past_len - w_buf + jnp.arange(w_buf + T)
    t_q = past_len + jnp.arange(T)
    att = nsa_attend(q, g, t_q, kvc_blk, pos_c, n_slc, fetch, kvw_all, pos_w, table)
    return merge_groups(conv_out, att, gn_conv, gn_att), (kvc, kvs, kvw_all[:, T:], conv_new)


def moe(h, router_w, router_b, w_gu, b_gu, w_down, b_down):
    T, D = h.shape
    logits = h.astype(jnp.float32) @ router_w.astype(jnp.float32) + router_b.astype(jnp.float32)
    top_val, top_idx = lax.top_k(logits, TOP_K)
    gate = jax.nn.softmax(top_val, axis=-1)
    n_assign = T * TOP_K
    flat_e = top_idx.reshape(-1)
    order = jnp.argsort(flat_e)
    sorted_e = flat_e[order]
    counts = jnp.bincount(flat_e, length=N_EXPERTS)
    padded = (counts + MOE_BLOCK - 1) // MOE_BLOCK * MOE_BLOCK
    pad_end = jnp.cumsum(padded)
    pad_start = pad_end - padded
    start = jnp.cumsum(counts) - counts
    dest = pad_start[sorted_e] + jnp.arange(n_assign) - start[sorted_e]
    n_blocks = -(-(n_assign + N_EXPERTS * (MOE_BLOCK - 1)) // MOE_BLOCK)
    n_rows = n_blocks * MOE_BLOCK
    row_tok = jnp.full((n_rows,), T, jnp.int32).at[dest].set((order // TOP_K).astype(jnp.int32))
    row_gate = jnp.zeros((n_rows,), jnp.float32).at[dest].set(gate.reshape(-1)[order])
    block_e = jnp.minimum(jnp.searchsorted(pad_end, jnp.arange(n_blocks) * MOE_BLOCK, side='right'), N_EXPERTS - 1)
    h_pad = jnp.concatenate([h, jnp.zeros((1, D), h.dtype)], axis=0)
    xb = h_pad[row_tok].reshape(n_blocks, MOE_BLOCK, D)

    def expert_block(args):
        xb_i, e = args
        gu = xb_i @ w_gu[e] + b_gu[e]
        gate_h = jnp.minimum(gu[:, :D_FF], SWIGLU_LIMIT)
        up_h = jnp.clip(gu[:, D_FF:], -SWIGLU_LIMIT, SWIGLU_LIMIT)
        act = (up_h + 1.0) * gate_h * jax.nn.sigmoid(SWIGLU_ALPHA * gate_h)
        return act @ w_down[e] + b_down[e]

    yb = lax.map(expert_block, (xb, block_e))
    y = jnp.zeros((T + 1, D), jnp.float32).at[row_tok].add(yb.reshape(n_rows, D) * row_gate[:, None])
    return y[:T].astype(h.dtype)


def residual_block(x, c, mixer, ln1_g, ln2_g, w_ada, b_ada, w_o, router_w, router_b, w_gu, b_gu, w_down, b_down):
    B, T, D = x.shape
    mod = (jax.nn.silu(c.astype(jnp.float32)) @ w_ada.astype(jnp.float32) + b_ada.astype(jnp.float32)).reshape(B, 6, 1, D)
    h = (rms_norm(x, ln1_g) * (1.0 + mod[:, 1]) + mod[:, 0]).astype(x.dtype)
    mix, state = mixer(h)
    x1 = (x + mod[:, 2] * (mix @ w_o)).astype(x.dtype)
    h2 = (rms_norm(x1, ln2_g) * (1.0 + mod[:, 4]) + mod[:, 3]).astype(x.dtype)
    ff = moe(h2.reshape(B * T, D), router_w, router_b, w_gu, b_gu, w_down, b_down).reshape(B, T, D)
    return (x1 + mod[:, 5] * ff).astype(x.dtype), state


def setup_inputs(seed: int = 0) -> dict:
    key = jax.random.key(seed)
    ks = jax.random.split(key, 32)
    f32 = jnp.float32
    n_pages = PAST_LEN // PAGE_SIZE
    n_used = DEC_BATCH * n_pages
    n_pool = n_used + max(1, n_used // 4)
    w_buf = min(WINDOW, PAST_LEN)

    def nrm(k, shape, s):
        return s * jax.random.normal(k, shape, f32)

    page_table = jax.random.permutation(ks[6], n_pool)[:n_used].reshape(DEC_BATCH, n_pages).astype(jnp.int32)
    kv_tail = (2, N_KV_HEADS, HEAD_DIM)
    return {
        'x_prompt': nrm(ks[0], (BATCH, SEQ, D_MODEL), 1.0),
        'x_sample': nrm(ks[1], (DEC_BATCH, DEC_SEQ, D_MODEL), 1.0),
        'cache_kv_cmp': nrm(ks[2], (DEPTH, n_pool, PAGE_SIZE) + kv_tail, 1.0),
        'cache_kv_slc': nrm(ks[3], (DEPTH, n_pool, PAGE_SIZE) + kv_tail, 1.0),
        'state_kv_win': nrm(ks[4], (DEPTH, DEC_BATCH, w_buf) + kv_tail, 1.0),
        'state_conv': nrm(ks[5], (DEPTH, DEC_BATCH, CONV_K - 1, CONV_WIDTH), 1.0),
        'page_table': page_table,
        'c_prompt': nrm(ks[7], (BATCH, D_MODEL), 1.0),
        'c_sample': nrm(ks[8], (DEC_BATCH, D_MODEL), 1.0),
        'rel_bias_table': nrm(ks[9], (N_BUCKETS, N_HEADS), 0.5),
        'ln1_g': 1.0 + nrm(ks[10], (DEPTH, D_MODEL), 0.01),
        'ln2_g': 1.0 + nrm(ks[11], (DEPTH, D_MODEL), 0.01),
        'w_ada': nrm(ks[12], (DEPTH, D_MODEL, 6 * D_MODEL), 0.2 * D_MODEL ** -0.5),
        'b_ada': nrm(ks[13], (DEPTH, 6 * D_MODEL), 0.02),
        'w_in': nrm(ks[14], (DEPTH, D_MODEL, IN_COLS), D_MODEL ** -0.5),
        'conv_w': nrm(ks[15], (DEPTH, CONV_K, CONV_WIDTH), CONV_K ** -0.5),
        'cmp_pe': nrm(ks[16], (DEPTH, 2, CMP_LEN, HEAD_DIM), 0.02),
        'cmp_w1': nrm(ks[17], (DEPTH, 2, CMP_LEN * HEAD_DIM, CMP_HIDDEN), (CMP_LEN * HEAD_DIM) ** -0.5),
        'cmp_w2': nrm(ks[18], (DEPTH, 2, CMP_HIDDEN, HEAD_DIM), CMP_HIDDEN ** -0.5),
        'gn_conv': 1.0 + nrm(ks[19], (DEPTH, CONV_WIDTH), 0.01),
        'gn_att': 1.0 + nrm(ks[20], (DEPTH, ATT_WIDTH), 0.01),
        'w_o': nrm(ks[21], (DEPTH, MIX_WIDTH, D_MODEL), MIX_WIDTH ** -0.5),
        'router_w': nrm(ks[22], (DEPTH, D_MODEL, N_EXPERTS), D_MODEL ** -0.5),
        'router_b': nrm(ks[23], (DEPTH, N_EXPERTS), 0.01),
        'w_gu': nrm(ks[24], (DEPTH, N_EXPERTS, D_MODEL, 2 * D_FF), D_MODEL ** -0.5),
        'b_gu': nrm(ks[25], (DEPTH, N_EXPERTS, 2 * D_FF), 0.01),
        'w_down': nrm(ks[26], (DEPTH, N_EXPERTS, D_FF, D_MODEL), D_FF ** -0.5),
        'b_down': nrm(ks[27], (DEPTH, N_EXPERTS, D_MODEL), 0.01),
        'final_g': 1.0 + nrm(ks[28], (D_MODEL,), 0.01),
    }


def reference(x_prompt, x_sample, cache_kv_cmp, cache_kv_slc, state_kv_win, state_conv, page_table, c_prompt, c_sample,
              rel_bias_table, ln1_g, ln2_g, w_ada, b_ada, w_in, conv_w, cmp_pe, cmp_w1, cmp_w2, gn_conv, gn_att, w_o,
              router_w, router_b, w_gu, b_gu, w_down, b_down, final_g):
    xp, xs = x_prompt, x_sample
    st_p = [[], [], [], []]
    st_s = [[], [], [], []]
    for l in range(DEPTH):
        ffn = (router_w[l], router_b[l], w_gu[l], b_gu[l], w_down[l], b_down[l])
        shared = dict(w_in=w_in[l], conv_w=conv_w[l], cmp_pe=cmp_pe[l], cmp_w1=cmp_w1[l], cmp_w2=cmp_w2[l],
                      gn_conv=gn_conv[l], gn_att=gn_att[l], table=rel_bias_table)
        mix_p = functools.partial(prompt_mixer, **shared)
        xp, sp = residual_block(xp, c_prompt, mix_p, ln1_g[l], ln2_g[l], w_ada[l], b_ada[l], w_o[l], *ffn)
        mix_s = functools.partial(sample_mixer, pool_cmp=cache_kv_cmp[l], pool_slc=cache_kv_slc[l],
                                  win_buf=state_kv_win[l], conv_buf=state_conv[l], page_table=page_table, **shared)
        xs, ss = residual_block(xs, c_sample, mix_s, ln1_g[l], ln2_g[l], w_ada[l], b_ada[l], w_o[l], *ffn)
        for j in range(4):
            st_p[j].append(sp[j])
            st_s[j].append(ss[j])
    y_prompt = rms_norm(xp, final_g)
    y_sample = rms_norm(xs, final_g)
    return (y_prompt, y_sample, jnp.stack(st_p[0]), jnp.stack(st_s[0]), jnp.stack(st_p[1]), jnp.stack(st_s[1]),
            jnp.stack(st_p[2]), jnp.stack(st_s[2]), jnp.stack(st_p[3]), jnp.stack(st_s[3]))
```

```python
import functools
import math

import numpy as np
import jax
import jax.numpy as jnp
from jax import lax
from jax.experimental import pallas as pl
from jax.experimental.pallas import tpu as pltpu

F32 = jnp.float32
BF16 = jnp.bfloat16

CONV_K = 3
N_HEADS = 8
N_KV_HEADS = 2
GROUP = N_HEADS // N_KV_HEADS
HEAD_DIM = 64
KV_WIDTH = N_KV_HEADS * HEAD_DIM
CMP_LEN = 32
CMP_STRIDE = 16
SLC_BLOCK = 64
RATIO = SLC_BLOCK // CMP_STRIDE
N_SEL = 16
WINDOW = 512
Q_BLOCK = 128
N_BUCKETS = 32
REL_MAX_DIST = 128
TOP_K = 4
SWIGLU_LIMIT = 7.0
SWIGLU_ALPHA = 1.702
EPS = 1e-6
NEG_INF = -1e30
FORCE_SCORE = 1e4
FAR_DIST = 113

LANE = 128
ROWS_Q = GROUP * Q_BLOCK
KEY_CHUNK = 512
MOE_ROWS = 256
VMEM_LIMIT = 48 * 1024 * 1024


def _cparams(n_axes):
    return pltpu.CompilerParams(dimension_semantics=("arbitrary",) * n_axes, vmem_limit_bytes=VMEM_LIMIT)


def _dot(a, b):
    return jnp.dot(a, b, preferred_element_type=F32)


def _dot_nt(a, b):
    return lax.dot_general(a, b, (((1,), (1,)), ((), ())), preferred_element_type=F32)


def _split2(x):
    hi = x.astype(BF16)
    lo = (x - hi.astype(F32)).astype(BF16)
    return hi, lo


def _split3(x):
    a = x.astype(BF16)
    r = x - a.astype(F32)
    b = r.astype(BF16)
    c = (r - b.astype(F32)).astype(BF16)
    return a, b, c


def _dot3(a, b):
    ah, al = _split2(a)
    bh, bl = _split2(b)
    return _dot(ah, bh) + _dot(ah, bl) + _dot(al, bh)


def _sigmoid(x):
    return 1.0 / (1.0 + jnp.exp(-x))


def _rms(x, g):
    return x * lax.rsqrt(jnp.mean(x * x, axis=-1, keepdims=True) + EPS) * g


def _rel_bucket(dist):
    n = jnp.maximum(dist, 0)
    max_exact = N_BUCKETS // 2
    large = max_exact + (jnp.log(jnp.maximum(n, 1).astype(F32) / max_exact)
                         / math.log(REL_MAX_DIST / max_exact) * (N_BUCKETS - max_exact)).astype(jnp.int32)
    return jnp.where(n < max_exact, n, jnp.minimum(large, N_BUCKETS - 1))


def _mod_kernel(c_ref, w_ref, b_ref, o_ref):
    c = c_ref[...]
    o_ref[...] = _dot3(c * _sigmoid(c), w_ref[...]) + b_ref[...]


def _modulation(c, w_ada, b_ada):
    n, d = c.shape
    cols = w_ada.shape[1]
    bn = 1536
    return pl.pallas_call(
        _mod_kernel,
        out_shape=jax.ShapeDtypeStruct((n, cols), F32),
        grid=(cols // bn,),
        in_specs=[pl.BlockSpec((n, d), lambda i: (0, 0)),
                  pl.BlockSpec((d, bn), lambda i: (0, i)),
                  pl.BlockSpec((1, bn), lambda i: (0, i))],
        out_specs=pl.BlockSpec((n, bn), lambda i: (0, i)),
        compiler_params=_cparams(1),
        name="modulation",
    )(c, w_ada, b_ada.reshape(1, cols))


def _inproj_kernel(x_ref, m0_ref, m1_ref, g1_ref, w_ref, cw_ref, gnc_ref, pa_ref, pb_ref,
                   convn_ref, q_ref, kvc_ref, kvs_ref, kvw_ref, kvh_ref, gate_ref, vlast_ref,
                   carry_ref, *, seq_mode, tm, cw_width):
    x = x_ref[0]
    h = _rms(x, g1_ref[...]) * (1.0 + m1_ref[0]) + m0_ref[0]
    hb = h.astype(BF16)
    c3 = 3 * cw_width
    uc = _dot(hb, w_ref[:, 0:c3])
    b_g = uc[:, 0:cw_width]
    v = uc[:, cw_width:2 * cw_width] * uc[:, 2 * cw_width:c3]
    if seq_mode:
        @pl.when(pl.program_id(1) == 0)
        def _():
            carry_ref[0:2, :] = pa_ref[0]
        c0 = carry_ref[0:1, :]
        c1 = carry_ref[1:2, :]
        row = lax.broadcasted_iota(jnp.int32, (tm, 1), 0)
        vm1 = jnp.where(row == 0, c1, pltpu.roll(v, 1, 0))
        vm2 = jnp.where(row == 0, c0, jnp.where(row == 1, c1, pltpu.roll(v, 2, 0)))
        carry_ref[0:2, :] = v[tm - 2:tm, :]
        vlast_ref[0] = v[tm - 2:tm, :]
    else:
        vm2 = pa_ref[0]
        vm1 = pb_ref[0]
        vlast_ref[0] = v
    cw = cw_ref[...]
    y = cw[0:1, :] * vm2 + cw[1:2, :] * vm1 + cw[2:3, :] * v
    convn_ref[0] = _rms(b_g * y, gnc_ref[...]).astype(BF16)

    aw = N_HEADS * HEAD_DIM
    uq = _dot(hb, w_ref[:, c3:c3 + aw]) * (HEAD_DIM ** -0.5)
    for hh in range(N_HEADS):
        q_ref[0, hh] = uq[:, hh * HEAD_DIM:(hh + 1) * HEAD_DIM].astype(BF16)
    kv0 = c3 + aw
    kvw3 = 3 * 2 * KV_WIDTH
    ukv = _dot(hb, w_ref[:, kv0:kv0 + kvw3])
    kvc_ref[0] = ukv[:, 0:2 * KV_WIDTH]
    kvs_ref[0] = ukv[:, 2 * KV_WIDTH:4 * KV_WIDTH]
    kvw_ref[0] = ukv[:, 4 * KV_WIDTH:6 * KV_WIDTH]
    for s in range(kvw3 // HEAD_DIM):
        kvh_ref[0, s] = ukv[:, s * HEAD_DIM:(s + 1) * HEAD_DIM].astype(BF16)
    ug = _dot(hb, w_ref[:, kv0 + kvw3:kv0 + kvw3 + LANE])
    gate_ref[0] = _sigmoid(ug)


def _inproj(x, m0, m1, ln_g, w_pad, conv_w, gn_conv, pa, pb, *, seq_mode, tm):
    B, T, D = x.shape
    cw_width = conv_w.shape[1]
    n_slab = 3 * 2 * KV_WIDTH // HEAD_DIM
    nt = T // tm
    row_blk = lambda w: pl.BlockSpec((1, tm, w), lambda b, t: (b, t, 0))
    if seq_mode:
        mod_spec = pl.BlockSpec((1, 1, D), lambda b, t: (b, 0, 0))
        prev_spec = pl.BlockSpec((1, 2, cw_width), lambda b, t: (b, 0, 0))
        vlast_shape = jax.ShapeDtypeStruct((B, 2, cw_width), F32)
        vlast_spec = pl.BlockSpec((1, 2, cw_width), lambda b, t: (b, 0, 0))
    else:
        mod_spec = row_blk(D)
        prev_spec = row_blk(cw_width)
        vlast_shape = jax.ShapeDtypeStruct((B, T, cw_width), F32)
        vlast_spec = row_blk(cw_width)
    const = lambda shape: pl.BlockSpec(shape, lambda b, t: (0,) * len(shape))
    kern = functools.partial(_inproj_kernel, seq_mode=seq_mode, tm=tm, cw_width=cw_width)
    return pl.pallas_call(
        kern,
        out_shape=(jax.ShapeDtypeStruct((B, T, cw_width), BF16),
                   jax.ShapeDtypeStruct((B, N_HEADS, T, HEAD_DIM), BF16),
                   jax.ShapeDtypeStruct((B, T, 2 * KV_WIDTH), F32),
                   jax.ShapeDtypeStruct((B, T, 2 * KV_WIDTH), F32),
                   jax.ShapeDtypeStruct((B, T, 2 * KV_WIDTH), F32),
                   jax.ShapeDtypeStruct((B, n_slab, T, HEAD_DIM), BF16),
                   jax.ShapeDtypeStruct((B, T, LANE), F32),
                   vlast_shape),
        grid=(B, nt),
        in_specs=[row_blk(D), mod_spec, mod_spec, const((1, D)), const(w_pad.shape), const(conv_w.shape),
                  const((1, cw_width)), prev_spec, prev_spec],
        out_specs=(row_blk(cw_width),
                   pl.BlockSpec((1, N_HEADS, tm, HEAD_DIM), lambda b, t: (b, 0, t, 0)),
                   row_blk(2 * KV_WIDTH), row_blk(2 * KV_WIDTH), row_blk(2 * KV_WIDTH),
                   pl.BlockSpec((1, n_slab, tm, HEAD_DIM), lambda b, t: (b, 0, t, 0)),
                   row_blk(LANE), vlast_spec),
        scratch_shapes=[pltpu.VMEM((8, cw_width), F32)],
        compiler_params=_cparams(2),
        name="inproj_seq" if seq_mode else "inproj_rows",
    )(x, m0, m1, ln_g.reshape(1, D), w_pad, conv_w, gn_conv.reshape(1, cw_width), pa, pb)


def _cmp_ab_kernel(pt_ref, *refs, G):
    pages = refs[:2 * G]
    w_ref, out_ref, x_ref = refs[2 * G], refs[2 * G + 1], refs[2 * G + 2]
    half = KV_WIDTH
    for j in range(G):
        for r in range(CMP_STRIDE):
            for c in range(2):
                x_ref[c, j * 8:(j + 1) * 8, r * half:(r + 1) * half] = (
                    pages[2 * j + c][0, pl.ds(r, 8, stride=CMP_STRIDE), :])
    wcols = w_ref.shape[2]
    for c in range(2):
        out_ref[0, :, c * wcols:(c + 1) * wcols] = _dot(x_ref[c].astype(BF16), w_ref[c])


def _cmp_ab(pool, page_table, w_blk, G):
    P, page, width = pool.shape
    B, n_pages = page_table.shape
    cpp = page // CMP_STRIDE
    assert cpp == 8 and n_pages % G == 0
    kdim = CMP_STRIDE * width // 2

    def pg_spec(j, c):
        return pl.BlockSpec((1, page, width // 2), lambda b, g, pt: (pt[b * n_pages + g * G + j], 0, c))

    return pl.pallas_call(
        functools.partial(_cmp_ab_kernel, G=G),
        out_shape=jax.ShapeDtypeStruct((B, n_pages * cpp, 2 * w_blk.shape[2]), F32),
        grid_spec=pltpu.PrefetchScalarGridSpec(
            num_scalar_prefetch=1,
            grid=(B, n_pages // G),
            in_specs=[pg_spec(j, c) for j in range(G) for c in range(2)]
            + [pl.BlockSpec(w_blk.shape, lambda b, g, pt: (0, 0, 0))],
            out_specs=pl.BlockSpec((1, G * cpp, 2 * w_blk.shape[2]), lambda b, g, pt: (b, g, 0)),
            scratch_shapes=[pltpu.VMEM((2, G * cpp, kdim), F32)]),
        compiler_params=_cparams(2),
        name="cmp_partial",
    )(page_table.reshape(-1).astype(jnp.int32), *([pool] * (2 * G)), w_blk)


def _gelu_tanh(x):
    return 0.5 * x * (1.0 + jnp.tanh(math.sqrt(2.0 / math.pi) * (x + 0.044715 * (x * x * x))))


def _cmp_fin_kernel(ab_ref, pe_ref, w1_ref, w2_ref, out_ref, *, nch):
    for c in range(2):
        pe_t = _dot(pe_ref[c:c + 1, :].astype(BF16), w1_ref[c].astype(BF16))
        w2 = w2_ref[c].astype(BF16)
        for k in range(N_KV_HEADS):
            base = (c * N_KV_HEADS + k) * 2 * HEAD_DIM
            slab = ab_ref[0, :, base:base + 2 * HEAD_DIM]
            nxt = pltpu.roll(slab, nch - 1, 0)
            pre = slab[:, 0:HEAD_DIM] + nxt[:, HEAD_DIM:2 * HEAD_DIM] + pe_t
            out_ref[0, c * N_KV_HEADS + k] = _dot(_gelu_tanh(pre).astype(BF16), w2).astype(BF16)


def _cmp_finish(ab, cmp_pe, cmp_w1, cmp_w2):
    B, nch, w = ab.shape
    pe = cmp_pe.reshape(2, CMP_LEN * HEAD_DIM)
    return pl.pallas_call(
        functools.partial(_cmp_fin_kernel, nch=nch),
        out_shape=jax.ShapeDtypeStruct((B, 2 * N_KV_HEADS, nch, HEAD_DIM), BF16),
        grid=(B,),
        in_specs=[pl.BlockSpec((1, nch, w), lambda b: (b, 0, 0)),
                  pl.BlockSpec(pe.shape, lambda b: (0, 0)),
                  pl.BlockSpec(cmp_w1.shape, lambda b: (0, 0, 0)),
                  pl.BlockSpec(cmp_w2.shape, lambda b: (0, 0, 0))],
        out_specs=pl.BlockSpec((1, 2 * N_KV_HEADS, nch, HEAD_DIM), lambda b: (b, 0, 0, 0)),
        compiler_params=_cparams(1),
        name="cmp_finish",
    )(ab, pe, cmp_w1, cmp_w2)


def _cmp_block_weight(cmp_w1):
    hid = cmp_w1.shape[2]
    w = cmp_w1.reshape(2, 2, CMP_STRIDE, HEAD_DIM, hid)
    eye = jnp.eye(N_KV_HEADS, dtype=cmp_w1.dtype)
    wb = jnp.einsum('cardh,kj->crkdjah', w, eye)
    return wb.reshape(2, CMP_STRIDE * N_KV_HEADS * HEAD_DIM, N_KV_HEADS * 2 * hid).astype(BF16)


def _score_matrix(nch, n_slc):
    i = np.arange(nch)[:, None]
    j = np.arange(n_slc)[None, :]
    m = 2.0 * ((i // RATIO == j) & (i % RATIO < RATIO - 1)) + 1.0 * (i == RATIO * j + RATIO - 1) \
        + 1.0 * (i == RATIO * j - 1)
    return jnp.asarray(m, dtype=BF16)


def _topk_mask(sc, blk, k):
    big = jnp.int32(1 << 30)
    sel = jnp.zeros(sc.shape, dtype=jnp.bool_)
    for _ in range(k):
        m = jnp.max(sc, axis=-1, keepdims=True)
        first = jnp.min(jnp.where(sc == m, blk, big), axis=-1, keepdims=True)
        pick = blk == first
        sel = jnp.logical_or(sel, pick)
        sc = jnp.where(pick, -3e38, sc)
    return sel


def _nsa_kernel(q_ref, kc_ref, vc_ref, ks_ref, vs_ref, kw_ref, vw_ref, gate_ref, cfar_ref, p0_ref, p1_ref,
                pcd_ref, mm_ref, o_ref, m_ref, l_ref, acc_ref, *, nch, n_slc):
    j = pl.program_id(2)
    q0 = pl.multiple_of(j * Q_BLOCK, Q_BLOCK)
    qm = q_ref[0].reshape(ROWS_Q, HEAD_DIM)
    cfar = cfar_ref[0]
    rowq = jnp.bitwise_and(lax.broadcasted_iota(jnp.int32, (ROWS_Q, 1), 0), Q_BLOCK - 1)
    tq = q0 + rowq

    lc = _dot_nt(qm, kc_ref[0, 0]) + cfar
    npatch = pcd_ref.shape[2]
    place = jnp.where(
        lax.broadcasted_iota(jnp.int32, (npatch, nch), 1)
        == (j * (Q_BLOCK // CMP_STRIDE) - npatch // 2 + lax.broadcasted_iota(jnp.int32, (npatch, nch), 0)),
        1.0, 0.0).astype(BF16)
    ph, plo = _split2(pcd_ref[0])
    lc = lc + _dot(ph, place) + _dot(plo, place)
    coli = lax.broadcasted_iota(jnp.int32, (1, nch), 1)
    maskc = (coli * CMP_STRIDE + (CMP_LEN - 1)) <= tq
    lcm = jnp.where(maskc, lc, NEG_INF)
    p = jnp.where(maskc, jnp.exp(lcm - jnp.max(lcm, axis=-1, keepdims=True)), 0.0)
    anyc = tq >= CMP_LEN - 1
    s = jnp.where(anyc, jnp.sum(p, axis=-1, keepdims=True), 1.0)
    pc = p * (1.0 / s)
    o_c = _dot(pc.astype(BF16), vc_ref[0, 0])

    imp = pc[0:Q_BLOCK]
    for g in range(1, GROUP):
        imp = imp + pc[g * Q_BLOCK:(g + 1) * Q_BLOCK]
    mm = mm_ref[...]
    i1, i2, i3 = _split3(imp)
    score = _dot(i1, mm) + _dot(i2, mm) + _dot(i3, mm)
    blk = lax.broadcasted_iota(jnp.int32, (1, n_slc), 1)
    cur = lax.shift_right_logical(q0 + lax.broadcasted_iota(jnp.int32, (Q_BLOCK, 1), 0), 6)
    forced = (blk == 0) | (blk == cur) | (blk == cur - 1)
    sc = jnp.where(forced, FORCE_SCORE, jnp.where(blk <= cur, score, NEG_INF))
    selb = jnp.where(_topk_mask(sc, blk, min(N_SEL, n_slc)), 1.0, 0.0).astype(BF16)

    def sel_rows(k0, width):
        kb = lax.shift_right_logical(k0 + lax.broadcasted_iota(jnp.int32, (n_slc, width), 1), 6)
        e = jnp.where(kb == lax.broadcasted_iota(jnp.int32, (n_slc, width), 0), 1.0, 0.0).astype(BF16)
        se = _dot(selb, e)
        return jnp.concatenate([se] * GROUP, axis=0) > 0.5

    def flash(k, v, bias, mask, first):
        lg = _dot_nt(qm, k) + bias
        if mask is not None:
            lg = jnp.where(mask, lg, NEG_INF)
        mx = jnp.max(lg, axis=-1, keepdims=True)
        if first:
            pe = jnp.exp(lg - mx)
            l_ref[...] = jnp.sum(pe, axis=-1, keepdims=True)
            acc_ref[...] = _dot(pe.astype(BF16), v)
            m_ref[...] = mx
        else:
            m_old = m_ref[...]
            m_new = jnp.maximum(m_old, mx)
            a = jnp.exp(m_old - m_new)
            pe = jnp.exp(lg - m_new)
            l_ref[...] = a * l_ref[...] + jnp.sum(pe, axis=-1, keepdims=True)
            acc_ref[...] = a * acc_ref[...] + _dot(pe.astype(BF16), v)
            m_ref[...] = m_new

    def tile(kref, vref, k0):
        k0 = pl.multiple_of(k0, Q_BLOCK)
        return kref[0, 0, pl.ds(k0, Q_BLOCK), :], vref[0, 0, pl.ds(k0, Q_BLOCK), :]

    colk = lax.broadcasted_iota(jnp.int32, (1, Q_BLOCK), 1)
    causal = colk <= rowq

    k, v = tile(ks_ref, vs_ref, q0)
    flash(k, v, p0_ref[0], sel_rows(q0, Q_BLOCK) & causal, True)

    @pl.when(j >= 1)
    def _():
        k, v = tile(ks_ref, vs_ref, q0 - Q_BLOCK)
        flash(k, v, p1_ref[0], sel_rows(q0 - Q_BLOCK, Q_BLOCK), False)

    far_end = q0 - Q_BLOCK
    colc = lax.broadcasted_iota(jnp.int32, (1, KEY_CHUNK), 1)

    def far_body(c, carry):
        k0 = pl.multiple_of(c * KEY_CHUNK, KEY_CHUNK)
        k = ks_ref[0, 0, pl.ds(k0, KEY_CHUNK), :]
        v = vs_ref[0, 0, pl.ds(k0, KEY_CHUNK), :]
        flash(k, v, cfar, sel_rows(k0, KEY_CHUNK) & ((k0 + colc) < far_end), False)
        return carry

    n_far = lax.shift_right_logical(jnp.maximum(far_end, 0) + (KEY_CHUNK - 1), 9)
    lax.fori_loop(0, n_far, far_body, 0)
    o_s = acc_ref[...] * (1.0 / l_ref[...])

    k, v = tile(kw_ref, vw_ref, q0)
    flash(k, v, p0_ref[0], causal, True)

    @pl.when(j >= 1)
    def _():
        k, v = tile(kw_ref, vw_ref, q0 - Q_BLOCK)
        flash(k, v, p1_ref[0], None, False)

    for back in range(2, WINDOW // Q_BLOCK):
        @pl.when(j >= back)
        def _(back=back):
            k, v = tile(kw_ref, vw_ref, q0 - back * Q_BLOCK)
            flash(k, v, cfar, None, False)

    @pl.when(j >= WINDOW // Q_BLOCK)
    def _():
        k, v = tile(kw_ref, vw_ref, q0 - WINDOW)
        flash(k, v, cfar, colk > rowq, False)

    o_w = acc_ref[...] * (1.0 / l_ref[...])

    g3 = gate_ref[0].reshape(ROWS_Q, 3)
    o = g3[:, 0:1] * o_c + g3[:, 1:2] * o_s + g3[:, 2:3] * o_w
    o_ref[0] = o.reshape(GROUP, Q_BLOCK, HEAD_DIM)


def _bias_tables(table):
    tbl = table.reshape(N_BUCKETS, N_KV_HEADS, GROUP)
    i = jnp.arange(Q_BLOCK)
    rows = lambda a: a.transpose(2, 3, 0, 1).reshape(N_KV_HEADS, ROWS_Q, a.shape[1])
    p0 = rows(tbl[_rel_bucket(i[:, None] - i[None, :])])
    p1 = rows(tbl[_rel_bucket(Q_BLOCK + i[:, None] - i[None, :])])
    cfar = tbl[N_BUCKETS - 1].reshape(N_KV_HEADS, GROUP, 1)
    cfar = jnp.broadcast_to(cfar, (N_KV_HEADS, GROUP, Q_BLOCK)).reshape(N_KV_HEADS, ROWS_Q, 1)
    npatch = 2 * Q_BLOCK // CMP_STRIDE
    rel = i[:, None] - CMP_STRIDE * (jnp.arange(npatch)[None, :] - npatch // 2) - (CMP_LEN - 1)
    pcd = jnp.where(jnp.tile(rel >= 0, (GROUP, 1))[None], rows(tbl[_rel_bucket(rel)]) - cfar, 0.0)
    return cfar, p0, p1, pcd


def _nsa_prompt(q_hm, kv_hm, cmpkv, gates_hm, table):
    B, H, T, hd = q_hm.shape
    nch = cmpkv.shape[2]
    n_slc = T // SLC_BLOCK
    nq = T // Q_BLOCK
    assert T % KEY_CHUNK == 0 and nch == RATIO * n_slc
    cfar, p0, p1, pcd = _bias_tables(table)
    mm = _score_matrix(nch, n_slc)
    nk = N_KV_HEADS
    kv_spec = lambda slab0: pl.BlockSpec((1, 1, T, hd), lambda b, k, j: (b, slab0 + k, 0, 0))
    cmp_spec = lambda slab0: pl.BlockSpec((1, 1, nch, hd), lambda b, k, j: (b, slab0 + k, 0, 0))
    per_head = lambda a: pl.BlockSpec((1,) + a.shape[1:], lambda b, k, j: (k, 0, 0))
    return pl.pallas_call(
        functools.partial(_nsa_kernel, nch=nch, n_slc=n_slc),
        out_shape=jax.ShapeDtypeStruct((B, H, T, hd), F32),
        grid=(B, nk, nq),
        in_specs=[pl.BlockSpec((1, GROUP, Q_BLOCK, hd), lambda b, k, j: (b, k, j, 0)),
                  cmp_spec(0), cmp_spec(nk),
                  kv_spec(2 * nk), kv_spec(3 * nk), kv_spec(4 * nk), kv_spec(5 * nk),
                  pl.BlockSpec((1, GROUP, Q_BLOCK, 3), lambda b, k, j: (b, k, j, 0)),
                  per_head(cfar), per_head(p0), per_head(p1), per_head(pcd),
                  pl.BlockSpec(mm.shape, lambda b, k, j: (0, 0))],
        out_specs=pl.BlockSpec((1, GROUP, Q_BLOCK, hd), lambda b, k, j: (b, k, j, 0)),
        scratch_shapes=[pltpu.VMEM((ROWS_Q, 1), F32), pltpu.VMEM((ROWS_Q, 1), F32),
                        pltpu.VMEM((ROWS_Q, hd), F32)],
        compiler_params=_cparams(3),
        name="nsa_prompt",
    )(q_hm, cmpkv, cmpkv, kv_hm, kv_hm, kv_hm, kv_hm, gates_hm, cfar, p0, p1, pcd, mm)


def _smp_cmp_kernel(q_ref, kc_ref, bias_ref, mm_ref, oc_ref, idx_ref, *, n_cmp, n_slc, cur):
    vc_ref = kc_ref
    nch = kc_ref.shape[2]
    ncol = mm_ref.shape[1]
    coli = lax.broadcasted_iota(jnp.int32, (1, nch), 1)
    maskc = coli < n_cmp
    blk = lax.broadcasted_iota(jnp.int32, (1, ncol), 1)
    k_sel = min(N_SEL, n_slc)
    lane = lax.broadcasted_iota(jnp.int32, (1, LANE), 1)
    for kh in range(N_KV_HEADS):
        qm = q_ref[0, kh * GROUP:(kh + 1) * GROUP].astype(BF16)
        lc = _dot_nt(qm, kc_ref[0, kh]) + bias_ref[kh]
        lcm = jnp.where(maskc, lc, NEG_INF)
        p = jnp.where(maskc, jnp.exp(lcm - jnp.max(lcm, axis=-1, keepdims=True)), 0.0)
        pc = p * (1.0 / jnp.sum(p, axis=-1, keepdims=True))
        oc_ref[0, kh * GROUP:(kh + 1) * GROUP] = _dot(pc.astype(BF16), vc_ref[0, N_KV_HEADS + kh])
        imp = pc[0:1]
        for g in range(1, GROUP):
            imp = imp + pc[g:g + 1]
        mm = mm_ref[...]
        i1, i2, i3 = _split3(imp)
        score = _dot(i1, mm) + _dot(i2, mm) + _dot(i3, mm)
        forced = (blk == 0) | (blk == cur) | (blk == cur - 1)
        sc = jnp.where(forced, FORCE_SCORE, jnp.where(blk <= cur, score, NEG_INF))
        sc = jnp.where(blk < n_slc, sc, -2e38)
        big = jnp.int32(1 << 30)
        out = jnp.zeros((1, LANE), jnp.int32)
        for it in range(k_sel):
            m = jnp.max(sc, axis=-1, keepdims=True)
            first = jnp.min(jnp.where(sc == m, blk, big), axis=-1, keepdims=True)
            out = jnp.where(lane == it, first, out)
            sc = jnp.where(blk == first, -3e38, sc)
        idx_ref[0, kh:kh + 1, :] = out


def _smp_cmp(q_hm, cmpkv, bias_c, mm, n_cmp, n_slc, cur):
    B = q_hm.shape[0]
    nch = cmpkv.shape[2]
    return pl.pallas_call(
        functools.partial(_smp_cmp_kernel, n_cmp=n_cmp, n_slc=n_slc, cur=cur),
        out_shape=(jax.ShapeDtypeStruct((B, N_HEADS, HEAD_DIM), F32),
                   jax.ShapeDtypeStruct((B, N_KV_HEADS, LANE), jnp.int32)),
        grid=(B,),
        in_specs=[pl.BlockSpec((1, N_HEADS, HEAD_DIM), lambda b: (b, 0, 0)),
                  pl.BlockSpec((1, 2 * N_KV_HEADS, nch, HEAD_DIM), lambda b: (b, 0, 0, 0)),
                  pl.BlockSpec(bias_c.shape, lambda b: (0, 0, 0)),
                  pl.BlockSpec(mm.shape, lambda b: (0, 0))],
        out_specs=(pl.BlockSpec((1, N_HEADS, HEAD_DIM), lambda b: (b, 0, 0)),
                   pl.BlockSpec((1, N_KV_HEADS, LANE), lambda b: (b, 0, 0))),
        compiler_params=_cparams(1),
        name="sample_cmp",
    )(q_hm, cmpkv, bias_c, mm)


def _smp_att_kernel(phys_ref, q_ref, pool_ref, bsel_ref, msel_ref, new_ref, win_ref, bwin_ref, bnew_ref,
                    gate_ref, oc_ref, o_ref, buf_ref, sem_ref, *, k_sel):
    b = pl.program_id(0)
    n_blk = N_KV_HEADS * k_sel

    def blk_copy(i):
        return pltpu.make_async_copy(pool_ref.at[phys_ref[b * n_blk + i]], buf_ref.at[i], sem_ref.at[i])

    for i in range(n_blk):
        blk_copy(i).start()
    for i in range(n_blk):
        blk_copy(i).wait()

    nw = win_ref.shape[1]
    for kh in range(N_KV_HEADS):
        rows = slice(kh * GROUP, (kh + 1) * GROUP)
        qm = q_ref[0, rows].astype(BF16)
        kcol = slice(kh * HEAD_DIM, (kh + 1) * HEAD_DIM)
        vcol = slice(KV_WIDTH + kh * HEAD_DIM, KV_WIDTH + (kh + 1) * HEAD_DIM)
        bnew = bnew_ref[kh]

        def branch(kp, vp, bias, mask, knew, vnew, new_ok):
            lg = _dot_nt(qm, kp) + bias
            lg = jnp.where(mask, lg, NEG_INF)
            ln = jnp.sum(qm.astype(F32) * knew.astype(F32), axis=-1, keepdims=True) + bnew
            if new_ok is not None:
                ln = jnp.where(new_ok, ln, NEG_INF)
            mx = jnp.maximum(jnp.max(lg, axis=-1, keepdims=True), ln)
            pe = jnp.where(mask, jnp.exp(lg - mx), 0.0)
            pn = jnp.exp(ln - mx)
            if new_ok is not None:
                pn = jnp.where(new_ok, pn, 0.0)
            den = jnp.sum(pe, axis=-1, keepdims=True) + pn
            any_ok = den > 0.0
            inv = 1.0 / jnp.where(any_ok, den, 1.0)
            o = _dot(pe.astype(BF16), vp) + pn.astype(BF16).astype(F32) * vnew.astype(F32)
            return jnp.where(any_ok, o * inv, 0.0)

        kp = buf_ref[kh * k_sel:(kh + 1) * k_sel, :, kcol].reshape(k_sel * SLC_BLOCK, HEAD_DIM).astype(BF16)
        vp = buf_ref[kh * k_sel:(kh + 1) * k_sel, :, vcol].reshape(k_sel * SLC_BLOCK, HEAD_DIM).astype(BF16)
        knew = new_ref[0, 0:1, kcol].astype(BF16)
        vnew = new_ref[0, 0:1, vcol].astype(BF16)
        msel = msel_ref[0, kh]
        nkeys = k_sel * SLC_BLOCK
        o_s = branch(kp, vp, bsel_ref[0, kh], msel[:, 0:nkeys] > 0.5, knew, vnew, msel[:, nkeys:nkeys + 1] > 0.5)

        kw = win_ref[0, :, kcol].astype(BF16)
        vw = win_ref[0, :, vcol].astype(BF16)
        knw = new_ref[0, 1:2, kcol].astype(BF16)
        vnw = new_ref[0, 1:2, vcol].astype(BF16)
        dist = nw - lax.broadcasted_iota(jnp.int32, (1, nw), 1)
        o_w = branch(kw, vw, bwin_ref[kh], dist < WINDOW, knw, vnw, None)

        g3 = gate_ref[0, rows]
        o_ref[0, rows] = g3[:, 0:1] * oc_ref[0, rows] + g3[:, 1:2] * o_s + g3[:, 2:3] * o_w


def _smp_att(phys, q_hm, pool_blk, bias_sel, mask_sel, new_rows, win_buf, bias_win, bias_new, gates, o_c, k_sel):
    B = q_hm.shape[0]
    n_blk = N_KV_HEADS * k_sel
    nw = win_buf.shape[1]
    full = lambda a: pl.BlockSpec(a.shape, lambda b, ph: (0,) * a.ndim)
    per_b = lambda a: pl.BlockSpec((1,) + a.shape[1:], lambda b, ph: (b,) + (0,) * (a.ndim - 1))
    return pl.pallas_call(
        functools.partial(_smp_att_kernel, k_sel=k_sel),
        out_shape=jax.ShapeDtypeStruct((B, N_HEADS, HEAD_DIM), F32),
        grid_spec=pltpu.PrefetchScalarGridSpec(
            num_scalar_prefetch=1,
            grid=(B,),
            in_specs=[per_b(q_hm), pl.BlockSpec(memory_space=pl.ANY), per_b(bias_sel), per_b(mask_sel),
                      per_b(new_rows), per_b(win_buf), full(bias_win), full(bias_new), per_b(gates), per_b(o_c)],
            out_specs=pl.BlockSpec((1, N_HEADS, HEAD_DIM), lambda b, ph: (b, 0, 0)),
            scratch_shapes=[pltpu.VMEM((n_blk, SLC_BLOCK, pool_blk.shape[2]), F32),
                            pltpu.SemaphoreType.DMA((n_blk,))]),
        compiler_params=_cparams(1),
        name="sample_att",
    )(phys, q_hm, pool_blk, bias_sel, mask_sel, new_rows, win_buf, bias_win, bias_new, gates, o_c)


def _merge_kernel(x_ref, cn_ref, att_ref, gna_ref, wo_ref, m2_ref, m3_ref, m4_ref, g2_ref, rw_ref, rb_ref,
                  x1_ref, h2_ref, ti_ref, tg_ref, *, n_exp, cw_width):
    att_n = _rms(att_ref[0], gna_ref[...]).astype(BF16)
    mix = _dot(cn_ref[0], wo_ref[0:cw_width, :]) + _dot(att_n, wo_ref[cw_width:, :])
    x1 = x_ref[0] + m2_ref[0] * mix
    x1_ref[0] = x1
    h2 = _rms(x1, g2_ref[...]) * (1.0 + m4_ref[0]) + m3_ref[0]
    h2_ref[0] = h2.astype(BF16)
    logits = _dot3(h2, rw_ref[...]) + rb_ref[...]
    lane = lax.broadcasted_iota(jnp.int32, (1, LANE), 1)
    sc = jnp.where(lane < n_exp, logits, -2e38)
    big = jnp.int32(1 << 30)
    ti = jnp.zeros(sc.shape, jnp.int32)
    tv = jnp.zeros(sc.shape, F32)
    v0 = None
    den = None
    for k in range(TOP_K):
        m = jnp.max(sc, axis=-1, keepdims=True)
        first = jnp.min(jnp.where(sc == m, lane, big), axis=-1, keepdims=True)
        if k == 0:
            v0 = m
        e = jnp.exp(m - v0)
        den = e if den is None else den + e
        ti = jnp.where(lane == k, first, ti)
        tv = jnp.where(lane == k, e, tv)
        sc = jnp.where(lane == first, -3e38, sc)
    ti_ref[0] = ti
    tg_ref[0] = tv * (1.0 / den)


def _merge(x, convn, att, gn_att, wo_b, m2, m3, m4, ln2_g, rw_pad, rb_pad, *, seq_mode, tm, n_exp):
    B, T, D = x.shape
    cw_width = convn.shape[2]
    aw = att.shape[2]
    row_blk = lambda w: pl.BlockSpec((1, tm, w), lambda b, t: (b, t, 0))
    mod_spec = pl.BlockSpec((1, 1, D), lambda b, t: (b, 0, 0)) if seq_mode else row_blk(D)
    const = lambda shape: pl.BlockSpec(shape, lambda b, t: (0,) * len(shape))
    return pl.pallas_call(
        functools.partial(_merge_kernel, n_exp=n_exp, cw_width=cw_width),
        out_shape=(jax.ShapeDtypeStruct((B, T, D), F32), jax.ShapeDtypeStruct((B, T, D), BF16),
                   jax.ShapeDtypeStruct((B, T, LANE), jnp.int32), jax.ShapeDtypeStruct((B, T, LANE), F32)),
        grid=(B, T // tm),
        in_specs=[row_blk(D), row_blk(cw_width), row_blk(aw), const((1, aw)), const(wo_b.shape),
                  mod_spec, mod_spec, mod_spec, const((1, D)), const(rw_pad.shape), const((1, LANE))],
        out_specs=(row_blk(D), row_blk(D), row_blk(LANE), row_blk(LANE)),
        compiler_params=_cparams(2),
        name="merge_seq" if seq_mode else "merge_rows",
    )(x, convn, att, gn_att.reshape(1, aw), wo_b, m2, m3, m4, ln2_g.reshape(1, D), rw_pad, rb_pad)


def _moe_kernel(be_ref, nu_ref, xs_ref, wgu_ref, bgu_ref, wd_ref, bd_ref, y_ref, wgu_b, wd_b, *, d_ff):
    i = pl.program_id(0)
    prev = be_ref[jnp.maximum(i - 1, 0)]
    changed = jnp.logical_or(i == 0, be_ref[i] != prev)

    @pl.when(jnp.logical_and(changed, i < nu_ref[0]))
    def _():
        wgu_b[...] = wgu_ref[0].astype(BF16)
        wd_b[...] = wd_ref[0].astype(BF16)

    @pl.when(i < nu_ref[0])
    def _():
        gu = _dot(xs_ref[...], wgu_b[...]) + bgu_ref[0]
        gate_h = jnp.minimum(gu[:, 0:d_ff], SWIGLU_LIMIT)
        up_h = jnp.clip(gu[:, d_ff:], -SWIGLU_LIMIT, SWIGLU_LIMIT)
        act = (up_h + 1.0) * gate_h * _sigmoid(SWIGLU_ALPHA * gate_h)
        y_ref[...] = _dot(act.astype(BF16), wd_b[...]) + bd_ref[0]

    @pl.when(i >= nu_ref[0])
    def _():
        y_ref[...] = jnp.zeros(y_ref.shape, F32)


def _moe_experts(block_e, n_used, xs, w_gu, b_gu, w_down, b_down):
    n_rows, D = xs.shape
    E, _, two_ff = w_gu.shape
    d_ff = two_ff // 2
    n_blocks = n_rows // MOE_ROWS
    return pl.pallas_call(
        functools.partial(_moe_kernel, d_ff=d_ff),
        out_shape=jax.ShapeDtypeStruct((n_rows, D), F32),
        grid_spec=pltpu.PrefetchScalarGridSpec(
            num_scalar_prefetch=2,
            grid=(n_blocks,),
            in_specs=[pl.BlockSpec((MOE_ROWS, D), lambda i, be, nu: (i, 0)),
                      pl.BlockSpec((1, D, two_ff), lambda i, be, nu: (be[i], 0, 0)),
                      pl.BlockSpec((1, 1, two_ff), lambda i, be, nu: (be[i], 0, 0)),
                      pl.BlockSpec((1, d_ff, D), lambda i, be, nu: (be[i], 0, 0)),
                      pl.BlockSpec((1, 1, D), lambda i, be, nu: (be[i], 0, 0))],
            out_specs=pl.BlockSpec((MOE_ROWS, D), lambda i, be, nu: (i, 0)),
            scratch_shapes=[pltpu.VMEM((D, two_ff), BF16), pltpu.VMEM((d_ff, D), BF16)]),
        compiler_params=_cparams(1),
        name="moe_experts",
    )(block_e, n_used, xs, w_gu, b_gu.reshape(E, 1, two_ff), w_down, b_down.reshape(E, 1, D))


def _route(top_idx, n_tok, n_exp):
    n_assign = n_tok * TOP_K
    flat_e = top_idx.reshape(-1)
    onehot = (flat_e[:, None] == jnp.arange(n_exp, dtype=jnp.int32)[None, :]).astype(jnp.int32)
    csum = jnp.cumsum(onehot, axis=0)
    rank = jnp.take_along_axis(csum, flat_e[:, None], axis=1)[:, 0] - 1
    counts = csum[-1]
    padded = (counts + MOE_ROWS - 1) // MOE_ROWS * MOE_ROWS
    pad_end = jnp.cumsum(padded)
    pad_start = pad_end - padded
    start = jnp.cumsum(counts) - counts
    dest = (pad_start[flat_e] + rank).astype(jnp.int32)
    n_blocks = -(-(n_assign + n_exp * (MOE_ROWS - 1)) // MOE_ROWS)
    block_e = jnp.minimum(jnp.searchsorted(pad_end, jnp.arange(n_blocks) * MOE_ROWS, side='right'),
                          n_exp - 1).astype(jnp.int32)
    order = jnp.argsort(flat_e)
    r = jnp.arange(n_blocks * MOE_ROWS)
    e_r = block_e[r // MOE_ROWS]
    off = r - pad_start[e_r]
    src = jnp.clip(start[e_r] + off, 0, n_assign - 1)
    row_tok = jnp.where((off >= 0) & (off < counts[e_r]), order[src] // TOP_K, n_tok).astype(jnp.int32)
    n_used = (pad_end[-1] // MOE_ROWS).astype(jnp.int32).reshape(1)
    return row_tok, dest.reshape(n_tok, TOP_K), block_e, n_used


def _final_kernel(x1_ref, yg_ref, tg_ref, m5_ref, fg_ref, o_ref):
    tg = tg_ref[0]
    ff = tg[:, 0:1] * yg_ref[0, 0]
    for k in range(1, TOP_K):
        ff = ff + tg[:, k:k + 1] * yg_ref[0, k]
    o_ref[0] = _rms(x1_ref[0] + m5_ref[0] * ff, fg_ref[...])


def _final(x1, yg, tg, m5, final_g, *, seq_mode, tm):
    B, T, D = x1.shape
    row_blk = lambda w: pl.BlockSpec((1, tm, w), lambda b, t: (b, t, 0))
    mod_spec = pl.BlockSpec((1, 1, D), lambda b, t: (b, 0, 0)) if seq_mode else row_blk(D)
    return pl.pallas_call(
        _final_kernel,
        out_shape=jax.ShapeDtypeStruct((B, T, D), F32),
        grid=(B, T // tm),
        in_specs=[row_blk(D), pl.BlockSpec((1, TOP_K, tm, D), lambda b, t: (b, 0, t, 0)), row_blk(LANE),
                  mod_spec, pl.BlockSpec((1, D), lambda b, t: (0, 0))],
        out_specs=row_blk(D),
        compiler_params=_cparams(2),
        name="final_seq" if seq_mode else "final_rows",
    )(x1, yg, tg, m5, final_g.reshape(1, D))


def _row_tile(t):
    for tm in (512, 256, 128, 64, 32, 16, 8):
        if t % tm == 0:
            return tm
    raise ValueError(f"unsupported row count {t}")


def kernel(x_prompt, x_sample, cache_kv_cmp, cache_kv_slc, state_kv_win, state_conv, page_table, c_prompt, c_sample,
           rel_bias_table, ln1_g, ln2_g, w_ada, b_ada, w_in, conv_w, cmp_pe, cmp_w1, cmp_w2, gn_conv, gn_att, w_o,
           router_w, router_b, w_gu, b_gu, w_down, b_down, final_g):
    B, T, D = x_prompt.shape
    BS, TS, _ = x_sample.shape
    depth = w_in.shape[0]
    assert depth == 1 and TS == 1
    n_pool, page = cache_kv_cmp.shape[1], cache_kv_cmp.shape[2]
    n_pages = page_table.shape[1]
    past_len = n_pages * page
    n_exp = router_w.shape[2]
    cw_width = conv_w.shape[2]
    kvw2 = 2 * KV_WIDTH
    assert past_len % SLC_BLOCK == 0 and past_len % CMP_STRIDE == 0 and T % page == 0

    in_cols = w_in.shape[2]
    gate0 = 3 * cw_width + N_HEADS * HEAD_DIM + 3 * kvw2
    w_pad = jnp.pad(w_in[0], ((0, 0), (0, gate0 + LANE - in_cols))).astype(BF16)
    wo_b = w_o[0].astype(BF16)
    rw_pad = jnp.pad(router_w[0], ((0, 0), (0, LANE - n_exp)))
    rb_pad = jnp.pad(router_b[0], (0, LANE - n_exp)).reshape(1, LANE)
    w_blk = _cmp_block_weight(cmp_w1[0])

    n_c = B + BS
    n_cp = -(-n_c // 8) * 8
    c_all = jnp.pad(jnp.concatenate([c_prompt, c_sample], axis=0), ((0, n_cp - n_c), (0, 0)))
    mod = _modulation(c_all, w_ada[0], b_ada[0]).reshape(n_cp, 6, D)
    mp = [mod[:B, i].reshape(B, 1, D) for i in range(6)]
    ms = [mod[B:n_c, i].reshape(1, BS, D) for i in range(6)]

    tm = _row_tile(T)
    zeros_prev = jnp.zeros((B, CONV_K - 1, cw_width), F32)
    convn_p, q_p, kvc_p, kvs_p, kvw_p, kvh_p, gate_p, vlast_p = _inproj(
        x_prompt, mp[0], mp[1], ln1_g[0], w_pad, conv_w[0], gn_conv[0], zeros_prev, zeros_prev, seq_mode=True, tm=tm)
    pt_p = jnp.arange(B * (T // page), dtype=jnp.int32).reshape(B, T // page)
    g_p = math.gcd(T // page, 32)
    ab_p = _cmp_ab(kvc_p.reshape(B * (T // page), page, kvw2), pt_p, w_blk, g_p)
    cmpkv_p = _cmp_finish(ab_p, cmp_pe[0], cmp_w1[0], cmp_w2[0])
    gates_p = gate_p[:, :, :3 * N_HEADS].reshape(B, T, N_HEADS, 3).transpose(0, 2, 1, 3)
    att_hm = _nsa_prompt(q_p, kvh_p, cmpkv_p, gates_p, rel_bias_table)
    att_p = att_hm.transpose(0, 2, 1, 3).reshape(B, T, N_HEADS * HEAD_DIM)
    x1_p, h2_p, ti_p, tg_p = _merge(x_prompt, convn_p, att_p, gn_att[0], wo_b, mp[2], mp[3], mp[4], ln2_g[0],
                                    rw_pad, rb_pad, seq_mode=True, tm=tm, n_exp=n_exp)

    xs_rows = x_sample.reshape(1, BS, D)
    prev2 = state_conv[0][:, 0, :].reshape(1, BS, cw_width)
    prev1 = state_conv[0][:, 1, :].reshape(1, BS, cw_width)
    tms = _row_tile(BS)
    convn_s, q_s, kvc_s, kvs_s, kvw_s, _, gate_s, v_s = _inproj(
        xs_rows, ms[0], ms[1], ln1_g[0], w_pad, conv_w[0], gn_conv[0], prev2, prev1, seq_mode=False, tm=tms)
    n_cmp = (past_len + TS - CMP_LEN) // CMP_STRIDE + 1
    nch_s = past_len // CMP_STRIDE
    assert n_cmp + 1 == nch_s
    ab_s = _cmp_ab(cache_kv_cmp[0].reshape(n_pool, page, kvw2), page_table, w_blk, math.gcd(n_pages, 32))
    cmpkv_s = _cmp_finish(ab_s, cmp_pe[0], cmp_w1[0], cmp_w2[0])
    t_q = past_len
    n_slc = -(-(past_len + TS) // SLC_BLOCK)
    cur = t_q // SLC_BLOCK
    k_sel = min(N_SEL, n_slc)
    tbl = rel_bias_table.reshape(N_BUCKETS, N_KV_HEADS, GROUP)
    pos_c = jnp.arange(nch_s) * CMP_STRIDE + CMP_LEN - 1
    bias_c = tbl[_rel_bucket(t_q - pos_c)].transpose(1, 2, 0)
    ncol = -(-n_slc // LANE) * LANE
    mm_s = _score_matrix(nch_s, ncol)
    q_s_hm = q_s.reshape(N_HEADS, BS, HEAD_DIM).transpose(1, 0, 2).astype(F32)
    o_c_s, idx_pad = _smp_cmp(q_s_hm, cmpkv_s, bias_c, mm_s, n_cmp, n_slc, cur)
    idx = idx_pad[:, :, :k_sel]
    blk_per_page = page // SLC_BLOCK
    pg = jnp.minimum(idx // blk_per_page, n_pages - 1)
    phys = (jnp.take_along_axis(page_table, pg.reshape(BS, -1), axis=1).reshape(idx.shape) * blk_per_page
            + idx % blk_per_page).astype(jnp.int32)
    pos_s = idx[..., None] * SLC_BLOCK + jnp.arange(SLC_BLOCK)
    pos_s = pos_s.reshape(BS, N_KV_HEADS, k_sel * SLC_BLOCK)
    bias_sel = tbl[_rel_bucket(t_q - pos_s), jnp.arange(N_KV_HEADS)[None, :, None]]
    bias_sel = bias_sel.transpose(0, 1, 3, 2)
    new_sel = jnp.any(idx == cur, axis=-1, keepdims=True)
    mask_sel = jnp.concatenate([pos_s < past_len, new_sel], axis=-1).astype(F32)[:, :, None, :]
    nw = state_kv_win.shape[2]
    bias_win = tbl[_rel_bucket(nw - jnp.arange(nw))].transpose(1, 2, 0)
    bias_new = tbl[0].reshape(N_KV_HEADS, GROUP, 1)
    new_rows = jnp.stack([kvs_s[0], kvw_s[0]], axis=1)
    gates_s = gate_s[0, :, :3 * N_HEADS].reshape(BS, N_HEADS, 3)
    pool_blk = cache_kv_slc[0].reshape(n_pool * blk_per_page, SLC_BLOCK, kvw2)
    att_s = _smp_att(phys.reshape(-1), q_s_hm, pool_blk, bias_sel, mask_sel, new_rows,
                     state_kv_win[0].reshape(BS, nw, kvw2), bias_win, bias_new, gates_s, o_c_s, k_sel)
    att_s = att_s.reshape(1, BS, N_HEADS * HEAD_DIM)
    x1_s, h2_s, ti_s, tg_s = _merge(xs_rows, convn_s, att_s, gn_att[0], wo_b, ms[2], ms[3], ms[4], ln2_g[0],
                                    rw_pad, rb_pad, seq_mode=False, tm=tms, n_exp=n_exp)

    n_tok = B * T + BS
    h2_all = jnp.concatenate([h2_p.reshape(B * T, D), h2_s.reshape(BS, D), jnp.zeros((8, D), BF16)], axis=0)
    top_idx = jnp.concatenate([ti_p.reshape(B * T, LANE), ti_s.reshape(BS, LANE)], axis=0)[:, :TOP_K]
    row_tok, dest, block_e, n_used = _route(top_idx, n_tok, n_exp)
    xs = jnp.take(h2_all, row_tok, axis=0)
    yb = _moe_experts(block_e, n_used, xs, w_gu[0], b_gu[0], w_down[0], b_down[0])
    dest_p = dest[:B * T].reshape(B, T, TOP_K).transpose(0, 2, 1)
    yg_p = jnp.take(yb, dest_p.reshape(-1), axis=0).reshape(B, TOP_K, T, D)
    yg_s = jnp.take(yb, dest[B * T:].T.reshape(-1), axis=0).reshape(1, TOP_K, BS, D)
    y_p = _final(x1_p, yg_p, tg_p, mp[5], final_g, seq_mode=True, tm=tm)
    y_s = _final(x1_s, yg_s, tg_s, ms[5], final_g, seq_mode=False, tm=tms)

    kv_tail = (2, N_KV_HEADS, HEAD_DIM)
    page_shape = (depth, B, T // page, page) + kv_tail
    w_keep = min(WINDOW, T)
    new_win_s = jnp.concatenate([state_kv_win[0][:, TS:], kvw_s.reshape(BS, TS, *kv_tail)], axis=1)
    new_conv_s = jnp.concatenate([state_conv[0][:, TS:], v_s.reshape(BS, TS, cw_width)], axis=1)
    return (y_p, y_s.reshape(BS, TS, D),
            kvc_p.reshape(page_shape), kvc_s.reshape((depth, BS, TS) + kv_tail),
            kvs_p.reshape(page_shape), kvs_s.reshape((depth, BS, TS) + kv_tail),
            kvw_p[:, T - w_keep:].reshape((depth, B, w_keep) + kv_tail), new_win_s[None],
            vlast_p[None], new_conv_s[None])
```

```python
import functools
import math

import numpy as np
import jax
import jax.numpy as jnp
from jax import lax
from jax.experimental import pallas as pl
from jax.experimental.pallas import tpu as pltpu

F32 = jnp.float32
BF16 = jnp.bfloat16

CONV_K = 3
N_HEADS = 8
N_KV_HEADS = 2
GROUP = N_HEADS // N_KV_HEADS
HEAD_DIM = 64
KV_WIDTH = N_KV_HEADS * HEAD_DIM
CMP_LEN = 32
CMP_STRIDE = 16
SLC_BLOCK = 64
RATIO = SLC_BLOCK // CMP_STRIDE
N_SEL = 16
WINDOW = 512
Q_BLOCK = 128
N_BUCKETS = 32
REL_MAX_DIST = 128
TOP_K = 4
SWIGLU_LIMIT = 7.0
SWIGLU_ALPHA = 1.702
EPS = 1e-6
NEG_INF = -1e30
FORCE_SCORE = 1e4

LANE = 128
ROWS_Q = GROUP * Q_BLOCK
KEY_CHUNK = 512
TILES = KEY_CHUNK // Q_BLOCK
MOE_ROWS = 256
VMEM_LIMIT = 48 * 1024 * 1024


def _cparams(n_axes):
    return pltpu.CompilerParams(dimension_semantics=("arbitrary",) * n_axes, vmem_limit_bytes=VMEM_LIMIT)


def _dot(a, b):
    return jnp.dot(a, b, preferred_element_type=F32)


def _dot_nt(a, b):
    return lax.dot_general(a, b, (((1,), (1,)), ((), ())), preferred_element_type=F32)


def _dot_tn(a, b):
    return lax.dot_general(a, b, (((0,), (0,)), ((), ())), preferred_element_type=F32)


def _split2(x):
    hi = x.astype(BF16)
    lo = (x - hi.astype(F32)).astype(BF16)
    return hi, lo


def _split3(x):
    a = x.astype(BF16)
    r = x - a.astype(F32)
    b = r.astype(BF16)
    c = (r - b.astype(F32)).astype(BF16)
    return a, b, c


def _dot3(a, b):
    ah, al = _split2(a)
    bh, bl = _split2(b)
    return _dot(ah, bh) + _dot(ah, bl) + _dot(al, bh)


def _sigmoid(x):
    return 1.0 / (1.0 + jnp.exp(-x))


def _rms(x, g):
    return x * lax.rsqrt(jnp.mean(x * x, axis=-1, keepdims=True) + EPS) * g


def _rel_bucket(dist):
    n = jnp.maximum(dist, 0)
    max_exact = N_BUCKETS // 2
    large = max_exact + (jnp.log(jnp.maximum(n, 1).astype(F32) / max_exact)
                         / math.log(REL_MAX_DIST / max_exact) * (N_BUCKETS - max_exact)).astype(jnp.int32)
    return jnp.where(n < max_exact, n, jnp.minimum(large, N_BUCKETS - 1))


def _mod_kernel(c_ref, w_ref, b_ref, o_ref):
    c = c_ref[...]
    o_ref[...] = _dot3(c * _sigmoid(c), w_ref[...]) + b_ref[...]


def _modulation(c, w_ada, b_ada):
    n, d = c.shape
    cols = w_ada.shape[1]
    bn = 1536
    return pl.pallas_call(
        _mod_kernel,
        out_shape=jax.ShapeDtypeStruct((n, cols), F32),
        grid=(cols // bn,),
        in_specs=[pl.BlockSpec((n, d), lambda i: (0, 0)),
                  pl.BlockSpec((d, bn), lambda i: (0, i)),
                  pl.BlockSpec((1, bn), lambda i: (0, i))],
        out_specs=pl.BlockSpec((n, bn), lambda i: (0, i)),
        compiler_params=_cparams(1),
        name="modulation",
    )(c, w_ada, b_ada.reshape(1, cols))


def _inproj_kernel(x_ref, m0_ref, m1_ref, g1_ref, w_ref, cw_ref, gnc_ref, pa_ref, pb_ref,
                   convn_ref, q_ref, kvc_ref, kvs_ref, kvw_ref, kvh_ref, gate_ref, vlast_ref,
                   carry_ref, *, seq_mode, tm, cw_width):
    x = x_ref[0]
    h = _rms(x, g1_ref[...]) * (1.0 + m1_ref[0]) + m0_ref[0]
    hb = h.astype(BF16)
    c3 = 3 * cw_width
    uc = _dot(hb, w_ref[:, 0:c3])
    b_g = uc[:, 0:cw_width]
    v = uc[:, cw_width:2 * cw_width] * uc[:, 2 * cw_width:c3]
    if seq_mode:
        @pl.when(pl.program_id(1) == 0)
        def _():
            carry_ref[0:2, :] = pa_ref[0]
        c0 = carry_ref[0:1, :]
        c1 = carry_ref[1:2, :]
        row = lax.broadcasted_iota(jnp.int32, (tm, 1), 0)
        vm1 = jnp.where(row == 0, c1, pltpu.roll(v, 1, 0))
        vm2 = jnp.where(row == 0, c0, jnp.where(row == 1, c1, pltpu.roll(v, 2, 0)))
        carry_ref[0:2, :] = v[tm - 2:tm, :]
        vlast_ref[0] = v[tm - 2:tm, :]
    else:
        vm2 = pa_ref[0]
        vm1 = pb_ref[0]
        vlast_ref[0] = v
    cw = cw_ref[...]
    y = cw[0:1, :] * vm2 + cw[1:2, :] * vm1 + cw[2:3, :] * v
    convn_ref[0] = _rms(b_g * y, gnc_ref[...]).astype(BF16)

    aw = N_HEADS * HEAD_DIM
    uq = _dot(hb, w_ref[:, c3:c3 + aw]) * (HEAD_DIM ** -0.5)
    for hh in range(N_HEADS):
        q_ref[0, hh] = uq[:, hh * HEAD_DIM:(hh + 1) * HEAD_DIM].astype(BF16)
    kv0 = c3 + aw
    kvw3 = 3 * 2 * KV_WIDTH
    ukv = _dot(hb, w_ref[:, kv0:kv0 + kvw3])
    kvc_ref[0] = ukv[:, 0:2 * KV_WIDTH]
    kvs_ref[0] = ukv[:, 2 * KV_WIDTH:4 * KV_WIDTH]
    kvw_ref[0] = ukv[:, 4 * KV_WIDTH:6 * KV_WIDTH]
    for s in range(kvw3 // HEAD_DIM):
        kvh_ref[0, s] = ukv[:, s * HEAD_DIM:(s + 1) * HEAD_DIM].astype(BF16)
    ug = _dot(hb, w_ref[:, kv0 + kvw3:kv0 + kvw3 + LANE])
    gate_ref[0] = _sigmoid(ug)


def _inproj(x, m0, m1, ln_g, w_pad, conv_w, gn_conv, pa, pb, *, seq_mode, tm):
    B, T, D = x.shape
    cw_width = conv_w.shape[1]
    n_slab = 3 * 2 * KV_WIDTH // HEAD_DIM
    nt = T // tm
    row_blk = lambda w: pl.BlockSpec((1, tm, w), lambda b, t: (b, t, 0))
    if seq_mode:
        mod_spec = pl.BlockSpec((1, 1, D), lambda b, t: (b, 0, 0))
        prev_spec = pl.BlockSpec((1, 2, cw_width), lambda b, t: (b, 0, 0))
        vlast_shape = jax.ShapeDtypeStruct((B, 2, cw_width), F32)
        vlast_spec = pl.BlockSpec((1, 2, cw_width), lambda b, t: (b, 0, 0))
    else:
        mod_spec = row_blk(D)
        prev_spec = row_blk(cw_width)
        vlast_shape = jax.ShapeDtypeStruct((B, T, cw_width), F32)
        vlast_spec = row_blk(cw_width)
    const = lambda shape: pl.BlockSpec(shape, lambda b, t: (0,) * len(shape))
    kern = functools.partial(_inproj_kernel, seq_mode=seq_mode, tm=tm, cw_width=cw_width)
    return pl.pallas_call(
        kern,
        out_shape=(jax.ShapeDtypeStruct((B, T, cw_width), BF16),
                   jax.ShapeDtypeStruct((B, N_HEADS, T, HEAD_DIM), BF16),
                   jax.ShapeDtypeStruct((B, T, 2 * KV_WIDTH), F32),
                   jax.ShapeDtypeStruct((B, T, 2 * KV_WIDTH), F32),
                   jax.ShapeDtypeStruct((B, T, 2 * KV_WIDTH), F32),
                   jax.ShapeDtypeStruct((B, n_slab, T, HEAD_DIM), BF16),
                   jax.ShapeDtypeStruct((B, T, LANE), F32),
                   vlast_shape),
        grid=(B, nt),
        in_specs=[row_blk(D), mod_spec, mod_spec, const((1, D)), const(w_pad.shape), const(conv_w.shape),
                  const((1, cw_width)), prev_spec, prev_spec],
        out_specs=(row_blk(cw_width),
                   pl.BlockSpec((1, N_HEADS, tm, HEAD_DIM), lambda b, t: (b, 0, t, 0)),
                   row_blk(2 * KV_WIDTH), row_blk(2 * KV_WIDTH), row_blk(2 * KV_WIDTH),
                   pl.BlockSpec((1, n_slab, tm, HEAD_DIM), lambda b, t: (b, 0, t, 0)),
                   row_blk(LANE), vlast_spec),
        scratch_shapes=[pltpu.VMEM((8, cw_width), F32)],
        compiler_params=_cparams(2),
        name="inproj_seq" if seq_mode else "inproj_rows",
    )(x, m0, m1, ln_g.reshape(1, D), w_pad, conv_w, gn_conv.reshape(1, cw_width), pa, pb)


def _cmp_ab_kernel(pt_ref, *refs, G):
    pages = refs[:2 * G]
    w_ref, out_ref, x_ref = refs[2 * G], refs[2 * G + 1], refs[2 * G + 2]
    half = KV_WIDTH
    for j in range(G):
        for r in range(CMP_STRIDE):
            for c in range(2):
                x_ref[c, j * 8:(j + 1) * 8, r * half:(r + 1) * half] = (
                    pages[2 * j + c][0, pl.ds(r, 8, stride=CMP_STRIDE), :])
    wcols = w_ref.shape[2]
    for c in range(2):
        out_ref[0, :, c * wcols:(c + 1) * wcols] = _dot(x_ref[c].astype(BF16), w_ref[c])


def _cmp_ab(pool, page_table, w_blk, G):
    P, page, width = pool.shape
    B, n_pages = page_table.shape
    cpp = page // CMP_STRIDE
    assert cpp == 8 and n_pages % G == 0
    kdim = CMP_STRIDE * width // 2

    def pg_spec(j, c):
        return pl.BlockSpec((1, page, width // 2), lambda b, g, pt: (pt[b * n_pages + g * G + j], 0, c))

    return pl.pallas_call(
        functools.partial(_cmp_ab_kernel, G=G),
        out_shape=jax.ShapeDtypeStruct((B, n_pages * cpp, 2 * w_blk.shape[2]), F32),
        grid_spec=pltpu.PrefetchScalarGridSpec(
            num_scalar_prefetch=1,
            grid=(B, n_pages // G),
            in_specs=[pg_spec(j, c) for j in range(G) for c in range(2)]
            + [pl.BlockSpec(w_blk.shape, lambda b, g, pt: (0, 0, 0))],
            out_specs=pl.BlockSpec((1, G * cpp, 2 * w_blk.shape[2]), lambda b, g, pt: (b, g, 0)),
            scratch_shapes=[pltpu.VMEM((2, G * cpp, kdim), F32)]),
        compiler_params=_cparams(2),
        name="cmp_partial",
    )(page_table.reshape(-1).astype(jnp.int32), *([pool] * (2 * G)), w_blk)


def _gelu_tanh(x):
    return 0.5 * x * (1.0 + jnp.tanh(math.sqrt(2.0 / math.pi) * (x + 0.044715 * (x * x * x))))


def _cmp_fin_kernel(ab_ref, pe_ref, w1_ref, w2_ref, out_ref, *, nch):
    for c in range(2):
        pe_t = _dot(pe_ref[c:c + 1, :].astype(BF16), w1_ref[c].astype(BF16))
        w2 = w2_ref[c].astype(BF16)
        for k in range(N_KV_HEADS):
            base = (c * N_KV_HEADS + k) * 2 * HEAD_DIM
            slab = ab_ref[0, :, base:base + 2 * HEAD_DIM]
            nxt = pltpu.roll(slab, nch - 1, 0)
            pre = slab[:, 0:HEAD_DIM] + nxt[:, HEAD_DIM:2 * HEAD_DIM] + pe_t
            out_ref[0, c * N_KV_HEADS + k] = _dot(_gelu_tanh(pre).astype(BF16), w2).astype(BF16)


def _cmp_finish(ab, cmp_pe, cmp_w1, cmp_w2):
    B, nch, w = ab.shape
    pe = cmp_pe.reshape(2, CMP_LEN * HEAD_DIM)
    return pl.pallas_call(
        functools.partial(_cmp_fin_kernel, nch=nch),
        out_shape=jax.ShapeDtypeStruct((B, 2 * N_KV_HEADS, nch, HEAD_DIM), BF16),
        grid=(B,),
        in_specs=[pl.BlockSpec((1, nch, w), lambda b: (b, 0, 0)),
                  pl.BlockSpec(pe.shape, lambda b: (0, 0)),
                  pl.BlockSpec(cmp_w1.shape, lambda b: (0, 0, 0)),
                  pl.BlockSpec(cmp_w2.shape, lambda b: (0, 0, 0))],
        out_specs=pl.BlockSpec((1, 2 * N_KV_HEADS, nch, HEAD_DIM), lambda b: (b, 0, 0, 0)),
        compiler_params=_cparams(1),
        name="cmp_finish",
    )(ab, pe, cmp_w1, cmp_w2)


def _cmp_block_weight(cmp_w1):
    hid = cmp_w1.shape[2]
    w = cmp_w1.reshape(2, 2, CMP_STRIDE, HEAD_DIM, hid)
    eye = jnp.eye(N_KV_HEADS, dtype=cmp_w1.dtype)
    wb = jnp.einsum('cardh,kj->crkdjah', w, eye)
    return wb.reshape(2, CMP_STRIDE * N_KV_HEADS * HEAD_DIM, N_KV_HEADS * 2 * hid).astype(BF16)


def _score_matrix(nch, n_slc):
    i = np.arange(nch)[:, None]
    j = np.arange(n_slc)[None, :]
    m = 2.0 * ((i // RATIO == j) & (i % RATIO < RATIO - 1)) + 1.0 * (i == RATIO * j + RATIO - 1) \
        + 1.0 * (i == RATIO * j - 1)
    return jnp.asarray(m, dtype=BF16)


def _nsa_kernel(q_ref, kc_ref, vc_ref, ks_ref, vs_ref, kw_ref, vw_ref, gate_ref, p0_ref, p1_ref,
                pcd_ref, mm_ref, o_ref, m_ref, l_ref, acc_ref, *, nch, n_slc):
    j = pl.program_id(2)
    q0 = pl.multiple_of(j * Q_BLOCK, Q_BLOCK)
    cq = lax.shift_right_logical(j, 2)
    qm = q_ref[0].reshape(ROWS_Q, HEAD_DIM)
    lane_q = jnp.bitwise_and(lax.broadcasted_iota(jnp.int32, (1, ROWS_Q), 1), Q_BLOCK - 1)
    tq = q0 + lane_q

    lc = _dot_nt(kc_ref[0, 0], qm)
    npatch = pcd_ref.shape[1]
    place = jnp.where(
        lax.broadcasted_iota(jnp.int32, (nch, npatch), 0)
        == (j * (Q_BLOCK // CMP_STRIDE) - npatch // 2 + lax.broadcasted_iota(jnp.int32, (nch, npatch), 1)),
        1.0, 0.0).astype(BF16)
    ph, plo = _split2(pcd_ref[0])
    lc = lc + _dot(place, ph) + _dot(place, plo)
    rowi = lax.broadcasted_iota(jnp.int32, (nch, 1), 0)
    maskc = (rowi * CMP_STRIDE + (CMP_LEN - 1)) <= tq
    lcm = jnp.where(maskc, lc, NEG_INF)
    p = jnp.where(maskc, jnp.exp(lcm - jnp.max(lcm, axis=0, keepdims=True)), 0.0)
    anyc = tq >= CMP_LEN - 1
    s = jnp.where(anyc, jnp.sum(p, axis=0, keepdims=True), 1.0)
    pc = p * (1.0 / s)
    o_c = _dot_tn(vc_ref[0, 0], pc.astype(BF16))

    imp = pc[:, 0:Q_BLOCK]
    for g in range(1, GROUP):
        imp = imp + pc[:, g * Q_BLOCK:(g + 1) * Q_BLOCK]
    mm = mm_ref[...]
    i1, i2, i3 = _split3(imp)
    score = _dot(mm, i1) + _dot(mm, i2) + _dot(mm, i3)
    blk = lax.broadcasted_iota(jnp.int32, (n_slc, 1), 0)
    cur = lax.shift_right_logical(q0 + lax.broadcasted_iota(jnp.int32, (1, Q_BLOCK), 1), 6)
    forced = (blk == 0) | (blk == cur) | (blk == cur - 1)
    sc = jnp.where(forced, FORCE_SCORE, jnp.where(blk <= cur, score, NEG_INF))
    big = jnp.int32(1 << 30)
    sel = jnp.zeros(sc.shape, F32)
    for _ in range(min(N_SEL, n_slc)):
        mx = jnp.max(sc, axis=0, keepdims=True)
        first = jnp.min(jnp.where(sc == mx, blk, big), axis=0, keepdims=True)
        pick = blk == first
        sel = jnp.where(pick, 1.0, sel)
        sc = jnp.where(pick, -3e38, sc)
    selb = sel.astype(BF16)

    def neg_unselected(k0):
        kb = lax.shift_right_logical(k0 + lax.broadcasted_iota(jnp.int32, (KEY_CHUNK, n_slc), 0), 6)
        e = jnp.where(kb == lax.broadcasted_iota(jnp.int32, (KEY_CHUNK, n_slc), 1), 1.0, 0.0).astype(BF16)
        ng = (_dot(e, selb) - 1.0) * (-NEG_INF)
        return jnp.concatenate([ng] * GROUP, axis=1)

    def flash(st, v, first):
        mx = jnp.max(st, axis=0, keepdims=True)
        if first:
            pe = jnp.exp(st - mx)
            l_ref[...] = jnp.sum(pe, axis=0, keepdims=True)
            acc_ref[...] = _dot_tn(v, pe.astype(BF16))
            m_ref[...] = mx
        else:
            m_old = m_ref[...]
            m_new = jnp.maximum(m_old, mx)
            a = jnp.exp(m_old - m_new)
            pe = jnp.exp(st - m_new)
            l_ref[...] = a * l_ref[...] + jnp.sum(pe, axis=0, keepdims=True)
            acc_ref[...] = a * acc_ref[...] + _dot_tn(v, pe.astype(BF16))
            m_ref[...] = m_new

    def near_chunk(kref, vref, c, window, first):
        k0 = pl.multiple_of(c * KEY_CHUNK, KEY_CHUNK)
        st = _dot_nt(kref[0, 0, pl.ds(k0, KEY_CHUNK), :], qm)
        tiles = []
        for i in range(TILES):
            dt = j - (c * TILES + i)
            tiles.append(jnp.where(dt == 0, p0_ref[0], jnp.where(dt == 1, p1_ref[0], 0.0)))
        st = st + jnp.concatenate(tiles, axis=0)
        dist = tq - (k0 + lax.broadcasted_iota(jnp.int32, (KEY_CHUNK, 1), 0))
        if window:
            ok = (dist >= 0) & (dist < WINDOW)
        else:
            ok = dist >= 0
            st = st + neg_unselected(k0)
        flash(jnp.where(ok, st, NEG_INF), vref[0, 0, pl.ds(k0, KEY_CHUNK), :], first)

    near_chunk(ks_ref, vs_ref, cq, False, True)

    @pl.when(cq >= 1)
    def _():
        near_chunk(ks_ref, vs_ref, cq - 1, False, False)

    def far_body(c, carry):
        k0 = pl.multiple_of(c * KEY_CHUNK, KEY_CHUNK)
        st = _dot_nt(ks_ref[0, 0, pl.ds(k0, KEY_CHUNK), :], qm) + neg_unselected(k0)
        flash(st, vs_ref[0, 0, pl.ds(k0, KEY_CHUNK), :], False)
        return carry

    lax.fori_loop(0, jnp.maximum(cq - 1, 0), far_body, 0)
    o_s = acc_ref[...] * (1.0 / l_ref[...])

    near_chunk(kw_ref, vw_ref, cq, True, True)

    @pl.when(cq >= 1)
    def _():
        near_chunk(kw_ref, vw_ref, cq - 1, True, False)

    o_w = acc_ref[...] * (1.0 / l_ref[...])

    g3 = gate_ref[0, 0, 0]
    o = g3[0:1, :] * o_c + g3[1:2, :] * o_s + g3[2:3, :] * o_w
    stacked = jnp.concatenate([o[:, g * Q_BLOCK:(g + 1) * Q_BLOCK] for g in range(GROUP)], axis=0)
    o_ref[0] = stacked.T


def _bias_tables(table):
    tbl = table.reshape(N_BUCKETS, N_KV_HEADS, GROUP)
    far = tbl[N_BUCKETS - 1]
    i = jnp.arange(Q_BLOCK)

    def cols(rel):
        b = tbl[_rel_bucket(rel)] - far
        b = jnp.where((rel >= 0)[:, :, None, None], b, 0.0)
        return b.transpose(2, 0, 3, 1).reshape(N_KV_HEADS, rel.shape[0], ROWS_Q)

    p0 = cols(i[None, :] - i[:, None])
    p1 = cols(Q_BLOCK + i[None, :] - i[:, None])
    npatch = 2 * Q_BLOCK // CMP_STRIDE
    pcd = cols(i[None, :] - CMP_STRIDE * (jnp.arange(npatch)[:, None] - npatch // 2) - (CMP_LEN - 1))
    return p0, p1, pcd


def _nsa_prompt(q_hm, kv_hm, cmpkv, gates_t, table):
    B, H, T, hd = q_hm.shape
    nch = cmpkv.shape[2]
    n_slc = T // SLC_BLOCK
    nq = T // Q_BLOCK
    assert T % KEY_CHUNK == 0 and nch == RATIO * n_slc and WINDOW <= KEY_CHUNK
    p0, p1, pcd = _bias_tables(table)
    mm = _score_matrix(nch, n_slc).T
    nk = N_KV_HEADS
    kv_spec = lambda slab0: pl.BlockSpec((1, 1, T, hd), lambda b, k, j: (b, slab0 + k, 0, 0))
    cmp_spec = lambda slab0: pl.BlockSpec((1, 1, nch, hd), lambda b, k, j: (b, slab0 + k, 0, 0))
    per_head = lambda a: pl.BlockSpec((1,) + a.shape[1:], lambda b, k, j: (k, 0, 0))
    return pl.pallas_call(
        functools.partial(_nsa_kernel, nch=nch, n_slc=n_slc),
        out_shape=jax.ShapeDtypeStruct((B, T, H * hd), F32),
        grid=(B, nk, nq),
        in_specs=[pl.BlockSpec((1, GROUP, Q_BLOCK, hd), lambda b, k, j: (b, k, j, 0)),
                  cmp_spec(0), cmp_spec(nk),
                  kv_spec(2 * nk), kv_spec(3 * nk), kv_spec(4 * nk), kv_spec(5 * nk),
                  pl.BlockSpec((1, 1, 1, 3, ROWS_Q), lambda b, k, j: (b, k, j, 0, 0)),
                  per_head(p0), per_head(p1), per_head(pcd),
                  pl.BlockSpec(mm.shape, lambda b, k, j: (0, 0))],
        out_specs=pl.BlockSpec((1, Q_BLOCK, GROUP * hd), lambda b, k, j: (b, j, k)),
        scratch_shapes=[pltpu.VMEM((1, ROWS_Q), F32), pltpu.VMEM((1, ROWS_Q), F32),
                        pltpu.VMEM((hd, ROWS_Q), F32)],
        compiler_params=_cparams(3),
        name="nsa_prompt",
    )(q_hm, cmpkv, cmpkv, kv_hm, kv_hm, kv_hm, kv_hm, gates_t, p0, p1, pcd, mm)


def _smp_cmp_kernel(q_ref, kc_ref, bias_ref, mm_ref, oc_ref, idx_ref, *, n_cmp, n_slc, cur):
    nch = kc_ref.shape[2]
    ncol = mm_ref.shape[1]
    coli = lax.broadcasted_iota(jnp.int32, (1, nch), 1)
    maskc = coli < n_cmp
    blk = lax.broadcasted_iota(jnp.int32, (1, ncol), 1)
    k_sel = min(N_SEL, n_slc)
    lane = lax.broadcasted_iota(jnp.int32, (1, LANE), 1)
    for kh in range(N_KV_HEADS):
        qm = q_ref[0, kh * GROUP:(kh + 1) * GROUP].astype(BF16)
        lc = _dot_nt(qm, kc_ref[0, kh]) + bias_ref[kh]
        lcm = jnp.where(maskc, lc, NEG_INF)
        p = jnp.where(maskc, jnp.exp(lcm - jnp.max(lcm, axis=-1, keepdims=True)), 0.0)
        pc = p * (1.0 / jnp.sum(p, axis=-1, keepdims=True))
        oc_ref[0, kh * GROUP:(kh + 1) * GROUP] = _dot(pc.astype(BF16), kc_ref[0, N_KV_HEADS + kh])
        imp = pc[0:1]
        for g in range(1, GROUP):
            imp = imp + pc[g:g + 1]
        mm = mm_ref[...]
        i1, i2, i3 = _split3(imp)
        score = _dot(i1, mm) + _dot(i2, mm) + _dot(i3, mm)
        forced = (blk == 0) | (blk == cur) | (blk == cur - 1)
        sc = jnp.where(forced, FORCE_SCORE, jnp.where(blk <= cur, score, NEG_INF))
        sc = jnp.where(blk < n_slc, sc, -2e38)
        big = jnp.int32(1 << 30)
        out = jnp.zeros((1, LANE), jnp.int32)
        for it in range(k_sel):
            m = jnp.max(sc, axis=-1, keepdims=True)
            first = jnp.min(jnp.where(sc == m, blk, big), axis=-1, keepdims=True)
            out = jnp.where(lane == it, first, out)
            sc = jnp.where(blk == first, -3e38, sc)
        idx_ref[0, kh:kh + 1, :] = out


def _smp_cmp(q_hm, cmpkv, bias_c, mm, n_cmp, n_slc, cur):
    B = q_hm.shape[0]
    nch = cmpkv.shape[2]
    return pl.pallas_call(
        functools.partial(_smp_cmp_kernel, n_cmp=n_cmp, n_slc=n_slc, cur=cur),
        out_shape=(jax.ShapeDtypeStruct((B, N_HEADS, HEAD_DIM), F32),
                   jax.ShapeDtypeStruct((B, N_KV_HEADS, LANE), jnp.int32)),
        grid=(B,),
        in_specs=[pl.BlockSpec((1, N_HEADS, HEAD_DIM), lambda b: (b, 0, 0)),
                  pl.BlockSpec((1, 2 * N_KV_HEADS, nch, HEAD_DIM), lambda b: (b, 0, 0, 0)),
                  pl.BlockSpec(bias_c.shape, lambda b: (0, 0, 0)),
                  pl.BlockSpec(mm.shape, lambda b: (0, 0))],
        out_specs=(pl.BlockSpec((1, N_HEADS, HEAD_DIM), lambda b: (b, 0, 0)),
                   pl.BlockSpec((1, N_KV_HEADS, LANE), lambda b: (b, 0, 0))),
        compiler_params=_cparams(1),
        name="sample_cmp",
    )(q_hm, cmpkv, bias_c, mm)


def _smp_att_kernel(phys_ref, q_ref, pool_ref, bsel_ref, msel_ref, new_ref, win_ref, bwin_ref, bnew_ref,
                    gate_ref, oc_ref, o_ref, buf_ref, sem_ref, *, k_sel):
    b = pl.program_id(0)
    n_blk = N_KV_HEADS * k_sel

    def blk_copy(i):
        return pltpu.make_async_copy(pool_ref.at[phys_ref[b * n_blk + i]], buf_ref.at[i], sem_ref.at[i])

    for i in range(n_blk):
        blk_copy(i).start()
    for i in range(n_blk):
        blk_copy(i).wait()

    nw = win_ref.shape[1]
    for kh in range(N_KV_HEADS):
        rows = slice(kh * GROUP, (kh + 1) * GROUP)
        qm = q_ref[0, rows].astype(BF16)
        kcol = slice(kh * HEAD_DIM, (kh + 1) * HEAD_DIM)
        vcol = slice(KV_WIDTH + kh * HEAD_DIM, KV_WIDTH + (kh + 1) * HEAD_DIM)
        bnew = bnew_ref[kh]

        def branch(kp, vp, bias, mask, knew, vnew, new_ok):
            lg = _dot_nt(qm, kp) + bias
            lg = jnp.where(mask, lg, NEG_INF)
            ln = jnp.sum(qm.astype(F32) * knew.astype(F32), axis=-1, keepdims=True) + bnew
            if new_ok is not None:
                ln = jnp.where(new_ok, ln, NEG_INF)
            mx = jnp.maximum(jnp.max(lg, axis=-1, keepdims=True), ln)
            pe = jnp.where(mask, jnp.exp(lg - mx), 0.0)
            pn = jnp.exp(ln - mx)
            if new_ok is not None:
                pn = jnp.where(new_ok, pn, 0.0)
            den = jnp.sum(pe, axis=-1, keepdims=True) + pn
            any_ok = den > 0.0
            inv = 1.0 / jnp.where(any_ok, den, 1.0)
            o = _dot(pe.astype(BF16), vp) + pn.astype(BF16).astype(F32) * vnew.astype(F32)
            return jnp.where(any_ok, o * inv, 0.0)

        nkeys = k_sel * SLC_BLOCK
        kp = buf_ref[kh * k_sel:(kh + 1) * k_sel, :, 0, kh, :].reshape(nkeys, HEAD_DIM).astype(BF16)
        vp = buf_ref[kh * k_sel:(kh + 1) * k_sel, :, 1, kh, :].reshape(nkeys, HEAD_DIM).astype(BF16)
        knew = new_ref[0, 0:1, kcol].astype(BF16)
        vnew = new_ref[0, 0:1, vcol].astype(BF16)
        msel = msel_ref[0, kh]
        o_s = branch(kp, vp, bsel_ref[0, kh], msel[:, 0:nkeys] > 0.5, knew, vnew, msel[:, nkeys:nkeys + 1] > 0.5)

        kw = win_ref[0, :, 0, kh, :].astype(BF16)
        vw = win_ref[0, :, 1, kh, :].astype(BF16)
        knw = new_ref[0, 1:2, kcol].astype(BF16)
        vnw = new_ref[0, 1:2, vcol].astype(BF16)
        dist = nw - lax.broadcasted_iota(jnp.int32, (1, nw), 1)
        o_w = branch(kw, vw, bwin_ref[kh], dist < WINDOW, knw, vnw, None)

        g3 = gate_ref[0, rows]
        o_ref[0, rows] = g3[:, 0:1] * oc_ref[0, rows] + g3[:, 1:2] * o_s + g3[:, 2:3] * o_w


def _smp_att(phys, q_hm, pool_blk, bias_sel, mask_sel, new_rows, win_buf, bias_win, bias_new, gates, o_c, k_sel):
    B = q_hm.shape[0]
    n_blk = N_KV_HEADS * k_sel
    full = lambda a: pl.BlockSpec(a.shape, lambda b, ph: (0,) * a.ndim)
    per_b = lambda a: pl.BlockSpec((1,) + a.shape[1:], lambda b, ph: (b,) + (0,) * (a.ndim - 1))
    return pl.pallas_call(
        functools.partial(_smp_att_kernel, k_sel=k_sel),
        out_shape=jax.ShapeDtypeStruct((B, N_HEADS, HEAD_DIM), F32),
        grid_spec=pltpu.PrefetchScalarGridSpec(
            num_scalar_prefetch=1,
            grid=(B,),
            in_specs=[per_b(q_hm), pl.BlockSpec(memory_space=pl.ANY), per_b(bias_sel), per_b(mask_sel),
                      per_b(new_rows), per_b(win_buf), full(bias_win), full(bias_new), per_b(gates), per_b(o_c)],
            out_specs=pl.BlockSpec((1, N_HEADS, HEAD_DIM), lambda b, ph: (b, 0, 0)),
            scratch_shapes=[pltpu.VMEM((n_blk,) + pool_blk.shape[1:], F32),
                            pltpu.SemaphoreType.DMA((n_blk,))]),
        compiler_params=_cparams(1),
        name="sample_att",
    )(phys, q_hm, pool_blk, bias_sel, mask_sel, new_rows, win_buf, bias_win, bias_new, gates, o_c)


def _merge_kernel(x_ref, cn_ref, att_ref, gna_ref, wo_ref, m2_ref, m3_ref, m4_ref, g2_ref, rw_ref, rb_ref,
                  x1_ref, h2_ref, ti_ref, tg_ref, *, n_exp, cw_width):
    att_n = _rms(att_ref[0], gna_ref[...]).astype(BF16)
    mix = _dot(cn_ref[0], wo_ref[0:cw_width, :]) + _dot(att_n, wo_ref[cw_width:, :])
    x1 = x_ref[0] + m2_ref[0] * mix
    x1_ref[0] = x1
    h2 = _rms(x1, g2_ref[...]) * (1.0 + m4_ref[0]) + m3_ref[0]
    h2_ref[0] = h2.astype(BF16)
    logits = _dot3(h2, rw_ref[...]) + rb_ref[...]
    lane = lax.broadcasted_iota(jnp.int32, (1, LANE), 1)
    sc = jnp.where(lane < n_exp, logits, -2e38)
    big = jnp.int32(1 << 30)
    ti = jnp.zeros(sc.shape, jnp.int32)
    tv = jnp.zeros(sc.shape, F32)
    v0 = None
    den = None
    for k in range(TOP_K):
        m = jnp.max(sc, axis=-1, keepdims=True)
        first = jnp.min(jnp.where(sc == m, lane, big), axis=-1, keepdims=True)
        if k == 0:
            v0 = m
        e = jnp.exp(m - v0)
        den = e if den is None else den + e
        ti = jnp.where(lane == k, first, ti)
        tv = jnp.where(lane == k, e, tv)
        sc = jnp.where(lane == first, -3e38, sc)
    ti_ref[0] = ti
    tg_ref[0] = tv * (1.0 / den)


def _merge(x, convn, att, gn_att, wo_b, m2, m3, m4, ln2_g, rw_pad, rb_pad, *, seq_mode, tm, n_exp):
    B, T, D = x.shape
    cw_width = convn.shape[2]
    aw = att.shape[2]
    row_blk = lambda w: pl.BlockSpec((1, tm, w), lambda b, t: (b, t, 0))
    mod_spec = pl.BlockSpec((1, 1, D), lambda b, t: (b, 0, 0)) if seq_mode else row_blk(D)
    const = lambda shape: pl.BlockSpec(shape, lambda b, t: (0,) * len(shape))
    return pl.pallas_call(
        functools.partial(_merge_kernel, n_exp=n_exp, cw_width=cw_width),
        out_shape=(jax.ShapeDtypeStruct((B, T, D), F32), jax.ShapeDtypeStruct((B, T, D), BF16),
                   jax.ShapeDtypeStruct((B, T, LANE), jnp.int32), jax.ShapeDtypeStruct((B, T, LANE), F32)),
        grid=(B, T // tm),
        in_specs=[row_blk(D), row_blk(cw_width), row_blk(aw), const((1, aw)), const(wo_b.shape),
                  mod_spec, mod_spec, mod_spec, const((1, D)), const(rw_pad.shape), const((1, LANE))],
        out_specs=(row_blk(D), row_blk(D), row_blk(LANE), row_blk(LANE)),
        compiler_params=_cparams(2),
        name="merge_seq" if seq_mode else "merge_rows",
    )(x, convn, att, gn_att.reshape(1, aw), wo_b, m2, m3, m4, ln2_g.reshape(1, D), rw_pad, rb_pad)


def _moe_kernel(be_ref, nu_ref, xs_ref, wgu_ref, bgu_ref, wd_ref, bd_ref, y_ref, wgu_b, wd_b, *, d_ff):
    i = pl.program_id(0)
    prev = be_ref[jnp.maximum(i - 1, 0)]
    changed = jnp.logical_or(i == 0, be_ref[i] != prev)

    @pl.when(jnp.logical_and(changed, i < nu_ref[0]))
    def _():
        wgu_b[...] = wgu_ref[0].astype(BF16)
        wd_b[...] = wd_ref[0].astype(BF16)

    @pl.when(i < nu_ref[0])
    def _():
        gu = _dot(xs_ref[...], wgu_b[...]) + bgu_ref[0]
        gate_h = jnp.minimum(gu[:, 0:d_ff], SWIGLU_LIMIT)
        up_h = jnp.clip(gu[:, d_ff:], -SWIGLU_LIMIT, SWIGLU_LIMIT)
        act = (up_h + 1.0) * gate_h * _sigmoid(SWIGLU_ALPHA * gate_h)
        y_ref[...] = _dot(act.astype(BF16), wd_b[...]) + bd_ref[0]

    @pl.when(i >= nu_ref[0])
    def _():
        y_ref[...] = jnp.zeros(y_ref.shape, F32)


def _moe_experts(block_e, n_used, xs, w_gu, b_gu, w_down, b_down):
    n_rows, D = xs.shape
    E, _, two_ff = w_gu.shape
    d_ff = two_ff // 2
    n_blocks = n_rows // MOE_ROWS
    return pl.pallas_call(
        functools.partial(_moe_kernel, d_ff=d_ff),
        out_shape=jax.ShapeDtypeStruct((n_rows, D), F32),
        grid_spec=pltpu.PrefetchScalarGridSpec(
            num_scalar_prefetch=2,
            grid=(n_blocks,),
            in_specs=[pl.BlockSpec((MOE_ROWS, D), lambda i, be, nu: (i, 0)),
                      pl.BlockSpec((1, D, two_ff), lambda i, be, nu: (be[i], 0, 0)),
                      pl.BlockSpec((1, 1, two_ff), lambda i, be, nu: (be[i], 0, 0)),
                      pl.BlockSpec((1, d_ff, D), lambda i, be, nu: (be[i], 0, 0)),
                      pl.BlockSpec((1, 1, D), lambda i, be, nu: (be[i], 0, 0))],
            out_specs=pl.BlockSpec((MOE_ROWS, D), lambda i, be, nu: (i, 0)),
            scratch_shapes=[pltpu.VMEM((D, two_ff), BF16), pltpu.VMEM((d_ff, D), BF16)]),
        compiler_params=_cparams(1),
        name="moe_experts",
    )(block_e, n_used, xs, w_gu, b_gu.reshape(E, 1, two_ff), w_down, b_down.reshape(E, 1, D))


def _route(top_idx, n_tok, n_exp):
    n_assign = n_tok * TOP_K
    flat_e = top_idx.reshape(-1)
    experts = jnp.arange(n_exp, dtype=jnp.int32)
    onehot = (flat_e[:, None] == experts[None, :]).astype(jnp.int32)
    csum = jnp.cumsum(onehot, axis=0)
    counts = csum[-1]
    padded = (counts + MOE_ROWS - 1) // MOE_ROWS * MOE_ROWS
    pad_end = jnp.cumsum(padded)
    pad_start = pad_end - padded
    dest = jnp.sum(onehot * (csum - 1 + pad_start[None, :]), axis=1).astype(jnp.int32)
    n_blocks = -(-(n_assign + n_exp * (MOE_ROWS - 1)) // MOE_ROWS)
    n_rows = n_blocks * MOE_ROWS
    blk_row0 = jnp.arange(n_blocks, dtype=jnp.int32) * MOE_ROWS
    block_e = jnp.minimum(jnp.sum((pad_end[None, :] <= blk_row0[:, None]).astype(jnp.int32), axis=1), n_exp - 1)
    n_fill = n_rows - n_assign
    fill_e = jnp.arange(n_fill, dtype=jnp.int32) // (MOE_ROWS - 1)
    fill_i = jnp.arange(n_fill, dtype=jnp.int32) % (MOE_ROWS - 1)
    fill_on = (fill_e < n_exp) & (fill_i < jnp.sum(
        (fill_e[:, None] == experts[None, :]) * (padded - counts)[None, :], axis=1))
    keys = jnp.concatenate([2 * flat_e, jnp.where(fill_on, 2 * fill_e + 1, 2 * n_exp + 1)])
    toks = jnp.concatenate([jnp.arange(n_assign, dtype=jnp.int32) // TOP_K, jnp.full((n_fill,), n_tok, jnp.int32)])
    _, row_tok = lax.sort((keys, toks), num_keys=1, is_stable=True)
    n_used = (pad_end[-1] // MOE_ROWS).astype(jnp.int32).reshape(1)
    return row_tok, dest.reshape(n_tok, TOP_K), block_e.astype(jnp.int32), n_used


def _final_kernel(x1_ref, yg_ref, tg_ref, m5_ref, fg_ref, o_ref):
    tg = tg_ref[0]
    ff = tg[:, 0:1] * yg_ref[0, 0]
    for k in range(1, TOP_K):
        ff = ff + tg[:, k:k + 1] * yg_ref[0, k]
    o_ref[0] = _rms(x1_ref[0] + m5_ref[0] * ff, fg_ref[...])


def _final(x1, yg, tg, m5, final_g, *, seq_mode, tm):
    B, T, D = x1.shape
    row_blk = lambda w: pl.BlockSpec((1, tm, w), lambda b, t: (b, t, 0))
    mod_spec = pl.BlockSpec((1, 1, D), lambda b, t: (b, 0, 0)) if seq_mode else row_blk(D)
    return pl.pallas_call(
        _final_kernel,
        out_shape=jax.ShapeDtypeStruct((B, T, D), F32),
        grid=(B, T // tm),
        in_specs=[row_blk(D), pl.BlockSpec((1, TOP_K, tm, D), lambda b, t: (b, 0, t, 0)), row_blk(LANE),
                  mod_spec, pl.BlockSpec((1, D), lambda b, t: (0, 0))],
        out_specs=row_blk(D),
        compiler_params=_cparams(2),
        name="final_seq" if seq_mode else "final_rows",
    )(x1, yg, tg, m5, final_g.reshape(1, D))


def _row_tile(t):
    for tm in (512, 256, 128, 64, 32, 16, 8):
        if t % tm == 0:
            return tm
    raise ValueError(f"unsupported row count {t}")


def _rows(a, idx):
    return a.at[idx].get(mode="promise_in_bounds")


def kernel(x_prompt, x_sample, cache_kv_cmp, cache_kv_slc, state_kv_win, state_conv, page_table, c_prompt, c_sample,
           rel_bias_table, ln1_g, ln2_g, w_ada, b_ada, w_in, conv_w, cmp_pe, cmp_w1, cmp_w2, gn_conv, gn_att, w_o,
           router_w, router_b, w_gu, b_gu, w_down, b_down, final_g):
    B, T, D = x_prompt.shape
    BS, TS, _ = x_sample.shape
    depth = w_in.shape[0]
    assert depth == 1 and TS == 1
    n_pool, page = cache_kv_cmp.shape[1], cache_kv_cmp.shape[2]
    n_pages = page_table.shape[1]
    past_len = n_pages * page
    n_exp = router_w.shape[2]
    cw_width = conv_w.shape[2]
    kvw2 = 2 * KV_WIDTH
    assert past_len % SLC_BLOCK == 0 and past_len % CMP_STRIDE == 0 and T % page == 0

    in_cols = w_in.shape[2]
    gate0 = 3 * cw_width + N_HEADS * HEAD_DIM + 3 * kvw2
    w_pad = jnp.pad(w_in[0], ((0, 0), (0, gate0 + LANE - in_cols))).astype(BF16)
    wo_b = w_o[0].astype(BF16)
    rw_pad = jnp.pad(router_w[0], ((0, 0), (0, LANE - n_exp)))
    rb_pad = jnp.pad(router_b[0], (0, LANE - n_exp)).reshape(1, LANE)
    w_blk = _cmp_block_weight(cmp_w1[0])

    n_c = B + BS
    n_cp = -(-n_c // 8) * 8
    c_all = jnp.pad(jnp.concatenate([c_prompt, c_sample], axis=0), ((0, n_cp - n_c), (0, 0)))
    mod = _modulation(c_all, w_ada[0], b_ada[0]).reshape(n_cp, 6, D)
    mp = [mod[:B, i].reshape(B, 1, D) for i in range(6)]
    ms = [mod[B:n_c, i].reshape(1, BS, D) for i in range(6)]

    tm = _row_tile(T)
    nq = T // Q_BLOCK
    zeros_prev = jnp.zeros((B, CONV_K - 1, cw_width), F32)
    convn_p, q_p, kvc_p, kvs_p, kvw_p, kvh_p, gate_p, vlast_p = _inproj(
        x_prompt, mp[0], mp[1], ln1_g[0], w_pad, conv_w[0], gn_conv[0], zeros_prev, zeros_prev, seq_mode=True, tm=tm)
    pt_p = jnp.arange(B * (T // page), dtype=jnp.int32).reshape(B, T // page)
    g_p = math.gcd(T // page, 32)
    ab_p = _cmp_ab(kvc_p.reshape(B * (T // page), page, kvw2), pt_p, w_blk, g_p)
    cmpkv_p = _cmp_finish(ab_p, cmp_pe[0], cmp_w1[0], cmp_w2[0])
    gates_p = gate_p[:, :, :3 * N_HEADS].reshape(B, nq, Q_BLOCK, N_KV_HEADS, GROUP, 3)
    gates_p = gates_p.transpose(0, 3, 1, 5, 4, 2).reshape(B, N_KV_HEADS, nq, 3, ROWS_Q)
    att_p = _nsa_prompt(q_p, kvh_p, cmpkv_p, gates_p, rel_bias_table)
    x1_p, h2_p, ti_p, tg_p = _merge(x_prompt, convn_p, att_p, gn_att[0], wo_b, mp[2], mp[3], mp[4], ln2_g[0],
                                    rw_pad, rb_pad, seq_mode=True, tm=tm, n_exp=n_exp)

    xs_rows = x_sample.reshape(1, BS, D)
    prev2 = state_conv[0][:, 0, :].reshape(1, BS, cw_width)
    prev1 = state_conv[0][:, 1, :].reshape(1, BS, cw_width)
    tms = _row_tile(BS)
    convn_s, q_s, kvc_s, kvs_s, kvw_s, _, gate_s, v_s = _inproj(
        xs_rows, ms[0], ms[1], ln1_g[0], w_pad, conv_w[0], gn_conv[0], prev2, prev1, seq_mode=False, tm=tms)
    n_cmp = (past_len + TS - CMP_LEN) // CMP_STRIDE + 1
    nch_s = past_len // CMP_STRIDE
    assert n_cmp + 1 == nch_s
    ab_s = _cmp_ab(cache_kv_cmp[0].reshape(n_pool, page, kvw2), page_table, w_blk, math.gcd(n_pages, 32))
    cmpkv_s = _cmp_finish(ab_s, cmp_pe[0], cmp_w1[0], cmp_w2[0])
    t_q = past_len
    n_slc = -(-(past_len + TS) // SLC_BLOCK)
    cur = t_q // SLC_BLOCK
    k_sel = min(N_SEL, n_slc)
    tbl = rel_bias_table.reshape(N_BUCKETS, N_KV_HEADS, GROUP)
    pos_c = jnp.arange(nch_s) * CMP_STRIDE + CMP_LEN - 1
    bias_c = tbl[_rel_bucket(t_q - pos_c)].transpose(1, 2, 0)
    ncol = -(-n_slc // LANE) * LANE
    mm_s = _score_matrix(nch_s, ncol)
    q_s_hm = q_s.reshape(N_HEADS, BS, HEAD_DIM).transpose(1, 0, 2).astype(F32)
    o_c_s, idx_pad = _smp_cmp(q_s_hm, cmpkv_s, bias_c, mm_s, n_cmp, n_slc, cur)
    idx = idx_pad[:, :, :k_sel]
    blk_per_page = page // SLC_BLOCK
    pg = jnp.minimum(idx // blk_per_page, n_pages - 1)
    phys = (jnp.take_along_axis(page_table, pg.reshape(BS, -1), axis=1).reshape(idx.shape) * blk_per_page
            + idx % blk_per_page).astype(jnp.int32)
    pos_s = idx[..., None] * SLC_BLOCK + jnp.arange(SLC_BLOCK)
    pos_s = pos_s.reshape(BS, N_KV_HEADS, k_sel * SLC_BLOCK)
    bucket_hot = (_rel_bucket(t_q - pos_s)[..., None] == jnp.arange(N_BUCKETS)).astype(F32)
    bias_sel = jnp.einsum('bksn,nkg->bkgs', bucket_hot, tbl, precision=lax.Precision.HIGHEST)
    new_sel = jnp.any(idx == cur, axis=-1, keepdims=True)
    mask_sel = jnp.concatenate([pos_s < past_len, new_sel], axis=-1).astype(F32)[:, :, None, :]
    nw = state_kv_win.shape[2]
    bias_win = tbl[_rel_bucket(nw - jnp.arange(nw))].transpose(1, 2, 0)
    bias_new = tbl[0].reshape(N_KV_HEADS, GROUP, 1)
    new_rows = jnp.stack([kvs_s[0], kvw_s[0]], axis=1)
    gates_s = gate_s[0, :, :3 * N_HEADS].reshape(BS, N_HEADS, 3)
    pool_blk = cache_kv_slc[0].reshape(n_pool * blk_per_page, SLC_BLOCK, 2, N_KV_HEADS, HEAD_DIM)
    att_s = _smp_att(phys.reshape(-1), q_s_hm, pool_blk, bias_sel, mask_sel, new_rows,
                     state_kv_win[0], bias_win, bias_new, gates_s, o_c_s, k_sel)
    att_s = att_s.reshape(1, BS, N_HEADS * HEAD_DIM)
    x1_s, h2_s, ti_s, tg_s = _merge(xs_rows, convn_s, att_s, gn_att[0], wo_b, ms[2], ms[3], ms[4], ln2_g[0],
                                    rw_pad, rb_pad, seq_mode=False, tm=tms, n_exp=n_exp)

    n_tok = B * T + BS
    h2_all = jnp.concatenate([h2_p.reshape(B * T, D), h2_s.reshape(BS, D), jnp.zeros((8, D), BF16)], axis=0)
    top_idx = jnp.concatenate([ti_p.reshape(B * T, LANE), ti_s.reshape(BS, LANE)], axis=0)[:, :TOP_K]
    row_tok, dest, block_e, n_used = _route(top_idx, n_tok, n_exp)
    xs = _rows(h2_all, row_tok)
    yb = _moe_experts(block_e, n_used, xs, w_gu[0], b_gu[0], w_down[0], b_down[0])
    dest_p = dest[:B * T].reshape(B, T, TOP_K).transpose(0, 2, 1)
    yg_p = _rows(yb, dest_p.reshape(-1)).reshape(B, TOP_K, T, D)
    yg_s = _rows(yb, dest[B * T:].T.reshape(-1)).reshape(1, TOP_K, BS, D)
    y_p = _final(x1_p, yg_p, tg_p, mp[5], final_g, seq_mode=True, tm=tm)
    y_s = _final(x1_s, yg_s, tg_s, ms[5], final_g, seq_mode=False, tm=tms)

    kv_tail = (2, N_KV_HEADS, HEAD_DIM)
    page_shape = (depth, B, T // page, page) + kv_tail
    w_keep = min(WINDOW, T)
    new_win_s = jnp.concatenate([state_kv_win[0][:, TS:], kvw_s.reshape(BS, TS, *kv_tail)], axis=1)
    new_conv_s = jnp.concatenate([state_conv[0][:, TS:], v_s.reshape(BS, TS, cw_width)], axis=1)
    return (y_p, y_s.reshape(BS, TS, D),
            kvc_p.reshape(page_shape), kvc_s.reshape((depth, BS, TS) + kv_tail),
            kvs_p.reshape(page_shape), kvs_s.reshape((depth, BS, TS) + kv_tail),
            kvw_p[:, T - w_keep:].reshape((depth, B, w_keep) + kv_tail), new_win_s[None],
            vlast_p[None], new_conv_s[None])
```

```python
import functools
import math

import numpy as np
import jax
import jax.numpy as jnp
from jax import lax
from jax.experimental import pallas as pl
from jax.experimental.pallas import tpu as pltpu

F32 = jnp.float32
BF16 = jnp.bfloat16

CONV_K = 3
N_HEADS = 8
N_KV_HEADS = 2
GROUP = N_HEADS // N_KV_HEADS
HEAD_DIM = 64
KV_WIDTH = N_KV_HEADS * HEAD_DIM
CMP_LEN = 32
CMP_STRIDE = 16
SLC_BLOCK = 64
RATIO = SLC_BLOCK // CMP_STRIDE
N_SEL = 16
WINDOW = 512
Q_BLOCK = 128
N_BUCKETS = 32
REL_MAX_DIST = 128
TOP_K = 4
SWIGLU_LIMIT = 7.0
SWIGLU_ALPHA = 1.702
EPS = 1e-6
NEG_INF = -1e30
FORCE_SCORE = 1e4

LANE = 128
ROWS_Q = GROUP * Q_BLOCK
KEY_CHUNK = 512
TILES = KEY_CHUNK // Q_BLOCK
MOE_ROWS = 256
VMEM_LIMIT = 48 * 1024 * 1024


def _cparams(n_axes):
    return pltpu.CompilerParams(dimension_semantics=("arbitrary",) * n_axes, vmem_limit_bytes=VMEM_LIMIT)


def _dot(a, b):
    return jnp.dot(a, b, preferred_element_type=F32)


def _dot_nt(a, b):
    return lax.dot_general(a, b, (((1,), (1,)), ((), ())), preferred_element_type=F32)


def _dot_tn(a, b):
    return lax.dot_general(a, b, (((0,), (0,)), ((), ())), preferred_element_type=F32)


def _split2(x):
    hi = x.astype(BF16)
    lo = (x - hi.astype(F32)).astype(BF16)
    return hi, lo


def _split3(x):
    a = x.astype(BF16)
    r = x - a.astype(F32)
    b = r.astype(BF16)
    c = (r - b.astype(F32)).astype(BF16)
    return a, b, c


def _dot3(a, b):
    ah, al = _split2(a)
    bh, bl = _split2(b)
    return _dot(ah, bh) + _dot(ah, bl) + _dot(al, bh)


def _sigmoid(x):
    return 1.0 / (1.0 + jnp.exp(-x))


def _rms(x, g):
    return x * lax.rsqrt(jnp.mean(x * x, axis=-1, keepdims=True) + EPS) * g


def _rel_bucket(dist):
    n = jnp.maximum(dist, 0)
    max_exact = N_BUCKETS // 2
    large = max_exact + (jnp.log(jnp.maximum(n, 1).astype(F32) / max_exact)
                         / math.log(REL_MAX_DIST / max_exact) * (N_BUCKETS - max_exact)).astype(jnp.int32)
    return jnp.where(n < max_exact, n, jnp.minimum(large, N_BUCKETS - 1))


def _mod_kernel(c_ref, w_ref, b_ref, o_ref):
    c = c_ref[...]
    o_ref[...] = _dot3(c * _sigmoid(c), w_ref[...]) + b_ref[...]


def _modulation(c, w_ada, b_ada):
    n, d = c.shape
    cols = w_ada.shape[1]
    bn = 1536
    return pl.pallas_call(
        _mod_kernel,
        out_shape=jax.ShapeDtypeStruct((n, cols), F32),
        grid=(cols // bn,),
        in_specs=[pl.BlockSpec((n, d), lambda i: (0, 0)),
                  pl.BlockSpec((d, bn), lambda i: (0, i)),
                  pl.BlockSpec((1, bn), lambda i: (0, i))],
        out_specs=pl.BlockSpec((n, bn), lambda i: (0, i)),
        compiler_params=_cparams(1),
        name="modulation",
    )(c, w_ada, b_ada.reshape(1, cols))


def _inproj_kernel(x_ref, m0_ref, m1_ref, g1_ref, w_ref, cw_ref, gnc_ref, pa_ref, pb_ref,
                   convn_ref, q_ref, kvc_ref, kvs_ref, kvw_ref, kvh_ref, gate_ref, vlast_ref,
                   carry_ref, *, seq_mode, tm, cw_width):
    x = x_ref[0]
    h = _rms(x, g1_ref[...]) * (1.0 + m1_ref[0]) + m0_ref[0]
    hb = h.astype(BF16)
    c3 = 3 * cw_width
    uc = _dot(hb, w_ref[:, 0:c3])
    b_g = uc[:, 0:cw_width]
    v = uc[:, cw_width:2 * cw_width] * uc[:, 2 * cw_width:c3]
    if seq_mode:
        @pl.when(pl.program_id(1) == 0)
        def _():
            carry_ref[0:2, :] = pa_ref[0]
        c0 = carry_ref[0:1, :]
        c1 = carry_ref[1:2, :]
        row = lax.broadcasted_iota(jnp.int32, (tm, 1), 0)
        vm1 = jnp.where(row == 0, c1, pltpu.roll(v, 1, 0))
        vm2 = jnp.where(row == 0, c0, jnp.where(row == 1, c1, pltpu.roll(v, 2, 0)))
        carry_ref[0:2, :] = v[tm - 2:tm, :]
        vlast_ref[0] = v[tm - 2:tm, :]
    else:
        vm2 = pa_ref[0]
        vm1 = pb_ref[0]
        vlast_ref[0] = v
    cw = cw_ref[...]
    y = cw[0:1, :] * vm2 + cw[1:2, :] * vm1 + cw[2:3, :] * v
    convn_ref[0] = _rms(b_g * y, gnc_ref[...]).astype(BF16)

    aw = N_HEADS * HEAD_DIM
    uq = _dot(hb, w_ref[:, c3:c3 + aw]) * (HEAD_DIM ** -0.5)
    for hh in range(N_HEADS):
        q_ref[0, hh] = uq[:, hh * HEAD_DIM:(hh + 1) * HEAD_DIM].astype(BF16)
    kv0 = c3 + aw
    kvw3 = 3 * 2 * KV_WIDTH
    ukv = _dot(hb, w_ref[:, kv0:kv0 + kvw3])
    kvc_ref[0] = ukv[:, 0:2 * KV_WIDTH]
    kvs_ref[0] = ukv[:, 2 * KV_WIDTH:4 * KV_WIDTH]
    kvw_ref[0] = ukv[:, 4 * KV_WIDTH:6 * KV_WIDTH]
    for s in range(kvw3 // HEAD_DIM):
        kvh_ref[0, s] = ukv[:, s * HEAD_DIM:(s + 1) * HEAD_DIM].astype(BF16)
    ug = _dot(hb, w_ref[:, kv0 + kvw3:kv0 + kvw3 + LANE])
    gate_ref[0] = _sigmoid(ug)


def _inproj(x, m0, m1, ln_g, w_pad, conv_w, gn_conv, pa, pb, *, seq_mode, tm):
    B, T, D = x.shape
    cw_width = conv_w.shape[1]
    n_slab = 3 * 2 * KV_WIDTH // HEAD_DIM
    nt = T // tm
    row_blk = lambda w: pl.BlockSpec((1, tm, w), lambda b, t: (b, t, 0))
    if seq_mode:
        mod_spec = pl.BlockSpec((1, 1, D), lambda b, t: (b, 0, 0))
        prev_spec = pl.BlockSpec((1, 2, cw_width), lambda b, t: (b, 0, 0))
        vlast_shape = jax.ShapeDtypeStruct((B, 2, cw_width), F32)
        vlast_spec = pl.BlockSpec((1, 2, cw_width), lambda b, t: (b, 0, 0))
    else:
        mod_spec = row_blk(D)
        prev_spec = row_blk(cw_width)
        vlast_shape = jax.ShapeDtypeStruct((B, T, cw_width), F32)
        vlast_spec = row_blk(cw_width)
    const = lambda shape: pl.BlockSpec(shape, lambda b, t: (0,) * len(shape))
    kern = functools.partial(_inproj_kernel, seq_mode=seq_mode, tm=tm, cw_width=cw_width)
    return pl.pallas_call(
        kern,
        out_shape=(jax.ShapeDtypeStruct((B, T, cw_width), BF16),
                   jax.ShapeDtypeStruct((B, N_HEADS, T, HEAD_DIM), BF16),
                   jax.ShapeDtypeStruct((B, T, 2 * KV_WIDTH), F32),
                   jax.ShapeDtypeStruct((B, T, 2 * KV_WIDTH), F32),
                   jax.ShapeDtypeStruct((B, T, 2 * KV_WIDTH), F32),
                   jax.ShapeDtypeStruct((B, n_slab, T, HEAD_DIM), BF16),
                   jax.ShapeDtypeStruct((B, T, LANE), F32),
                   vlast_shape),
        grid=(B, nt),
        in_specs=[row_blk(D), mod_spec, mod_spec, const((1, D)), const(w_pad.shape), const(conv_w.shape),
                  const((1, cw_width)), prev_spec, prev_spec],
        out_specs=(row_blk(cw_width),
                   pl.BlockSpec((1, N_HEADS, tm, HEAD_DIM), lambda b, t: (b, 0, t, 0)),
                   row_blk(2 * KV_WIDTH), row_blk(2 * KV_WIDTH), row_blk(2 * KV_WIDTH),
                   pl.BlockSpec((1, n_slab, tm, HEAD_DIM), lambda b, t: (b, 0, t, 0)),
                   row_blk(LANE), vlast_spec),
        scratch_shapes=[pltpu.VMEM((8, cw_width), F32)],
        compiler_params=_cparams(2),
        name="inproj_seq" if seq_mode else "inproj_rows",
    )(x, m0, m1, ln_g.reshape(1, D), w_pad, conv_w, gn_conv.reshape(1, cw_width), pa, pb)


def _cmp_ab_kernel(pt_ref, *refs, G):
    pages = refs[:2 * G]
    w_ref, out_ref, x_ref = refs[2 * G], refs[2 * G + 1], refs[2 * G + 2]
    half = KV_WIDTH
    for j in range(G):
        for r in range(CMP_STRIDE):
            for c in range(2):
                x_ref[c, j * 8:(j + 1) * 8, r * half:(r + 1) * half] = (
                    pages[2 * j + c][0, pl.ds(r, 8, stride=CMP_STRIDE), :])
    wcols = w_ref.shape[2]
    for c in range(2):
        out_ref[0, :, c * wcols:(c + 1) * wcols] = _dot(x_ref[c].astype(BF16), w_ref[c])


def _cmp_ab(pool, page_table, w_blk, G):
    P, page, width = pool.shape
    B, n_pages = page_table.shape
    cpp = page // CMP_STRIDE
    assert cpp == 8 and n_pages % G == 0
    kdim = CMP_STRIDE * width // 2

    def pg_spec(j, c):
        return pl.BlockSpec((1, page, width // 2), lambda b, g, pt: (pt[b * n_pages + g * G + j], 0, c))

    return pl.pallas_call(
        functools.partial(_cmp_ab_kernel, G=G),
        out_shape=jax.ShapeDtypeStruct((B, n_pages * cpp, 2 * w_blk.shape[2]), F32),
        grid_spec=pltpu.PrefetchScalarGridSpec(
            num_scalar_prefetch=1,
            grid=(B, n_pages // G),
            in_specs=[pg_spec(j, c) for j in range(G) for c in range(2)]
            + [pl.BlockSpec(w_blk.shape, lambda b, g, pt: (0, 0, 0))],
            out_specs=pl.BlockSpec((1, G * cpp, 2 * w_blk.shape[2]), lambda b, g, pt: (b, g, 0)),
            scratch_shapes=[pltpu.VMEM((2, G * cpp, kdim), F32)]),
        compiler_params=_cparams(2),
        name="cmp_partial",
    )(page_table.reshape(-1).astype(jnp.int32), *([pool] * (2 * G)), w_blk)


def _cmp_ab_t_kernel(pt_ref, *refs, G):
    pages = refs[:G]
    w_ref, out_ref, x_ref, s_ref = refs[G], refs[G + 1], refs[G + 2], refs[G + 3]
    half = KV_WIDTH
    for j in range(G):
        for c in range(2):
            for k in range(N_KV_HEADS):
                s_ref[c, j, :, k * HEAD_DIM:(k + 1) * HEAD_DIM] = pages[j][0, c, k].T
    for j in range(G):
        for r in range(CMP_STRIDE):
            for c in range(2):
                x_ref[c, j * 8:(j + 1) * 8, r * half:(r + 1) * half] = (
                    s_ref[c, j, pl.ds(r, 8, stride=CMP_STRIDE), :])
    wcols = w_ref.shape[2]
    for c in range(2):
        out_ref[0, :, c * wcols:(c + 1) * wcols] = _dot(x_ref[c].astype(BF16), w_ref[c])


def _cmp_ab_t(pool_t, page_table, w_blk, G):
    P, _, _, hd, page = pool_t.shape
    B, n_pages = page_table.shape
    cpp = page // CMP_STRIDE
    assert cpp == 8 and n_pages % G == 0
    kdim = CMP_STRIDE * KV_WIDTH

    def pg_spec(j):
        return pl.BlockSpec((1,) + pool_t.shape[1:], lambda b, g, pt: (pt[b * n_pages + g * G + j], 0, 0, 0, 0))

    return pl.pallas_call(
        functools.partial(_cmp_ab_t_kernel, G=G),
        out_shape=jax.ShapeDtypeStruct((B, n_pages * cpp, 2 * w_blk.shape[2]), F32),
        grid_spec=pltpu.PrefetchScalarGridSpec(
            num_scalar_prefetch=1,
            grid=(B, n_pages // G),
            in_specs=[pg_spec(j) for j in range(G)] + [pl.BlockSpec(w_blk.shape, lambda b, g, pt: (0, 0, 0))],
            out_specs=pl.BlockSpec((1, G * cpp, 2 * w_blk.shape[2]), lambda b, g, pt: (b, g, 0)),
            scratch_shapes=[pltpu.VMEM((2, G * cpp, kdim), F32), pltpu.VMEM((2, G, page, KV_WIDTH), F32)]),
        compiler_params=_cparams(2),
        name="cmp_partial_t",
    )(page_table.reshape(-1).astype(jnp.int32), *([pool_t] * G), w_blk)


def _gelu_tanh(x):
    return 0.5 * x * (1.0 + jnp.tanh(math.sqrt(2.0 / math.pi) * (x + 0.044715 * (x * x * x))))


def _cmp_fin_kernel(ab_ref, pe_ref, w1_ref, w2_ref, out_ref, *, nch):
    for c in range(2):
        pe_t = _dot(pe_ref[c:c + 1, :].astype(BF16), w1_ref[c].astype(BF16))
        w2 = w2_ref[c].astype(BF16)
        for k in range(N_KV_HEADS):
            base = (c * N_KV_HEADS + k) * 2 * HEAD_DIM
            slab = ab_ref[0, :, base:base + 2 * HEAD_DIM]
            nxt = pltpu.roll(slab, nch - 1, 0)
            pre = slab[:, 0:HEAD_DIM] + nxt[:, HEAD_DIM:2 * HEAD_DIM] + pe_t
            out_ref[0, c * N_KV_HEADS + k] = _dot(_gelu_tanh(pre).astype(BF16), w2).astype(BF16)


def _cmp_finish(ab, cmp_pe, cmp_w1, cmp_w2):
    B, nch, w = ab.shape
    pe = cmp_pe.reshape(2, CMP_LEN * HEAD_DIM)
    return pl.pallas_call(
        functools.partial(_cmp_fin_kernel, nch=nch),
        out_shape=jax.ShapeDtypeStruct((B, 2 * N_KV_HEADS, nch, HEAD_DIM), BF16),
        grid=(B,),
        in_specs=[pl.BlockSpec((1, nch, w), lambda b: (b, 0, 0)),
                  pl.BlockSpec(pe.shape, lambda b: (0, 0)),
                  pl.BlockSpec(cmp_w1.shape, lambda b: (0, 0, 0)),
                  pl.BlockSpec(cmp_w2.shape, lambda b: (0, 0, 0))],
        out_specs=pl.BlockSpec((1, 2 * N_KV_HEADS, nch, HEAD_DIM), lambda b: (b, 0, 0, 0)),
        compiler_params=_cparams(1),
        name="cmp_finish",
    )(ab, pe, cmp_w1, cmp_w2)


def _cmp_block_weight(cmp_w1):
    hid = cmp_w1.shape[2]
    w = cmp_w1.reshape(2, 2, CMP_STRIDE, HEAD_DIM, hid)
    eye = jnp.eye(N_KV_HEADS, dtype=cmp_w1.dtype)
    wb = jnp.einsum('cardh,kj->crkdjah', w, eye)
    return wb.reshape(2, CMP_STRIDE * N_KV_HEADS * HEAD_DIM, N_KV_HEADS * 2 * hid).astype(BF16)


def _score_matrix(nch, n_slc):
    i = np.arange(nch)[:, None]
    j = np.arange(n_slc)[None, :]
    m = 2.0 * ((i // RATIO == j) & (i % RATIO < RATIO - 1)) + 1.0 * (i == RATIO * j + RATIO - 1) \
        + 1.0 * (i == RATIO * j - 1)
    return jnp.asarray(m, dtype=BF16)


def _nsa_kernel(q_ref, kc_ref, vc_ref, ks_ref, vs_ref, kw_ref, vw_ref, gate_ref, p0_ref, p1_ref,
                pcd_ref, mm_ref, o_ref, m_ref, l_ref, acc_ref, *, nch, n_slc):
    j = pl.program_id(2)
    q0 = pl.multiple_of(j * Q_BLOCK, Q_BLOCK)
    cq = lax.shift_right_logical(j, 2)
    qm = q_ref[0].reshape(ROWS_Q, HEAD_DIM)
    lane_q = jnp.bitwise_and(lax.broadcasted_iota(jnp.int32, (1, ROWS_Q), 1), Q_BLOCK - 1)
    tq = q0 + lane_q

    lc = _dot_nt(kc_ref[0, 0], qm)
    npatch = pcd_ref.shape[1]
    place = jnp.where(
        lax.broadcasted_iota(jnp.int32, (nch, npatch), 0)
        == (j * (Q_BLOCK // CMP_STRIDE) - npatch // 2 + lax.broadcasted_iota(jnp.int32, (nch, npatch), 1)),
        1.0, 0.0).astype(BF16)
    ph, plo = _split2(pcd_ref[0])
    lc = lc + _dot(place, ph) + _dot(place, plo)
    rowi = lax.broadcasted_iota(jnp.int32, (nch, 1), 0)
    maskc = (rowi * CMP_STRIDE + (CMP_LEN - 1)) <= tq
    lcm = jnp.where(maskc, lc, NEG_INF)
    p = jnp.where(maskc, jnp.exp(lcm - jnp.max(lcm, axis=0, keepdims=True)), 0.0)
    anyc = tq >= CMP_LEN - 1
    s = jnp.where(anyc, jnp.sum(p, axis=0, keepdims=True), 1.0)
    pc = p * (1.0 / s)
    o_c = _dot_tn(vc_ref[0, 0], pc.astype(BF16))

    imp = pc[:, 0:Q_BLOCK]
    for g in range(1, GROUP):
        imp = imp + pc[:, g * Q_BLOCK:(g + 1) * Q_BLOCK]
    mm = mm_ref[...]
    i1, i2, i3 = _split3(imp)
    score = _dot(mm, i1) + _dot(mm, i2) + _dot(mm, i3)
    blk = lax.broadcasted_iota(jnp.int32, (n_slc, 1), 0)
    cur = lax.shift_right_logical(q0 + lax.broadcasted_iota(jnp.int32, (1, Q_BLOCK), 1), 6)
    forced = (blk == 0) | (blk == cur) | (blk == cur - 1)
    sc = jnp.where(forced, FORCE_SCORE, jnp.where(blk <= cur, score, NEG_INF))
    big = jnp.int32(1 << 30)
    sel = jnp.zeros(sc.shape, F32)
    for _ in range(min(N_SEL, n_slc)):
        mx = jnp.max(sc, axis=0, keepdims=True)
        first = jnp.min(jnp.where(sc == mx, blk, big), axis=0, keepdims=True)
        pick = blk == first
        sel = jnp.where(pick, 1.0, sel)
        sc = jnp.where(pick, -3e38, sc)
    selb = sel.astype(BF16)

    def neg_unselected(k0):
        kb = lax.shift_right_logical(k0 + lax.broadcasted_iota(jnp.int32, (KEY_CHUNK, n_slc), 0), 6)
        e = jnp.where(kb == lax.broadcasted_iota(jnp.int32, (KEY_CHUNK, n_slc), 1), 1.0, 0.0).astype(BF16)
        ng = (_dot(e, selb) - 1.0) * (-NEG_INF)
        return jnp.concatenate([ng] * GROUP, axis=1)

    def flash(st, v, first):
        mx = jnp.max(st, axis=0, keepdims=True)
        if first:
            pe = jnp.exp(st - mx)
            l_ref[...] = jnp.sum(pe, axis=0, keepdims=True)
            acc_ref[...] = _dot_tn(v, pe.astype(BF16))
            m_ref[...] = mx
        else:
            m_old = m_ref[...]
            m_new = jnp.maximum(m_old, mx)
            a = jnp.exp(m_old - m_new)
            pe = jnp.exp(st - m_new)
            l_ref[...] = a * l_ref[...] + jnp.sum(pe, axis=0, keepdims=True)
            acc_ref[...] = a * acc_ref[...] + _dot_tn(v, pe.astype(BF16))
            m_ref[...] = m_new

    def near_chunk(kref, vref, c, window, first):
        k0 = pl.multiple_of(c * KEY_CHUNK, KEY_CHUNK)
        st = _dot_nt(kref[0, 0, pl.ds(k0, KEY_CHUNK), :], qm)
        tiles = []
        for i in range(TILES):
            dt = j - (c * TILES + i)
            tiles.append(jnp.where(dt == 0, p0_ref[0], jnp.where(dt == 1, p1_ref[0], 0.0)))
        st = st + jnp.concatenate(tiles, axis=0)
        dist = tq - (k0 + lax.broadcasted_iota(jnp.int32, (KEY_CHUNK, 1), 0))
        if window:
            ok = (dist >= 0) & (dist < WINDOW)
        else:
            ok = dist >= 0
            st = st + neg_unselected(k0)
        flash(jnp.where(ok, st, NEG_INF), vref[0, 0, pl.ds(k0, KEY_CHUNK), :], first)

    near_chunk(ks_ref, vs_ref, cq, False, True)

    @pl.when(cq >= 1)
    def _():
        near_chunk(ks_ref, vs_ref, cq - 1, False, False)

    def far_body(c, carry):
        k0 = pl.multiple_of(c * KEY_CHUNK, KEY_CHUNK)
        st = _dot_nt(ks_ref[0, 0, pl.ds(k0, KEY_CHUNK), :], qm) + neg_unselected(k0)
        flash(st, vs_ref[0, 0, pl.ds(k0, KEY_CHUNK), :], False)
        return carry

    lax.fori_loop(0, jnp.maximum(cq - 1, 0), far_body, 0)
    o_s = acc_ref[...] * (1.0 / l_ref[...])

    near_chunk(kw_ref, vw_ref, cq, True, True)

    @pl.when(cq >= 1)
    def _():
        near_chunk(kw_ref, vw_ref, cq - 1, True, False)

    o_w = acc_ref[...] * (1.0 / l_ref[...])

    g3 = gate_ref[0, 0, 0]
    o = g3[0:1, :] * o_c + g3[1:2, :] * o_s + g3[2:3, :] * o_w
    stacked = jnp.concatenate([o[:, g * Q_BLOCK:(g + 1) * Q_BLOCK] for g in range(GROUP)], axis=0)
    o_ref[0] = stacked.T


def _bias_tables(table):
    tbl = table.reshape(N_BUCKETS, N_KV_HEADS, GROUP)
    far = tbl[N_BUCKETS - 1]
    i = jnp.arange(Q_BLOCK)

    def cols(rel):
        b = tbl[_rel_bucket(rel)] - far
        b = jnp.where((rel >= 0)[:, :, None, None], b, 0.0)
        return b.transpose(2, 0, 3, 1).reshape(N_KV_HEADS, rel.shape[0], ROWS_Q)

    p0 = cols(i[None, :] - i[:, None])
    p1 = cols(Q_BLOCK + i[None, :] - i[:, None])
    npatch = 2 * Q_BLOCK // CMP_STRIDE
    pcd = cols(i[None, :] - CMP_STRIDE * (jnp.arange(npatch)[:, None] - npatch // 2) - (CMP_LEN - 1))
    return p0, p1, pcd


def _nsa_prompt(q_hm, kv_hm, cmpkv, gates_t, table):
    B, H, T, hd = q_hm.shape
    nch = cmpkv.shape[2]
    n_slc = T // SLC_BLOCK
    nq = T // Q_BLOCK
    assert T % KEY_CHUNK == 0 and nch == RATIO * n_slc and WINDOW <= KEY_CHUNK
    p0, p1, pcd = _bias_tables(table)
    mm = _score_matrix(nch, n_slc).T
    nk = N_KV_HEADS
    kv_spec = lambda slab0: pl.BlockSpec((1, 1, T, hd), lambda b, k, j: (b, slab0 + k, 0, 0))
    cmp_spec = lambda slab0: pl.BlockSpec((1, 1, nch, hd), lambda b, k, j: (b, slab0 + k, 0, 0))
    per_head = lambda a: pl.BlockSpec((1,) + a.shape[1:], lambda b, k, j: (k, 0, 0))
    return pl.pallas_call(
        functools.partial(_nsa_kernel, nch=nch, n_slc=n_slc),
        out_shape=jax.ShapeDtypeStruct((B, T, H * hd), F32),
        grid=(B, nk, nq),
        in_specs=[pl.BlockSpec((1, GROUP, Q_BLOCK, hd), lambda b, k, j: (b, k, j, 0)),
                  cmp_spec(0), cmp_spec(nk),
                  kv_spec(2 * nk), kv_spec(3 * nk), kv_spec(4 * nk), kv_spec(5 * nk),
                  pl.BlockSpec((1, 1, 1, 3, ROWS_Q), lambda b, k, j: (b, k, j, 0, 0)),
                  per_head(p0), per_head(p1), per_head(pcd),
                  pl.BlockSpec(mm.shape, lambda b, k, j: (0, 0))],
        out_specs=pl.BlockSpec((1, Q_BLOCK, GROUP * hd), lambda b, k, j: (b, j, k)),
        scratch_shapes=[pltpu.VMEM((1, ROWS_Q), F32), pltpu.VMEM((1, ROWS_Q), F32),
                        pltpu.VMEM((hd, ROWS_Q), F32)],
        compiler_params=_cparams(3),
        name="nsa_prompt",
    )(q_hm, cmpkv, cmpkv, kv_hm, kv_hm, kv_hm, kv_hm, gates_t, p0, p1, pcd, mm)


def _smp_cmp_kernel(q_ref, kc_ref, bias_ref, mm_ref, oc_ref, idx_ref, *, n_cmp, n_slc, cur):
    nch = kc_ref.shape[2]
    ncol = mm_ref.shape[1]
    coli = lax.broadcasted_iota(jnp.int32, (1, nch), 1)
    maskc = coli < n_cmp
    blk = lax.broadcasted_iota(jnp.int32, (1, ncol), 1)
    k_sel = min(N_SEL, n_slc)
    lane = lax.broadcasted_iota(jnp.int32, (1, LANE), 1)
    for kh in range(N_KV_HEADS):
        qm = q_ref[0, kh * GROUP:(kh + 1) * GROUP].astype(BF16)
        lc = _dot_nt(qm, kc_ref[0, kh]) + bias_ref[kh]
        lcm = jnp.where(maskc, lc, NEG_INF)
        p = jnp.where(maskc, jnp.exp(lcm - jnp.max(lcm, axis=-1, keepdims=True)), 0.0)
        pc = p * (1.0 / jnp.sum(p, axis=-1, keepdims=True))
        oc_ref[0, kh * GROUP:(kh + 1) * GROUP] = _dot(pc.astype(BF16), kc_ref[0, N_KV_HEADS + kh])
        imp = pc[0:1]
        for g in range(1, GROUP):
            imp = imp + pc[g:g + 1]
        mm = mm_ref[...]
        i1, i2, i3 = _split3(imp)
        score = _dot(i1, mm) + _dot(i2, mm) + _dot(i3, mm)
        forced = (blk == 0) | (blk == cur) | (blk == cur - 1)
        sc = jnp.where(forced, FORCE_SCORE, jnp.where(blk <= cur, score, NEG_INF))
        sc = jnp.where(blk < n_slc, sc, -2e38)
        big = jnp.int32(1 << 30)
        out = jnp.zeros((1, LANE), jnp.int32)
        for it in range(k_sel):
            m = jnp.max(sc, axis=-1, keepdims=True)
            first = jnp.min(jnp.where(sc == m, blk, big), axis=-1, keepdims=True)
            out = jnp.where(lane == it, first, out)
            sc = jnp.where(blk == first, -3e38, sc)
        idx_ref[0, kh:kh + 1, :] = out


def _smp_cmp(q_hm, cmpkv, bias_c, mm, n_cmp, n_slc, cur):
    B = q_hm.shape[0]
    nch = cmpkv.shape[2]
    return pl.pallas_call(
        functools.partial(_smp_cmp_kernel, n_cmp=n_cmp, n_slc=n_slc, cur=cur),
        out_shape=(jax.ShapeDtypeStruct((B, N_HEADS, HEAD_DIM), F32),
                   jax.ShapeDtypeStruct((B, N_KV_HEADS, LANE), jnp.int32)),
        grid=(B,),
        in_specs=[pl.BlockSpec((1, N_HEADS, HEAD_DIM), lambda b: (b, 0, 0)),
                  pl.BlockSpec((1, 2 * N_KV_HEADS, nch, HEAD_DIM), lambda b: (b, 0, 0, 0)),
                  pl.BlockSpec(bias_c.shape, lambda b: (0, 0, 0)),
                  pl.BlockSpec(mm.shape, lambda b: (0, 0))],
        out_specs=(pl.BlockSpec((1, N_HEADS, HEAD_DIM), lambda b: (b, 0, 0)),
                   pl.BlockSpec((1, N_KV_HEADS, LANE), lambda b: (b, 0, 0))),
        compiler_params=_cparams(1),
        name="sample_cmp",
    )(q_hm, cmpkv, bias_c, mm)


def _smp_att_kernel(phys_ref, q_ref, pool_ref, bsel_ref, msel_ref, new_ref, win_ref, bwin_ref, bnew_ref,
                    gate_ref, oc_ref, o_ref, buf_ref, sem_ref, *, k_sel):
    b = pl.program_id(0)
    n_blk = N_KV_HEADS * k_sel

    def blk_copy(i):
        return pltpu.make_async_copy(pool_ref.at[phys_ref[b * n_blk + i]], buf_ref.at[i], sem_ref.at[i])

    for i in range(n_blk):
        blk_copy(i).start()
    for i in range(n_blk):
        blk_copy(i).wait()

    nw = win_ref.shape[4]
    for kh in range(N_KV_HEADS):
        rows = slice(kh * GROUP, (kh + 1) * GROUP)
        qm = q_ref[0, rows].astype(BF16)
        kcol = slice(kh * HEAD_DIM, (kh + 1) * HEAD_DIM)
        vcol = slice(KV_WIDTH + kh * HEAD_DIM, KV_WIDTH + (kh + 1) * HEAD_DIM)
        bnew = bnew_ref[kh]

        def branch(kt, vt, bias, mask, knew, vnew, new_ok):
            lg = _dot(qm, kt) + bias
            lg = jnp.where(mask, lg, NEG_INF)
            ln = jnp.sum(qm.astype(F32) * knew.astype(F32), axis=-1, keepdims=True) + bnew
            if new_ok is not None:
                ln = jnp.where(new_ok, ln, NEG_INF)
            mx = jnp.maximum(jnp.max(lg, axis=-1, keepdims=True), ln)
            pe = jnp.where(mask, jnp.exp(lg - mx), 0.0)
            pn = jnp.exp(ln - mx)
            if new_ok is not None:
                pn = jnp.where(new_ok, pn, 0.0)
            den = jnp.sum(pe, axis=-1, keepdims=True) + pn
            any_ok = den > 0.0
            inv = 1.0 / jnp.where(any_ok, den, 1.0)
            o = _dot_nt(pe.astype(BF16), vt) + pn.astype(BF16).astype(F32) * vnew.astype(F32)
            return jnp.where(any_ok, o * inv, 0.0)

        page = buf_ref.shape[4]
        nkeys = k_sel * page
        kt = jnp.concatenate([buf_ref[kh * k_sel + i, 0, kh] for i in range(k_sel)], axis=1).astype(BF16)
        vt = jnp.concatenate([buf_ref[kh * k_sel + i, 1, kh] for i in range(k_sel)], axis=1).astype(BF16)
        knew = new_ref[0, 0:1, kcol].astype(BF16)
        vnew = new_ref[0, 0:1, vcol].astype(BF16)
        msel = msel_ref[0, kh]
        o_s = branch(kt, vt, bsel_ref[0, kh], msel[:, 0:nkeys] > 0.5, knew, vnew, msel[:, nkeys:nkeys + 1] > 0.5)

        kw = win_ref[0, 0, kh].astype(BF16)
        vw = win_ref[0, 1, kh].astype(BF16)
        knw = new_ref[0, 1:2, kcol].astype(BF16)
        vnw = new_ref[0, 1:2, vcol].astype(BF16)
        dist = nw - lax.broadcasted_iota(jnp.int32, (1, nw), 1)
        o_w = branch(kw, vw, bwin_ref[kh], dist < WINDOW, knw, vnw, None)

        g3 = gate_ref[0, rows]
        o_ref[0, rows] = g3[:, 0:1] * oc_ref[0, rows] + g3[:, 1:2] * o_s + g3[:, 2:3] * o_w


def _smp_att(phys, q_hm, pool_blk, bias_sel, mask_sel, new_rows, win_buf, bias_win, bias_new, gates, o_c, k_sel):
    B = q_hm.shape[0]
    n_blk = N_KV_HEADS * k_sel
    full = lambda a: pl.BlockSpec(a.shape, lambda b, ph: (0,) * a.ndim)
    per_b = lambda a: pl.BlockSpec((1,) + a.shape[1:], lambda b, ph: (b,) + (0,) * (a.ndim - 1))
    return pl.pallas_call(
        functools.partial(_smp_att_kernel, k_sel=k_sel),
        out_shape=jax.ShapeDtypeStruct((B, N_HEADS, HEAD_DIM), F32),
        grid_spec=pltpu.PrefetchScalarGridSpec(
            num_scalar_prefetch=1,
            grid=(B,),
            in_specs=[per_b(q_hm), pl.BlockSpec(memory_space=pl.ANY), per_b(bias_sel), per_b(mask_sel),
                      per_b(new_rows), per_b(win_buf), full(bias_win), full(bias_new), per_b(gates), per_b(o_c)],
            out_specs=pl.BlockSpec((1, N_HEADS, HEAD_DIM), lambda b, ph: (b, 0, 0)),
            scratch_shapes=[pltpu.VMEM((n_blk,) + pool_blk.shape[1:], F32),
                            pltpu.SemaphoreType.DMA((n_blk,))]),
        compiler_params=_cparams(1),
        name="sample_att",
    )(phys, q_hm, pool_blk, bias_sel, mask_sel, new_rows, win_buf, bias_win, bias_new, gates, o_c)


def _merge_kernel(x_ref, cn_ref, att_ref, gna_ref, wo_ref, m2_ref, m3_ref, m4_ref, g2_ref, rw_ref, rb_ref,
                  x1_ref, h2_ref, ti_ref, tg_ref, *, n_exp, cw_width):
    att_n = _rms(att_ref[0], gna_ref[...]).astype(BF16)
    mix = _dot(cn_ref[0], wo_ref[0:cw_width, :]) + _dot(att_n, wo_ref[cw_width:, :])
    x1 = x_ref[0] + m2_ref[0] * mix
    x1_ref[0] = x1
    h2 = _rms(x1, g2_ref[...]) * (1.0 + m4_ref[0]) + m3_ref[0]
    h2_ref[0] = h2.astype(BF16)
    logits = _dot3(h2, rw_ref[...]) + rb_ref[...]
    lane = lax.broadcasted_iota(jnp.int32, (1, LANE), 1)
    sc = jnp.where(lane < n_exp, logits, -2e38)
    big = jnp.int32(1 << 30)
    ti = jnp.zeros(sc.shape, jnp.int32)
    tv = jnp.zeros(sc.shape, F32)
    v0 = None
    den = None
    for k in range(TOP_K):
        m = jnp.max(sc, axis=-1, keepdims=True)
        first = jnp.min(jnp.where(sc == m, lane, big), axis=-1, keepdims=True)
        if k == 0:
            v0 = m
        e = jnp.exp(m - v0)
        den = e if den is None else den + e
        ti = jnp.where(lane == k, first, ti)
        tv = jnp.where(lane == k, e, tv)
        sc = jnp.where(lane == first, -3e38, sc)
    ti_ref[0] = ti
    tg_ref[0] = tv * (1.0 / den)


def _merge(x, convn, att, gn_att, wo_b, m2, m3, m4, ln2_g, rw_pad, rb_pad, *, seq_mode, tm, n_exp):
    B, T, D = x.shape
    cw_width = convn.shape[2]
    aw = att.shape[2]
    row_blk = lambda w: pl.BlockSpec((1, tm, w), lambda b, t: (b, t, 0))
    mod_spec = pl.BlockSpec((1, 1, D), lambda b, t: (b, 0, 0)) if seq_mode else row_blk(D)
    const = lambda shape: pl.BlockSpec(shape, lambda b, t: (0,) * len(shape))
    return pl.pallas_call(
        functools.partial(_merge_kernel, n_exp=n_exp, cw_width=cw_width),
        out_shape=(jax.ShapeDtypeStruct((B, T, D), F32), jax.ShapeDtypeStruct((B, T, D), BF16),
                   jax.ShapeDtypeStruct((B, T, LANE), jnp.int32), jax.ShapeDtypeStruct((B, T, LANE), F32)),
        grid=(B, T // tm),
        in_specs=[row_blk(D), row_blk(cw_width), row_blk(aw), const((1, aw)), const(wo_b.shape),
                  mod_spec, mod_spec, mod_spec, const((1, D)), const(rw_pad.shape), const((1, LANE))],
        out_specs=(row_blk(D), row_blk(D), row_blk(LANE), row_blk(LANE)),
        compiler_params=_cparams(2),
        name="merge_seq" if seq_mode else "merge_rows",
    )(x, convn, att, gn_att.reshape(1, aw), wo_b, m2, m3, m4, ln2_g.reshape(1, D), rw_pad, rb_pad)


def _moe_kernel(be_ref, nu_ref, xs_ref, wgu_ref, bgu_ref, wd_ref, bd_ref, y_ref, wgu_b, wd_b, *, d_ff):
    i = pl.program_id(0)
    prev = be_ref[jnp.maximum(i - 1, 0)]
    changed = jnp.logical_or(i == 0, be_ref[i] != prev)

    @pl.when(jnp.logical_and(changed, i < nu_ref[0]))
    def _():
        wgu_b[...] = wgu_ref[0].astype(BF16)
        wd_b[...] = wd_ref[0].astype(BF16)

    @pl.when(i < nu_ref[0])
    def _():
        gu = _dot(xs_ref[...], wgu_b[...]) + bgu_ref[0]
        gate_h = jnp.minimum(gu[:, 0:d_ff], SWIGLU_LIMIT)
        up_h = jnp.clip(gu[:, d_ff:], -SWIGLU_LIMIT, SWIGLU_LIMIT)
        act = (up_h + 1.0) * gate_h * _sigmoid(SWIGLU_ALPHA * gate_h)
        y_ref[...] = _dot(act.astype(BF16), wd_b[...]) + bd_ref[0]

    @pl.when(i >= nu_ref[0])
    def _():
        y_ref[...] = jnp.zeros(y_ref.shape, F32)


def _moe_experts(block_e, n_used, xs, w_gu, b_gu, w_down, b_down):
    n_rows, D = xs.shape
    E, _, two_ff = w_gu.shape
    d_ff = two_ff // 2
    n_blocks = n_rows // MOE_ROWS
    return pl.pallas_call(
        functools.partial(_moe_kernel, d_ff=d_ff),
        out_shape=jax.ShapeDtypeStruct((n_rows, D), F32),
        grid_spec=pltpu.PrefetchScalarGridSpec(
            num_scalar_prefetch=2,
            grid=(n_blocks,),
            in_specs=[pl.BlockSpec((MOE_ROWS, D), lambda i, be, nu: (i, 0)),
                      pl.BlockSpec((1, D, two_ff), lambda i, be, nu: (be[i], 0, 0)),
                      pl.BlockSpec((1, 1, two_ff), lambda i, be, nu: (be[i], 0, 0)),
                      pl.BlockSpec((1, d_ff, D), lambda i, be, nu: (be[i], 0, 0)),
                      pl.BlockSpec((1, 1, D), lambda i, be, nu: (be[i], 0, 0))],
            out_specs=pl.BlockSpec((MOE_ROWS, D), lambda i, be, nu: (i, 0)),
            scratch_shapes=[pltpu.VMEM((D, two_ff), BF16), pltpu.VMEM((d_ff, D), BF16)]),
        compiler_params=_cparams(1),
        name="moe_experts",
    )(block_e, n_used, xs, w_gu, b_gu.reshape(E, 1, two_ff), w_down, b_down.reshape(E, 1, D))


def _route(top_idx, n_tok, n_exp):
    n_assign = n_tok * TOP_K
    flat_e = top_idx.reshape(-1)
    experts = jnp.arange(n_exp, dtype=jnp.int32)
    onehot = (flat_e[:, None] == experts[None, :]).astype(jnp.int32)
    csum = jnp.cumsum(onehot, axis=0)
    counts = csum[-1]
    padded = (counts + MOE_ROWS - 1) // MOE_ROWS * MOE_ROWS
    pad_end = jnp.cumsum(padded)
    pad_start = pad_end - padded
    dest = jnp.sum(onehot * (csum - 1 + pad_start[None, :]), axis=1).astype(jnp.int32)
    n_blocks = -(-(n_assign + n_exp * (MOE_ROWS - 1)) // MOE_ROWS)
    n_rows = n_blocks * MOE_ROWS
    blk_row0 = jnp.arange(n_blocks, dtype=jnp.int32) * MOE_ROWS
    block_e = jnp.minimum(jnp.sum((pad_end[None, :] <= blk_row0[:, None]).astype(jnp.int32), axis=1), n_exp - 1)
    n_fill = n_rows - n_assign
    fill_e = jnp.arange(n_fill, dtype=jnp.int32) // (MOE_ROWS - 1)
    fill_i = jnp.arange(n_fill, dtype=jnp.int32) % (MOE_ROWS - 1)
    fill_on = (fill_e < n_exp) & (fill_i < jnp.sum(
        (fill_e[:, None] == experts[None, :]) * (padded - counts)[None, :], axis=1))
    keys = jnp.concatenate([2 * flat_e, jnp.where(fill_on, 2 * fill_e + 1, 2 * n_exp + 1)])
    toks = jnp.concatenate([jnp.arange(n_assign, dtype=jnp.int32) // TOP_K, jnp.full((n_fill,), n_tok, jnp.int32)])
    _, row_tok = lax.sort((keys, toks), num_keys=1, is_stable=True)
    n_used = (pad_end[-1] // MOE_ROWS).astype(jnp.int32).reshape(1)
    return row_tok, dest.reshape(n_tok, TOP_K), block_e.astype(jnp.int32), n_used


def _final_kernel(x1_ref, yg_ref, tg_ref, m5_ref, fg_ref, o_ref):
    tg = tg_ref[0]
    ff = tg[:, 0:1] * yg_ref[0, 0]
    for k in range(1, TOP_K):
        ff = ff + tg[:, k:k + 1] * yg_ref[0, k]
    o_ref[0] = _rms(x1_ref[0] + m5_ref[0] * ff, fg_ref[...])


def _final(x1, yg, tg, m5, final_g, *, seq_mode, tm):
    B, T, D = x1.shape
    row_blk = lambda w: pl.BlockSpec((1, tm, w), lambda b, t: (b, t, 0))
    mod_spec = pl.BlockSpec((1, 1, D), lambda b, t: (b, 0, 0)) if seq_mode else row_blk(D)
    return pl.pallas_call(
        _final_kernel,
        out_shape=jax.ShapeDtypeStruct((B, T, D), F32),
        grid=(B, T // tm),
        in_specs=[row_blk(D), pl.BlockSpec((1, TOP_K, tm, D), lambda b, t: (b, 0, t, 0)), row_blk(LANE),
                  mod_spec, pl.BlockSpec((1, D), lambda b, t: (0, 0))],
        out_specs=row_blk(D),
        compiler_params=_cparams(2),
        name="final_seq" if seq_mode else "final_rows",
    )(x1, yg, tg, m5, final_g.reshape(1, D))


def _row_tile(t):
    for tm in (512, 256, 128, 64, 32, 16, 8):
        if t % tm == 0:
            return tm
    raise ValueError(f"unsupported row count {t}")


def _rows(a, idx):
    return a.at[idx].get(mode="promise_in_bounds")


def kernel(x_prompt, x_sample, cache_kv_cmp, cache_kv_slc, state_kv_win, state_conv, page_table, c_prompt, c_sample,
           rel_bias_table, ln1_g, ln2_g, w_ada, b_ada, w_in, conv_w, cmp_pe, cmp_w1, cmp_w2, gn_conv, gn_att, w_o,
           router_w, router_b, w_gu, b_gu, w_down, b_down, final_g):
    B, T, D = x_prompt.shape
    BS, TS, _ = x_sample.shape
    depth = w_in.shape[0]
    assert depth == 1 and TS == 1
    n_pool, page = cache_kv_cmp.shape[1], cache_kv_cmp.shape[2]
    n_pages = page_table.shape[1]
    past_len = n_pages * page
    n_exp = router_w.shape[2]
    cw_width = conv_w.shape[2]
    kvw2 = 2 * KV_WIDTH
    assert past_len % SLC_BLOCK == 0 and past_len % CMP_STRIDE == 0 and T % page == 0

    in_cols = w_in.shape[2]
    gate0 = 3 * cw_width + N_HEADS * HEAD_DIM + 3 * kvw2
    w_pad = jnp.pad(w_in[0], ((0, 0), (0, gate0 + LANE - in_cols))).astype(BF16)
    wo_b = w_o[0].astype(BF16)
    rw_pad = jnp.pad(router_w[0], ((0, 0), (0, LANE - n_exp)))
    rb_pad = jnp.pad(router_b[0], (0, LANE - n_exp)).reshape(1, LANE)
    w_blk = _cmp_block_weight(cmp_w1[0])

    n_c = B + BS
    n_cp = -(-n_c // 8) * 8
    c_all = jnp.pad(jnp.concatenate([c_prompt, c_sample], axis=0), ((0, n_cp - n_c), (0, 0)))
    mod = _modulation(c_all, w_ada[0], b_ada[0]).reshape(n_cp, 6, D)
    mp = [mod[:B, i].reshape(B, 1, D) for i in range(6)]
    ms = [mod[B:n_c, i].reshape(1, BS, D) for i in range(6)]

    tm = _row_tile(T)
    nq = T // Q_BLOCK
    zeros_prev = jnp.zeros((B, CONV_K - 1, cw_width), F32)
    convn_p, q_p, kvc_p, kvs_p, kvw_p, kvh_p, gate_p, vlast_p = _inproj(
        x_prompt, mp[0], mp[1], ln1_g[0], w_pad, conv_w[0], gn_conv[0], zeros_prev, zeros_prev, seq_mode=True, tm=tm)
    pt_p = jnp.arange(B * (T // page), dtype=jnp.int32).reshape(B, T // page)
    g_p = math.gcd(T // page, 32)
    ab_p = _cmp_ab(kvc_p.reshape(B * (T // page), page, kvw2), pt_p, w_blk, g_p)
    cmpkv_p = _cmp_finish(ab_p, cmp_pe[0], cmp_w1[0], cmp_w2[0])
    gates_p = gate_p[:, :, :3 * N_HEADS].reshape(B, nq, Q_BLOCK, N_KV_HEADS, GROUP, 3)
    gates_p = gates_p.transpose(0, 3, 1, 5, 4, 2).reshape(B, N_KV_HEADS, nq, 3, ROWS_Q)
    att_p = _nsa_prompt(q_p, kvh_p, cmpkv_p, gates_p, rel_bias_table)
    x1_p, h2_p, ti_p, tg_p = _merge(x_prompt, convn_p, att_p, gn_att[0], wo_b, mp[2], mp[3], mp[4], ln2_g[0],
                                    rw_pad, rb_pad, seq_mode=True, tm=tm, n_exp=n_exp)

    xs_rows = x_sample.reshape(1, BS, D)
    prev2 = state_conv[0][:, 0, :].reshape(1, BS, cw_width)
    prev1 = state_conv[0][:, 1, :].reshape(1, BS, cw_width)
    tms = _row_tile(BS)
    convn_s, q_s, kvc_s, kvs_s, kvw_s, _, gate_s, v_s = _inproj(
        xs_rows, ms[0], ms[1], ln1_g[0], w_pad, conv_w[0], gn_conv[0], prev2, prev1, seq_mode=False, tm=tms)
    n_cmp = (past_len + TS - CMP_LEN) // CMP_STRIDE + 1
    nch_s = past_len // CMP_STRIDE
    assert n_cmp + 1 == nch_s
    ab_s = _cmp_ab_t(cache_kv_cmp[0].transpose(0, 2, 3, 4, 1), page_table, w_blk, math.gcd(n_pages, 32))
    cmpkv_s = _cmp_finish(ab_s, cmp_pe[0], cmp_w1[0], cmp_w2[0])
    t_q = past_len
    n_slc = -(-(past_len + TS) // SLC_BLOCK)
    cur = t_q // SLC_BLOCK
    k_sel = min(N_SEL, n_slc)
    tbl = rel_bias_table.reshape(N_BUCKETS, N_KV_HEADS, GROUP)
    pos_c = jnp.arange(nch_s) * CMP_STRIDE + CMP_LEN - 1
    bias_c = tbl[_rel_bucket(t_q - pos_c)].transpose(1, 2, 0)
    ncol = -(-n_slc // LANE) * LANE
    mm_s = _score_matrix(nch_s, ncol)
    q_s_hm = q_s.reshape(N_HEADS, BS, HEAD_DIM).transpose(1, 0, 2).astype(F32)
    o_c_s, idx_pad = _smp_cmp(q_s_hm, cmpkv_s, bias_c, mm_s, n_cmp, n_slc, cur)
    idx = idx_pad[:, :, :k_sel]
    blk_per_page = page // SLC_BLOCK
    pg = jnp.minimum(idx // blk_per_page, n_pages - 1)
    phys = jnp.take_along_axis(page_table, pg.reshape(BS, -1), axis=1).astype(jnp.int32)
    pos_s = pg[..., None] * page + jnp.arange(page)
    in_blk = (pos_s // SLC_BLOCK == idx[..., None]) & (pos_s < past_len)
    pos_s = pos_s.reshape(BS, N_KV_HEADS, k_sel * page)
    bucket_hot = (_rel_bucket(t_q - pos_s)[..., None] == jnp.arange(N_BUCKETS)).astype(F32)
    bias_sel = jnp.einsum('bksn,nkg->bkgs', bucket_hot, tbl, precision=lax.Precision.HIGHEST)
    new_sel = jnp.any(idx == cur, axis=-1, keepdims=True)
    mask_sel = jnp.concatenate([in_blk.reshape(BS, N_KV_HEADS, k_sel * page), new_sel], axis=-1)
    mask_sel = mask_sel.astype(F32)[:, :, None, :]
    nw = state_kv_win.shape[2]
    bias_win = tbl[_rel_bucket(nw - jnp.arange(nw))].transpose(1, 2, 0)
    bias_new = tbl[0].reshape(N_KV_HEADS, GROUP, 1)
    new_rows = jnp.stack([kvs_s[0], kvw_s[0]], axis=1)
    gates_s = gate_s[0, :, :3 * N_HEADS].reshape(BS, N_HEADS, 3)
    pool_t = cache_kv_slc[0].transpose(0, 2, 3, 4, 1)
    win_t = state_kv_win[0].transpose(0, 2, 3, 4, 1)
    att_s = _smp_att(phys.reshape(-1), q_s_hm, pool_t, bias_sel, mask_sel, new_rows,
                     win_t, bias_win, bias_new, gates_s, o_c_s, k_sel)
    att_s = att_s.reshape(1, BS, N_HEADS * HEAD_DIM)
    x1_s, h2_s, ti_s, tg_s = _merge(xs_rows, convn_s, att_s, gn_att[0], wo_b, ms[2], ms[3], ms[4], ln2_g[0],
                                    rw_pad, rb_pad, seq_mode=False, tm=tms, n_exp=n_exp)

    n_tok = B * T + BS
    h2_all = jnp.concatenate([h2_p.reshape(B * T, D), h2_s.reshape(BS, D), jnp.zeros((8, D), BF16)], axis=0)
    top_idx = jnp.concatenate([ti_p.reshape(B * T, LANE), ti_s.reshape(BS, LANE)], axis=0)[:, :TOP_K]
    row_tok, dest, block_e, n_used = _route(top_idx, n_tok, n_exp)
    xs = _rows(h2_all, row_tok)
    yb = _moe_experts(block_e, n_used, xs, w_gu[0], b_gu[0], w_down[0], b_down[0])
    dest_p = dest[:B * T].reshape(B, T, TOP_K).transpose(0, 2, 1)
    yg_p = _rows(yb, dest_p.reshape(-1)).reshape(B, TOP_K, T, D)
    yg_s = _rows(yb, dest[B * T:].T.reshape(-1)).reshape(1, TOP_K, BS, D)
    y_p = _final(x1_p, yg_p, tg_p, mp[5], final_g, seq_mode=True, tm=tm)
    y_s = _final(x1_s, yg_s, tg_s, ms[5], final_g, seq_mode=False, tm=tms)

    kv_tail = (2, N_KV_HEADS, HEAD_DIM)
    page_shape = (depth, B, T // page, page) + kv_tail
    w_keep = min(WINDOW, T)
    new_win_s = jnp.concatenate([state_kv_win[0][:, TS:], kvw_s.reshape(BS, TS, *kv_tail)], axis=1)
    new_conv_s = jnp.concatenate([state_conv[0][:, TS:], v_s.reshape(BS, TS, cw_width)], axis=1)
    return (y_p, y_s.reshape(BS, TS, D),
            kvc_p.reshape(page_shape), kvc_s.reshape((depth, BS, TS) + kv_tail),
            kvs_p.reshape(page_shape), kvs_s.reshape((depth, BS, TS) + kv_tail),
            kvw_p[:, T - w_keep:].reshape((depth, B, w_keep) + kv_tail), new_win_s[None],
            vlast_p[None], new_conv_s[None])
```

```python
import functools
import math

import numpy as np
import jax
import jax.numpy as jnp
from jax import lax
from jax.experimental import pallas as pl
from jax.experimental.pallas import tpu as pltpu

F32 = jnp.float32
BF16 = jnp.bfloat16

CONV_K = 3
N_HEADS = 8
N_KV_HEADS = 2
GROUP = N_HEADS // N_KV_HEADS
HEAD_DIM = 64
KV_WIDTH = N_KV_HEADS * HEAD_DIM
CMP_LEN = 32
CMP_STRIDE = 16
SLC_BLOCK = 64
RATIO = SLC_BLOCK // CMP_STRIDE
N_SEL = 16
WINDOW = 512
Q_BLOCK = 128
N_BUCKETS = 32
REL_MAX_DIST = 128
TOP_K = 4
SWIGLU_LIMIT = 7.0
SWIGLU_ALPHA = 1.702
EPS = 1e-6
NEG_INF = -1e30
FORCE_SCORE = 1e4

LANE = 128
ROWS_Q = GROUP * Q_BLOCK
KEY_CHUNK = 512
TILES = KEY_CHUNK // Q_BLOCK
MOE_ROWS = 256
VMEM_LIMIT = 48 * 1024 * 1024


def _cparams(n_axes):
    return pltpu.CompilerParams(dimension_semantics=("arbitrary",) * n_axes, vmem_limit_bytes=VMEM_LIMIT)


def _dot(a, b):
    return jnp.dot(a, b, preferred_element_type=F32)


def _dot_nt(a, b):
    return lax.dot_general(a, b, (((1,), (1,)), ((), ())), preferred_element_type=F32)


def _dot_tn(a, b):
    return lax.dot_general(a, b, (((0,), (0,)), ((), ())), preferred_element_type=F32)


def _split2(x):
    hi = x.astype(BF16)
    lo = (x - hi.astype(F32)).astype(BF16)
    return hi, lo


def _split3(x):
    a = x.astype(BF16)
    r = x - a.astype(F32)
    b = r.astype(BF16)
    c = (r - b.astype(F32)).astype(BF16)
    return a, b, c


def _dot3(a, b):
    ah, al = _split2(a)
    bh, bl = _split2(b)
    return _dot(ah, bh) + _dot(ah, bl) + _dot(al, bh)


def _sigmoid(x):
    return 1.0 / (1.0 + jnp.exp(-x))


def _rms(x, g):
    return x * lax.rsqrt(jnp.mean(x * x, axis=-1, keepdims=True) + EPS) * g


def _rel_bucket(dist):
    n = jnp.maximum(dist, 0)
    max_exact = N_BUCKETS // 2
    large = max_exact + (jnp.log(jnp.maximum(n, 1).astype(F32) / max_exact)
                         / math.log(REL_MAX_DIST / max_exact) * (N_BUCKETS - max_exact)).astype(jnp.int32)
    return jnp.where(n < max_exact, n, jnp.minimum(large, N_BUCKETS - 1))


def _mod_kernel(c_ref, w_ref, b_ref, o_ref):
    c = c_ref[...]
    o_ref[...] = _dot3(c * _sigmoid(c), w_ref[...]) + b_ref[...]


def _modulation(c, w_ada, b_ada):
    n, d = c.shape
    cols = w_ada.shape[1]
    bn = 1536
    return pl.pallas_call(
        _mod_kernel,
        out_shape=jax.ShapeDtypeStruct((n, cols), F32),
        grid=(cols // bn,),
        in_specs=[pl.BlockSpec((n, d), lambda i: (0, 0)),
                  pl.BlockSpec((d, bn), lambda i: (0, i)),
                  pl.BlockSpec((1, bn), lambda i: (0, i))],
        out_specs=pl.BlockSpec((n, bn), lambda i: (0, i)),
        compiler_params=_cparams(1),
        name="modulation",
    )(c, w_ada, b_ada.reshape(1, cols))


def _inproj_kernel(x_ref, m0_ref, m1_ref, g1_ref, w_ref, cw_ref, gnc_ref, pa_ref, pb_ref,
                   convn_ref, q_ref, kvc_ref, kvs_ref, kvw_ref, gate_ref, vlast_ref, *rest, seq_mode, tm, cw_width):
    if seq_mode:
        kh_ref, vth_ref, carry_ref = rest
    x = x_ref[0]
    h = _rms(x, g1_ref[...]) * (1.0 + m1_ref[0]) + m0_ref[0]
    hb = h.astype(BF16)
    c3 = 3 * cw_width
    uc = _dot(hb, w_ref[:, 0:c3])
    b_g = uc[:, 0:cw_width]
    v = uc[:, cw_width:2 * cw_width] * uc[:, 2 * cw_width:c3]
    if seq_mode:
        @pl.when(pl.program_id(1) == 0)
        def _():
            carry_ref[0:2, :] = pa_ref[0]
        c0 = carry_ref[0:1, :]
        c1 = carry_ref[1:2, :]
        row = lax.broadcasted_iota(jnp.int32, (tm, 1), 0)
        vm1 = jnp.where(row == 0, c1, pltpu.roll(v, 1, 0))
        vm2 = jnp.where(row == 0, c0, jnp.where(row == 1, c1, pltpu.roll(v, 2, 0)))
        carry_ref[0:2, :] = v[tm - 2:tm, :]
        vlast_ref[0] = v[tm - 2:tm, :]
    else:
        vm2 = pa_ref[0]
        vm1 = pb_ref[0]
        vlast_ref[0] = v
    cw = cw_ref[...]
    y = cw[0:1, :] * vm2 + cw[1:2, :] * vm1 + cw[2:3, :] * v
    convn_ref[0] = _rms(b_g * y, gnc_ref[...]).astype(BF16)

    aw = N_HEADS * HEAD_DIM
    uq = _dot(hb, w_ref[:, c3:c3 + aw]) * (HEAD_DIM ** -0.5)
    for hh in range(N_HEADS):
        q_ref[0, hh] = uq[:, hh * HEAD_DIM:(hh + 1) * HEAD_DIM].astype(BF16)
    kv0 = c3 + aw
    kvw3 = 3 * 2 * KV_WIDTH
    ukv = _dot(hb, w_ref[:, kv0:kv0 + kvw3])
    kvc_ref[0] = ukv[:, 0:2 * KV_WIDTH]
    kvs_ref[0] = ukv[:, 2 * KV_WIDTH:4 * KV_WIDTH]
    kvw_ref[0] = ukv[:, 4 * KV_WIDTH:6 * KV_WIDTH]
    if seq_mode:
        for br in range(2):
            c0 = (br + 1) * 2 * KV_WIDTH
            for k in range(N_KV_HEADS):
                kh_ref[0, br * N_KV_HEADS + k] = ukv[:, c0 + k * HEAD_DIM:c0 + (k + 1) * HEAD_DIM].astype(BF16)
            vt = ukv[:, c0 + KV_WIDTH:c0 + 2 * KV_WIDTH].T.astype(BF16)
            for k in range(N_KV_HEADS):
                vth_ref[0, br * N_KV_HEADS + k] = vt[k * HEAD_DIM:(k + 1) * HEAD_DIM, :]
    ug = _dot(hb, w_ref[:, kv0 + kvw3:kv0 + kvw3 + LANE])
    gate_ref[0] = _sigmoid(ug)


def _inproj(x, m0, m1, ln_g, w_pad, conv_w, gn_conv, pa, pb, *, seq_mode, tm):
    B, T, D = x.shape
    cw_width = conv_w.shape[1]
    nt = T // tm
    row_blk = lambda w: pl.BlockSpec((1, tm, w), lambda b, t: (b, t, 0))
    if seq_mode:
        mod_spec = pl.BlockSpec((1, 1, D), lambda b, t: (b, 0, 0))
        prev_spec = pl.BlockSpec((1, 2, cw_width), lambda b, t: (b, 0, 0))
        vlast_shape = jax.ShapeDtypeStruct((B, 2, cw_width), F32)
        vlast_spec = pl.BlockSpec((1, 2, cw_width), lambda b, t: (b, 0, 0))
    else:
        mod_spec = row_blk(D)
        prev_spec = row_blk(cw_width)
        vlast_shape = jax.ShapeDtypeStruct((B, T, cw_width), F32)
        vlast_spec = row_blk(cw_width)
    const = lambda shape: pl.BlockSpec(shape, lambda b, t: (0,) * len(shape))
    kern = functools.partial(_inproj_kernel, seq_mode=seq_mode, tm=tm, cw_width=cw_width)
    out_shape = [jax.ShapeDtypeStruct((B, T, cw_width), BF16),
                 jax.ShapeDtypeStruct((B, N_HEADS, T, HEAD_DIM), BF16),
                 jax.ShapeDtypeStruct((B, T, 2 * KV_WIDTH), F32),
                 jax.ShapeDtypeStruct((B, T, 2 * KV_WIDTH), F32),
                 jax.ShapeDtypeStruct((B, T, 2 * KV_WIDTH), F32),
                 jax.ShapeDtypeStruct((B, T, LANE), F32),
                 vlast_shape]
    out_specs = [row_blk(cw_width),
                 pl.BlockSpec((1, N_HEADS, tm, HEAD_DIM), lambda b, t: (b, 0, t, 0)),
                 row_blk(2 * KV_WIDTH), row_blk(2 * KV_WIDTH), row_blk(2 * KV_WIDTH),
                 row_blk(LANE), vlast_spec]
    scratch = []
    if seq_mode:
        n_att = 2 * N_KV_HEADS
        out_shape += [jax.ShapeDtypeStruct((B, n_att, T, HEAD_DIM), BF16),
                      jax.ShapeDtypeStruct((B, n_att, HEAD_DIM, T), BF16)]
        out_specs += [pl.BlockSpec((1, n_att, tm, HEAD_DIM), lambda b, t: (b, 0, t, 0)),
                      pl.BlockSpec((1, n_att, HEAD_DIM, tm), lambda b, t: (b, 0, 0, t))]
        scratch = [pltpu.VMEM((8, cw_width), F32)]
    return pl.pallas_call(
        kern,
        out_shape=tuple(out_shape),
        grid=(B, nt),
        in_specs=[row_blk(D), mod_spec, mod_spec, const((1, D)), const(w_pad.shape), const(conv_w.shape),
                  const((1, cw_width)), prev_spec, prev_spec],
        out_specs=tuple(out_specs),
        scratch_shapes=scratch,
        compiler_params=_cparams(2),
        name="inproj_seq" if seq_mode else "inproj_rows",
    )(x, m0, m1, ln_g.reshape(1, D), w_pad, conv_w, gn_conv.reshape(1, cw_width), pa, pb)


def _cmp_ab_kernel(pt_ref, *refs, G):
    pages = refs[:2 * G]
    w_ref, out_ref, x_ref = refs[2 * G], refs[2 * G + 1], refs[2 * G + 2]
    half = KV_WIDTH
    for j in range(G):
        for r in range(CMP_STRIDE):
            for c in range(2):
                x_ref[c, j * 8:(j + 1) * 8, r * half:(r + 1) * half] = (
                    pages[2 * j + c][0, pl.ds(r, 8, stride=CMP_STRIDE), :])
    wcols = w_ref.shape[2]
    for c in range(2):
        out_ref[0, :, c * wcols:(c + 1) * wcols] = _dot(x_ref[c].astype(BF16), w_ref[c])


def _cmp_ab(pool, page_table, w_blk, G):
    P, page, width = pool.shape
    B, n_pages = page_table.shape
    cpp = page // CMP_STRIDE
    assert cpp == 8 and n_pages % G == 0
    kdim = CMP_STRIDE * width // 2

    def pg_spec(j, c):
        return pl.BlockSpec((1, page, width // 2), lambda b, g, pt: (pt[b * n_pages + g * G + j], 0, c))

    return pl.pallas_call(
        functools.partial(_cmp_ab_kernel, G=G),
        out_shape=jax.ShapeDtypeStruct((B, n_pages * cpp, 2 * w_blk.shape[2]), F32),
        grid_spec=pltpu.PrefetchScalarGridSpec(
            num_scalar_prefetch=1,
            grid=(B, n_pages // G),
            in_specs=[pg_spec(j, c) for j in range(G) for c in range(2)]
            + [pl.BlockSpec(w_blk.shape, lambda b, g, pt: (0, 0, 0))],
            out_specs=pl.BlockSpec((1, G * cpp, 2 * w_blk.shape[2]), lambda b, g, pt: (b, g, 0)),
            scratch_shapes=[pltpu.VMEM((2, G * cpp, kdim), F32)]),
        compiler_params=_cparams(2),
        name="cmp_partial",
    )(page_table.reshape(-1).astype(jnp.int32), *([pool] * (2 * G)), w_blk)


def _cmp_ab_t_kernel(pt_ref, *refs, G):
    pages = refs[:G]
    w_ref, out_ref, x_ref, s_ref = refs[G], refs[G + 1], refs[G + 2], refs[G + 3]
    half = KV_WIDTH
    for j in range(G):
        for c in range(2):
            for k in range(N_KV_HEADS):
                s_ref[c, j, :, k * HEAD_DIM:(k + 1) * HEAD_DIM] = pages[j][0, c, k].T
    for j in range(G):
        for r in range(CMP_STRIDE):
            for c in range(2):
                x_ref[c, j * 8:(j + 1) * 8, r * half:(r + 1) * half] = (
                    s_ref[c, j, pl.ds(r, 8, stride=CMP_STRIDE), :])
    wcols = w_ref.shape[2]
    for c in range(2):
        out_ref[0, :, c * wcols:(c + 1) * wcols] = _dot(x_ref[c].astype(BF16), w_ref[c])


def _cmp_ab_t(pool_t, page_table, w_blk, G):
    P, _, _, hd, page = pool_t.shape
    B, n_pages = page_table.shape
    cpp = page // CMP_STRIDE
    assert cpp == 8 and n_pages % G == 0
    kdim = CMP_STRIDE * KV_WIDTH

    def pg_spec(j):
        return pl.BlockSpec((1,) + pool_t.shape[1:], lambda b, g, pt: (pt[b * n_pages + g * G + j], 0, 0, 0, 0))

    return pl.pallas_call(
        functools.partial(_cmp_ab_t_kernel, G=G),
        out_shape=jax.ShapeDtypeStruct((B, n_pages * cpp, 2 * w_blk.shape[2]), F32),
        grid_spec=pltpu.PrefetchScalarGridSpec(
            num_scalar_prefetch=1,
            grid=(B, n_pages // G),
            in_specs=[pg_spec(j) for j in range(G)] + [pl.BlockSpec(w_blk.shape, lambda b, g, pt: (0, 0, 0))],
            out_specs=pl.BlockSpec((1, G * cpp, 2 * w_blk.shape[2]), lambda b, g, pt: (b, g, 0)),
            scratch_shapes=[pltpu.VMEM((2, G * cpp, kdim), F32), pltpu.VMEM((2, G, page, KV_WIDTH), F32)]),
        compiler_params=_cparams(2),
        name="cmp_partial_t",
    )(page_table.reshape(-1).astype(jnp.int32), *([pool_t] * G), w_blk)


def _gelu_tanh(x):
    return 0.5 * x * (1.0 + jnp.tanh(math.sqrt(2.0 / math.pi) * (x + 0.044715 * (x * x * x))))


def _cmp_fin_kernel(ab_ref, pe_ref, w1_ref, w2_ref, kc_ref, vct_ref, *, nch):
    for c in range(2):
        pe_t = _dot(pe_ref[c:c + 1, :].astype(BF16), w1_ref[c].astype(BF16))
        w2 = w2_ref[c].astype(BF16)
        for k in range(N_KV_HEADS):
            base = (c * N_KV_HEADS + k) * 2 * HEAD_DIM
            slab = ab_ref[0, :, base:base + 2 * HEAD_DIM]
            nxt = pltpu.roll(slab, nch - 1, 0)
            pre = slab[:, 0:HEAD_DIM] + nxt[:, HEAD_DIM:2 * HEAD_DIM] + pe_t
            blocks = _dot(_gelu_tanh(pre).astype(BF16), w2)
            if c == 0:
                kc_ref[0, k] = blocks.astype(BF16)
            else:
                vct_ref[0, k] = blocks.T.astype(BF16)


def _cmp_finish(ab, cmp_pe, cmp_w1, cmp_w2):
    B, nch, w = ab.shape
    pe = cmp_pe.reshape(2, CMP_LEN * HEAD_DIM)
    return pl.pallas_call(
        functools.partial(_cmp_fin_kernel, nch=nch),
        out_shape=(jax.ShapeDtypeStruct((B, N_KV_HEADS, nch, HEAD_DIM), BF16),
                   jax.ShapeDtypeStruct((B, N_KV_HEADS, HEAD_DIM, nch), BF16)),
        grid=(B,),
        in_specs=[pl.BlockSpec((1, nch, w), lambda b: (b, 0, 0)),
                  pl.BlockSpec(pe.shape, lambda b: (0, 0)),
                  pl.BlockSpec(cmp_w1.shape, lambda b: (0, 0, 0)),
                  pl.BlockSpec(cmp_w2.shape, lambda b: (0, 0, 0))],
        out_specs=(pl.BlockSpec((1, N_KV_HEADS, nch, HEAD_DIM), lambda b: (b, 0, 0, 0)),
                   pl.BlockSpec((1, N_KV_HEADS, HEAD_DIM, nch), lambda b: (b, 0, 0, 0))),
        compiler_params=_cparams(1),
        name="cmp_finish",
    )(ab, pe, cmp_w1, cmp_w2)


def _cmp_block_weight(cmp_w1):
    hid = cmp_w1.shape[2]
    w = cmp_w1.reshape(2, 2, CMP_STRIDE, HEAD_DIM, hid)
    eye = jnp.eye(N_KV_HEADS, dtype=cmp_w1.dtype)
    wb = jnp.einsum('cardh,kj->crkdjah', w, eye)
    return wb.reshape(2, CMP_STRIDE * N_KV_HEADS * HEAD_DIM, N_KV_HEADS * 2 * hid).astype(BF16)


def _score_matrix(nch, n_slc):
    i = np.arange(nch)[:, None]
    j = np.arange(n_slc)[None, :]
    m = 2.0 * ((i // RATIO == j) & (i % RATIO < RATIO - 1)) + 1.0 * (i == RATIO * j + RATIO - 1) \
        + 1.0 * (i == RATIO * j - 1)
    return jnp.asarray(m, dtype=BF16)


def _nsa_kernel(q_ref, kc_ref, vc_ref, ks_ref, vs_ref, kw_ref, vw_ref, gate_ref, sa_ref, sw_ref,
                pcd_ref, mm_ref, e_ref, o_ref, m_ref, l_ref, acc_ref, sta_ref, stb_ref, stc_ref, std_ref,
                *, nch, n_slc):
    j = pl.program_id(2)
    q0 = pl.multiple_of(j * Q_BLOCK, Q_BLOCK)
    cq = lax.shift_right_logical(j, 2)
    jm = jnp.bitwise_and(j, TILES - 1)
    qt = q_ref[0].reshape(ROWS_Q, HEAD_DIM).astype(F32).T.astype(BF16)
    lane_q = jnp.bitwise_and(lax.broadcasted_iota(jnp.int32, (1, ROWS_Q), 1), Q_BLOCK - 1)
    tq = q0 + lane_q
    has_prev = cq >= 1
    k_own = pl.multiple_of(cq * KEY_CHUNK, KEY_CHUNK)
    k_prev = pl.multiple_of(jnp.maximum(cq - 1, 0) * KEY_CHUNK, KEY_CHUNK)
    own0 = pl.multiple_of((2 * TILES - 1 - jm) * Q_BLOCK, Q_BLOCK)
    prev0 = pl.multiple_of(jnp.where(has_prev, TILES - 1 - jm, 2 * TILES) * Q_BLOCK, Q_BLOCK)
    prevw0 = pl.multiple_of(jnp.where(has_prev, TILES - 1 - jm, 2 * TILES - 1) * Q_BLOCK, Q_BLOCK)

    stc_ref[...] = _dot(kw_ref[0, 0, pl.ds(k_own, KEY_CHUNK), :], qt) + sa_ref[0, pl.ds(own0, KEY_CHUNK), :]
    std_ref[...] = _dot(kw_ref[0, 0, pl.ds(k_prev, KEY_CHUNK), :], qt) + sw_ref[0, pl.ds(prevw0, KEY_CHUNK), :]

    lc = _dot(kc_ref[0, 0], qt)
    npatch = pcd_ref.shape[1]
    place = jnp.where(
        lax.broadcasted_iota(jnp.int32, (nch, npatch), 0)
        == (j * (Q_BLOCK // CMP_STRIDE) - npatch // 2 + lax.broadcasted_iota(jnp.int32, (nch, npatch), 1)),
        1.0, 0.0).astype(BF16)
    ph, plo = _split2(pcd_ref[0])
    lc = lc + _dot(place, ph) + _dot(place, plo)
    rowi = lax.broadcasted_iota(jnp.int32, (nch, 1), 0)
    maskc = (rowi * CMP_STRIDE + (CMP_LEN - 1)) <= tq
    lcm = jnp.where(maskc, lc, NEG_INF)
    p = jnp.where(maskc, jnp.exp(lcm - jnp.max(lcm, axis=0, keepdims=True)), 0.0)
    anyc = tq >= CMP_LEN - 1
    s = jnp.where(anyc, jnp.sum(p, axis=0, keepdims=True), 1.0)
    pc = p * (1.0 / s)
    o_c = _dot(vc_ref[0, 0], pc.astype(BF16))

    imp = pc[:, 0:Q_BLOCK]
    for g in range(1, GROUP):
        imp = imp + pc[:, g * Q_BLOCK:(g + 1) * Q_BLOCK]
    mm = mm_ref[...]
    i1, i2, i3 = _split3(imp)
    score = _dot(mm, i1) + _dot(mm, i2) + _dot(mm, i3)
    blk = lax.broadcasted_iota(jnp.int32, (n_slc, 1), 0)
    cur = lax.shift_right_logical(q0 + lax.broadcasted_iota(jnp.int32, (1, Q_BLOCK), 1), 6)
    forced = (blk == 0) | (blk == cur) | (blk == cur - 1)
    sc = jnp.where(forced, FORCE_SCORE, jnp.where(blk <= cur, score, NEG_INF))
    big = jnp.int32(1 << 30)
    sel = jnp.zeros(sc.shape, F32)
    for _ in range(min(N_SEL, n_slc)):
        mx = jnp.max(sc, axis=0, keepdims=True)
        first = jnp.min(jnp.where(sc == mx, blk, big), axis=0, keepdims=True)
        pick = blk == first
        sel = jnp.where(pick, 1.0, sel)
        sc = jnp.where(pick, -3e38, sc)
    pen = ((sel - 1.0) * (-NEG_INF)).astype(BF16)
    qsel = jnp.concatenate([jnp.concatenate([pen] * GROUP, axis=1), qt], axis=0)

    def scores_sel(k0):
        ke = jnp.concatenate([e_ref[pl.ds(k0, KEY_CHUNK), :], ks_ref[0, 0, pl.ds(k0, KEY_CHUNK), :]], axis=1)
        return _dot(ke, qsel)

    def flash(st, v, first):
        mx = jnp.max(st, axis=0, keepdims=True)
        if first:
            pe = jnp.exp(st - mx)
            l_ref[...] = jnp.sum(pe, axis=0, keepdims=True)
            acc_ref[...] = _dot(v, pe.astype(BF16))
            m_ref[...] = mx
        else:
            m_old = m_ref[...]
            m_new = jnp.maximum(m_old, mx)
            a = jnp.exp(m_old - m_new)
            pe = jnp.exp(st - m_new)
            l_ref[...] = a * l_ref[...] + jnp.sum(pe, axis=0, keepdims=True)
            acc_ref[...] = a * acc_ref[...] + _dot(v, pe.astype(BF16))
            m_ref[...] = m_new

    sta_ref[...] = scores_sel(k_own) + sa_ref[0, pl.ds(own0, KEY_CHUNK), :]
    stb_ref[...] = scores_sel(k_prev) + sa_ref[0, pl.ds(prev0, KEY_CHUNK), :]
    flash(stc_ref[...], vw_ref[0, 0, :, pl.ds(k_own, KEY_CHUNK)], True)
    flash(std_ref[...], vw_ref[0, 0, :, pl.ds(k_prev, KEY_CHUNK)], False)
    o_w = acc_ref[...] * (1.0 / l_ref[...])

    flash(sta_ref[...], vs_ref[0, 0, :, pl.ds(k_own, KEY_CHUNK)], True)
    n_far = jnp.maximum(cq - 1, 0)
    last_chunk = ks_ref.shape[2] // KEY_CHUNK - 1

    def chunk_start(c):
        return pl.multiple_of(jnp.minimum(c, last_chunk) * KEY_CHUNK, KEY_CHUNK)

    sta_ref[...] = scores_sel(chunk_start(0))
    flash(stb_ref[...], vs_ref[0, 0, :, pl.ds(k_prev, KEY_CHUNK)], False)

    def far_pair(i, carry):
        c0 = 2 * i
        stb_ref[...] = scores_sel(chunk_start(c0 + 1))
        flash(sta_ref[...], vs_ref[0, 0, :, pl.ds(chunk_start(c0), KEY_CHUNK)], False)
        sta_ref[...] = scores_sel(chunk_start(c0 + 2))
        flash(stb_ref[...], vs_ref[0, 0, :, pl.ds(chunk_start(c0 + 1), KEY_CHUNK)], False)
        return carry

    lax.fori_loop(0, lax.shift_right_logical(n_far, 1), far_pair, 0)

    @pl.when(jnp.bitwise_and(n_far, 1) == 1)
    def _():
        flash(sta_ref[...], vs_ref[0, 0, :, pl.ds(chunk_start(n_far - 1), KEY_CHUNK)], False)

    o_s = acc_ref[...] * (1.0 / l_ref[...])

    g3 = gate_ref[0, 0, 0]
    o = g3[0:1, :] * o_c + g3[1:2, :] * o_s + g3[2:3, :] * o_w
    stacked = jnp.concatenate([o[:, g * Q_BLOCK:(g + 1) * Q_BLOCK] for g in range(GROUP)], axis=0)
    o_ref[0] = stacked.T


def _bias_tables(table):
    tbl = table.reshape(N_BUCKETS, N_KV_HEADS, GROUP)
    far = tbl[N_BUCKETS - 1]
    i = jnp.arange(Q_BLOCK)

    def cols(rel):
        b = tbl[_rel_bucket(rel)] - far
        b = jnp.where((rel >= 0)[:, :, None, None], b, 0.0)
        return b.transpose(2, 0, 3, 1).reshape(N_KV_HEADS, rel.shape[0], ROWS_Q)

    p0 = cols(i[None, :] - i[:, None])
    p1 = cols(Q_BLOCK + i[None, :] - i[:, None])
    npatch = 2 * Q_BLOCK // CMP_STRIDE
    pcd = cols(i[None, :] - CMP_STRIDE * (jnp.arange(npatch)[:, None] - npatch // 2) - (CMP_LEN - 1))
    kk = jnp.arange(Q_BLOCK)[:, None]
    qq = jnp.tile(jnp.arange(Q_BLOCK), GROUP)[None, :]
    bc = lambda a: jnp.broadcast_to(a[None], (N_KV_HEADS, Q_BLOCK, ROWS_Q))
    neg = bc(jnp.full((Q_BLOCK, ROWS_Q), NEG_INF, F32))
    zero = bc(jnp.zeros((Q_BLOCK, ROWS_Q), F32))
    p0_causal = jnp.where((kk <= qq)[None], p0, NEG_INF)
    in_window = bc(jnp.where(kk > qq, 0.0, NEG_INF).astype(F32))
    strip_own = jnp.concatenate([zero] * (2 * TILES - 2) + [p1, p0_causal] + [neg] * TILES, axis=1)
    strip_win = jnp.concatenate([neg] * (TILES - 1) + [in_window] + [zero] * (TILES - 2) + [p1] + [neg] * TILES,
                                axis=1)
    return strip_own, strip_win, pcd


def _nsa_prompt(q_hm, k_hm, vt_hm, kc, vct, gates_t, table):
    B, H, T, hd = q_hm.shape
    nch = kc.shape[2]
    n_slc = T // SLC_BLOCK
    nq = T // Q_BLOCK
    assert T % KEY_CHUNK == 0 and nch == RATIO * n_slc and WINDOW <= KEY_CHUNK
    strip_own, strip_win, pcd = _bias_tables(table)
    e_tab = (jnp.arange(T)[:, None] // SLC_BLOCK == jnp.arange(n_slc)[None, :]).astype(BF16)
    mm = _score_matrix(nch, n_slc).T
    nk = N_KV_HEADS
    k_spec = lambda slab0: pl.BlockSpec((1, 1, T, hd), lambda b, k, j: (b, slab0 + k, 0, 0))
    vt_spec = lambda slab0: pl.BlockSpec((1, 1, hd, T), lambda b, k, j: (b, slab0 + k, 0, 0))
    per_head = lambda a: pl.BlockSpec((1,) + a.shape[1:], lambda b, k, j: (k, 0, 0))
    score_buf = pltpu.VMEM((KEY_CHUNK, ROWS_Q), F32)
    return pl.pallas_call(
        functools.partial(_nsa_kernel, nch=nch, n_slc=n_slc),
        out_shape=jax.ShapeDtypeStruct((B, T, H * hd), F32),
        grid=(B, nk, nq),
        in_specs=[pl.BlockSpec((1, GROUP, Q_BLOCK, hd), lambda b, k, j: (b, k, j, 0)),
                  pl.BlockSpec((1, 1, nch, hd), lambda b, k, j: (b, k, 0, 0)),
                  pl.BlockSpec((1, 1, hd, nch), lambda b, k, j: (b, k, 0, 0)),
                  k_spec(0), vt_spec(0), k_spec(nk), vt_spec(nk),
                  pl.BlockSpec((1, 1, 1, 3, ROWS_Q), lambda b, k, j: (b, k, j, 0, 0)),
                  per_head(strip_own), per_head(strip_win), per_head(pcd),
                  pl.BlockSpec(mm.shape, lambda b, k, j: (0, 0)),
                  pl.BlockSpec(e_tab.shape, lambda b, k, j: (0, 0))],
        out_specs=pl.BlockSpec((1, Q_BLOCK, GROUP * hd), lambda b, k, j: (b, j, k)),
        scratch_shapes=[pltpu.VMEM((1, ROWS_Q), F32), pltpu.VMEM((1, ROWS_Q), F32),
                        pltpu.VMEM((hd, ROWS_Q), F32), score_buf, score_buf, score_buf, score_buf],
        compiler_params=_cparams(3),
        name="nsa_prompt",
    )(q_hm, kc, vct, k_hm, vt_hm, k_hm, vt_hm, gates_t, strip_own, strip_win, pcd, mm, e_tab)


def _smp_cmp_kernel(q_ref, kc_ref, vct_ref, bias_ref, mm_ref, oc_ref, idx_ref, *, n_cmp, n_slc, cur):
    nch = kc_ref.shape[2]
    ncol = mm_ref.shape[1]
    coli = lax.broadcasted_iota(jnp.int32, (1, nch), 1)
    maskc = coli < n_cmp
    blk = lax.broadcasted_iota(jnp.int32, (1, ncol), 1)
    k_sel = min(N_SEL, n_slc)
    lane = lax.broadcasted_iota(jnp.int32, (1, LANE), 1)
    for kh in range(N_KV_HEADS):
        qm = q_ref[0, kh * GROUP:(kh + 1) * GROUP].astype(BF16)
        lc = _dot_nt(qm, kc_ref[0, kh]) + bias_ref[kh]
        lcm = jnp.where(maskc, lc, NEG_INF)
        p = jnp.where(maskc, jnp.exp(lcm - jnp.max(lcm, axis=-1, keepdims=True)), 0.0)
        pc = p * (1.0 / jnp.sum(p, axis=-1, keepdims=True))
        oc_ref[0, kh * GROUP:(kh + 1) * GROUP] = _dot_nt(pc.astype(BF16), vct_ref[0, kh])
        imp = pc[0:1]
        for g in range(1, GROUP):
            imp = imp + pc[g:g + 1]
        mm = mm_ref[...]
        i1, i2, i3 = _split3(imp)
        score = _dot(i1, mm) + _dot(i2, mm) + _dot(i3, mm)
        forced = (blk == 0) | (blk == cur) | (blk == cur - 1)
        sc = jnp.where(forced, FORCE_SCORE, jnp.where(blk <= cur, score, NEG_INF))
        sc = jnp.where(blk < n_slc, sc, -2e38)
        big = jnp.int32(1 << 30)
        out = jnp.zeros((1, LANE), jnp.int32)
        for it in range(k_sel):
            m = jnp.max(sc, axis=-1, keepdims=True)
            first = jnp.min(jnp.where(sc == m, blk, big), axis=-1, keepdims=True)
            out = jnp.where(lane == it, first, out)
            sc = jnp.where(blk == first, -3e38, sc)
        idx_ref[0, kh:kh + 1, :] = out


def _smp_cmp(q_hm, kc, vct, bias_c, mm, n_cmp, n_slc, cur):
    B = q_hm.shape[0]
    nch = kc.shape[2]
    return pl.pallas_call(
        functools.partial(_smp_cmp_kernel, n_cmp=n_cmp, n_slc=n_slc, cur=cur),
        out_shape=(jax.ShapeDtypeStruct((B, N_HEADS, HEAD_DIM), F32),
                   jax.ShapeDtypeStruct((B, N_KV_HEADS, LANE), jnp.int32)),
        grid=(B,),
        in_specs=[pl.BlockSpec((1, N_HEADS, HEAD_DIM), lambda b: (b, 0, 0)),
                  pl.BlockSpec((1, N_KV_HEADS, nch, HEAD_DIM), lambda b: (b, 0, 0, 0)),
                  pl.BlockSpec((1, N_KV_HEADS, HEAD_DIM, nch), lambda b: (b, 0, 0, 0)),
                  pl.BlockSpec(bias_c.shape, lambda b: (0, 0, 0)),
                  pl.BlockSpec(mm.shape, lambda b: (0, 0))],
        out_specs=(pl.BlockSpec((1, N_HEADS, HEAD_DIM), lambda b: (b, 0, 0)),
                   pl.BlockSpec((1, N_KV_HEADS, LANE), lambda b: (b, 0, 0))),
        compiler_params=_cparams(1),
        name="sample_cmp",
    )(q_hm, kc, vct, bias_c, mm)


def _smp_att_kernel(phys_ref, q_ref, pool_ref, bsel_ref, msel_ref, new_ref, win_ref, bwin_ref, bnew_ref,
                    gate_ref, oc_ref, o_ref, buf_ref, sem_ref, *, k_sel):
    b = pl.program_id(0)
    n_blk = N_KV_HEADS * k_sel

    def blk_copy(i):
        return pltpu.make_async_copy(pool_ref.at[phys_ref[b * n_blk + i]], buf_ref.at[i], sem_ref.at[i])

    for i in range(n_blk):
        blk_copy(i).start()
    for i in range(n_blk):
        blk_copy(i).wait()

    nw = win_ref.shape[4]
    for kh in range(N_KV_HEADS):
        rows = slice(kh * GROUP, (kh + 1) * GROUP)
        qm = q_ref[0, rows].astype(BF16)
        kcol = slice(kh * HEAD_DIM, (kh + 1) * HEAD_DIM)
        vcol = slice(KV_WIDTH + kh * HEAD_DIM, KV_WIDTH + (kh + 1) * HEAD_DIM)
        bnew = bnew_ref[kh]

        def branch(kt, vt, bias, mask, knew, vnew, new_ok):
            lg = _dot(qm, kt) + bias
            lg = jnp.where(mask, lg, NEG_INF)
            ln = jnp.sum(qm.astype(F32) * knew.astype(F32), axis=-1, keepdims=True) + bnew
            if new_ok is not None:
                ln = jnp.where(new_ok, ln, NEG_INF)
            mx = jnp.maximum(jnp.max(lg, axis=-1, keepdims=True), ln)
            pe = jnp.where(mask, jnp.exp(lg - mx), 0.0)
            pn = jnp.exp(ln - mx)
            if new_ok is not None:
                pn = jnp.where(new_ok, pn, 0.0)
            den = jnp.sum(pe, axis=-1, keepdims=True) + pn
            any_ok = den > 0.0
            inv = 1.0 / jnp.where(any_ok, den, 1.0)
            o = _dot_nt(pe.astype(BF16), vt) + pn.astype(BF16).astype(F32) * vnew.astype(F32)
            return jnp.where(any_ok, o * inv, 0.0)

        page = buf_ref.shape[4]
        nkeys = k_sel * page
        kt = jnp.concatenate([buf_ref[kh * k_sel + i, 0, kh] for i in range(k_sel)], axis=1).astype(BF16)
        vt = jnp.concatenate([buf_ref[kh * k_sel + i, 1, kh] for i in range(k_sel)], axis=1).astype(BF16)
        knew = new_ref[0, 0:1, kcol].astype(BF16)
        vnew = new_ref[0, 0:1, vcol].astype(BF16)
        msel = msel_ref[0, kh]
        o_s = branch(kt, vt, bsel_ref[0, kh], msel[:, 0:nkeys] > 0.5, knew, vnew, msel[:, nkeys:nkeys + 1] > 0.5)

        kw = win_ref[0, 0, kh].astype(BF16)
        vw = win_ref[0, 1, kh].astype(BF16)
        knw = new_ref[0, 1:2, kcol].astype(BF16)
        vnw = new_ref[0, 1:2, vcol].astype(BF16)
        dist = nw - lax.broadcasted_iota(jnp.int32, (1, nw), 1)
        o_w = branch(kw, vw, bwin_ref[kh], dist < WINDOW, knw, vnw, None)

        g3 = gate_ref[0, rows]
        o_ref[0, rows] = g3[:, 0:1] * oc_ref[0, rows] + g3[:, 1:2] * o_s + g3[:, 2:3] * o_w


def _smp_att(phys, q_hm, pool_blk, bias_sel, mask_sel, new_rows, win_buf, bias_win, bias_new, gates, o_c, k_sel):
    B = q_hm.shape[0]
    n_blk = N_KV_HEADS * k_sel
    full = lambda a: pl.BlockSpec(a.shape, lambda b, ph: (0,) * a.ndim)
    per_b = lambda a: pl.BlockSpec((1,) + a.shape[1:], lambda b, ph: (b,) + (0,) * (a.ndim - 1))
    return pl.pallas_call(
        functools.partial(_smp_att_kernel, k_sel=k_sel),
        out_shape=jax.ShapeDtypeStruct((B, N_HEADS, HEAD_DIM), F32),
        grid_spec=pltpu.PrefetchScalarGridSpec(
            num_scalar_prefetch=1,
            grid=(B,),
            in_specs=[per_b(q_hm), pl.BlockSpec(memory_space=pl.ANY), per_b(bias_sel), per_b(mask_sel),
                      per_b(new_rows), per_b(win_buf), full(bias_win), full(bias_new), per_b(gates), per_b(o_c)],
            out_specs=pl.BlockSpec((1, N_HEADS, HEAD_DIM), lambda b, ph: (b, 0, 0)),
            scratch_shapes=[pltpu.VMEM((n_blk,) + pool_blk.shape[1:], F32),
                            pltpu.SemaphoreType.DMA((n_blk,))]),
        compiler_params=_cparams(1),
        name="sample_att",
    )(phys, q_hm, pool_blk, bias_sel, mask_sel, new_rows, win_buf, bias_win, bias_new, gates, o_c)


def _merge_kernel(x_ref, cn_ref, att_ref, gna_ref, wo_ref, m2_ref, m3_ref, m4_ref, g2_ref, rw_ref, rb_ref,
                  x1_ref, h2_ref, ti_ref, tg_ref, *, n_exp, cw_width):
    att_n = _rms(att_ref[0], gna_ref[...]).astype(BF16)
    mix = _dot(cn_ref[0], wo_ref[0:cw_width, :]) + _dot(att_n, wo_ref[cw_width:, :])
    x1 = x_ref[0] + m2_ref[0] * mix
    x1_ref[0] = x1
    h2 = _rms(x1, g2_ref[...]) * (1.0 + m4_ref[0]) + m3_ref[0]
    h2_ref[0] = h2.astype(BF16)
    logits = _dot3(h2, rw_ref[...]) + rb_ref[...]
    lane = lax.broadcasted_iota(jnp.int32, (1, LANE), 1)
    sc = jnp.where(lane < n_exp, logits, -2e38)
    big = jnp.int32(1 << 30)
    ti = jnp.zeros(sc.shape, jnp.int32)
    tv = jnp.zeros(sc.shape, F32)
    v0 = None
    den = None
    for k in range(TOP_K):
        m = jnp.max(sc, axis=-1, keepdims=True)
        first = jnp.min(jnp.where(sc == m, lane, big), axis=-1, keepdims=True)
        if k == 0:
            v0 = m
        e = jnp.exp(m - v0)
        den = e if den is None else den + e
        ti = jnp.where(lane == k, first, ti)
        tv = jnp.where(lane == k, e, tv)
        sc = jnp.where(lane == first, -3e38, sc)
    ti_ref[0] = ti
    tg_ref[0] = tv * (1.0 / den)


def _merge(x, convn, att, gn_att, wo_b, m2, m3, m4, ln2_g, rw_pad, rb_pad, *, seq_mode, tm, n_exp):
    B, T, D = x.shape
    cw_width = convn.shape[2]
    aw = att.shape[2]
    row_blk = lambda w: pl.BlockSpec((1, tm, w), lambda b, t: (b, t, 0))
    mod_spec = pl.BlockSpec((1, 1, D), lambda b, t: (b, 0, 0)) if seq_mode else row_blk(D)
    const = lambda shape: pl.BlockSpec(shape, lambda b, t: (0,) * len(shape))
    return pl.pallas_call(
        functools.partial(_merge_kernel, n_exp=n_exp, cw_width=cw_width),
        out_shape=(jax.ShapeDtypeStruct((B, T, D), F32), jax.ShapeDtypeStruct((B, T, D), BF16),
                   jax.ShapeDtypeStruct((B, T, LANE), jnp.int32), jax.ShapeDtypeStruct((B, T, LANE), F32)),
        grid=(B, T // tm),
        in_specs=[row_blk(D), row_blk(cw_width), row_blk(aw), const((1, aw)), const(wo_b.shape),
                  mod_spec, mod_spec, mod_spec, const((1, D)), const(rw_pad.shape), const((1, LANE))],
        out_specs=(row_blk(D), row_blk(D), row_blk(LANE), row_blk(LANE)),
        compiler_params=_cparams(2),
        name="merge_seq" if seq_mode else "merge_rows",
    )(x, convn, att, gn_att.reshape(1, aw), wo_b, m2, m3, m4, ln2_g.reshape(1, D), rw_pad, rb_pad)


def _moe_kernel(be_ref, nu_ref, xs_ref, wgu_ref, bgu_ref, wd_ref, bd_ref, y_ref, wgu_b, wd_b, *, d_ff):
    i = pl.program_id(0)
    prev = be_ref[jnp.maximum(i - 1, 0)]
    changed = jnp.logical_or(i == 0, be_ref[i] != prev)

    @pl.when(jnp.logical_and(changed, i < nu_ref[0]))
    def _():
        wgu_b[...] = wgu_ref[0].astype(BF16)
        wd_b[...] = wd_ref[0].astype(BF16)

    @pl.when(i < nu_ref[0])
    def _():
        gu = _dot(xs_ref[...], wgu_b[...]) + bgu_ref[0]
        gate_h = jnp.minimum(gu[:, 0:d_ff], SWIGLU_LIMIT)
        up_h = jnp.clip(gu[:, d_ff:], -SWIGLU_LIMIT, SWIGLU_LIMIT)
        act = (up_h + 1.0) * gate_h * _sigmoid(SWIGLU_ALPHA * gate_h)
        y_ref[...] = _dot(act.astype(BF16), wd_b[...]) + bd_ref[0]

    @pl.when(i >= nu_ref[0])
    def _():
        y_ref[...] = jnp.zeros(y_ref.shape, F32)


def _moe_experts(block_e, n_used, xs, w_gu, b_gu, w_down, b_down):
    n_rows, D = xs.shape
    E, _, two_ff = w_gu.shape
    d_ff = two_ff // 2
    n_blocks = n_rows // MOE_ROWS
    return pl.pallas_call(
        functools.partial(_moe_kernel, d_ff=d_ff),
        out_shape=jax.ShapeDtypeStruct((n_rows, D), F32),
        grid_spec=pltpu.PrefetchScalarGridSpec(
            num_scalar_prefetch=2,
            grid=(n_blocks,),
            in_specs=[pl.BlockSpec((MOE_ROWS, D), lambda i, be, nu: (i, 0)),
                      pl.BlockSpec((1, D, two_ff), lambda i, be, nu: (be[i], 0, 0)),
                      pl.BlockSpec((1, 1, two_ff), lambda i, be, nu: (be[i], 0, 0)),
                      pl.BlockSpec((1, d_ff, D), lambda i, be, nu: (be[i], 0, 0)),
                      pl.BlockSpec((1, 1, D), lambda i, be, nu: (be[i], 0, 0))],
            out_specs=pl.BlockSpec((MOE_ROWS, D), lambda i, be, nu: (i, 0)),
            scratch_shapes=[pltpu.VMEM((D, two_ff), BF16), pltpu.VMEM((d_ff, D), BF16)]),
        compiler_params=_cparams(1),
        name="moe_experts",
    )(block_e, n_used, xs, w_gu, b_gu.reshape(E, 1, two_ff), w_down, b_down.reshape(E, 1, D))


def _route(top_idx, n_tok, n_exp):
    n_assign = n_tok * TOP_K
    flat_e = top_idx.reshape(-1)
    experts = jnp.arange(n_exp, dtype=jnp.int32)
    onehot = (flat_e[:, None] == experts[None, :]).astype(jnp.int32)
    csum = jnp.cumsum(onehot, axis=0)
    counts = csum[-1]
    padded = (counts + MOE_ROWS - 1) // MOE_ROWS * MOE_ROWS
    pad_end = jnp.cumsum(padded)
    pad_start = pad_end - padded
    dest = jnp.sum(onehot * (csum - 1 + pad_start[None, :]), axis=1).astype(jnp.int32)
    n_blocks = -(-(n_assign + n_exp * (MOE_ROWS - 1)) // MOE_ROWS)
    n_rows = n_blocks * MOE_ROWS
    blk_row0 = jnp.arange(n_blocks, dtype=jnp.int32) * MOE_ROWS
    block_e = jnp.minimum(jnp.sum((pad_end[None, :] <= blk_row0[:, None]).astype(jnp.int32), axis=1), n_exp - 1)
    n_fill = n_rows - n_assign
    fill_e = jnp.arange(n_fill, dtype=jnp.int32) // (MOE_ROWS - 1)
    fill_i = jnp.arange(n_fill, dtype=jnp.int32) % (MOE_ROWS - 1)
    fill_on = (fill_e < n_exp) & (fill_i < jnp.sum(
        (fill_e[:, None] == experts[None, :]) * (padded - counts)[None, :], axis=1))
    keys = jnp.concatenate([2 * flat_e, jnp.where(fill_on, 2 * fill_e + 1, 2 * n_exp + 1)])
    toks = jnp.concatenate([jnp.arange(n_assign, dtype=jnp.int32) // TOP_K, jnp.full((n_fill,), n_tok, jnp.int32)])
    _, row_tok = lax.sort((keys, toks), num_keys=1, is_stable=True)
    n_used = (pad_end[-1] // MOE_ROWS).astype(jnp.int32).reshape(1)
    return row_tok, dest.reshape(n_tok, TOP_K), block_e.astype(jnp.int32), n_used


def _final_kernel(x1_ref, yg_ref, tg_ref, m5_ref, fg_ref, o_ref):
    tg = tg_ref[0]
    ff = tg[:, 0:1] * yg_ref[0, 0]
    for k in range(1, TOP_K):
        ff = ff + tg[:, k:k + 1] * yg_ref[0, k]
    o_ref[0] = _rms(x1_ref[0] + m5_ref[0] * ff, fg_ref[...])


def _final(x1, yg, tg, m5, final_g, *, seq_mode, tm):
    B, T, D = x1.shape
    row_blk = lambda w: pl.BlockSpec((1, tm, w), lambda b, t: (b, t, 0))
    mod_spec = pl.BlockSpec((1, 1, D), lambda b, t: (b, 0, 0)) if seq_mode else row_blk(D)
    return pl.pallas_call(
        _final_kernel,
        out_shape=jax.ShapeDtypeStruct((B, T, D), F32),
        grid=(B, T // tm),
        in_specs=[row_blk(D), pl.BlockSpec((1, TOP_K, tm, D), lambda b, t: (b, 0, t, 0)), row_blk(LANE),
                  mod_spec, pl.BlockSpec((1, D), lambda b, t: (0, 0))],
        out_specs=row_blk(D),
        compiler_params=_cparams(2),
        name="final_seq" if seq_mode else "final_rows",
    )(x1, yg, tg, m5, final_g.reshape(1, D))


def _row_tile(t):
    for tm in (512, 256, 128, 64, 32, 16, 8):
        if t % tm == 0:
            return tm
    raise ValueError(f"unsupported row count {t}")


def _rows(a, idx):
    return a.at[idx].get(mode="promise_in_bounds")


def kernel(x_prompt, x_sample, cache_kv_cmp, cache_kv_slc, state_kv_win, state_conv, page_table, c_prompt, c_sample,
           rel_bias_table, ln1_g, ln2_g, w_ada, b_ada, w_in, conv_w, cmp_pe, cmp_w1, cmp_w2, gn_conv, gn_att, w_o,
           router_w, router_b, w_gu, b_gu, w_down, b_down, final_g):
    B, T, D = x_prompt.shape
    BS, TS, _ = x_sample.shape
    depth = w_in.shape[0]
    assert depth == 1 and TS == 1
    n_pool, page = cache_kv_cmp.shape[1], cache_kv_cmp.shape[2]
    n_pages = page_table.shape[1]
    past_len = n_pages * page
    n_exp = router_w.shape[2]
    cw_width = conv_w.shape[2]
    kvw2 = 2 * KV_WIDTH
    assert past_len % SLC_BLOCK == 0 and past_len % CMP_STRIDE == 0 and T % page == 0

    in_cols = w_in.shape[2]
    gate0 = 3 * cw_width + N_HEADS * HEAD_DIM + 3 * kvw2
    w_pad = jnp.pad(w_in[0], ((0, 0), (0, gate0 + LANE - in_cols))).astype(BF16)
    wo_b = w_o[0].astype(BF16)
    rw_pad = jnp.pad(router_w[0], ((0, 0), (0, LANE - n_exp)))
    rb_pad = jnp.pad(router_b[0], (0, LANE - n_exp)).reshape(1, LANE)
    w_blk = _cmp_block_weight(cmp_w1[0])

    n_c = B + BS
    n_cp = -(-n_c // 8) * 8
    c_all = jnp.pad(jnp.concatenate([c_prompt, c_sample], axis=0), ((0, n_cp - n_c), (0, 0)))
    mod = _modulation(c_all, w_ada[0], b_ada[0]).reshape(n_cp, 6, D)
    mp = [mod[:B, i].reshape(B, 1, D) for i in range(6)]
    ms = [mod[B:n_c, i].reshape(1, BS, D) for i in range(6)]

    tm = _row_tile(T)
    nq = T // Q_BLOCK
    zeros_prev = jnp.zeros((B, CONV_K - 1, cw_width), F32)
    convn_p, q_p, kvc_p, kvs_p, kvw_p, gate_p, vlast_p, kh_p, vth_p = _inproj(
        x_prompt, mp[0], mp[1], ln1_g[0], w_pad, conv_w[0], gn_conv[0], zeros_prev, zeros_prev, seq_mode=True, tm=tm)
    pt_p = jnp.arange(B * (T // page), dtype=jnp.int32).reshape(B, T // page)
    g_p = math.gcd(T // page, 32)
    ab_p = _cmp_ab(kvc_p.reshape(B * (T // page), page, kvw2), pt_p, w_blk, g_p)
    kc_p, vct_p = _cmp_finish(ab_p, cmp_pe[0], cmp_w1[0], cmp_w2[0])
    gates_p = gate_p[:, :, :3 * N_HEADS].reshape(B, nq, Q_BLOCK, N_KV_HEADS, GROUP, 3)
    gates_p = gates_p.transpose(0, 3, 1, 5, 4, 2).reshape(B, N_KV_HEADS, nq, 3, ROWS_Q)
    att_p = _nsa_prompt(q_p, kh_p, vth_p, kc_p, vct_p, gates_p, rel_bias_table)
    x1_p, h2_p, ti_p, tg_p = _merge(x_prompt, convn_p, att_p, gn_att[0], wo_b, mp[2], mp[3], mp[4], ln2_g[0],
                                    rw_pad, rb_pad, seq_mode=True, tm=tm, n_exp=n_exp)

    xs_rows = x_sample.reshape(1, BS, D)
    prev2 = state_conv[0][:, 0, :].reshape(1, BS, cw_width)
    prev1 = state_conv[0][:, 1, :].reshape(1, BS, cw_width)
    tms = _row_tile(BS)
    convn_s, q_s, kvc_s, kvs_s, kvw_s, gate_s, v_s = _inproj(
        xs_rows, ms[0], ms[1], ln1_g[0], w_pad, conv_w[0], gn_conv[0], prev2, prev1, seq_mode=False, tm=tms)
    n_cmp = (past_len + TS - CMP_LEN) // CMP_STRIDE + 1
    nch_s = past_len // CMP_STRIDE
    assert n_cmp + 1 == nch_s
    ab_s = _cmp_ab_t(cache_kv_cmp[0].transpose(0, 2, 3, 4, 1), page_table, w_blk, math.gcd(n_pages, 32))
    kc_s, vct_s = _cmp_finish(ab_s, cmp_pe[0], cmp_w1[0], cmp_w2[0])
    t_q = past_len
    n_slc = -(-(past_len + TS) // SLC_BLOCK)
    cur = t_q // SLC_BLOCK
    k_sel = min(N_SEL, n_slc)
    tbl = rel_bias_table.reshape(N_BUCKETS, N_KV_HEADS, GROUP)
    pos_c = jnp.arange(nch_s) * CMP_STRIDE + CMP_LEN - 1
    bias_c = tbl[_rel_bucket(t_q - pos_c)].transpose(1, 2, 0)
    ncol = -(-n_slc // LANE) * LANE
    mm_s = _score_matrix(nch_s, ncol)
    q_s_hm = q_s.reshape(N_HEADS, BS, HEAD_DIM).transpose(1, 0, 2).astype(F32)
    o_c_s, idx_pad = _smp_cmp(q_s_hm, kc_s, vct_s, bias_c, mm_s, n_cmp, n_slc, cur)
    idx = idx_pad[:, :, :k_sel]
    blk_per_page = page // SLC_BLOCK
    pg = jnp.minimum(idx // blk_per_page, n_pages - 1)
    phys = jnp.take_along_axis(page_table, pg.reshape(BS, -1), axis=1).astype(jnp.int32)
    pos_s = pg[..., None] * page + jnp.arange(page)
    in_blk = (pos_s // SLC_BLOCK == idx[..., None]) & (pos_s < past_len)
    pos_s = pos_s.reshape(BS, N_KV_HEADS, k_sel * page)
    bucket_hot = (_rel_bucket(t_q - pos_s)[..., None] == jnp.arange(N_BUCKETS)).astype(F32)
    bias_sel = jnp.einsum('bksn,nkg->bkgs', bucket_hot, tbl, precision=lax.Precision.HIGHEST)
    new_sel = jnp.any(idx == cur, axis=-1, keepdims=True)
    mask_sel = jnp.concatenate([in_blk.reshape(BS, N_KV_HEADS, k_sel * page), new_sel], axis=-1)
    mask_sel = mask_sel.astype(F32)[:, :, None, :]
    nw = state_kv_win.shape[2]
    bias_win = tbl[_rel_bucket(nw - jnp.arange(nw))].transpose(1, 2, 0)
    bias_new = tbl[0].reshape(N_KV_HEADS, GROUP, 1)
    new_rows = jnp.stack([kvs_s[0], kvw_s[0]], axis=1)
    gates_s = gate_s[0, :, :3 * N_HEADS].reshape(BS, N_HEADS, 3)
    pool_t = cache_kv_slc[0].transpose(0, 2, 3, 4, 1)
    win_t = state_kv_win[0].transpose(0, 2, 3, 4, 1)
    att_s = _smp_att(phys.reshape(-1), q_s_hm, pool_t, bias_sel, mask_sel, new_rows,
                     win_t, bias_win, bias_new, gates_s, o_c_s, k_sel)
    att_s = att_s.reshape(1, BS, N_HEADS * HEAD_DIM)
    x1_s, h2_s, ti_s, tg_s = _merge(xs_rows, convn_s, att_s, gn_att[0], wo_b, ms[2], ms[3], ms[4], ln2_g[0],
                                    rw_pad, rb_pad, seq_mode=False, tm=tms, n_exp=n_exp)

    n_tok = B * T + BS
    h2_all = jnp.concatenate([h2_p.reshape(B * T, D), h2_s.reshape(BS, D), jnp.zeros((8, D), BF16)], axis=0)
    top_idx = jnp.concatenate([ti_p.reshape(B * T, LANE), ti_s.reshape(BS, LANE)], axis=0)[:, :TOP_K]
    row_tok, dest, block_e, n_used = _route(top_idx, n_tok, n_exp)
    xs = _rows(h2_all, row_tok)
    yb = _moe_experts(block_e, n_used, xs, w_gu[0], b_gu[0], w_down[0], b_down[0])
    dest_p = dest[:B * T].reshape(B, T, TOP_K).transpose(0, 2, 1)
    yg_p = _rows(yb, dest_p.reshape(-1)).reshape(B, TOP_K, T, D)
    yg_s = _rows(yb, dest[B * T:].T.reshape(-1)).reshape(1, TOP_K, BS, D)
    y_p = _final(x1_p, yg_p, tg_p, mp[5], final_g, seq_mode=True, tm=tm)
    y_s = _final(x1_s, yg_s, tg_s, ms[5], final_g, seq_mode=False, tm=tms)

    kv_tail = (2, N_KV_HEADS, HEAD_DIM)
    page_shape = (depth, B, T // page, page) + kv_tail
    w_keep = min(WINDOW, T)
    new_win_s = jnp.concatenate([state_kv_win[0][:, TS:], kvw_s.reshape(BS, TS, *kv_tail)], axis=1)
    new_conv_s = jnp.concatenate([state_conv[0][:, TS:], v_s.reshape(BS, TS, cw_width)], axis=1)
    return (y_p, y_s.reshape(BS, TS, D),
            kvc_p.reshape(page_shape), kvc_s.reshape((depth, BS, TS) + kv_tail),
            kvs_p.reshape(page_shape), kvs_s.reshape((depth, BS, TS) + kv_tail),
            kvw_p[:, T - w_keep:].reshape((depth, B, w_keep) + kv_tail), new_win_s[None],
            vlast_p[None], new_conv_s[None])
```

```python
import functools
import math

import numpy as np
import jax
import jax.numpy as jnp
from jax import lax
from jax.experimental import pallas as pl
from jax.experimental.pallas import tpu as pltpu

F32 = jnp.float32
BF16 = jnp.bfloat16

CONV_K = 3
N_HEADS = 8
N_KV_HEADS = 2
GROUP = N_HEADS // N_KV_HEADS
HEAD_DIM = 64
KV_WIDTH = N_KV_HEADS * HEAD_DIM
CMP_LEN = 32
CMP_STRIDE = 16
SLC_BLOCK = 64
RATIO = SLC_BLOCK // CMP_STRIDE
N_SEL = 16
WINDOW = 512
Q_BLOCK = 128
N_BUCKETS = 32
REL_MAX_DIST = 128
TOP_K = 4
SWIGLU_LIMIT = 7.0
SWIGLU_ALPHA = 1.702
EPS = 1e-6
NEG_INF = -1e30
FORCE_SCORE = 1e4

LANE = 128
ROWS_Q = GROUP * Q_BLOCK
KEY_CHUNK = 512
TILES = KEY_CHUNK // Q_BLOCK
MOE_ROWS = 512
VMEM_LIMIT = 48 * 1024 * 1024


def _cparams(n_axes, vmem_limit=VMEM_LIMIT):
    return pltpu.CompilerParams(dimension_semantics=("arbitrary",) * n_axes, vmem_limit_bytes=vmem_limit)


def _dot(a, b):
    return jnp.dot(a, b, preferred_element_type=F32)


def _dot_nt(a, b):
    return lax.dot_general(a, b, (((1,), (1,)), ((), ())), preferred_element_type=F32)


def _dot_tn(a, b):
    return lax.dot_general(a, b, (((0,), (0,)), ((), ())), preferred_element_type=F32)


def _split2(x):
    hi = x.astype(BF16)
    lo = (x - hi.astype(F32)).astype(BF16)
    return hi, lo


def _split3(x):
    a = x.astype(BF16)
    r = x - a.astype(F32)
    b = r.astype(BF16)
    c = (r - b.astype(F32)).astype(BF16)
    return a, b, c


def _dot3(a, b):
    ah, al = _split2(a)
    bh, bl = _split2(b)
    return _dot(ah, bh) + _dot(ah, bl) + _dot(al, bh)


def _sigmoid(x):
    return 1.0 / (1.0 + jnp.exp(-x))


def _rms(x, g):
    return x * lax.rsqrt(jnp.mean(x * x, axis=-1, keepdims=True) + EPS) * g


def _rel_bucket(dist):
    n = jnp.maximum(dist, 0)
    max_exact = N_BUCKETS // 2
    large = max_exact + (jnp.log(jnp.maximum(n, 1).astype(F32) / max_exact)
                         / math.log(REL_MAX_DIST / max_exact) * (N_BUCKETS - max_exact)).astype(jnp.int32)
    return jnp.where(n < max_exact, n, jnp.minimum(large, N_BUCKETS - 1))


def _mod_kernel(c_ref, w_ref, b_ref, o_ref):
    c = c_ref[...]
    o_ref[...] = _dot3(c * _sigmoid(c), w_ref[...]) + b_ref[...]


def _modulation(c, w_ada, b_ada):
    n, d = c.shape
    cols = w_ada.shape[1]
    bn = 1536
    return pl.pallas_call(
        _mod_kernel,
        out_shape=jax.ShapeDtypeStruct((n, cols), F32),
        grid=(cols // bn,),
        in_specs=[pl.BlockSpec((n, d), lambda i: (0, 0)),
                  pl.BlockSpec((d, bn), lambda i: (0, i)),
                  pl.BlockSpec((1, bn), lambda i: (0, i))],
        out_specs=pl.BlockSpec((n, bn), lambda i: (0, i)),
        compiler_params=_cparams(1),
        name="modulation",
    )(c, w_ada, b_ada.reshape(1, cols))


def _inproj_kernel(x_ref, m0_ref, m1_ref, g1_ref, w_ref, cw_ref, gnc_ref, pa_ref, pb_ref,
                   convn_ref, q_ref, kvc_ref, kvs_ref, kvw_ref, gate_ref, vlast_ref, *rest, seq_mode, tm, cw_width):
    if seq_mode:
        kh_ref, vth_ref, carry_ref = rest
    x = x_ref[0]
    h = _rms(x, g1_ref[...]) * (1.0 + m1_ref[0]) + m0_ref[0]
    hb = h.astype(BF16)
    c3 = 3 * cw_width
    uc = _dot(hb, w_ref[:, 0:c3])
    b_g = uc[:, 0:cw_width]
    v = uc[:, cw_width:2 * cw_width] * uc[:, 2 * cw_width:c3]
    if seq_mode:
        @pl.when(pl.program_id(1) == 0)
        def _():
            carry_ref[0:2, :] = pa_ref[0]
        c0 = carry_ref[0:1, :]
        c1 = carry_ref[1:2, :]
        row = lax.broadcasted_iota(jnp.int32, (tm, 1), 0)
        vm1 = jnp.where(row == 0, c1, pltpu.roll(v, 1, 0))
        vm2 = jnp.where(row == 0, c0, jnp.where(row == 1, c1, pltpu.roll(v, 2, 0)))
        carry_ref[0:2, :] = v[tm - 2:tm, :]
        vlast_ref[0] = v[tm - 2:tm, :]
    else:
        vm2 = pa_ref[0]
        vm1 = pb_ref[0]
        vlast_ref[0] = v
    cw = cw_ref[...]
    y = cw[0:1, :] * vm2 + cw[1:2, :] * vm1 + cw[2:3, :] * v
    convn_ref[0] = _rms(b_g * y, gnc_ref[...]).astype(BF16)

    aw = N_HEADS * HEAD_DIM
    uq = _dot(hb, w_ref[:, c3:c3 + aw]) * (HEAD_DIM ** -0.5)
    for hh in range(N_HEADS):
        q_ref[0, hh] = uq[:, hh * HEAD_DIM:(hh + 1) * HEAD_DIM].astype(BF16)
    kv0 = c3 + aw
    kvw3 = 3 * 2 * KV_WIDTH
    ukv = _dot(hb, w_ref[:, kv0:kv0 + kvw3])
    kvc_ref[0] = ukv[:, 0:2 * KV_WIDTH]
    kvs_ref[0] = ukv[:, 2 * KV_WIDTH:4 * KV_WIDTH]
    kvw_ref[0] = ukv[:, 4 * KV_WIDTH:6 * KV_WIDTH]
    if seq_mode:
        for br in range(2):
            c0 = (br + 1) * 2 * KV_WIDTH
            for k in range(N_KV_HEADS):
                kh_ref[0, br * N_KV_HEADS + k] = ukv[:, c0 + k * HEAD_DIM:c0 + (k + 1) * HEAD_DIM].astype(BF16)
            vt = ukv[:, c0 + KV_WIDTH:c0 + 2 * KV_WIDTH].T.astype(BF16)
            for k in range(N_KV_HEADS):
                vth_ref[0, br * N_KV_HEADS + k] = vt[k * HEAD_DIM:(k + 1) * HEAD_DIM, :]
    ug = _dot(hb, w_ref[:, kv0 + kvw3:kv0 + kvw3 + LANE])
    gate_ref[0] = _sigmoid(ug)


def _inproj(x, m0, m1, ln_g, w_pad, conv_w, gn_conv, pa, pb, *, seq_mode, tm):
    B, T, D = x.shape
    cw_width = conv_w.shape[1]
    nt = T // tm
    row_blk = lambda w: pl.BlockSpec((1, tm, w), lambda b, t: (b, t, 0))
    if seq_mode:
        mod_spec = pl.BlockSpec((1, 1, D), lambda b, t: (b, 0, 0))
        prev_spec = pl.BlockSpec((1, 2, cw_width), lambda b, t: (b, 0, 0))
        vlast_shape = jax.ShapeDtypeStruct((B, 2, cw_width), F32)
        vlast_spec = pl.BlockSpec((1, 2, cw_width), lambda b, t: (b, 0, 0))
    else:
        mod_spec = row_blk(D)
        prev_spec = row_blk(cw_width)
        vlast_shape = jax.ShapeDtypeStruct((B, T, cw_width), F32)
        vlast_spec = row_blk(cw_width)
    const = lambda shape: pl.BlockSpec(shape, lambda b, t: (0,) * len(shape))
    kern = functools.partial(_inproj_kernel, seq_mode=seq_mode, tm=tm, cw_width=cw_width)
    out_shape = [jax.ShapeDtypeStruct((B, T, cw_width), BF16),
                 jax.ShapeDtypeStruct((B, N_HEADS, T, HEAD_DIM), BF16),
                 jax.ShapeDtypeStruct((B, T, 2 * KV_WIDTH), F32),
                 jax.ShapeDtypeStruct((B, T, 2 * KV_WIDTH), F32),
                 jax.ShapeDtypeStruct((B, T, 2 * KV_WIDTH), F32),
                 jax.ShapeDtypeStruct((B, T, LANE), F32),
                 vlast_shape]
    out_specs = [row_blk(cw_width),
                 pl.BlockSpec((1, N_HEADS, tm, HEAD_DIM), lambda b, t: (b, 0, t, 0)),
                 row_blk(2 * KV_WIDTH), row_blk(2 * KV_WIDTH), row_blk(2 * KV_WIDTH),
                 row_blk(LANE), vlast_spec]
    scratch = []
    if seq_mode:
        n_att = 2 * N_KV_HEADS
        out_shape += [jax.ShapeDtypeStruct((B, n_att, T, HEAD_DIM), BF16),
                      jax.ShapeDtypeStruct((B, n_att, HEAD_DIM, T), BF16)]
        out_specs += [pl.BlockSpec((1, n_att, tm, HEAD_DIM), lambda b, t: (b, 0, t, 0)),
                      pl.BlockSpec((1, n_att, HEAD_DIM, tm), lambda b, t: (b, 0, 0, t))]
        scratch = [pltpu.VMEM((8, cw_width), F32)]
    return pl.pallas_call(
        kern,
        out_shape=tuple(out_shape),
        grid=(B, nt),
        in_specs=[row_blk(D), mod_spec, mod_spec, const((1, D)), const(w_pad.shape), const(conv_w.shape),
                  const((1, cw_width)), prev_spec, prev_spec],
        out_specs=tuple(out_specs),
        scratch_shapes=scratch,
        compiler_params=_cparams(2),
        name="inproj_seq" if seq_mode else "inproj_rows",
    )(x, m0, m1, ln_g.reshape(1, D), w_pad, conv_w, gn_conv.reshape(1, cw_width), pa, pb)


def _cmp_ab_kernel(pt_ref, *refs, G):
    pages = refs[:2 * G]
    w_ref, out_ref, x_ref = refs[2 * G], refs[2 * G + 1], refs[2 * G + 2]
    half = KV_WIDTH
    for j in range(G):
        for r in range(CMP_STRIDE):
            for c in range(2):
                x_ref[c, j * 8:(j + 1) * 8, r * half:(r + 1) * half] = (
                    pages[2 * j + c][0, pl.ds(r, 8, stride=CMP_STRIDE), :])
    wcols = w_ref.shape[2]
    for c in range(2):
        out_ref[0, :, c * wcols:(c + 1) * wcols] = _dot(x_ref[c].astype(BF16), w_ref[c])


def _cmp_ab(pool, page_table, w_blk, G):
    P, page, width = pool.shape
    B, n_pages = page_table.shape
    cpp = page // CMP_STRIDE
    assert cpp == 8 and n_pages % G == 0
    kdim = CMP_STRIDE * width // 2

    def pg_spec(j, c):
        return pl.BlockSpec((1, page, width // 2), lambda b, g, pt: (pt[b * n_pages + g * G + j], 0, c))

    return pl.pallas_call(
        functools.partial(_cmp_ab_kernel, G=G),
        out_shape=jax.ShapeDtypeStruct((B, n_pages * cpp, 2 * w_blk.shape[2]), F32),
        grid_spec=pltpu.PrefetchScalarGridSpec(
            num_scalar_prefetch=1,
            grid=(B, n_pages // G),
            in_specs=[pg_spec(j, c) for j in range(G) for c in range(2)]
            + [pl.BlockSpec(w_blk.shape, lambda b, g, pt: (0, 0, 0))],
            out_specs=pl.BlockSpec((1, G * cpp, 2 * w_blk.shape[2]), lambda b, g, pt: (b, g, 0)),
            scratch_shapes=[pltpu.VMEM((2, G * cpp, kdim), F32)]),
        compiler_params=_cparams(2),
        name="cmp_partial",
    )(page_table.reshape(-1).astype(jnp.int32), *([pool] * (2 * G)), w_blk)


def _cmp_ab_t_kernel(pt_ref, *refs, G):
    pages = refs[:G]
    w_ref, out_ref, x_ref, s_ref = refs[G], refs[G + 1], refs[G + 2], refs[G + 3]
    half = KV_WIDTH
    for j in range(G):
        for c in range(2):
            for k in range(N_KV_HEADS):
                s_ref[c, j, :, k * HEAD_DIM:(k + 1) * HEAD_DIM] = pages[j][0, c, k].astype(BF16).T.astype(F32)
    for j in range(G):
        for r in range(CMP_STRIDE):
            for c in range(2):
                x_ref[c, j * 8:(j + 1) * 8, r * half:(r + 1) * half] = (
                    s_ref[c, j, pl.ds(r, 8, stride=CMP_STRIDE), :])
    wcols = w_ref.shape[2]
    for c in range(2):
        out_ref[0, :, c * wcols:(c + 1) * wcols] = _dot(x_ref[c].astype(BF16), w_ref[c])


def _cmp_ab_t(pool_t, page_table, w_blk, G):
    P, _, _, hd, page = pool_t.shape
    B, n_pages = page_table.shape
    cpp = page // CMP_STRIDE
    assert cpp == 8 and n_pages % G == 0
    kdim = CMP_STRIDE * KV_WIDTH

    def pg_spec(j):
        return pl.BlockSpec((1,) + pool_t.shape[1:], lambda b, g, pt: (pt[b * n_pages + g * G + j], 0, 0, 0, 0))

    return pl.pallas_call(
        functools.partial(_cmp_ab_t_kernel, G=G),
        out_shape=jax.ShapeDtypeStruct((B, n_pages * cpp, 2 * w_blk.shape[2]), F32),
        grid_spec=pltpu.PrefetchScalarGridSpec(
            num_scalar_prefetch=1,
            grid=(B, n_pages // G),
            in_specs=[pg_spec(j) for j in range(G)] + [pl.BlockSpec(w_blk.shape, lambda b, g, pt: (0, 0, 0))],
            out_specs=pl.BlockSpec((1, G * cpp, 2 * w_blk.shape[2]), lambda b, g, pt: (b, g, 0)),
            scratch_shapes=[pltpu.VMEM((2, G * cpp, kdim), F32), pltpu.VMEM((2, G, page, KV_WIDTH), F32)]),
        compiler_params=_cparams(2),
        name="cmp_partial_t",
    )(page_table.reshape(-1).astype(jnp.int32), *([pool_t] * G), w_blk)


def _gelu_tanh(x):
    return 0.5 * x * (1.0 + jnp.tanh(math.sqrt(2.0 / math.pi) * (x + 0.044715 * (x * x * x))))


def _cmp_fin_kernel(ab_ref, pe_ref, w1_ref, w2_ref, kc_ref, vct_ref, *, nch):
    for c in range(2):
        pe_t = _dot(pe_ref[c:c + 1, :].astype(BF16), w1_ref[c].astype(BF16))
        w2 = w2_ref[c].astype(BF16)
        for k in range(N_KV_HEADS):
            base = (c * N_KV_HEADS + k) * 2 * HEAD_DIM
            slab = ab_ref[0, :, base:base + 2 * HEAD_DIM]
            nxt = pltpu.roll(slab, nch - 1, 0)
            pre = slab[:, 0:HEAD_DIM] + nxt[:, HEAD_DIM:2 * HEAD_DIM] + pe_t
            blocks = _dot(_gelu_tanh(pre).astype(BF16), w2)
            if c == 0:
                kc_ref[0, k] = blocks.astype(BF16)
            else:
                vct_ref[0, k] = blocks.T.astype(BF16)


def _cmp_finish(ab, cmp_pe, cmp_w1, cmp_w2):
    B, nch, w = ab.shape
    pe = cmp_pe.reshape(2, CMP_LEN * HEAD_DIM)
    return pl.pallas_call(
        functools.partial(_cmp_fin_kernel, nch=nch),
        out_shape=(jax.ShapeDtypeStruct((B, N_KV_HEADS, nch, HEAD_DIM), BF16),
                   jax.ShapeDtypeStruct((B, N_KV_HEADS, HEAD_DIM, nch), BF16)),
        grid=(B,),
        in_specs=[pl.BlockSpec((1, nch, w), lambda b: (b, 0, 0)),
                  pl.BlockSpec(pe.shape, lambda b: (0, 0)),
                  pl.BlockSpec(cmp_w1.shape, lambda b: (0, 0, 0)),
                  pl.BlockSpec(cmp_w2.shape, lambda b: (0, 0, 0))],
        out_specs=(pl.BlockSpec((1, N_KV_HEADS, nch, HEAD_DIM), lambda b: (b, 0, 0, 0)),
                   pl.BlockSpec((1, N_KV_HEADS, HEAD_DIM, nch), lambda b: (b, 0, 0, 0))),
        compiler_params=_cparams(1),
        name="cmp_finish",
    )(ab, pe, cmp_w1, cmp_w2)


def _cmp_block_weight(cmp_w1):
    hid = cmp_w1.shape[2]
    w = cmp_w1.reshape(2, 2, CMP_STRIDE, HEAD_DIM, hid)
    eye = jnp.eye(N_KV_HEADS, dtype=cmp_w1.dtype)
    wb = jnp.einsum('cardh,kj->crkdjah', w, eye)
    return wb.reshape(2, CMP_STRIDE * N_KV_HEADS * HEAD_DIM, N_KV_HEADS * 2 * hid).astype(BF16)


def _score_matrix(nch, n_slc):
    i = np.arange(nch)[:, None]
    j = np.arange(n_slc)[None, :]
    m = 2.0 * ((i // RATIO == j) & (i % RATIO < RATIO - 1)) + 1.0 * (i == RATIO * j + RATIO - 1) \
        + 1.0 * (i == RATIO * j - 1)
    return jnp.asarray(m, dtype=BF16)


def _nsa_kernel(q_ref, kc_ref, vc_ref, ks_ref, vs_ref, kw_ref, vw_ref, gate_ref, sa_ref, sw_ref,
                pcd_ref, mm_ref, e_ref, o_ref, m_ref, l_ref, acc_ref, sta_ref, stb_ref, stc_ref, std_ref,
                *, nch, n_slc):
    j = pl.program_id(2)
    q0 = pl.multiple_of(j * Q_BLOCK, Q_BLOCK)
    cq = lax.shift_right_logical(j, 2)
    jm = jnp.bitwise_and(j, TILES - 1)
    qt = q_ref[0].reshape(ROWS_Q, HEAD_DIM).astype(F32).T.astype(BF16)
    lane_q = jnp.bitwise_and(lax.broadcasted_iota(jnp.int32, (1, ROWS_Q), 1), Q_BLOCK - 1)
    tq = q0 + lane_q
    has_prev = cq >= 1
    k_own = pl.multiple_of(cq * KEY_CHUNK, KEY_CHUNK)
    k_prev = pl.multiple_of(jnp.maximum(cq - 1, 0) * KEY_CHUNK, KEY_CHUNK)
    own0 = pl.multiple_of((2 * TILES - 1 - jm) * Q_BLOCK, Q_BLOCK)
    prev0 = pl.multiple_of(jnp.where(has_prev, TILES - 1 - jm, 2 * TILES) * Q_BLOCK, Q_BLOCK)
    prevw0 = pl.multiple_of(jnp.where(has_prev, TILES - 1 - jm, 2 * TILES - 1) * Q_BLOCK, Q_BLOCK)

    stc_ref[...] = _dot(kw_ref[0, 0, pl.ds(k_own, KEY_CHUNK), :], qt) + sa_ref[0, pl.ds(own0, KEY_CHUNK), :]
    std_ref[...] = _dot(kw_ref[0, 0, pl.ds(k_prev, KEY_CHUNK), :], qt) + sw_ref[0, pl.ds(prevw0, KEY_CHUNK), :]

    lc = _dot(kc_ref[0, 0], qt)
    npatch = pcd_ref.shape[1]
    place = jnp.where(
        lax.broadcasted_iota(jnp.int32, (nch, npatch), 0)
        == (j * (Q_BLOCK // CMP_STRIDE) - npatch // 2 + lax.broadcasted_iota(jnp.int32, (nch, npatch), 1)),
        1.0, 0.0).astype(BF16)
    ph, plo = _split2(pcd_ref[0])
    lc = lc + _dot(place, ph) + _dot(place, plo)
    rowi = lax.broadcasted_iota(jnp.int32, (nch, 1), 0)
    maskc = (rowi * CMP_STRIDE + (CMP_LEN - 1)) <= tq
    lcm = jnp.where(maskc, lc, NEG_INF)
    p = jnp.where(maskc, jnp.exp(lcm - jnp.max(lcm, axis=0, keepdims=True)), 0.0)
    anyc = tq >= CMP_LEN - 1
    s = jnp.where(anyc, jnp.sum(p, axis=0, keepdims=True), 1.0)
    pc = p * (1.0 / s)
    o_c = _dot(vc_ref[0, 0], pc.astype(BF16))

    imp = pc[:, 0:Q_BLOCK]
    for g in range(1, GROUP):
        imp = imp + pc[:, g * Q_BLOCK:(g + 1) * Q_BLOCK]
    mm = mm_ref[...]
    i1, i2, i3 = _split3(imp)
    score = _dot(mm, i1) + _dot(mm, i2) + _dot(mm, i3)
    blk = lax.broadcasted_iota(jnp.int32, (n_slc, 1), 0)
    cur = lax.shift_right_logical(q0 + lax.broadcasted_iota(jnp.int32, (1, Q_BLOCK), 1), 6)
    forced = (blk == 0) | (blk == cur) | (blk == cur - 1)
    sc = jnp.where(forced, FORCE_SCORE, jnp.where(blk <= cur, score, NEG_INF))
    big = jnp.int32(1 << 30)
    sel = jnp.zeros(sc.shape, F32)
    for _ in range(min(N_SEL, n_slc)):
        mx = jnp.max(sc, axis=0, keepdims=True)
        first = jnp.min(jnp.where(sc == mx, blk, big), axis=0, keepdims=True)
        pick = blk == first
        sel = jnp.where(pick, 1.0, sel)
        sc = jnp.where(pick, -3e38, sc)
    pen = ((sel - 1.0) * (-NEG_INF)).astype(BF16)
    qsel = jnp.concatenate([jnp.concatenate([pen] * GROUP, axis=1), qt], axis=0)

    def scores_sel(k0):
        ke = jnp.concatenate([e_ref[pl.ds(k0, KEY_CHUNK), :], ks_ref[0, 0, pl.ds(k0, KEY_CHUNK), :]], axis=1)
        return _dot(ke, qsel)

    def flash(st, v, first):
        mx = jnp.max(st, axis=0, keepdims=True)
        if first:
            pe = jnp.exp(st - mx)
            l_ref[...] = jnp.sum(pe, axis=0, keepdims=True)
            acc_ref[...] = _dot(v, pe.astype(BF16))
            m_ref[...] = mx
        else:
            m_old = m_ref[...]
            m_new = jnp.maximum(m_old, mx)
            a = jnp.exp(m_old - m_new)
            pe = jnp.exp(st - m_new)
            l_ref[...] = a * l_ref[...] + jnp.sum(pe, axis=0, keepdims=True)
            acc_ref[...] = a * acc_ref[...] + _dot(v, pe.astype(BF16))
            m_ref[...] = m_new

    sta_ref[...] = scores_sel(k_own) + sa_ref[0, pl.ds(own0, KEY_CHUNK), :]
    stb_ref[...] = scores_sel(k_prev) + sa_ref[0, pl.ds(prev0, KEY_CHUNK), :]
    flash(stc_ref[...], vw_ref[0, 0, :, pl.ds(k_own, KEY_CHUNK)], True)
    flash(std_ref[...], vw_ref[0, 0, :, pl.ds(k_prev, KEY_CHUNK)], False)
    o_w = acc_ref[...] * (1.0 / l_ref[...])

    flash(sta_ref[...], vs_ref[0, 0, :, pl.ds(k_own, KEY_CHUNK)], True)
    n_far = jnp.maximum(cq - 1, 0)
    last_chunk = ks_ref.shape[2] // KEY_CHUNK - 1

    def chunk_start(c):
        return pl.multiple_of(jnp.minimum(c, last_chunk) * KEY_CHUNK, KEY_CHUNK)

    sta_ref[...] = scores_sel(chunk_start(0))
    flash(stb_ref[...], vs_ref[0, 0, :, pl.ds(k_prev, KEY_CHUNK)], False)

    def far_pair(i, carry):
        c0 = 2 * i
        stb_ref[...] = scores_sel(chunk_start(c0 + 1))
        flash(sta_ref[...], vs_ref[0, 0, :, pl.ds(chunk_start(c0), KEY_CHUNK)], False)
        sta_ref[...] = scores_sel(chunk_start(c0 + 2))
        flash(stb_ref[...], vs_ref[0, 0, :, pl.ds(chunk_start(c0 + 1), KEY_CHUNK)], False)
        return carry

    lax.fori_loop(0, lax.shift_right_logical(n_far, 1), far_pair, 0)

    @pl.when(jnp.bitwise_and(n_far, 1) == 1)
    def _():
        flash(sta_ref[...], vs_ref[0, 0, :, pl.ds(chunk_start(n_far - 1), KEY_CHUNK)], False)

    o_s = acc_ref[...] * (1.0 / l_ref[...])

    g3 = gate_ref[0, 0, 0]
    o = g3[0:1, :] * o_c + g3[1:2, :] * o_s + g3[2:3, :] * o_w
    stacked = jnp.concatenate([o[:, g * Q_BLOCK:(g + 1) * Q_BLOCK] for g in range(GROUP)], axis=0)
    o_ref[0] = stacked.T


def _bias_tables(table):
    tbl = table.reshape(N_BUCKETS, N_KV_HEADS, GROUP)
    far = tbl[N_BUCKETS - 1]
    i = jnp.arange(Q_BLOCK)

    def cols(rel):
        hot = (_rel_bucket(rel)[..., None] == jnp.arange(N_BUCKETS)).astype(F32)
        b = jnp.einsum('kqn,nhg->kqhg', hot, tbl, precision=lax.Precision.HIGHEST) - far
        b = jnp.where((rel >= 0)[:, :, None, None], b, 0.0)
        return b.transpose(2, 0, 3, 1).reshape(N_KV_HEADS, rel.shape[0], ROWS_Q)

    p0 = cols(i[None, :] - i[:, None])
    p1 = cols(Q_BLOCK + i[None, :] - i[:, None])
    npatch = 2 * Q_BLOCK // CMP_STRIDE
    pcd = cols(i[None, :] - CMP_STRIDE * (jnp.arange(npatch)[:, None] - npatch // 2) - (CMP_LEN - 1))
    kk = jnp.arange(Q_BLOCK)[:, None]
    qq = jnp.tile(jnp.arange(Q_BLOCK), GROUP)[None, :]
    bc = lambda a: jnp.broadcast_to(a[None], (N_KV_HEADS, Q_BLOCK, ROWS_Q))
    neg = bc(jnp.full((Q_BLOCK, ROWS_Q), NEG_INF, F32))
    zero = bc(jnp.zeros((Q_BLOCK, ROWS_Q), F32))
    p0_causal = jnp.where((kk <= qq)[None], p0, NEG_INF)
    in_window = bc(jnp.where(kk > qq, 0.0, NEG_INF).astype(F32))
    strip_own = jnp.concatenate([zero] * (2 * TILES - 2) + [p1, p0_causal] + [neg] * TILES, axis=1)
    strip_win = jnp.concatenate([neg] * (TILES - 1) + [in_window] + [zero] * (TILES - 2) + [p1] + [neg] * TILES,
                                axis=1)
    return strip_own, strip_win, pcd


def _nsa_prompt(q_hm, k_hm, vt_hm, kc, vct, gates_t, table):
    B, H, T, hd = q_hm.shape
    nch = kc.shape[2]
    n_slc = T // SLC_BLOCK
    nq = T // Q_BLOCK
    assert T % KEY_CHUNK == 0 and nch == RATIO * n_slc and WINDOW <= KEY_CHUNK
    strip_own, strip_win, pcd = _bias_tables(table)
    e_tab = (jnp.arange(T)[:, None] // SLC_BLOCK == jnp.arange(n_slc)[None, :]).astype(BF16)
    mm = _score_matrix(nch, n_slc).T
    nk = N_KV_HEADS
    k_spec = lambda slab0: pl.BlockSpec((1, 1, T, hd), lambda b, k, j: (b, slab0 + k, 0, 0))
    vt_spec = lambda slab0: pl.BlockSpec((1, 1, hd, T), lambda b, k, j: (b, slab0 + k, 0, 0))
    per_head = lambda a: pl.BlockSpec((1,) + a.shape[1:], lambda b, k, j: (k, 0, 0))
    score_buf = pltpu.VMEM((KEY_CHUNK, ROWS_Q), F32)
    return pl.pallas_call(
        functools.partial(_nsa_kernel, nch=nch, n_slc=n_slc),
        out_shape=jax.ShapeDtypeStruct((B, T, H * hd), F32),
        grid=(B, nk, nq),
        in_specs=[pl.BlockSpec((1, GROUP, Q_BLOCK, hd), lambda b, k, j: (b, k, j, 0)),
                  pl.BlockSpec((1, 1, nch, hd), lambda b, k, j: (b, k, 0, 0)),
                  pl.BlockSpec((1, 1, hd, nch), lambda b, k, j: (b, k, 0, 0)),
                  k_spec(0), vt_spec(0), k_spec(nk), vt_spec(nk),
                  pl.BlockSpec((1, 1, 1, 3, ROWS_Q), lambda b, k, j: (b, k, j, 0, 0)),
                  per_head(strip_own), per_head(strip_win), per_head(pcd),
                  pl.BlockSpec(mm.shape, lambda b, k, j: (0, 0)),
                  pl.BlockSpec(e_tab.shape, lambda b, k, j: (0, 0))],
        out_specs=pl.BlockSpec((1, Q_BLOCK, GROUP * hd), lambda b, k, j: (b, j, k)),
        scratch_shapes=[pltpu.VMEM((1, ROWS_Q), F32), pltpu.VMEM((1, ROWS_Q), F32),
                        pltpu.VMEM((hd, ROWS_Q), F32), score_buf, score_buf, score_buf, score_buf],
        compiler_params=_cparams(3),
        name="nsa_prompt",
    )(q_hm, kc, vct, k_hm, vt_hm, k_hm, vt_hm, gates_t, strip_own, strip_win, pcd, mm, e_tab)


def _smp_cmp_kernel(q_ref, kc_ref, vct_ref, bias_ref, mm_ref, oc_ref, idx_ref, *, n_cmp, n_slc, cur):
    nch = kc_ref.shape[2]
    ncol = mm_ref.shape[1]
    coli = lax.broadcasted_iota(jnp.int32, (1, nch), 1)
    maskc = coli < n_cmp
    blk = lax.broadcasted_iota(jnp.int32, (1, ncol), 1)
    k_sel = min(N_SEL, n_slc)
    lane = lax.broadcasted_iota(jnp.int32, (1, LANE), 1)
    for kh in range(N_KV_HEADS):
        qm = q_ref[0, kh * GROUP:(kh + 1) * GROUP].astype(BF16)
        lc = _dot_nt(qm, kc_ref[0, kh]) + bias_ref[kh]
        lcm = jnp.where(maskc, lc, NEG_INF)
        p = jnp.where(maskc, jnp.exp(lcm - jnp.max(lcm, axis=-1, keepdims=True)), 0.0)
        pc = p * (1.0 / jnp.sum(p, axis=-1, keepdims=True))
        oc_ref[0, kh * GROUP:(kh + 1) * GROUP] = _dot_nt(pc.astype(BF16), vct_ref[0, kh])
        imp = pc[0:1]
        for g in range(1, GROUP):
            imp = imp + pc[g:g + 1]
        mm = mm_ref[...]
        i1, i2, i3 = _split3(imp)
        score = _dot(i1, mm) + _dot(i2, mm) + _dot(i3, mm)
        forced = (blk == 0) | (blk == cur) | (blk == cur - 1)
        sc = jnp.where(forced, FORCE_SCORE, jnp.where(blk <= cur, score, NEG_INF))
        sc = jnp.where(blk < n_slc, sc, -2e38)
        big = jnp.int32(1 << 30)
        out = jnp.zeros((1, LANE), jnp.int32)
        for it in range(k_sel):
            m = jnp.max(sc, axis=-1, keepdims=True)
            first = jnp.min(jnp.where(sc == m, blk, big), axis=-1, keepdims=True)
            out = jnp.where(lane == it, first, out)
            sc = jnp.where(blk == first, -3e38, sc)
        idx_ref[0, kh:kh + 1, :] = out


def _smp_cmp(q_hm, kc, vct, bias_c, mm, n_cmp, n_slc, cur):
    B = q_hm.shape[0]
    nch = kc.shape[2]
    return pl.pallas_call(
        functools.partial(_smp_cmp_kernel, n_cmp=n_cmp, n_slc=n_slc, cur=cur),
        out_shape=(jax.ShapeDtypeStruct((B, N_HEADS, HEAD_DIM), F32),
                   jax.ShapeDtypeStruct((B, N_KV_HEADS, LANE), jnp.int32)),
        grid=(B,),
        in_specs=[pl.BlockSpec((1, N_HEADS, HEAD_DIM), lambda b: (b, 0, 0)),
                  pl.BlockSpec((1, N_KV_HEADS, nch, HEAD_DIM), lambda b: (b, 0, 0, 0)),
                  pl.BlockSpec((1, N_KV_HEADS, HEAD_DIM, nch), lambda b: (b, 0, 0, 0)),
                  pl.BlockSpec(bias_c.shape, lambda b: (0, 0, 0)),
                  pl.BlockSpec(mm.shape, lambda b: (0, 0))],
        out_specs=(pl.BlockSpec((1, N_HEADS, HEAD_DIM), lambda b: (b, 0, 0)),
                   pl.BlockSpec((1, N_KV_HEADS, LANE), lambda b: (b, 0, 0))),
        compiler_params=_cparams(1),
        name="sample_cmp",
    )(q_hm, kc, vct, bias_c, mm)


def _smp_att_kernel(phys_ref, q_ref, pool_ref, bsel_ref, msel_ref, new_ref, win_ref, bwin_ref, bnew_ref,
                    gate_ref, oc_ref, o_ref, buf_ref, sem_ref, *, k_sel):
    b = pl.program_id(0)
    n_blk = N_KV_HEADS * k_sel

    def blk_copy(i):
        return pltpu.make_async_copy(pool_ref.at[phys_ref[b * n_blk + i]], buf_ref.at[i], sem_ref.at[i])

    for i in range(n_blk):
        blk_copy(i).start()
    for i in range(n_blk):
        blk_copy(i).wait()

    nw = win_ref.shape[4]
    for kh in range(N_KV_HEADS):
        rows = slice(kh * GROUP, (kh + 1) * GROUP)
        qm = q_ref[0, rows].astype(BF16)
        kcol = slice(kh * HEAD_DIM, (kh + 1) * HEAD_DIM)
        vcol = slice(KV_WIDTH + kh * HEAD_DIM, KV_WIDTH + (kh + 1) * HEAD_DIM)
        bnew = bnew_ref[kh]

        def branch(kt, vt, bias, mask, knew, vnew, new_ok):
            lg = _dot(qm, kt) + bias
            lg = jnp.where(mask, lg, NEG_INF)
            ln = jnp.sum(qm.astype(F32) * knew.astype(F32), axis=-1, keepdims=True) + bnew
            if new_ok is not None:
                ln = jnp.where(new_ok, ln, NEG_INF)
            mx = jnp.maximum(jnp.max(lg, axis=-1, keepdims=True), ln)
            pe = jnp.where(mask, jnp.exp(lg - mx), 0.0)
            pn = jnp.exp(ln - mx)
            if new_ok is not None:
                pn = jnp.where(new_ok, pn, 0.0)
            den = jnp.sum(pe, axis=-1, keepdims=True) + pn
            any_ok = den > 0.0
            inv = 1.0 / jnp.where(any_ok, den, 1.0)
            o = _dot_nt(pe.astype(BF16), vt) + pn.astype(BF16).astype(F32) * vnew.astype(F32)
            return jnp.where(any_ok, o * inv, 0.0)

        page = buf_ref.shape[4]
        nkeys = k_sel * page
        kt = jnp.concatenate([buf_ref[kh * k_sel + i, 0, kh] for i in range(k_sel)], axis=1).astype(BF16)
        vt = jnp.concatenate([buf_ref[kh * k_sel + i, 1, kh] for i in range(k_sel)], axis=1).astype(BF16)
        knew = new_ref[0, 0:1, kcol].astype(BF16)
        vnew = new_ref[0, 0:1, vcol].astype(BF16)
        msel = msel_ref[0, kh]
        o_s = branch(kt, vt, bsel_ref[0, kh], msel[:, 0:nkeys] > 0.5, knew, vnew, msel[:, nkeys:nkeys + 1] > 0.5)

        kw = win_ref[0, 0, kh].astype(BF16)
        vw = win_ref[0, 1, kh].astype(BF16)
        knw = new_ref[0, 1:2, kcol].astype(BF16)
        vnw = new_ref[0, 1:2, vcol].astype(BF16)
        dist = nw - lax.broadcasted_iota(jnp.int32, (1, nw), 1)
        o_w = branch(kw, vw, bwin_ref[kh], dist < WINDOW, knw, vnw, None)

        g3 = gate_ref[0, rows]
        o_ref[0, rows] = g3[:, 0:1] * oc_ref[0, rows] + g3[:, 1:2] * o_s + g3[:, 2:3] * o_w


def _smp_att(phys, q_hm, pool_blk, bias_sel, mask_sel, new_rows, win_buf, bias_win, bias_new, gates, o_c, k_sel):
    B = q_hm.shape[0]
    n_blk = N_KV_HEADS * k_sel
    full = lambda a: pl.BlockSpec(a.shape, lambda b, ph: (0,) * a.ndim)
    per_b = lambda a: pl.BlockSpec((1,) + a.shape[1:], lambda b, ph: (b,) + (0,) * (a.ndim - 1))
    return pl.pallas_call(
        functools.partial(_smp_att_kernel, k_sel=k_sel),
        out_shape=jax.ShapeDtypeStruct((B, N_HEADS, HEAD_DIM), F32),
        grid_spec=pltpu.PrefetchScalarGridSpec(
            num_scalar_prefetch=1,
            grid=(B,),
            in_specs=[per_b(q_hm), pl.BlockSpec(memory_space=pl.ANY), per_b(bias_sel), per_b(mask_sel),
                      per_b(new_rows), per_b(win_buf), full(bias_win), full(bias_new), per_b(gates), per_b(o_c)],
            out_specs=pl.BlockSpec((1, N_HEADS, HEAD_DIM), lambda b, ph: (b, 0, 0)),
            scratch_shapes=[pltpu.VMEM((n_blk,) + pool_blk.shape[1:], F32),
                            pltpu.SemaphoreType.DMA((n_blk,))]),
        compiler_params=_cparams(1),
        name="sample_att",
    )(phys, q_hm, pool_blk, bias_sel, mask_sel, new_rows, win_buf, bias_win, bias_new, gates, o_c)


def _merge_kernel(x_ref, cn_ref, att_ref, gna_ref, wo_ref, m2_ref, m3_ref, m4_ref, g2_ref, rw_ref, rb_ref,
                  x1_ref, h2_ref, ti_ref, tg_ref, *, n_exp, cw_width):
    att_n = _rms(att_ref[0], gna_ref[...]).astype(BF16)
    mix = _dot(cn_ref[0], wo_ref[0:cw_width, :]) + _dot(att_n, wo_ref[cw_width:, :])
    x1 = x_ref[0] + m2_ref[0] * mix
    x1_ref[0] = x1
    h2 = _rms(x1, g2_ref[...]) * (1.0 + m4_ref[0]) + m3_ref[0]
    h2_ref[0] = h2
    logits = _dot3(h2, rw_ref[...]) + rb_ref[...]
    lane = lax.broadcasted_iota(jnp.int32, (1, LANE), 1)
    sc = jnp.where(lane < n_exp, logits, -2e38)
    big = jnp.int32(1 << 30)
    ti = jnp.zeros(sc.shape, jnp.int32)
    tv = jnp.zeros(sc.shape, F32)
    v0 = None
    den = None
    for k in range(TOP_K):
        m = jnp.max(sc, axis=-1, keepdims=True)
        first = jnp.min(jnp.where(sc == m, lane, big), axis=-1, keepdims=True)
        if k == 0:
            v0 = m
        e = jnp.exp(m - v0)
        den = e if den is None else den + e
        ti = jnp.where(lane == k, first, ti)
        tv = jnp.where(lane == k, e, tv)
        sc = jnp.where(lane == first, -3e38, sc)
    ti_ref[0] = ti
    tg_ref[0] = tv * (1.0 / den)


def _merge(x, convn, att, gn_att, wo_b, m2, m3, m4, ln2_g, rw_pad, rb_pad, *, seq_mode, tm, n_exp):
    B, T, D = x.shape
    cw_width = convn.shape[2]
    aw = att.shape[2]
    row_blk = lambda w: pl.BlockSpec((1, tm, w), lambda b, t: (b, t, 0))
    mod_spec = pl.BlockSpec((1, 1, D), lambda b, t: (b, 0, 0)) if seq_mode else row_blk(D)
    const = lambda shape: pl.BlockSpec(shape, lambda b, t: (0,) * len(shape))
    return pl.pallas_call(
        functools.partial(_merge_kernel, n_exp=n_exp, cw_width=cw_width),
        out_shape=(jax.ShapeDtypeStruct((B, T, D), F32), jax.ShapeDtypeStruct((B, T, D), F32),
                   jax.ShapeDtypeStruct((B, T, LANE), jnp.int32), jax.ShapeDtypeStruct((B, T, LANE), F32)),
        grid=(B, T // tm),
        in_specs=[row_blk(D), row_blk(cw_width), row_blk(aw), const((1, aw)), const(wo_b.shape),
                  mod_spec, mod_spec, mod_spec, const((1, D)), const(rw_pad.shape), const((1, LANE))],
        out_specs=(row_blk(D), row_blk(D), row_blk(LANE), row_blk(LANE)),
        compiler_params=_cparams(2),
        name="merge_seq" if seq_mode else "merge_rows",
    )(x, convn, att, gn_att.reshape(1, aw), wo_b, m2, m3, m4, ln2_g.reshape(1, D), rw_pad, rb_pad)


def _moe_kernel(be_ref, nu_ref, xs_ref, wgu_ref, bgu_ref, wd_ref, bd_ref, y_ref, wgu_b, wd_b, *, d_ff):
    i = pl.program_id(0)
    prev = be_ref[jnp.maximum(i - 1, 0)]
    changed = jnp.logical_or(i == 0, be_ref[i] != prev)

    @pl.when(jnp.logical_and(changed, i < nu_ref[0]))
    def _():
        wgu_b[...] = wgu_ref[0].astype(BF16)
        wd_b[...] = wd_ref[0].astype(BF16)

    @pl.when(i < nu_ref[0])
    def _():
        gu = _dot(xs_ref[...].astype(BF16), wgu_b[...]) + bgu_ref[0]
        gate_h = jnp.minimum(gu[:, 0:d_ff], SWIGLU_LIMIT)
        up_h = jnp.clip(gu[:, d_ff:], -SWIGLU_LIMIT, SWIGLU_LIMIT)
        act = (up_h + 1.0) * gate_h * _sigmoid(SWIGLU_ALPHA * gate_h)
        y_ref[...] = _dot(act.astype(BF16), wd_b[...]) + bd_ref[0]

    @pl.when(i >= nu_ref[0])
    def _():
        y_ref[...] = jnp.zeros(y_ref.shape, F32)


def _moe_experts(block_e, n_used, xs, w_gu, b_gu, w_down, b_down):
    n_rows, D = xs.shape
    E, _, two_ff = w_gu.shape
    d_ff = two_ff // 2
    n_blocks = n_rows // MOE_ROWS
    return pl.pallas_call(
        functools.partial(_moe_kernel, d_ff=d_ff),
        out_shape=jax.ShapeDtypeStruct((n_rows, D), F32),
        grid_spec=pltpu.PrefetchScalarGridSpec(
            num_scalar_prefetch=2,
            grid=(n_blocks,),
            in_specs=[pl.BlockSpec((MOE_ROWS, D), lambda i, be, nu: (i, 0)),
                      pl.BlockSpec((1, D, two_ff), lambda i, be, nu: (be[i], 0, 0)),
                      pl.BlockSpec((1, 1, two_ff), lambda i, be, nu: (be[i], 0, 0)),
                      pl.BlockSpec((1, d_ff, D), lambda i, be, nu: (be[i], 0, 0)),
                      pl.BlockSpec((1, 1, D), lambda i, be, nu: (be[i], 0, 0))],
            out_specs=pl.BlockSpec((MOE_ROWS, D), lambda i, be, nu: (i, 0)),
            scratch_shapes=[pltpu.VMEM((D, two_ff), BF16), pltpu.VMEM((d_ff, D), BF16)]),
        compiler_params=_cparams(1),
        name="moe_experts",
    )(block_e, n_used, xs, w_gu, b_gu.reshape(E, 1, two_ff), w_down, b_down.reshape(E, 1, D))


def _route(top_idx, n_tok, n_exp):
    n_assign = n_tok * TOP_K
    flat_e = top_idx.reshape(-1)
    experts = jnp.arange(n_exp, dtype=jnp.int32)
    onehot = (flat_e[:, None] == experts[None, :]).astype(jnp.int32)
    csum = jnp.cumsum(onehot, axis=0)
    counts = csum[-1]
    padded = (counts + MOE_ROWS - 1) // MOE_ROWS * MOE_ROWS
    pad_end = jnp.cumsum(padded)
    pad_start = pad_end - padded
    dest = jnp.sum(onehot * (csum - 1 + pad_start[None, :]), axis=1).astype(jnp.int32)
    n_blocks = -(-(n_assign + n_exp * (MOE_ROWS - 1)) // MOE_ROWS)
    n_rows = n_blocks * MOE_ROWS
    blk_row0 = jnp.arange(n_blocks, dtype=jnp.int32) * MOE_ROWS
    block_e = jnp.minimum(jnp.sum((pad_end[None, :] <= blk_row0[:, None]).astype(jnp.int32), axis=1), n_exp - 1)
    n_fill = n_rows - n_assign
    fill_e = jnp.arange(n_fill, dtype=jnp.int32) // (MOE_ROWS - 1)
    fill_i = jnp.arange(n_fill, dtype=jnp.int32) % (MOE_ROWS - 1)
    fill_on = (fill_e < n_exp) & (fill_i < jnp.sum(
        (fill_e[:, None] == experts[None, :]) * (padded - counts)[None, :], axis=1))
    keys = jnp.concatenate([2 * flat_e, jnp.where(fill_on, 2 * fill_e + 1, 2 * n_exp + 1)])
    toks = jnp.concatenate([jnp.arange(n_assign, dtype=jnp.int32) // TOP_K, jnp.full((n_fill,), n_tok, jnp.int32)])
    _, row_tok = lax.sort((keys, toks), num_keys=1, is_stable=True)
    n_used = (pad_end[-1] // MOE_ROWS).astype(jnp.int32).reshape(1)
    return row_tok, dest.reshape(n_tok, TOP_K), block_e.astype(jnp.int32), n_used


def _final_kernel(x1_ref, yg_ref, tg_ref, m5_ref, fg_ref, o_ref):
    tg = tg_ref[0]
    ff = tg[:, 0:1] * yg_ref[0, 0]
    for k in range(1, TOP_K):
        ff = ff + tg[:, k:k + 1] * yg_ref[0, k]
    o_ref[0] = _rms(x1_ref[0] + m5_ref[0] * ff, fg_ref[...])


def _final(x1, yg, tg, m5, final_g, *, seq_mode, tm):
    B, T, D = x1.shape
    row_blk = lambda w: pl.BlockSpec((1, tm, w), lambda b, t: (b, t, 0))
    mod_spec = pl.BlockSpec((1, 1, D), lambda b, t: (b, 0, 0)) if seq_mode else row_blk(D)
    return pl.pallas_call(
        _final_kernel,
        out_shape=jax.ShapeDtypeStruct((B, T, D), F32),
        grid=(B, T // tm),
        in_specs=[row_blk(D), pl.BlockSpec((1, TOP_K, tm, D), lambda b, t: (b, 0, t, 0)), row_blk(LANE),
                  mod_spec, pl.BlockSpec((1, D), lambda b, t: (0, 0))],
        out_specs=row_blk(D),
        compiler_params=_cparams(2),
        name="final_seq" if seq_mode else "final_rows",
    )(x1, yg, tg, m5, final_g.reshape(1, D))


def _row_tile(t):
    for tm in (512, 256, 128, 64, 32, 16, 8):
        if t % tm == 0:
            return tm
    raise ValueError(f"unsupported row count {t}")


def _rows(a, idx):
    return a.at[idx].get(mode="promise_in_bounds")


def kernel(x_prompt, x_sample, cache_kv_cmp, cache_kv_slc, state_kv_win, state_conv, page_table, c_prompt, c_sample,
           rel_bias_table, ln1_g, ln2_g, w_ada, b_ada, w_in, conv_w, cmp_pe, cmp_w1, cmp_w2, gn_conv, gn_att, w_o,
           router_w, router_b, w_gu, b_gu, w_down, b_down, final_g):
    B, T, D = x_prompt.shape
    BS, TS, _ = x_sample.shape
    depth = w_in.shape[0]
    assert depth == 1 and TS == 1
    n_pool, page = cache_kv_cmp.shape[1], cache_kv_cmp.shape[2]
    n_pages = page_table.shape[1]
    past_len = n_pages * page
    n_exp = router_w.shape[2]
    cw_width = conv_w.shape[2]
    kvw2 = 2 * KV_WIDTH
    assert past_len % SLC_BLOCK == 0 and past_len % CMP_STRIDE == 0 and T % page == 0

    in_cols = w_in.shape[2]
    gate0 = 3 * cw_width + N_HEADS * HEAD_DIM + 3 * kvw2
    w_pad = jnp.pad(w_in[0], ((0, 0), (0, gate0 + LANE - in_cols))).astype(BF16)
    wo_b = w_o[0].astype(BF16)
    rw_pad = jnp.pad(router_w[0], ((0, 0), (0, LANE - n_exp)))
    rb_pad = jnp.pad(router_b[0], (0, LANE - n_exp)).reshape(1, LANE)
    w_blk = _cmp_block_weight(cmp_w1[0])

    n_c = B + BS
    n_cp = -(-n_c // 8) * 8
    c_all = jnp.pad(jnp.concatenate([c_prompt, c_sample], axis=0), ((0, n_cp - n_c), (0, 0)))
    mod = _modulation(c_all, w_ada[0], b_ada[0]).reshape(n_cp, 6, D)
    mp = [mod[:B, i].reshape(B, 1, D) for i in range(6)]
    ms = [mod[B:n_c, i].reshape(1, BS, D) for i in range(6)]

    tm = _row_tile(T)
    nq = T // Q_BLOCK
    zeros_prev = jnp.zeros((B, CONV_K - 1, cw_width), F32)
    convn_p, q_p, kvc_p, kvs_p, kvw_p, gate_p, vlast_p, kh_p, vth_p = _inproj(
        x_prompt, mp[0], mp[1], ln1_g[0], w_pad, conv_w[0], gn_conv[0], zeros_prev, zeros_prev, seq_mode=True, tm=tm)
    pt_p = jnp.arange(B * (T // page), dtype=jnp.int32).reshape(B, T // page)
    g_p = math.gcd(T // page, 32)
    ab_p = _cmp_ab(kvc_p.reshape(B * (T // page), page, kvw2), pt_p, w_blk, g_p)
    kc_p, vct_p = _cmp_finish(ab_p, cmp_pe[0], cmp_w1[0], cmp_w2[0])
    gates_p = gate_p[:, :, :3 * N_HEADS].reshape(B, nq, Q_BLOCK, N_KV_HEADS, GROUP, 3)
    gates_p = gates_p.transpose(0, 3, 1, 5, 4, 2).reshape(B, N_KV_HEADS, nq, 3, ROWS_Q)
    att_p = _nsa_prompt(q_p, kh_p, vth_p, kc_p, vct_p, gates_p, rel_bias_table)
    x1_p, h2_p, ti_p, tg_p = _merge(x_prompt, convn_p, att_p, gn_att[0], wo_b, mp[2], mp[3], mp[4], ln2_g[0],
                                    rw_pad, rb_pad, seq_mode=True, tm=tm, n_exp=n_exp)

    xs_rows = x_sample.reshape(1, BS, D)
    prev2 = state_conv[0][:, 0, :].reshape(1, BS, cw_width)
    prev1 = state_conv[0][:, 1, :].reshape(1, BS, cw_width)
    tms = _row_tile(BS)
    convn_s, q_s, kvc_s, kvs_s, kvw_s, gate_s, v_s = _inproj(
        xs_rows, ms[0], ms[1], ln1_g[0], w_pad, conv_w[0], gn_conv[0], prev2, prev1, seq_mode=False, tm=tms)
    n_cmp = (past_len + TS - CMP_LEN) // CMP_STRIDE + 1
    nch_s = past_len // CMP_STRIDE
    assert n_cmp + 1 == nch_s
    ab_s = _cmp_ab_t(cache_kv_cmp[0].transpose(0, 2, 3, 4, 1), page_table, w_blk, math.gcd(n_pages, 32))
    kc_s, vct_s = _cmp_finish(ab_s, cmp_pe[0], cmp_w1[0], cmp_w2[0])
    t_q = past_len
    n_slc = -(-(past_len + TS) // SLC_BLOCK)
    cur = t_q // SLC_BLOCK
    k_sel = min(N_SEL, n_slc)
    tbl = rel_bias_table.reshape(N_BUCKETS, N_KV_HEADS, GROUP)
    pos_c = jnp.arange(nch_s) * CMP_STRIDE + CMP_LEN - 1
    bias_c = tbl[_rel_bucket(t_q - pos_c)].transpose(1, 2, 0)
    ncol = -(-n_slc // LANE) * LANE
    mm_s = _score_matrix(nch_s, ncol)
    q_s_hm = q_s.reshape(N_HEADS, BS, HEAD_DIM).transpose(1, 0, 2).astype(F32)
    o_c_s, idx_pad = _smp_cmp(q_s_hm, kc_s, vct_s, bias_c, mm_s, n_cmp, n_slc, cur)
    idx = idx_pad[:, :, :k_sel]
    blk_per_page = page // SLC_BLOCK
    pg = jnp.minimum(idx // blk_per_page, n_pages - 1)
    phys = jnp.take_along_axis(page_table, pg.reshape(BS, -1), axis=1).astype(jnp.int32)
    pos_s = pg[..., None] * page + jnp.arange(page)
    in_blk = (pos_s // SLC_BLOCK == idx[..., None]) & (pos_s < past_len)
    pos_s = pos_s.reshape(BS, N_KV_HEADS, k_sel * page)
    bucket_hot = (_rel_bucket(t_q - pos_s)[..., None] == jnp.arange(N_BUCKETS)).astype(F32)
    bias_sel = jnp.einsum('bksn,nkg->bkgs', bucket_hot, tbl, precision=lax.Precision.HIGHEST)
    new_sel = jnp.any(idx == cur, axis=-1, keepdims=True)
    mask_sel = jnp.concatenate([in_blk.reshape(BS, N_KV_HEADS, k_sel * page), new_sel], axis=-1)
    mask_sel = mask_sel.astype(F32)[:, :, None, :]
    nw = state_kv_win.shape[2]
    bias_win = tbl[_rel_bucket(nw - jnp.arange(nw))].transpose(1, 2, 0)
    bias_new = tbl[0].reshape(N_KV_HEADS, GROUP, 1)
    new_rows = jnp.stack([kvs_s[0], kvw_s[0]], axis=1)
    gates_s = gate_s[0, :, :3 * N_HEADS].reshape(BS, N_HEADS, 3)
    pool_t = cache_kv_slc[0].transpose(0, 2, 3, 4, 1)
    win_t = state_kv_win[0].transpose(0, 2, 3, 4, 1)
    att_s = _smp_att(phys.reshape(-1), q_s_hm, pool_t, bias_sel, mask_sel, new_rows,
                     win_t, bias_win, bias_new, gates_s, o_c_s, k_sel)
    att_s = att_s.reshape(1, BS, N_HEADS * HEAD_DIM)
    x1_s, h2_s, ti_s, tg_s = _merge(xs_rows, convn_s, att_s, gn_att[0], wo_b, ms[2], ms[3], ms[4], ln2_g[0],
                                    rw_pad, rb_pad, seq_mode=False, tm=tms, n_exp=n_exp)

    n_tok = B * T + BS
    h2_all = jnp.concatenate([h2_p.reshape(B * T, D), h2_s.reshape(BS, D), jnp.zeros((8, D), F32)], axis=0)
    top_idx = jnp.concatenate([ti_p.reshape(B * T, LANE), ti_s.reshape(BS, LANE)], axis=0)[:, :TOP_K]
    row_tok, dest, block_e, n_used = _route(top_idx, n_tok, n_exp)
    xs = _rows(h2_all, row_tok)
    yb = _moe_experts(block_e, n_used, xs, w_gu[0], b_gu[0], w_down[0], b_down[0])
    dest_p = dest[:B * T].reshape(B, T, TOP_K).transpose(0, 2, 1)
    yg_p = _rows(yb, dest_p.reshape(-1)).reshape(B, TOP_K, T, D)
    yg_s = _rows(yb, dest[B * T:].T.reshape(-1)).reshape(1, TOP_K, BS, D)
    y_p = _final(x1_p, yg_p, tg_p, mp[5], final_g, seq_mode=True, tm=tm)
    y_s = _final(x1_s, yg_s, tg_s, ms[5], final_g, seq_mode=False, tm=tms)

    kv_tail = (2, N_KV_HEADS, HEAD_DIM)
    page_shape = (depth, B, T // page, page) + kv_tail
    w_keep = min(WINDOW, T)
    new_win_s = jnp.concatenate([state_kv_win[0][:, TS:], kvw_s.reshape(BS, TS, *kv_tail)], axis=1)
    new_conv_s = jnp.concatenate([state_conv[0][:, TS:], v_s.reshape(BS, TS, cw_width)], axis=1)
    return (y_p, y_s.reshape(BS, TS, D),
            kvc_p.reshape(page_shape), kvc_s.reshape((depth, BS, TS) + kv_tail),
            kvs_p.reshape(page_shape), kvs_s.reshape((depth, BS, TS) + kv_tail),
            kvw_p[:, T - w_keep:].reshape((depth, B, w_keep) + kv_tail), new_win_s[None],
            vlast_p[None], new_conv_s[None])
```

```python
import functools
import math

import numpy as np
import jax
import jax.numpy as jnp
from jax import lax
from jax.experimental import pallas as pl
from jax.experimental.pallas import tpu as pltpu

F32 = jnp.float32
BF16 = jnp.bfloat16

CONV_K = 3
N_HEADS = 8
N_KV_HEADS = 2
GROUP = N_HEADS // N_KV_HEADS
HEAD_DIM = 64
KV_WIDTH = N_KV_HEADS * HEAD_DIM
CMP_LEN = 32
CMP_STRIDE = 16
SLC_BLOCK = 64
RATIO = SLC_BLOCK // CMP_STRIDE
N_SEL = 16
WINDOW = 512
Q_BLOCK = 128
N_BUCKETS = 32
REL_MAX_DIST = 128
TOP_K = 4
SWIGLU_LIMIT = 7.0
SWIGLU_ALPHA = 1.702
EPS = 1e-6
NEG_INF = -1e30
FORCE_SCORE = 1e4

LANE = 128
ROWS_Q = GROUP * Q_BLOCK
KEY_CHUNK = 512
TILES = KEY_CHUNK // Q_BLOCK
MOE_ROWS = 512
VMEM_LIMIT = 48 * 1024 * 1024


def _cparams(n_axes, vmem_limit=VMEM_LIMIT):
    return pltpu.CompilerParams(dimension_semantics=("arbitrary",) * n_axes, vmem_limit_bytes=vmem_limit)


def _dot(a, b):
    return jnp.dot(a, b, preferred_element_type=F32)


def _dot_nt(a, b):
    return lax.dot_general(a, b, (((1,), (1,)), ((), ())), preferred_element_type=F32)


def _dot_tn(a, b):
    return lax.dot_general(a, b, (((0,), (0,)), ((), ())), preferred_element_type=F32)


def _split2(x):
    hi = x.astype(BF16)
    lo = (x - hi.astype(F32)).astype(BF16)
    return hi, lo


def _split3(x):
    a = x.astype(BF16)
    r = x - a.astype(F32)
    b = r.astype(BF16)
    c = (r - b.astype(F32)).astype(BF16)
    return a, b, c


def _dot3(a, b):
    ah, al = _split2(a)
    bh, bl = _split2(b)
    return _dot(ah, bh) + _dot(ah, bl) + _dot(al, bh)


def _sigmoid(x):
    return 1.0 / (1.0 + jnp.exp(-x))


def _rms(x, g):
    return x * lax.rsqrt(jnp.mean(x * x, axis=-1, keepdims=True) + EPS) * g


def _rel_bucket(dist):
    n = jnp.maximum(dist, 0)
    max_exact = N_BUCKETS // 2
    large = max_exact + (jnp.log(jnp.maximum(n, 1).astype(F32) / max_exact)
                         / math.log(REL_MAX_DIST / max_exact) * (N_BUCKETS - max_exact)).astype(jnp.int32)
    return jnp.where(n < max_exact, n, jnp.minimum(large, N_BUCKETS - 1))


def _mod_kernel(c_ref, w_ref, b_ref, o_ref):
    c = c_ref[...]
    o_ref[...] = _dot3(c * _sigmoid(c), w_ref[...]) + b_ref[...]


def _modulation(c, w_ada, b_ada):
    n, d = c.shape
    cols = w_ada.shape[1]
    bn = 1536
    return pl.pallas_call(
        _mod_kernel,
        out_shape=jax.ShapeDtypeStruct((n, cols), F32),
        grid=(cols // bn,),
        in_specs=[pl.BlockSpec((n, d), lambda i: (0, 0)),
                  pl.BlockSpec((d, bn), lambda i: (0, i)),
                  pl.BlockSpec((1, bn), lambda i: (0, i))],
        out_specs=pl.BlockSpec((n, bn), lambda i: (0, i)),
        compiler_params=_cparams(1),
        name="modulation",
    )(c, w_ada, b_ada.reshape(1, cols))


def _inproj_kernel(x_ref, m0_ref, m1_ref, g1_ref, w_ref, cw_ref, gnc_ref, pa_ref, pb_ref,
                   convn_ref, q_ref, kvc_ref, kvs_ref, kvw_ref, gate_ref, vlast_ref, *rest, seq_mode, tm, cw_width):
    if seq_mode:
        kh_ref, vth_ref, carry_ref = rest
    x = x_ref[0]
    h = _rms(x, g1_ref[...]) * (1.0 + m1_ref[0]) + m0_ref[0]
    hb = h.astype(BF16)
    c3 = 3 * cw_width
    uc = _dot(hb, w_ref[:, 0:c3])
    b_g = uc[:, 0:cw_width]
    v = uc[:, cw_width:2 * cw_width] * uc[:, 2 * cw_width:c3]
    if seq_mode:
        @pl.when(pl.program_id(1) == 0)
        def _():
            carry_ref[0:2, :] = pa_ref[0]
        c0 = carry_ref[0:1, :]
        c1 = carry_ref[1:2, :]
        row = lax.broadcasted_iota(jnp.int32, (tm, 1), 0)
        vm1 = jnp.where(row == 0, c1, pltpu.roll(v, 1, 0))
        vm2 = jnp.where(row == 0, c0, jnp.where(row == 1, c1, pltpu.roll(v, 2, 0)))
        carry_ref[0:2, :] = v[tm - 2:tm, :]
        vlast_ref[0] = v[tm - 2:tm, :]
    else:
        vm2 = pa_ref[0]
        vm1 = pb_ref[0]
        vlast_ref[0] = v
    cw = cw_ref[...]
    y = cw[0:1, :] * vm2 + cw[1:2, :] * vm1 + cw[2:3, :] * v
    convn_ref[0] = _rms(b_g * y, gnc_ref[...]).astype(BF16)

    aw = N_HEADS * HEAD_DIM
    uq = _dot(hb, w_ref[:, c3:c3 + aw]) * (HEAD_DIM ** -0.5)
    if seq_mode:
        uqt = uq.T
        for k in range(N_KV_HEADS):
            for jj in range(tm // Q_BLOCK):
                q_ref[0, k, jj] = jnp.concatenate(
                    [uqt[(k * GROUP + g) * HEAD_DIM:(k * GROUP + g + 1) * HEAD_DIM, jj * Q_BLOCK:(jj + 1) * Q_BLOCK]
                     for g in range(GROUP)], axis=1).astype(BF16)
    else:
        for hh in range(N_HEADS):
            q_ref[0, hh] = uq[:, hh * HEAD_DIM:(hh + 1) * HEAD_DIM].astype(BF16)
    kv0 = c3 + aw
    kvw3 = 3 * 2 * KV_WIDTH
    ukv = _dot(hb, w_ref[:, kv0:kv0 + kvw3])
    kvc_ref[0] = ukv[:, 0:2 * KV_WIDTH]
    kvs_ref[0] = ukv[:, 2 * KV_WIDTH:4 * KV_WIDTH]
    kvw_ref[0] = ukv[:, 4 * KV_WIDTH:6 * KV_WIDTH]
    if seq_mode:
        for br in range(2):
            c0 = (br + 1) * 2 * KV_WIDTH
            for k in range(N_KV_HEADS):
                kh_ref[0, br * N_KV_HEADS + k] = ukv[:, c0 + k * HEAD_DIM:c0 + (k + 1) * HEAD_DIM].astype(BF16)
            vt = ukv[:, c0 + KV_WIDTH:c0 + 2 * KV_WIDTH].T.astype(BF16)
            for k in range(N_KV_HEADS):
                vth_ref[0, br * N_KV_HEADS + k] = vt[k * HEAD_DIM:(k + 1) * HEAD_DIM, :]
    ug = _dot(hb, w_ref[:, kv0 + kvw3:kv0 + kvw3 + LANE])
    gate_ref[0] = _sigmoid(ug)


def _inproj(x, m0, m1, ln_g, w_pad, conv_w, gn_conv, pa, pb, *, seq_mode, tm):
    B, T, D = x.shape
    cw_width = conv_w.shape[1]
    nt = T // tm
    row_blk = lambda w: pl.BlockSpec((1, tm, w), lambda b, t: (b, t, 0))
    if seq_mode:
        mod_spec = pl.BlockSpec((1, 1, D), lambda b, t: (b, 0, 0))
        prev_spec = pl.BlockSpec((1, 2, cw_width), lambda b, t: (b, 0, 0))
        vlast_shape = jax.ShapeDtypeStruct((B, 2, cw_width), F32)
        vlast_spec = pl.BlockSpec((1, 2, cw_width), lambda b, t: (b, 0, 0))
    else:
        mod_spec = row_blk(D)
        prev_spec = row_blk(cw_width)
        vlast_shape = jax.ShapeDtypeStruct((B, T, cw_width), F32)
        vlast_spec = row_blk(cw_width)
    const = lambda shape: pl.BlockSpec(shape, lambda b, t: (0,) * len(shape))
    kern = functools.partial(_inproj_kernel, seq_mode=seq_mode, tm=tm, cw_width=cw_width)
    if seq_mode:
        assert tm % Q_BLOCK == 0
        q_shape = jax.ShapeDtypeStruct((B, N_KV_HEADS, T // Q_BLOCK, HEAD_DIM, ROWS_Q), BF16)
        q_spec = pl.BlockSpec((1, N_KV_HEADS, tm // Q_BLOCK, HEAD_DIM, ROWS_Q), lambda b, t: (b, 0, t, 0, 0))
    else:
        q_shape = jax.ShapeDtypeStruct((B, N_HEADS, T, HEAD_DIM), BF16)
        q_spec = pl.BlockSpec((1, N_HEADS, tm, HEAD_DIM), lambda b, t: (b, 0, t, 0))
    out_shape = [jax.ShapeDtypeStruct((B, T, cw_width), BF16),
                 q_shape,
                 jax.ShapeDtypeStruct((B, T, 2 * KV_WIDTH), F32),
                 jax.ShapeDtypeStruct((B, T, 2 * KV_WIDTH), F32),
                 jax.ShapeDtypeStruct((B, T, 2 * KV_WIDTH), F32),
                 jax.ShapeDtypeStruct((B, T, LANE), F32),
                 vlast_shape]
    out_specs = [row_blk(cw_width),
                 q_spec,
                 row_blk(2 * KV_WIDTH), row_blk(2 * KV_WIDTH), row_blk(2 * KV_WIDTH),
                 row_blk(LANE), vlast_spec]
    scratch = []
    if seq_mode:
        n_att = 2 * N_KV_HEADS
        out_shape += [jax.ShapeDtypeStruct((B, n_att, T, HEAD_DIM), BF16),
                      jax.ShapeDtypeStruct((B, n_att, HEAD_DIM, T), BF16)]
        out_specs += [pl.BlockSpec((1, n_att, tm, HEAD_DIM), lambda b, t: (b, 0, t, 0)),
                      pl.BlockSpec((1, n_att, HEAD_DIM, tm), lambda b, t: (b, 0, 0, t))]
        scratch = [pltpu.VMEM((8, cw_width), F32)]
    return pl.pallas_call(
        kern,
        out_shape=tuple(out_shape),
        grid=(B, nt),
        in_specs=[row_blk(D), mod_spec, mod_spec, const((1, D)), const(w_pad.shape), const(conv_w.shape),
                  const((1, cw_width)), prev_spec, prev_spec],
        out_specs=tuple(out_specs),
        scratch_shapes=scratch,
        compiler_params=_cparams(2),
        name="inproj_seq" if seq_mode else "inproj_rows",
    )(x, m0, m1, ln_g.reshape(1, D), w_pad, conv_w, gn_conv.reshape(1, cw_width), pa, pb)


def _cmp_ab_kernel(pt_ref, *refs, G):
    pages = refs[:2 * G]
    w_ref, out_ref, x_ref = refs[2 * G], refs[2 * G + 1], refs[2 * G + 2]
    half = KV_WIDTH
    for j in range(G):
        for r in range(CMP_STRIDE):
            for c in range(2):
                x_ref[c, j * 8:(j + 1) * 8, r * half:(r + 1) * half] = (
                    pages[2 * j + c][0, pl.ds(r, 8, stride=CMP_STRIDE), :])
    wcols = w_ref.shape[2]
    for c in range(2):
        out_ref[0, :, c * wcols:(c + 1) * wcols] = _dot(x_ref[c].astype(BF16), w_ref[c])


def _cmp_ab(pool, page_table, w_blk, G):
    P, page, width = pool.shape
    B, n_pages = page_table.shape
    cpp = page // CMP_STRIDE
    assert cpp == 8 and n_pages % G == 0
    kdim = CMP_STRIDE * width // 2

    def pg_spec(j, c):
        return pl.BlockSpec((1, page, width // 2), lambda b, g, pt: (pt[b * n_pages + g * G + j], 0, c))

    return pl.pallas_call(
        functools.partial(_cmp_ab_kernel, G=G),
        out_shape=jax.ShapeDtypeStruct((B, n_pages * cpp, 2 * w_blk.shape[2]), F32),
        grid_spec=pltpu.PrefetchScalarGridSpec(
            num_scalar_prefetch=1,
            grid=(B, n_pages // G),
            in_specs=[pg_spec(j, c) for j in range(G) for c in range(2)]
            + [pl.BlockSpec(w_blk.shape, lambda b, g, pt: (0, 0, 0))],
            out_specs=pl.BlockSpec((1, G * cpp, 2 * w_blk.shape[2]), lambda b, g, pt: (b, g, 0)),
            scratch_shapes=[pltpu.VMEM((2, G * cpp, kdim), F32)]),
        compiler_params=_cparams(2),
        name="cmp_partial",
    )(page_table.reshape(-1).astype(jnp.int32), *([pool] * (2 * G)), w_blk)


def _cmp_ab_t_kernel(pt_ref, *refs, G):
    pages = refs[:G]
    w_ref, out_ref, x_ref, s_ref = refs[G], refs[G + 1], refs[G + 2], refs[G + 3]
    half = KV_WIDTH
    for j in range(G):
        for c in range(2):
            for k in range(N_KV_HEADS):
                s_ref[c, j, :, k * HEAD_DIM:(k + 1) * HEAD_DIM] = pages[j][0, c, k].astype(BF16).T.astype(F32)
    for j in range(G):
        for r in range(CMP_STRIDE):
            for c in range(2):
                x_ref[c, j * 8:(j + 1) * 8, r * half:(r + 1) * half] = (
                    s_ref[c, j, pl.ds(r, 8, stride=CMP_STRIDE), :])
    wcols = w_ref.shape[2]
    for c in range(2):
        out_ref[0, :, c * wcols:(c + 1) * wcols] = _dot(x_ref[c].astype(BF16), w_ref[c])


def _cmp_ab_t(pool_t, page_table, w_blk, G):
    P, _, _, hd, page = pool_t.shape
    B, n_pages = page_table.shape
    cpp = page // CMP_STRIDE
    assert cpp == 8 and n_pages % G == 0
    kdim = CMP_STRIDE * KV_WIDTH

    def pg_spec(j):
        return pl.BlockSpec((1,) + pool_t.shape[1:], lambda b, g, pt: (pt[b * n_pages + g * G + j], 0, 0, 0, 0))

    return pl.pallas_call(
        functools.partial(_cmp_ab_t_kernel, G=G),
        out_shape=jax.ShapeDtypeStruct((B, n_pages * cpp, 2 * w_blk.shape[2]), F32),
        grid_spec=pltpu.PrefetchScalarGridSpec(
            num_scalar_prefetch=1,
            grid=(B, n_pages // G),
            in_specs=[pg_spec(j) for j in range(G)] + [pl.BlockSpec(w_blk.shape, lambda b, g, pt: (0, 0, 0))],
            out_specs=pl.BlockSpec((1, G * cpp, 2 * w_blk.shape[2]), lambda b, g, pt: (b, g, 0)),
            scratch_shapes=[pltpu.VMEM((2, G * cpp, kdim), F32), pltpu.VMEM((2, G, page, KV_WIDTH), F32)]),
        compiler_params=_cparams(2),
        name="cmp_partial_t",
    )(page_table.reshape(-1).astype(jnp.int32), *([pool_t] * G), w_blk)


def _gelu_tanh(x):
    return 0.5 * x * (1.0 + jnp.tanh(math.sqrt(2.0 / math.pi) * (x + 0.044715 * (x * x * x))))


def _cmp_fin_kernel(ab_ref, pe_ref, w1_ref, w2_ref, kc_ref, vct_ref, *, nch):
    for c in range(2):
        pe_t = _dot(pe_ref[c:c + 1, :].astype(BF16), w1_ref[c].astype(BF16))
        w2 = w2_ref[c].astype(BF16)
        for k in range(N_KV_HEADS):
            base = (c * N_KV_HEADS + k) * 2 * HEAD_DIM
            slab = ab_ref[0, :, base:base + 2 * HEAD_DIM]
            nxt = pltpu.roll(slab, nch - 1, 0)
            pre = slab[:, 0:HEAD_DIM] + nxt[:, HEAD_DIM:2 * HEAD_DIM] + pe_t
            blocks = _dot(_gelu_tanh(pre).astype(BF16), w2)
            if c == 0:
                kc_ref[0, k] = blocks.astype(BF16)
            else:
                vct_ref[0, k] = blocks.T.astype(BF16)


def _cmp_finish(ab, cmp_pe, cmp_w1, cmp_w2):
    B, nch, w = ab.shape
    pe = cmp_pe.reshape(2, CMP_LEN * HEAD_DIM)
    return pl.pallas_call(
        functools.partial(_cmp_fin_kernel, nch=nch),
        out_shape=(jax.ShapeDtypeStruct((B, N_KV_HEADS, nch, HEAD_DIM), BF16),
                   jax.ShapeDtypeStruct((B, N_KV_HEADS, HEAD_DIM, nch), BF16)),
        grid=(B,),
        in_specs=[pl.BlockSpec((1, nch, w), lambda b: (b, 0, 0)),
                  pl.BlockSpec(pe.shape, lambda b: (0, 0)),
                  pl.BlockSpec(cmp_w1.shape, lambda b: (0, 0, 0)),
                  pl.BlockSpec(cmp_w2.shape, lambda b: (0, 0, 0))],
        out_specs=(pl.BlockSpec((1, N_KV_HEADS, nch, HEAD_DIM), lambda b: (b, 0, 0, 0)),
                   pl.BlockSpec((1, N_KV_HEADS, HEAD_DIM, nch), lambda b: (b, 0, 0, 0))),
        compiler_params=_cparams(1),
        name="cmp_finish",
    )(ab, pe, cmp_w1, cmp_w2)


def _cmp_block_weight(cmp_w1):
    hid = cmp_w1.shape[2]
    w = cmp_w1.reshape(2, 2, CMP_STRIDE, HEAD_DIM, hid)
    eye = jnp.eye(N_KV_HEADS, dtype=cmp_w1.dtype)
    wb = jnp.einsum('cardh,kj->crkdjah', w, eye)
    return wb.reshape(2, CMP_STRIDE * N_KV_HEADS * HEAD_DIM, N_KV_HEADS * 2 * hid).astype(BF16)


def _score_matrix(nch, n_slc):
    i = np.arange(nch)[:, None]
    j = np.arange(n_slc)[None, :]
    m = 2.0 * ((i // RATIO == j) & (i % RATIO < RATIO - 1)) + 1.0 * (i == RATIO * j + RATIO - 1) \
        + 1.0 * (i == RATIO * j - 1)
    return jnp.asarray(m, dtype=BF16)


def _nsa_kernel(q_ref, kc_ref, vc_ref, ks_ref, vs_ref, kw_ref, vw_ref, gate_ref, sa_ref, sw_ref,
                cs_ref, mm_ref, e_ref, o_ref, m_ref, l_ref, acc_ref, sta_ref, stb_ref, stc_ref, std_ref,
                *, nch, n_slc):
    j = pl.program_id(2)
    q0 = pl.multiple_of(j * Q_BLOCK, Q_BLOCK)
    cq = lax.shift_right_logical(j, 2)
    jm = jnp.bitwise_and(j, TILES - 1)
    qt = q_ref[0, 0, 0]
    lane_q = jnp.bitwise_and(lax.broadcasted_iota(jnp.int32, (1, ROWS_Q), 1), Q_BLOCK - 1)
    tq = q0 + lane_q
    has_prev = cq >= 1
    k_own = pl.multiple_of(cq * KEY_CHUNK, KEY_CHUNK)
    k_prev = pl.multiple_of(jnp.maximum(cq - 1, 0) * KEY_CHUNK, KEY_CHUNK)
    own0 = pl.multiple_of((2 * TILES - 1 - jm) * Q_BLOCK, Q_BLOCK)
    prev0 = pl.multiple_of(jnp.where(has_prev, TILES - 1 - jm, 2 * TILES) * Q_BLOCK, Q_BLOCK)
    prevw0 = pl.multiple_of(jnp.where(has_prev, TILES - 1 - jm, 2 * TILES - 1) * Q_BLOCK, Q_BLOCK)

    stc_ref[...] = _dot(kw_ref[0, 0, pl.ds(k_own, KEY_CHUNK), :], qt) + sa_ref[0, pl.ds(own0, KEY_CHUNK), :]
    std_ref[...] = _dot(kw_ref[0, 0, pl.ds(k_prev, KEY_CHUNK), :], qt) + sw_ref[0, pl.ds(prevw0, KEY_CHUNK), :]

    per_q = Q_BLOCK // CMP_STRIDE
    c0 = pl.multiple_of((pl.num_programs(2) - 1 - j) * per_q, per_q)
    lc = _dot(kc_ref[0, 0], qt) + cs_ref[0, pl.ds(c0, nch), :]
    p = jnp.exp(lc - jnp.max(lc, axis=0, keepdims=True))
    anyc = tq >= CMP_LEN - 1
    pc = p * jnp.where(anyc, 1.0 / jnp.sum(p, axis=0, keepdims=True), 0.0)
    o_c = _dot(vc_ref[0, 0], pc.astype(BF16))

    imp = pc[:, 0:Q_BLOCK]
    for g in range(1, GROUP):
        imp = imp + pc[:, g * Q_BLOCK:(g + 1) * Q_BLOCK]
    mm = mm_ref[...]
    i1, i2, i3 = _split3(imp)
    score = _dot(mm, i1) + _dot(mm, i2) + _dot(mm, i3)
    blk = lax.broadcasted_iota(jnp.int32, (n_slc, 1), 0)
    cur = lax.shift_right_logical(q0 + lax.broadcasted_iota(jnp.int32, (1, Q_BLOCK), 1), 6)
    forced = (blk == 0) | (blk == cur) | (blk == cur - 1)
    sc = jnp.where(forced, FORCE_SCORE, jnp.where(blk <= cur, score, NEG_INF))
    big = jnp.int32(1 << 30)
    sel = jnp.zeros(sc.shape, F32)
    for _ in range(min(N_SEL, n_slc)):
        mx = jnp.max(sc, axis=0, keepdims=True)
        first = jnp.min(jnp.where(sc == mx, blk, big), axis=0, keepdims=True)
        pick = blk == first
        sel = jnp.where(pick, 1.0, sel)
        sc = jnp.where(pick, -3e38, sc)
    pen = ((sel - 1.0) * (-NEG_INF)).astype(BF16)
    qsel = jnp.concatenate([jnp.concatenate([pen] * GROUP, axis=1), qt], axis=0)

    def scores_sel(k0):
        ke = jnp.concatenate([e_ref[pl.ds(k0, KEY_CHUNK), :], ks_ref[0, 0, pl.ds(k0, KEY_CHUNK), :]], axis=1)
        return _dot(ke, qsel)

    def flash(st, v, first):
        mx = jnp.max(st, axis=0, keepdims=True)
        if first:
            pe = jnp.exp(st - mx)
            l_ref[...] = jnp.sum(pe, axis=0, keepdims=True)
            acc_ref[...] = _dot(v, pe.astype(BF16))
            m_ref[...] = mx
        else:
            m_old = m_ref[...]
            m_new = jnp.maximum(m_old, mx)
            a = jnp.exp(m_old - m_new)
            pe = jnp.exp(st - m_new)
            l_ref[...] = a * l_ref[...] + jnp.sum(pe, axis=0, keepdims=True)
            acc_ref[...] = a * acc_ref[...] + _dot(v, pe.astype(BF16))
            m_ref[...] = m_new

    sta_ref[...] = scores_sel(k_own) + sa_ref[0, pl.ds(own0, KEY_CHUNK), :]
    stb_ref[...] = scores_sel(k_prev) + sa_ref[0, pl.ds(prev0, KEY_CHUNK), :]
    flash(stc_ref[...], vw_ref[0, 0, :, pl.ds(k_own, KEY_CHUNK)], True)
    flash(std_ref[...], vw_ref[0, 0, :, pl.ds(k_prev, KEY_CHUNK)], False)
    o_w = acc_ref[...] * (1.0 / l_ref[...])

    flash(sta_ref[...], vs_ref[0, 0, :, pl.ds(k_own, KEY_CHUNK)], True)
    n_far = jnp.maximum(cq - 1, 0)
    last_chunk = ks_ref.shape[2] // KEY_CHUNK - 1

    def chunk_start(c):
        return pl.multiple_of(jnp.minimum(c, last_chunk) * KEY_CHUNK, KEY_CHUNK)

    sta_ref[...] = scores_sel(chunk_start(0))
    flash(stb_ref[...], vs_ref[0, 0, :, pl.ds(k_prev, KEY_CHUNK)], False)

    def far_pair(i, carry):
        c0 = 2 * i
        stb_ref[...] = scores_sel(chunk_start(c0 + 1))
        flash(sta_ref[...], vs_ref[0, 0, :, pl.ds(chunk_start(c0), KEY_CHUNK)], False)
        sta_ref[...] = scores_sel(chunk_start(c0 + 2))
        flash(stb_ref[...], vs_ref[0, 0, :, pl.ds(chunk_start(c0 + 1), KEY_CHUNK)], False)
        return carry

    lax.fori_loop(0, lax.shift_right_logical(n_far, 1), far_pair, 0)

    @pl.when(jnp.bitwise_and(n_far, 1) == 1)
    def _():
        flash(sta_ref[...], vs_ref[0, 0, :, pl.ds(chunk_start(n_far - 1), KEY_CHUNK)], False)

    o_s = acc_ref[...] * (1.0 / l_ref[...])

    g3 = gate_ref[0, 0, 0]
    o = g3[0:1, :] * o_c + g3[1:2, :] * o_s + g3[2:3, :] * o_w
    stacked = jnp.concatenate([o[:, g * Q_BLOCK:(g + 1) * Q_BLOCK] for g in range(GROUP)], axis=0)
    o_ref[0] = stacked.T


def _bias_tables(table, nch, nq):
    tbl = table.reshape(N_BUCKETS, N_KV_HEADS, GROUP)
    far = tbl[N_BUCKETS - 1]
    i = jnp.arange(Q_BLOCK)

    def cols(rel):
        hot = (_rel_bucket(rel)[..., None] == jnp.arange(N_BUCKETS)).astype(F32)
        b = jnp.einsum('kqn,nhg->kqhg', hot, tbl, precision=lax.Precision.HIGHEST) - far
        b = jnp.where((rel >= 0)[:, :, None, None], b, 0.0)
        return b.transpose(2, 0, 3, 1).reshape(N_KV_HEADS, rel.shape[0], ROWS_Q)

    p0 = cols(i[None, :] - i[:, None])
    p1 = cols(Q_BLOCK + i[None, :] - i[:, None])
    per_q = Q_BLOCK // CMP_STRIDE
    rel_c = i[None, :] - CMP_STRIDE * (jnp.arange(nch + per_q * (nq - 1))[:, None] - per_q * (nq - 1)) - (CMP_LEN - 1)
    strip_cmp = jnp.where(jnp.tile(rel_c >= 0, (1, GROUP))[None], cols(rel_c), NEG_INF)
    kk = jnp.arange(Q_BLOCK)[:, None]
    qq = jnp.tile(jnp.arange(Q_BLOCK), GROUP)[None, :]
    bc = lambda a: jnp.broadcast_to(a[None], (N_KV_HEADS, Q_BLOCK, ROWS_Q))
    neg = bc(jnp.full((Q_BLOCK, ROWS_Q), NEG_INF, F32))
    zero = bc(jnp.zeros((Q_BLOCK, ROWS_Q), F32))
    p0_causal = jnp.where((kk <= qq)[None], p0, NEG_INF)
    in_window = bc(jnp.where(kk > qq, 0.0, NEG_INF).astype(F32))
    strip_own = jnp.concatenate([zero] * (2 * TILES - 2) + [p1, p0_causal] + [neg] * TILES, axis=1)
    strip_win = jnp.concatenate([neg] * (TILES - 1) + [in_window] + [zero] * (TILES - 2) + [p1] + [neg] * TILES,
                                axis=1)
    return strip_own, strip_win, strip_cmp


def _nsa_prompt(q_t, k_hm, vt_hm, kc, vct, gates_t, table):
    B, _, T, hd = k_hm.shape
    H = N_HEADS
    nch = kc.shape[2]
    n_slc = T // SLC_BLOCK
    nq = T // Q_BLOCK
    assert T % KEY_CHUNK == 0 and nch == RATIO * n_slc and WINDOW <= KEY_CHUNK
    strip_own, strip_win, strip_cmp = _bias_tables(table, nch, nq)
    e_tab = (jnp.arange(T)[:, None] // SLC_BLOCK == jnp.arange(n_slc)[None, :]).astype(BF16)
    mm = _score_matrix(nch, n_slc).T
    nk = N_KV_HEADS
    k_spec = lambda slab0: pl.BlockSpec((1, 1, T, hd), lambda b, k, j: (b, slab0 + k, 0, 0))
    vt_spec = lambda slab0: pl.BlockSpec((1, 1, hd, T), lambda b, k, j: (b, slab0 + k, 0, 0))
    per_head = lambda a: pl.BlockSpec((1,) + a.shape[1:], lambda b, k, j: (k, 0, 0))
    score_buf = pltpu.VMEM((KEY_CHUNK, ROWS_Q), F32)
    return pl.pallas_call(
        functools.partial(_nsa_kernel, nch=nch, n_slc=n_slc),
        out_shape=jax.ShapeDtypeStruct((B, T, H * hd), F32),
        grid=(B, nk, nq),
        in_specs=[pl.BlockSpec((1, 1, 1, hd, ROWS_Q), lambda b, k, j: (b, k, j, 0, 0)),
                  pl.BlockSpec((1, 1, nch, hd), lambda b, k, j: (b, k, 0, 0)),
                  pl.BlockSpec((1, 1, hd, nch), lambda b, k, j: (b, k, 0, 0)),
                  k_spec(0), vt_spec(0), k_spec(nk), vt_spec(nk),
                  pl.BlockSpec((1, 1, 1, 3, ROWS_Q), lambda b, k, j: (b, k, j, 0, 0)),
                  per_head(strip_own), per_head(strip_win), per_head(strip_cmp),
                  pl.BlockSpec(mm.shape, lambda b, k, j: (0, 0)),
                  pl.BlockSpec(e_tab.shape, lambda b, k, j: (0, 0))],
        out_specs=pl.BlockSpec((1, Q_BLOCK, GROUP * hd), lambda b, k, j: (b, j, k)),
        scratch_shapes=[pltpu.VMEM((1, ROWS_Q), F32), pltpu.VMEM((1, ROWS_Q), F32),
                        pltpu.VMEM((hd, ROWS_Q), F32), score_buf, score_buf, score_buf, score_buf],
        compiler_params=_cparams(3),
        name="nsa_prompt",
    )(q_t, kc, vct, k_hm, vt_hm, k_hm, vt_hm, gates_t, strip_own, strip_win, strip_cmp, mm, e_tab)


def _smp_cmp_kernel(q_ref, kc_ref, vct_ref, bias_ref, mm_ref, oc_ref, idx_ref, *, n_cmp, n_slc, cur):
    nch = kc_ref.shape[2]
    ncol = mm_ref.shape[1]
    coli = lax.broadcasted_iota(jnp.int32, (1, nch), 1)
    maskc = coli < n_cmp
    blk = lax.broadcasted_iota(jnp.int32, (1, ncol), 1)
    k_sel = min(N_SEL, n_slc)
    lane = lax.broadcasted_iota(jnp.int32, (1, LANE), 1)
    scores = []
    for kh in range(N_KV_HEADS):
        qm = q_ref[0, kh * GROUP:(kh + 1) * GROUP].astype(BF16)
        lc = _dot_nt(qm, kc_ref[0, kh]) + bias_ref[kh]
        lcm = jnp.where(maskc, lc, NEG_INF)
        p = jnp.where(maskc, jnp.exp(lcm - jnp.max(lcm, axis=-1, keepdims=True)), 0.0)
        pc = p * (1.0 / jnp.sum(p, axis=-1, keepdims=True))
        oc_ref[0, kh * GROUP:(kh + 1) * GROUP] = _dot_nt(pc.astype(BF16), vct_ref[0, kh])
        imp = pc[0:1]
        for g in range(1, GROUP):
            imp = imp + pc[g:g + 1]
        mm = mm_ref[...]
        i1, i2, i3 = _split3(imp)
        score = _dot(i1, mm) + _dot(i2, mm) + _dot(i3, mm)
        forced = (blk == 0) | (blk == cur) | (blk == cur - 1)
        sc = jnp.where(forced, FORCE_SCORE, jnp.where(blk <= cur, score, NEG_INF))
        scores.append(jnp.where(blk < n_slc, sc, -2e38))
    sc = jnp.concatenate(scores, axis=0)
    big = jnp.int32(1 << 30)
    out = jnp.zeros((N_KV_HEADS, LANE), jnp.int32)
    for it in range(k_sel):
        m = jnp.max(sc, axis=-1, keepdims=True)
        first = jnp.min(jnp.where(sc == m, blk, big), axis=-1, keepdims=True)
        out = jnp.where(lane == it, first, out)
        sc = jnp.where(blk == first, -3e38, sc)
    idx_ref[0] = out


def _smp_cmp(q_hm, kc, vct, bias_c, mm, n_cmp, n_slc, cur):
    B = q_hm.shape[0]
    nch = kc.shape[2]
    return pl.pallas_call(
        functools.partial(_smp_cmp_kernel, n_cmp=n_cmp, n_slc=n_slc, cur=cur),
        out_shape=(jax.ShapeDtypeStruct((B, N_HEADS, HEAD_DIM), F32),
                   jax.ShapeDtypeStruct((B, N_KV_HEADS, LANE), jnp.int32)),
        grid=(B,),
        in_specs=[pl.BlockSpec((1, N_HEADS, HEAD_DIM), lambda b: (b, 0, 0)),
                  pl.BlockSpec((1, N_KV_HEADS, nch, HEAD_DIM), lambda b: (b, 0, 0, 0)),
                  pl.BlockSpec((1, N_KV_HEADS, HEAD_DIM, nch), lambda b: (b, 0, 0, 0)),
                  pl.BlockSpec(bias_c.shape, lambda b: (0, 0, 0)),
                  pl.BlockSpec(mm.shape, lambda b: (0, 0))],
        out_specs=(pl.BlockSpec((1, N_HEADS, HEAD_DIM), lambda b: (b, 0, 0)),
                   pl.BlockSpec((1, N_KV_HEADS, LANE), lambda b: (b, 0, 0))),
        compiler_params=_cparams(1),
        name="sample_cmp",
    )(q_hm, kc, vct, bias_c, mm)


def _smp_att_kernel(phys_ref, q_ref, pool_ref, bsel_ref, msel_ref, new_ref, win_ref, bwin_ref, bnew_ref,
                    gate_ref, oc_ref, o_ref, buf_ref, sem_ref, *, k_sel):
    b = pl.program_id(0)
    n_blk = N_KV_HEADS * k_sel

    def blk_copy(i):
        return pltpu.make_async_copy(pool_ref.at[phys_ref[b * n_blk + i]], buf_ref.at[i], sem_ref.at[i])

    for i in range(n_blk):
        blk_copy(i).start()
    for i in range(n_blk):
        blk_copy(i).wait()

    nw = win_ref.shape[4]
    for kh in range(N_KV_HEADS):
        rows = slice(kh * GROUP, (kh + 1) * GROUP)
        qm = q_ref[0, rows].astype(BF16)
        kcol = slice(kh * HEAD_DIM, (kh + 1) * HEAD_DIM)
        vcol = slice(KV_WIDTH + kh * HEAD_DIM, KV_WIDTH + (kh + 1) * HEAD_DIM)
        bnew = bnew_ref[kh]

        def branch(kt, vt, bias, mask, knew, vnew, new_ok):
            lg = _dot(qm, kt) + bias
            lg = jnp.where(mask, lg, NEG_INF)
            ln = jnp.sum(qm.astype(F32) * knew.astype(F32), axis=-1, keepdims=True) + bnew
            if new_ok is not None:
                ln = jnp.where(new_ok, ln, NEG_INF)
            mx = jnp.maximum(jnp.max(lg, axis=-1, keepdims=True), ln)
            pe = jnp.where(mask, jnp.exp(lg - mx), 0.0)
            pn = jnp.exp(ln - mx)
            if new_ok is not None:
                pn = jnp.where(new_ok, pn, 0.0)
            den = jnp.sum(pe, axis=-1, keepdims=True) + pn
            any_ok = den > 0.0
            inv = 1.0 / jnp.where(any_ok, den, 1.0)
            o = _dot_nt(pe.astype(BF16), vt) + pn.astype(BF16).astype(F32) * vnew.astype(F32)
            return jnp.where(any_ok, o * inv, 0.0)

        page = buf_ref.shape[4]
        nkeys = k_sel * page
        kt = jnp.concatenate([buf_ref[kh * k_sel + i, 0, kh] for i in range(k_sel)], axis=1).astype(BF16)
        vt = jnp.concatenate([buf_ref[kh * k_sel + i, 1, kh] for i in range(k_sel)], axis=1).astype(BF16)
        knew = new_ref[0, 0:1, kcol].astype(BF16)
        vnew = new_ref[0, 0:1, vcol].astype(BF16)
        msel = msel_ref[0, kh]
        o_s = branch(kt, vt, bsel_ref[0, kh], msel[:, 0:nkeys] > 0.5, knew, vnew, msel[:, nkeys:nkeys + 1] > 0.5)

        kw = win_ref[0, 0, kh].astype(BF16)
        vw = win_ref[0, 1, kh].astype(BF16)
        knw = new_ref[0, 1:2, kcol].astype(BF16)
        vnw = new_ref[0, 1:2, vcol].astype(BF16)
        dist = nw - lax.broadcasted_iota(jnp.int32, (1, nw), 1)
        o_w = branch(kw, vw, bwin_ref[kh], dist < WINDOW, knw, vnw, None)

        g3 = gate_ref[0, rows]
        o_ref[0, rows] = g3[:, 0:1] * oc_ref[0, rows] + g3[:, 1:2] * o_s + g3[:, 2:3] * o_w


def _smp_att(phys, q_hm, pool_blk, bias_sel, mask_sel, new_rows, win_buf, bias_win, bias_new, gates, o_c, k_sel):
    B = q_hm.shape[0]
    n_blk = N_KV_HEADS * k_sel
    full = lambda a: pl.BlockSpec(a.shape, lambda b, ph: (0,) * a.ndim)
    per_b = lambda a: pl.BlockSpec((1,) + a.shape[1:], lambda b, ph: (b,) + (0,) * (a.ndim - 1))
    return pl.pallas_call(
        functools.partial(_smp_att_kernel, k_sel=k_sel),
        out_shape=jax.ShapeDtypeStruct((B, N_HEADS, HEAD_DIM), F32),
        grid_spec=pltpu.PrefetchScalarGridSpec(
            num_scalar_prefetch=1,
            grid=(B,),
            in_specs=[per_b(q_hm), pl.BlockSpec(memory_space=pl.ANY), per_b(bias_sel), per_b(mask_sel),
                      per_b(new_rows), per_b(win_buf), full(bias_win), full(bias_new), per_b(gates), per_b(o_c)],
            out_specs=pl.BlockSpec((1, N_HEADS, HEAD_DIM), lambda b, ph: (b, 0, 0)),
            scratch_shapes=[pltpu.VMEM((n_blk,) + pool_blk.shape[1:], F32),
                            pltpu.SemaphoreType.DMA((n_blk,))]),
        compiler_params=_cparams(1),
        name="sample_att",
    )(phys, q_hm, pool_blk, bias_sel, mask_sel, new_rows, win_buf, bias_win, bias_new, gates, o_c)


def _merge_kernel(x_ref, cn_ref, att_ref, gna_ref, wo_ref, m2_ref, m3_ref, m4_ref, g2_ref, rw_ref, rb_ref,
                  x1_ref, h2_ref, ti_ref, tg_ref, *, n_exp, cw_width):
    att_n = _rms(att_ref[0], gna_ref[...]).astype(BF16)
    mix = _dot(cn_ref[0], wo_ref[0:cw_width, :]) + _dot(att_n, wo_ref[cw_width:, :])
    x1 = x_ref[0] + m2_ref[0] * mix
    x1_ref[0] = x1
    h2 = _rms(x1, g2_ref[...]) * (1.0 + m4_ref[0]) + m3_ref[0]
    h2_ref[0] = h2
    logits = _dot3(h2, rw_ref[...]) + rb_ref[...]
    lane = lax.broadcasted_iota(jnp.int32, (1, LANE), 1)
    sc = jnp.where(lane < n_exp, logits, -2e38)
    big = jnp.int32(1 << 30)
    ti = jnp.zeros(sc.shape, jnp.int32)
    tv = jnp.zeros(sc.shape, F32)
    v0 = None
    den = None
    for k in range(TOP_K):
        m = jnp.max(sc, axis=-1, keepdims=True)
        first = jnp.min(jnp.where(sc == m, lane, big), axis=-1, keepdims=True)
        if k == 0:
            v0 = m
        e = jnp.exp(m - v0)
        den = e if den is None else den + e
        ti = jnp.where(lane == k, first, ti)
        tv = jnp.where(lane == k, e, tv)
        sc = jnp.where(lane == first, -3e38, sc)
    ti_ref[0] = ti
    tg_ref[0] = tv * (1.0 / den)


def _merge(x, convn, att, gn_att, wo_b, m2, m3, m4, ln2_g, rw_pad, rb_pad, *, seq_mode, tm, n_exp):
    B, T, D = x.shape
    cw_width = convn.shape[2]
    aw = att.shape[2]
    row_blk = lambda w: pl.BlockSpec((1, tm, w), lambda b, t: (b, t, 0))
    mod_spec = pl.BlockSpec((1, 1, D), lambda b, t: (b, 0, 0)) if seq_mode else row_blk(D)
    const = lambda shape: pl.BlockSpec(shape, lambda b, t: (0,) * len(shape))
    return pl.pallas_call(
        functools.partial(_merge_kernel, n_exp=n_exp, cw_width=cw_width),
        out_shape=(jax.ShapeDtypeStruct((B, T, D), F32), jax.ShapeDtypeStruct((B, T, D), F32),
                   jax.ShapeDtypeStruct((B, T, LANE), jnp.int32), jax.ShapeDtypeStruct((B, T, LANE), F32)),
        grid=(B, T // tm),
        in_specs=[row_blk(D), row_blk(cw_width), row_blk(aw), const((1, aw)), const(wo_b.shape),
                  mod_spec, mod_spec, mod_spec, const((1, D)), const(rw_pad.shape), const((1, LANE))],
        out_specs=(row_blk(D), row_blk(D), row_blk(LANE), row_blk(LANE)),
        compiler_params=_cparams(2),
        name="merge_seq" if seq_mode else "merge_rows",
    )(x, convn, att, gn_att.reshape(1, aw), wo_b, m2, m3, m4, ln2_g.reshape(1, D), rw_pad, rb_pad)


def _moe_kernel(be_ref, nu_ref, xs_ref, wgu_ref, bgu_ref, wd_ref, bd_ref, y_ref, wgu_b, wd_b, *, d_ff):
    i = pl.program_id(0)
    prev = be_ref[jnp.maximum(i - 1, 0)]
    changed = jnp.logical_or(i == 0, be_ref[i] != prev)

    @pl.when(jnp.logical_and(changed, i < nu_ref[0]))
    def _():
        wgu_b[...] = wgu_ref[0].astype(BF16)
        wd_b[...] = wd_ref[0].astype(BF16)

    @pl.when(i < nu_ref[0])
    def _():
        gu = _dot(xs_ref[...].astype(BF16), wgu_b[...]) + bgu_ref[0]
        gate_h = jnp.minimum(gu[:, 0:d_ff], SWIGLU_LIMIT)
        up_h = jnp.clip(gu[:, d_ff:], -SWIGLU_LIMIT, SWIGLU_LIMIT)
        act = (up_h + 1.0) * gate_h * _sigmoid(SWIGLU_ALPHA * gate_h)
        y_ref[...] = _dot(act.astype(BF16), wd_b[...]) + bd_ref[0]

    @pl.when(i >= nu_ref[0])
    def _():
        y_ref[...] = jnp.zeros(y_ref.shape, F32)


def _moe_experts(block_e, n_used, xs, w_gu, b_gu, w_down, b_down):
    n_rows, D = xs.shape
    E, _, two_ff = w_gu.shape
    d_ff = two_ff // 2
    n_blocks = n_rows // MOE_ROWS
    return pl.pallas_call(
        functools.partial(_moe_kernel, d_ff=d_ff),
        out_shape=jax.ShapeDtypeStruct((n_rows, D), F32),
        grid_spec=pltpu.PrefetchScalarGridSpec(
            num_scalar_prefetch=2,
            grid=(n_blocks,),
            in_specs=[pl.BlockSpec((MOE_ROWS, D), lambda i, be, nu: (i, 0)),
                      pl.BlockSpec((1, D, two_ff), lambda i, be, nu: (be[i], 0, 0)),
                      pl.BlockSpec((1, 1, two_ff), lambda i, be, nu: (be[i], 0, 0)),
                      pl.BlockSpec((1, d_ff, D), lambda i, be, nu: (be[i], 0, 0)),
                      pl.BlockSpec((1, 1, D), lambda i, be, nu: (be[i], 0, 0))],
            out_specs=pl.BlockSpec((MOE_ROWS, D), lambda i, be, nu: (i, 0)),
            scratch_shapes=[pltpu.VMEM((D, two_ff), BF16), pltpu.VMEM((d_ff, D), BF16)]),
        compiler_params=_cparams(1),
        name="moe_experts",
    )(block_e, n_used, xs, w_gu, b_gu.reshape(E, 1, two_ff), w_down, b_down.reshape(E, 1, D))


def _route(top_idx, n_tok, n_exp):
    n_assign = n_tok * TOP_K
    flat_e = top_idx.reshape(-1)
    experts = jnp.arange(n_exp, dtype=jnp.int32)
    onehot = (flat_e[:, None] == experts[None, :]).astype(jnp.int32)
    csum = jnp.cumsum(onehot, axis=0)
    counts = csum[-1]
    padded = (counts + MOE_ROWS - 1) // MOE_ROWS * MOE_ROWS
    pad_end = jnp.cumsum(padded)
    pad_start = pad_end - padded
    dest = jnp.sum(onehot * (csum - 1 + pad_start[None, :]), axis=1).astype(jnp.int32)
    n_blocks = -(-(n_assign + n_exp * (MOE_ROWS - 1)) // MOE_ROWS)
    n_rows = n_blocks * MOE_ROWS
    blk_row0 = jnp.arange(n_blocks, dtype=jnp.int32) * MOE_ROWS
    block_e = jnp.minimum(jnp.sum((pad_end[None, :] <= blk_row0[:, None]).astype(jnp.int32), axis=1), n_exp - 1)
    n_fill = n_rows - n_assign
    fill_e = jnp.arange(n_fill, dtype=jnp.int32) // (MOE_ROWS - 1)
    fill_i = jnp.arange(n_fill, dtype=jnp.int32) % (MOE_ROWS - 1)
    fill_on = (fill_e < n_exp) & (fill_i < jnp.sum(
        (fill_e[:, None] == experts[None, :]) * (padded - counts)[None, :], axis=1))
    keys = jnp.concatenate([2 * flat_e, jnp.where(fill_on, 2 * fill_e + 1, 2 * n_exp + 1)])
    toks = jnp.concatenate([jnp.arange(n_assign, dtype=jnp.int32) // TOP_K, jnp.full((n_fill,), n_tok, jnp.int32)])
    _, row_tok = lax.sort((keys, toks), num_keys=1, is_stable=True)
    n_used = (pad_end[-1] // MOE_ROWS).astype(jnp.int32).reshape(1)
    return row_tok, dest.reshape(n_tok, TOP_K), block_e.astype(jnp.int32), n_used


def _final_kernel(x1_ref, yg_ref, tg_ref, m5_ref, fg_ref, o_ref):
    tg = tg_ref[0]
    ff = tg[:, 0:1] * yg_ref[0, 0]
    for k in range(1, TOP_K):
        ff = ff + tg[:, k:k + 1] * yg_ref[0, k]
    o_ref[0] = _rms(x1_ref[0] + m5_ref[0] * ff, fg_ref[...])


def _final(x1, yg, tg, m5, final_g, *, seq_mode, tm):
    B, T, D = x1.shape
    row_blk = lambda w: pl.BlockSpec((1, tm, w), lambda b, t: (b, t, 0))
    mod_spec = pl.BlockSpec((1, 1, D), lambda b, t: (b, 0, 0)) if seq_mode else row_blk(D)
    return pl.pallas_call(
        _final_kernel,
        out_shape=jax.ShapeDtypeStruct((B, T, D), F32),
        grid=(B, T // tm),
        in_specs=[row_blk(D), pl.BlockSpec((1, TOP_K, tm, D), lambda b, t: (b, 0, t, 0)), row_blk(LANE),
                  mod_spec, pl.BlockSpec((1, D), lambda b, t: (0, 0))],
        out_specs=row_blk(D),
        compiler_params=_cparams(2),
        name="final_seq" if seq_mode else "final_rows",
    )(x1, yg, tg, m5, final_g.reshape(1, D))


def _row_tile(t):
    for tm in (512, 256, 128, 64, 32, 16, 8):
        if t % tm == 0:
            return tm
    raise ValueError(f"unsupported row count {t}")


def _rows(a, idx):
    return a.at[idx].get(mode="promise_in_bounds")


def kernel(x_prompt, x_sample, cache_kv_cmp, cache_kv_slc, state_kv_win, state_conv, page_table, c_prompt, c_sample,
           rel_bias_table, ln1_g, ln2_g, w_ada, b_ada, w_in, conv_w, cmp_pe, cmp_w1, cmp_w2, gn_conv, gn_att, w_o,
           router_w, router_b, w_gu, b_gu, w_down, b_down, final_g):
    B, T, D = x_prompt.shape
    BS, TS, _ = x_sample.shape
    depth = w_in.shape[0]
    assert depth == 1 and TS == 1
    n_pool, page = cache_kv_cmp.shape[1], cache_kv_cmp.shape[2]
    n_pages = page_table.shape[1]
    past_len = n_pages * page
    n_exp = router_w.shape[2]
    cw_width = conv_w.shape[2]
    kvw2 = 2 * KV_WIDTH
    assert past_len % SLC_BLOCK == 0 and past_len % CMP_STRIDE == 0 and T % page == 0

    in_cols = w_in.shape[2]
    gate0 = 3 * cw_width + N_HEADS * HEAD_DIM + 3 * kvw2
    w_pad = jnp.pad(w_in[0], ((0, 0), (0, gate0 + LANE - in_cols))).astype(BF16)
    wo_b = w_o[0].astype(BF16)
    rw_pad = jnp.pad(router_w[0], ((0, 0), (0, LANE - n_exp)))
    rb_pad = jnp.pad(router_b[0], (0, LANE - n_exp)).reshape(1, LANE)
    w_blk = _cmp_block_weight(cmp_w1[0])

    n_c = B + BS
    n_cp = -(-n_c // 8) * 8
    c_all = jnp.pad(jnp.concatenate([c_prompt, c_sample], axis=0), ((0, n_cp - n_c), (0, 0)))
    mod = _modulation(c_all, w_ada[0], b_ada[0]).reshape(n_cp, 6, D)
    mp = [mod[:B, i].reshape(B, 1, D) for i in range(6)]
    ms = [mod[B:n_c, i].reshape(1, BS, D) for i in range(6)]

    tm = _row_tile(T)
    nq = T // Q_BLOCK
    zeros_prev = jnp.zeros((B, CONV_K - 1, cw_width), F32)
    convn_p, q_p, kvc_p, kvs_p, kvw_p, gate_p, vlast_p, kh_p, vth_p = _inproj(
        x_prompt, mp[0], mp[1], ln1_g[0], w_pad, conv_w[0], gn_conv[0], zeros_prev, zeros_prev, seq_mode=True, tm=tm)
    pt_p = jnp.arange(B * (T // page), dtype=jnp.int32).reshape(B, T // page)
    g_p = math.gcd(T // page, 32)
    ab_p = _cmp_ab(kvc_p.reshape(B * (T // page), page, kvw2), pt_p, w_blk, g_p)
    kc_p, vct_p = _cmp_finish(ab_p, cmp_pe[0], cmp_w1[0], cmp_w2[0])
    gates_p = gate_p[:, :, :3 * N_HEADS].reshape(B, nq, Q_BLOCK, N_KV_HEADS, GROUP, 3)
    gates_p = gates_p.transpose(0, 3, 1, 5, 4, 2).reshape(B, N_KV_HEADS, nq, 3, ROWS_Q)
    att_p = _nsa_prompt(q_p, kh_p, vth_p, kc_p, vct_p, gates_p, rel_bias_table)
    x1_p, h2_p, ti_p, tg_p = _merge(x_prompt, convn_p, att_p, gn_att[0], wo_b, mp[2], mp[3], mp[4], ln2_g[0],
                                    rw_pad, rb_pad, seq_mode=True, tm=tm, n_exp=n_exp)

    xs_rows = x_sample.reshape(1, BS, D)
    prev2 = state_conv[0][:, 0, :].reshape(1, BS, cw_width)
    prev1 = state_conv[0][:, 1, :].reshape(1, BS, cw_width)
    tms = _row_tile(BS)
    convn_s, q_s, kvc_s, kvs_s, kvw_s, gate_s, v_s = _inproj(
        xs_rows, ms[0], ms[1], ln1_g[0], w_pad, conv_w[0], gn_conv[0], prev2, prev1, seq_mode=False, tm=tms)
    n_cmp = (past_len + TS - CMP_LEN) // CMP_STRIDE + 1
    nch_s = past_len // CMP_STRIDE
    assert n_cmp + 1 == nch_s
    ab_s = _cmp_ab_t(cache_kv_cmp[0].transpose(0, 2, 3, 4, 1), page_table, w_blk, math.gcd(n_pages, 32))
    kc_s, vct_s = _cmp_finish(ab_s, cmp_pe[0], cmp_w1[0], cmp_w2[0])
    t_q = past_len
    n_slc = -(-(past_len + TS) // SLC_BLOCK)
    cur = t_q // SLC_BLOCK
    k_sel = min(N_SEL, n_slc)
    tbl = rel_bias_table.reshape(N_BUCKETS, N_KV_HEADS, GROUP)
    pos_c = jnp.arange(nch_s) * CMP_STRIDE + CMP_LEN - 1
    bias_c = tbl[_rel_bucket(t_q - pos_c)].transpose(1, 2, 0)
    ncol = -(-n_slc // LANE) * LANE
    mm_s = _score_matrix(nch_s, ncol)
    q_s_hm = q_s.reshape(N_HEADS, BS, HEAD_DIM).transpose(1, 0, 2).astype(F32)
    o_c_s, idx_pad = _smp_cmp(q_s_hm, kc_s, vct_s, bias_c, mm_s, n_cmp, n_slc, cur)
    idx = idx_pad[:, :, :k_sel]
    blk_per_page = page // SLC_BLOCK
    pg = jnp.minimum(idx // blk_per_page, n_pages - 1)
    phys = jnp.take_along_axis(page_table, pg.reshape(BS, -1), axis=1).astype(jnp.int32)
    pos_s = pg[..., None] * page + jnp.arange(page)
    in_blk = (pos_s // SLC_BLOCK == idx[..., None]) & (pos_s < past_len)
    pos_s = pos_s.reshape(BS, N_KV_HEADS, k_sel * page)
    bucket_hot = (_rel_bucket(t_q - pos_s)[..., None] == jnp.arange(N_BUCKETS)).astype(F32)
    bias_sel = jnp.einsum('bksn,nkg->bkgs', bucket_hot, tbl, precision=lax.Precision.HIGHEST)
    new_sel = jnp.any(idx == cur, axis=-1, keepdims=True)
    mask_sel = jnp.concatenate([in_blk.reshape(BS, N_KV_HEADS, k_sel * page), new_sel], axis=-1)
    mask_sel = mask_sel.astype(F32)[:, :, None, :]
    nw = state_kv_win.shape[2]
    bias_win = tbl[_rel_bucket(nw - jnp.arange(nw))].transpose(1, 2, 0)
    bias_new = tbl[0].reshape(N_KV_HEADS, GROUP, 1)
    new_rows = jnp.stack([kvs_s[0], kvw_s[0]], axis=1)
    gates_s = gate_s[0, :, :3 * N_HEADS].reshape(BS, N_HEADS, 3)
    pool_t = cache_kv_slc[0].transpose(0, 2, 3, 4, 1)
    win_t = state_kv_win[0].transpose(0, 2, 3, 4, 1)
    att_s = _smp_att(phys.reshape(-1), q_s_hm, pool_t, bias_sel, mask_sel, new_rows,
                     win_t, bias_win, bias_new, gates_s, o_c_s, k_sel)
    att_s = att_s.reshape(1, BS, N_HEADS * HEAD_DIM)
    x1_s, h2_s, ti_s, tg_s = _merge(xs_rows, convn_s, att_s, gn_att[0], wo_b, ms[2], ms[3], ms[4], ln2_g[0],
                                    rw_pad, rb_pad, seq_mode=False, tm=tms, n_exp=n_exp)

    n_tok = B * T + BS
    h2_all = jnp.concatenate([h2_p.reshape(B * T, D), h2_s.reshape(BS, D), jnp.zeros((8, D), F32)], axis=0)
    top_idx = jnp.concatenate([ti_p.reshape(B * T, LANE), ti_s.reshape(BS, LANE)], axis=0)[:, :TOP_K]
    row_tok, dest, block_e, n_used = _route(top_idx, n_tok, n_exp)
    xs = _rows(h2_all, row_tok)
    yb = _moe_experts(block_e, n_used, xs, w_gu[0], b_gu[0], w_down[0], b_down[0])
    dest_p = dest[:B * T].reshape(B, T, TOP_K).transpose(0, 2, 1)
    yg_p = _rows(yb, dest_p.reshape(-1)).reshape(B, TOP_K, T, D)
    yg_s = _rows(yb, dest[B * T:].T.reshape(-1)).reshape(1, TOP_K, BS, D)
    y_p = _final(x1_p, yg_p, tg_p, mp[5], final_g, seq_mode=True, tm=tm)
    y_s = _final(x1_s, yg_s, tg_s, ms[5], final_g, seq_mode=False, tm=tms)

    kv_tail = (2, N_KV_HEADS, HEAD_DIM)
    page_shape = (depth, B, T // page, page) + kv_tail
    w_keep = min(WINDOW, T)
    new_win_s = jnp.concatenate([state_kv_win[0][:, TS:], kvw_s.reshape(BS, TS, *kv_tail)], axis=1)
    new_conv_s = jnp.concatenate([state_conv[0][:, TS:], v_s.reshape(BS, TS, cw_width)], axis=1)
    return (y_p, y_s.reshape(BS, TS, D),
            kvc_p.reshape(page_shape), kvc_s.reshape((depth, BS, TS) + kv_tail),
            kvs_p.reshape(page_shape), kvs_s.reshape((depth, BS, TS) + kv_tail),
            kvw_p[:, T - w_keep:].reshape((depth, B, w_keep) + kv_tail), new_win_s[None],
            vlast_p[None], new_conv_s[None])
```

```python
import functools
import math

import numpy as np
import jax
import jax.numpy as jnp
from jax import lax
from jax.experimental import pallas as pl
from jax.experimental.pallas import tpu as pltpu

F32 = jnp.float32
BF16 = jnp.bfloat16

CONV_K = 3
N_HEADS = 8
N_KV_HEADS = 2
GROUP = N_HEADS // N_KV_HEADS
HEAD_DIM = 64
KV_WIDTH = N_KV_HEADS * HEAD_DIM
CMP_LEN = 32
CMP_STRIDE = 16
SLC_BLOCK = 64
RATIO = SLC_BLOCK // CMP_STRIDE
N_SEL = 16
WINDOW = 512
Q_BLOCK = 128
N_BUCKETS = 32
REL_MAX_DIST = 128
TOP_K = 4
SWIGLU_LIMIT = 7.0
SWIGLU_ALPHA = 1.702
EPS = 1e-6
NEG_INF = -1e30
FORCE_SCORE = 1e4

LANE = 128
ROWS_Q = GROUP * Q_BLOCK
KEY_CHUNK = 512
TILES = KEY_CHUNK // Q_BLOCK
MOE_ROWS = 512
CMP_PAGES = 32
MOD_COLS = 1536
VMEM_LIMIT = 48 * 1024 * 1024
PICKED = -3e38
NOT_A_SLOT = -2e38
INT_BIG = 1 << 30


def _cparams(n_axes, vmem_limit=VMEM_LIMIT):
    return pltpu.CompilerParams(dimension_semantics=("arbitrary",) * n_axes, vmem_limit_bytes=vmem_limit)


def _dot(a, b):
    return jnp.dot(a, b, preferred_element_type=F32)


def _dot_nt(a, b):
    return lax.dot_general(a, b, (((1,), (1,)), ((), ())), preferred_element_type=F32)


def _split2(x):
    hi = x.astype(BF16)
    lo = (x - hi.astype(F32)).astype(BF16)
    return hi, lo


def _split3(x):
    a = x.astype(BF16)
    r = x - a.astype(F32)
    b = r.astype(BF16)
    c = (r - b.astype(F32)).astype(BF16)
    return a, b, c


def _dot3(a, b):
    ah, al = _split2(a)
    bh, bl = _split2(b)
    return _dot(ah, bh) + _dot(ah, bl) + _dot(al, bh)


def _sigmoid(x):
    return 1.0 / (1.0 + jnp.exp(-x))


def _rms(x, g):
    return x * lax.rsqrt(jnp.mean(x * x, axis=-1, keepdims=True) + EPS) * g


def _rel_bucket(dist):
    n = jnp.maximum(dist, 0)
    max_exact = N_BUCKETS // 2
    large = max_exact + (jnp.log(jnp.maximum(n, 1).astype(F32) / max_exact)
                         / math.log(REL_MAX_DIST / max_exact) * (N_BUCKETS - max_exact)).astype(jnp.int32)
    return jnp.where(n < max_exact, n, jnp.minimum(large, N_BUCKETS - 1))


def _mod_kernel(c_ref, w_ref, b_ref, o_ref):
    c = c_ref[...]
    o_ref[...] = _dot3(c * _sigmoid(c), w_ref[...]) + b_ref[...]


def _modulation(c, w_ada, b_ada):
    n, d = c.shape
    cols = w_ada.shape[1]
    bn = math.gcd(cols, MOD_COLS)
    return pl.pallas_call(
        _mod_kernel,
        out_shape=jax.ShapeDtypeStruct((n, cols), F32),
        grid=(cols // bn,),
        in_specs=[pl.BlockSpec((n, d), lambda i: (0, 0)),
                  pl.BlockSpec((d, bn), lambda i: (0, i)),
                  pl.BlockSpec((1, bn), lambda i: (0, i))],
        out_specs=pl.BlockSpec((n, bn), lambda i: (0, i)),
        compiler_params=_cparams(1),
        name="modulation",
    )(c, w_ada, b_ada.reshape(1, cols))


def _inproj_kernel(x_ref, m0_ref, m1_ref, g1_ref, w_ref, cw_ref, gnc_ref, pa_ref, pb_ref,
                   convn_ref, q_ref, kvc_ref, kvs_ref, kvw_ref, gate_ref, vlast_ref, *rest, seq_mode, tm, cw_width):
    if seq_mode:
        kh_ref, vth_ref, carry_ref = rest
    x = x_ref[0]
    h = _rms(x, g1_ref[...]) * (1.0 + m1_ref[0]) + m0_ref[0]
    hb = h.astype(BF16)
    c3 = 3 * cw_width
    uc = _dot(hb, w_ref[:, 0:c3])
    b_g = uc[:, 0:cw_width]
    v = uc[:, cw_width:2 * cw_width] * uc[:, 2 * cw_width:c3]
    if seq_mode:
        @pl.when(pl.program_id(1) == 0)
        def _():
            carry_ref[0:2, :] = pa_ref[0]
        c0 = carry_ref[0:1, :]
        c1 = carry_ref[1:2, :]
        row = lax.broadcasted_iota(jnp.int32, (tm, 1), 0)
        vm1 = jnp.where(row == 0, c1, pltpu.roll(v, 1, 0))
        vm2 = jnp.where(row == 0, c0, jnp.where(row == 1, c1, pltpu.roll(v, 2, 0)))
        carry_ref[0:2, :] = v[tm - 2:tm, :]
        vlast_ref[0] = v[tm - 2:tm, :]
    else:
        vm2 = pa_ref[0]
        vm1 = pb_ref[0]
        vlast_ref[0] = v
    cw = cw_ref[...]
    y = cw[0:1, :] * vm2 + cw[1:2, :] * vm1 + cw[2:3, :] * v
    convn_ref[0] = _rms(b_g * y, gnc_ref[...]).astype(BF16)

    aw = N_HEADS * HEAD_DIM
    uq = _dot(hb, w_ref[:, c3:c3 + aw]) * (HEAD_DIM ** -0.5)
    if seq_mode:
        uqt = uq.T
        for k in range(N_KV_HEADS):
            for jj in range(tm // Q_BLOCK):
                q_ref[0, k, jj] = jnp.concatenate(
                    [uqt[(k * GROUP + g) * HEAD_DIM:(k * GROUP + g + 1) * HEAD_DIM, jj * Q_BLOCK:(jj + 1) * Q_BLOCK]
                     for g in range(GROUP)], axis=1).astype(BF16)
    else:
        for hh in range(N_HEADS):
            q_ref[0, hh] = uq[:, hh * HEAD_DIM:(hh + 1) * HEAD_DIM].astype(BF16)
    kv0 = c3 + aw
    kvw3 = 3 * 2 * KV_WIDTH
    ukv = _dot(hb, w_ref[:, kv0:kv0 + kvw3])
    kvc_ref[0] = ukv[:, 0:2 * KV_WIDTH]
    kvs_ref[0] = ukv[:, 2 * KV_WIDTH:4 * KV_WIDTH]
    kvw_ref[0] = ukv[:, 4 * KV_WIDTH:6 * KV_WIDTH]
    if seq_mode:
        for br in range(2):
            c0 = (br + 1) * 2 * KV_WIDTH
            for k in range(N_KV_HEADS):
                kh_ref[0, br * N_KV_HEADS + k] = ukv[:, c0 + k * HEAD_DIM:c0 + (k + 1) * HEAD_DIM].astype(BF16)
            vt = ukv[:, c0 + KV_WIDTH:c0 + 2 * KV_WIDTH].T.astype(BF16)
            for k in range(N_KV_HEADS):
                vth_ref[0, br * N_KV_HEADS + k] = vt[k * HEAD_DIM:(k + 1) * HEAD_DIM, :]
    ug = _dot(hb, w_ref[:, kv0 + kvw3:kv0 + kvw3 + LANE])
    gate_ref[0] = _sigmoid(ug)


def _inproj(x, m0, m1, ln_g, w_pad, conv_w, gn_conv, pa, pb, *, seq_mode, tm):
    B, T, D = x.shape
    cw_width = conv_w.shape[1]
    nt = T // tm
    row_blk = lambda w: pl.BlockSpec((1, tm, w), lambda b, t: (b, t, 0))
    if seq_mode:
        mod_spec = pl.BlockSpec((1, 1, D), lambda b, t: (b, 0, 0))
        prev_spec = pl.BlockSpec((1, 2, cw_width), lambda b, t: (b, 0, 0))
        vlast_shape = jax.ShapeDtypeStruct((B, 2, cw_width), F32)
        vlast_spec = pl.BlockSpec((1, 2, cw_width), lambda b, t: (b, 0, 0))
    else:
        mod_spec = row_blk(D)
        prev_spec = row_blk(cw_width)
        vlast_shape = jax.ShapeDtypeStruct((B, T, cw_width), F32)
        vlast_spec = row_blk(cw_width)
    const = lambda shape: pl.BlockSpec(shape, lambda b, t: (0,) * len(shape))
    kern = functools.partial(_inproj_kernel, seq_mode=seq_mode, tm=tm, cw_width=cw_width)
    if seq_mode:
        assert tm % Q_BLOCK == 0
        q_shape = jax.ShapeDtypeStruct((B, N_KV_HEADS, T // Q_BLOCK, HEAD_DIM, ROWS_Q), BF16)
        q_spec = pl.BlockSpec((1, N_KV_HEADS, tm // Q_BLOCK, HEAD_DIM, ROWS_Q), lambda b, t: (b, 0, t, 0, 0))
    else:
        q_shape = jax.ShapeDtypeStruct((B, N_HEADS, T, HEAD_DIM), BF16)
        q_spec = pl.BlockSpec((1, N_HEADS, tm, HEAD_DIM), lambda b, t: (b, 0, t, 0))
    out_shape = [jax.ShapeDtypeStruct((B, T, cw_width), BF16),
                 q_shape,
                 jax.ShapeDtypeStruct((B, T, 2 * KV_WIDTH), F32),
                 jax.ShapeDtypeStruct((B, T, 2 * KV_WIDTH), F32),
                 jax.ShapeDtypeStruct((B, T, 2 * KV_WIDTH), F32),
                 jax.ShapeDtypeStruct((B, T, LANE), F32),
                 vlast_shape]
    out_specs = [row_blk(cw_width),
                 q_spec,
                 row_blk(2 * KV_WIDTH), row_blk(2 * KV_WIDTH), row_blk(2 * KV_WIDTH),
                 row_blk(LANE), vlast_spec]
    scratch = []
    if seq_mode:
        n_att = 2 * N_KV_HEADS
        out_shape += [jax.ShapeDtypeStruct((B, n_att, T, HEAD_DIM), BF16),
                      jax.ShapeDtypeStruct((B, n_att, HEAD_DIM, T), BF16)]
        out_specs += [pl.BlockSpec((1, n_att, tm, HEAD_DIM), lambda b, t: (b, 0, t, 0)),
                      pl.BlockSpec((1, n_att, HEAD_DIM, tm), lambda b, t: (b, 0, 0, t))]
        scratch = [pltpu.VMEM((8, cw_width), F32)]
    return pl.pallas_call(
        kern,
        out_shape=tuple(out_shape),
        grid=(B, nt),
        in_specs=[row_blk(D), mod_spec, mod_spec, const((1, D)), const(w_pad.shape), const(conv_w.shape),
                  const((1, cw_width)), prev_spec, prev_spec],
        out_specs=tuple(out_specs),
        scratch_shapes=scratch,
        compiler_params=_cparams(2),
        name="inproj_seq" if seq_mode else "inproj_rows",
    )(x, m0, m1, ln_g.reshape(1, D), w_pad, conv_w, gn_conv.reshape(1, cw_width), pa, pb)


def _cmp_ab_kernel(pt_ref, *refs, G):
    pages = refs[:2 * G]
    w_ref, out_ref, x_ref = refs[2 * G], refs[2 * G + 1], refs[2 * G + 2]
    half = KV_WIDTH
    for j in range(G):
        for r in range(CMP_STRIDE):
            for c in range(2):
                x_ref[c, j * 8:(j + 1) * 8, r * half:(r + 1) * half] = (
                    pages[2 * j + c][0, pl.ds(r, 8, stride=CMP_STRIDE), :])
    wcols = w_ref.shape[2]
    for c in range(2):
        out_ref[0, :, c * wcols:(c + 1) * wcols] = _dot(x_ref[c].astype(BF16), w_ref[c])


def _cmp_ab(pool, page_table, w_blk, G):
    P, page, width = pool.shape
    B, n_pages = page_table.shape
    cpp = page // CMP_STRIDE
    assert cpp == 8 and n_pages % G == 0
    kdim = CMP_STRIDE * width // 2

    def pg_spec(j, c):
        return pl.BlockSpec((1, page, width // 2), lambda b, g, pt: (pt[b * n_pages + g * G + j], 0, c))

    return pl.pallas_call(
        functools.partial(_cmp_ab_kernel, G=G),
        out_shape=jax.ShapeDtypeStruct((B, n_pages * cpp, 2 * w_blk.shape[2]), F32),
        grid_spec=pltpu.PrefetchScalarGridSpec(
            num_scalar_prefetch=1,
            grid=(B, n_pages // G),
            in_specs=[pg_spec(j, c) for j in range(G) for c in range(2)]
            + [pl.BlockSpec(w_blk.shape, lambda b, g, pt: (0, 0, 0))],
            out_specs=pl.BlockSpec((1, G * cpp, 2 * w_blk.shape[2]), lambda b, g, pt: (b, g, 0)),
            scratch_shapes=[pltpu.VMEM((2, G * cpp, kdim), F32)]),
        compiler_params=_cparams(2),
        name="cmp_partial",
    )(page_table.reshape(-1).astype(jnp.int32), *([pool] * (2 * G)), w_blk)


def _cmp_ab_t_kernel(pt_ref, *refs, G):
    pages = refs[:G]
    w_ref, out_ref, x_ref, s_ref = refs[G], refs[G + 1], refs[G + 2], refs[G + 3]
    half = KV_WIDTH
    for j in range(G):
        for c in range(2):
            for k in range(N_KV_HEADS):
                s_ref[c, j, :, k * HEAD_DIM:(k + 1) * HEAD_DIM] = pages[j][0, c, k].astype(BF16).T.astype(F32)
    for j in range(G):
        for r in range(CMP_STRIDE):
            for c in range(2):
                x_ref[c, j * 8:(j + 1) * 8, r * half:(r + 1) * half] = (
                    s_ref[c, j, pl.ds(r, 8, stride=CMP_STRIDE), :])
    wcols = w_ref.shape[2]
    for c in range(2):
        out_ref[0, :, c * wcols:(c + 1) * wcols] = _dot(x_ref[c].astype(BF16), w_ref[c])


def _cmp_ab_t(pool_t, page_table, w_blk, G):
    P, _, _, hd, page = pool_t.shape
    B, n_pages = page_table.shape
    cpp = page // CMP_STRIDE
    assert cpp == 8 and n_pages % G == 0
    kdim = CMP_STRIDE * KV_WIDTH

    def pg_spec(j):
        return pl.BlockSpec((1,) + pool_t.shape[1:], lambda b, g, pt: (pt[b * n_pages + g * G + j], 0, 0, 0, 0))

    return pl.pallas_call(
        functools.partial(_cmp_ab_t_kernel, G=G),
        out_shape=jax.ShapeDtypeStruct((B, n_pages * cpp, 2 * w_blk.shape[2]), F32),
        grid_spec=pltpu.PrefetchScalarGridSpec(
            num_scalar_prefetch=1,
            grid=(B, n_pages // G),
            in_specs=[pg_spec(j) for j in range(G)] + [pl.BlockSpec(w_blk.shape, lambda b, g, pt: (0, 0, 0))],
            out_specs=pl.BlockSpec((1, G * cpp, 2 * w_blk.shape[2]), lambda b, g, pt: (b, g, 0)),
            scratch_shapes=[pltpu.VMEM((2, G * cpp, kdim), F32), pltpu.VMEM((2, G, page, KV_WIDTH), F32)]),
        compiler_params=_cparams(2),
        name="cmp_partial_t",
    )(page_table.reshape(-1).astype(jnp.int32), *([pool_t] * G), w_blk)


def _gelu_tanh(x):
    return 0.5 * x * (1.0 + jnp.tanh(math.sqrt(2.0 / math.pi) * (x + 0.044715 * (x * x * x))))


def _cmp_fin_kernel(ab_ref, pe_ref, w1_ref, w2_ref, kc_ref, vct_ref, *, nch):
    for c in range(2):
        pe_t = _dot(pe_ref[c:c + 1, :].astype(BF16), w1_ref[c].astype(BF16))
        w2 = w2_ref[c].astype(BF16)
        for k in range(N_KV_HEADS):
            base = (c * N_KV_HEADS + k) * 2 * HEAD_DIM
            slab = ab_ref[0, :, base:base + 2 * HEAD_DIM]
            nxt = pltpu.roll(slab, nch - 1, 0)
            pre = slab[:, 0:HEAD_DIM] + nxt[:, HEAD_DIM:2 * HEAD_DIM] + pe_t
            blocks = _dot(_gelu_tanh(pre).astype(BF16), w2)
            if c == 0:
                kc_ref[0, k] = blocks.astype(BF16)
            else:
                vct_ref[0, k] = blocks.T.astype(BF16)


def _cmp_finish(ab, cmp_pe, cmp_w1, cmp_w2):
    B, nch, w = ab.shape
    pe = cmp_pe.reshape(2, CMP_LEN * HEAD_DIM)
    return pl.pallas_call(
        functools.partial(_cmp_fin_kernel, nch=nch),
        out_shape=(jax.ShapeDtypeStruct((B, N_KV_HEADS, nch, HEAD_DIM), BF16),
                   jax.ShapeDtypeStruct((B, N_KV_HEADS, HEAD_DIM, nch), BF16)),
        grid=(B,),
        in_specs=[pl.BlockSpec((1, nch, w), lambda b: (b, 0, 0)),
                  pl.BlockSpec(pe.shape, lambda b: (0, 0)),
                  pl.BlockSpec(cmp_w1.shape, lambda b: (0, 0, 0)),
                  pl.BlockSpec(cmp_w2.shape, lambda b: (0, 0, 0))],
        out_specs=(pl.BlockSpec((1, N_KV_HEADS, nch, HEAD_DIM), lambda b: (b, 0, 0, 0)),
                   pl.BlockSpec((1, N_KV_HEADS, HEAD_DIM, nch), lambda b: (b, 0, 0, 0))),
        compiler_params=_cparams(1),
        name="cmp_finish",
    )(ab, pe, cmp_w1, cmp_w2)


def _cmp_block_weight(cmp_w1):
    hid = cmp_w1.shape[2]
    w = cmp_w1.reshape(2, 2, CMP_STRIDE, HEAD_DIM, hid)
    eye = jnp.eye(N_KV_HEADS, dtype=cmp_w1.dtype)
    wb = jnp.einsum('cardh,kj->crkdjah', w, eye)
    return wb.reshape(2, CMP_STRIDE * N_KV_HEADS * HEAD_DIM, N_KV_HEADS * 2 * hid).astype(BF16)


def _score_matrix(nch, n_slc):
    i = np.arange(nch)[:, None]
    j = np.arange(n_slc)[None, :]
    m = 2.0 * ((i // RATIO == j) & (i % RATIO < RATIO - 1)) + 1.0 * (i == RATIO * j + RATIO - 1) \
        + 1.0 * (i == RATIO * j - 1)
    return jnp.asarray(m, dtype=BF16)


def _nsa_kernel(q_ref, kc_ref, vc_ref, ks_ref, vs_ref, kw_ref, vw_ref, gate_ref, sa_ref, sw_ref,
                cs_ref, mm_ref, e_ref, o_ref, m_ref, l_ref, acc_ref, sta_ref, stb_ref, stc_ref, std_ref,
                *, nch, n_slc):
    j = pl.program_id(2)
    q0 = pl.multiple_of(j * Q_BLOCK, Q_BLOCK)
    cq = lax.shift_right_logical(j, TILES.bit_length() - 1)
    jm = jnp.bitwise_and(j, TILES - 1)
    qt = q_ref[0, 0, 0]
    lane_q = jnp.bitwise_and(lax.broadcasted_iota(jnp.int32, (1, ROWS_Q), 1), Q_BLOCK - 1)
    tq = q0 + lane_q
    has_prev = cq >= 1
    k_own = pl.multiple_of(cq * KEY_CHUNK, KEY_CHUNK)
    k_prev = pl.multiple_of(jnp.maximum(cq - 1, 0) * KEY_CHUNK, KEY_CHUNK)
    own0 = pl.multiple_of((2 * TILES - 1 - jm) * Q_BLOCK, Q_BLOCK)
    prev0 = pl.multiple_of(jnp.where(has_prev, TILES - 1 - jm, 2 * TILES) * Q_BLOCK, Q_BLOCK)
    prevw0 = pl.multiple_of(jnp.where(has_prev, TILES - 1 - jm, 2 * TILES - 1) * Q_BLOCK, Q_BLOCK)

    stc_ref[...] = _dot(kw_ref[0, 0, pl.ds(k_own, KEY_CHUNK), :], qt) + sa_ref[0, pl.ds(own0, KEY_CHUNK), :]
    std_ref[...] = _dot(kw_ref[0, 0, pl.ds(k_prev, KEY_CHUNK), :], qt) + sw_ref[0, pl.ds(prevw0, KEY_CHUNK), :]

    per_q = Q_BLOCK // CMP_STRIDE
    c0 = pl.multiple_of((pl.num_programs(2) - 1 - j) * per_q, per_q)
    lc = _dot(kc_ref[0, 0], qt) + cs_ref[0, pl.ds(c0, nch), :]
    p = jnp.exp(lc - jnp.max(lc, axis=0, keepdims=True))
    anyc = tq >= CMP_LEN - 1
    pc = p * jnp.where(anyc, 1.0 / jnp.sum(p, axis=0, keepdims=True), 0.0)
    o_c = _dot(vc_ref[0, 0], pc.astype(BF16))

    imp = pc[:, 0:Q_BLOCK]
    for g in range(1, GROUP):
        imp = imp + pc[:, g * Q_BLOCK:(g + 1) * Q_BLOCK]
    mm = mm_ref[...]
    i1, i2, i3 = _split3(imp)
    score = _dot(mm, i1) + _dot(mm, i2) + _dot(mm, i3)
    blk = lax.broadcasted_iota(jnp.int32, (n_slc, 1), 0)
    cur = lax.shift_right_logical(q0 + lax.broadcasted_iota(jnp.int32, (1, Q_BLOCK), 1),
                                  SLC_BLOCK.bit_length() - 1)
    forced = (blk == 0) | (blk == cur) | (blk == cur - 1)
    sc = jnp.where(forced, FORCE_SCORE, jnp.where(blk <= cur, score, NEG_INF))
    big = jnp.int32(INT_BIG)
    for _ in range(min(N_SEL, n_slc)):
        mx = jnp.max(sc, axis=0, keepdims=True)
        first = jnp.min(jnp.where(sc == mx, blk, big), axis=0, keepdims=True)
        sc = jnp.where(blk == first, PICKED, sc)
    pen = jnp.where(sc == PICKED, 0.0, NEG_INF).astype(BF16)
    qsel = jnp.concatenate([jnp.concatenate([pen] * GROUP, axis=1), qt], axis=0)

    def scores_sel(k0):
        ke = jnp.concatenate([e_ref[pl.ds(k0, KEY_CHUNK), :], ks_ref[0, 0, pl.ds(k0, KEY_CHUNK), :]], axis=1)
        return _dot(ke, qsel)

    def flash(st, v, first):
        mx = jnp.max(st, axis=0, keepdims=True)
        if first:
            pe = jnp.exp(st - mx)
            l_ref[...] = jnp.sum(pe, axis=0, keepdims=True)
            acc_ref[...] = _dot(v, pe.astype(BF16))
            m_ref[...] = mx
        else:
            m_old = m_ref[...]
            m_new = jnp.maximum(m_old, mx)
            a = jnp.exp(m_old - m_new)
            pe = jnp.exp(st - m_new)
            l_ref[...] = a * l_ref[...] + jnp.sum(pe, axis=0, keepdims=True)
            acc_ref[...] = a * acc_ref[...] + _dot(v, pe.astype(BF16))
            m_ref[...] = m_new

    sta_ref[...] = scores_sel(k_own) + sa_ref[0, pl.ds(own0, KEY_CHUNK), :]
    stb_ref[...] = scores_sel(k_prev) + sa_ref[0, pl.ds(prev0, KEY_CHUNK), :]
    flash(stc_ref[...], vw_ref[0, 0, :, pl.ds(k_own, KEY_CHUNK)], True)
    flash(std_ref[...], vw_ref[0, 0, :, pl.ds(k_prev, KEY_CHUNK)], False)
    o_w = acc_ref[...] * (1.0 / l_ref[...])

    flash(sta_ref[...], vs_ref[0, 0, :, pl.ds(k_own, KEY_CHUNK)], True)
    n_far = jnp.maximum(cq - 1, 0)
    last_chunk = ks_ref.shape[2] // KEY_CHUNK - 1

    def chunk_start(c):
        return pl.multiple_of(jnp.minimum(c, last_chunk) * KEY_CHUNK, KEY_CHUNK)

    sta_ref[...] = scores_sel(chunk_start(0))
    flash(stb_ref[...], vs_ref[0, 0, :, pl.ds(k_prev, KEY_CHUNK)], False)

    def far_pair(i, carry):
        c0 = 2 * i
        stb_ref[...] = scores_sel(chunk_start(c0 + 1))
        flash(sta_ref[...], vs_ref[0, 0, :, pl.ds(chunk_start(c0), KEY_CHUNK)], False)
        sta_ref[...] = scores_sel(chunk_start(c0 + 2))
        flash(stb_ref[...], vs_ref[0, 0, :, pl.ds(chunk_start(c0 + 1), KEY_CHUNK)], False)
        return carry

    lax.fori_loop(0, lax.shift_right_logical(n_far, 1), far_pair, 0)

    @pl.when(jnp.bitwise_and(n_far, 1) == 1)
    def _():
        flash(sta_ref[...], vs_ref[0, 0, :, pl.ds(chunk_start(n_far - 1), KEY_CHUNK)], False)

    o_s = acc_ref[...] * (1.0 / l_ref[...])

    g3 = gate_ref[0, 0, 0]
    o = g3[0:1, :] * o_c + g3[1:2, :] * o_s + g3[2:3, :] * o_w
    stacked = jnp.concatenate([o[:, g * Q_BLOCK:(g + 1) * Q_BLOCK] for g in range(GROUP)], axis=0)
    o_ref[0] = stacked.T


def _bias_tables(table, nch, nq):
    tbl = table.reshape(N_BUCKETS, N_KV_HEADS, GROUP)
    far = tbl[N_BUCKETS - 1]
    i = jnp.arange(Q_BLOCK)

    def cols(rel):
        hot = (_rel_bucket(rel)[..., None] == jnp.arange(N_BUCKETS)).astype(F32)
        b = jnp.einsum('kqn,nhg->kqhg', hot, tbl, precision=lax.Precision.HIGHEST) - far
        b = jnp.where((rel >= 0)[:, :, None, None], b, 0.0)
        return b.transpose(2, 0, 3, 1).reshape(N_KV_HEADS, rel.shape[0], ROWS_Q)

    p0 = cols(i[None, :] - i[:, None])
    p1 = cols(Q_BLOCK + i[None, :] - i[:, None])
    per_q = Q_BLOCK // CMP_STRIDE
    rel_c = i[None, :] - CMP_STRIDE * (jnp.arange(nch + per_q * (nq - 1))[:, None] - per_q * (nq - 1)) - (CMP_LEN - 1)
    strip_cmp = jnp.where(jnp.tile(rel_c >= 0, (1, GROUP))[None], cols(rel_c), NEG_INF)
    kk = jnp.arange(Q_BLOCK)[:, None]
    qq = jnp.tile(jnp.arange(Q_BLOCK), GROUP)[None, :]
    bc = lambda a: jnp.broadcast_to(a[None], (N_KV_HEADS, Q_BLOCK, ROWS_Q))
    neg = bc(jnp.full((Q_BLOCK, ROWS_Q), NEG_INF, F32))
    zero = bc(jnp.zeros((Q_BLOCK, ROWS_Q), F32))
    p0_causal = jnp.where((kk <= qq)[None], p0, NEG_INF)
    in_window = bc(jnp.where(kk > qq, 0.0, NEG_INF).astype(F32))
    strip_own = jnp.concatenate([zero] * (2 * TILES - 2) + [p1, p0_causal] + [neg] * TILES, axis=1)
    strip_win = jnp.concatenate([neg] * (TILES - 1) + [in_window] + [zero] * (TILES - 2) + [p1] + [neg] * TILES,
                                axis=1)
    return strip_own, strip_win, strip_cmp


def _nsa_prompt(q_t, k_hm, vt_hm, kc, vct, gates_t, table):
    B, _, T, hd = k_hm.shape
    H = N_HEADS
    nch = kc.shape[2]
    n_slc = T // SLC_BLOCK
    nq = T // Q_BLOCK
    assert T % KEY_CHUNK == 0 and nch == RATIO * n_slc and WINDOW <= KEY_CHUNK
    strip_own, strip_win, strip_cmp = _bias_tables(table, nch, nq)
    e_tab = (jnp.arange(T)[:, None] // SLC_BLOCK == jnp.arange(n_slc)[None, :]).astype(BF16)
    mm = _score_matrix(nch, n_slc).T
    nk = N_KV_HEADS
    k_spec = lambda slab0: pl.BlockSpec((1, 1, T, hd), lambda b, k, j: (b, slab0 + k, 0, 0))
    vt_spec = lambda slab0: pl.BlockSpec((1, 1, hd, T), lambda b, k, j: (b, slab0 + k, 0, 0))
    per_head = lambda a: pl.BlockSpec((1,) + a.shape[1:], lambda b, k, j: (k, 0, 0))
    score_buf = pltpu.VMEM((KEY_CHUNK, ROWS_Q), F32)
    return pl.pallas_call(
        functools.partial(_nsa_kernel, nch=nch, n_slc=n_slc),
        out_shape=jax.ShapeDtypeStruct((B, T, H * hd), F32),
        grid=(B, nk, nq),
        in_specs=[pl.BlockSpec((1, 1, 1, hd, ROWS_Q), lambda b, k, j: (b, k, j, 0, 0)),
                  pl.BlockSpec((1, 1, nch, hd), lambda b, k, j: (b, k, 0, 0)),
                  pl.BlockSpec((1, 1, hd, nch), lambda b, k, j: (b, k, 0, 0)),
                  k_spec(0), vt_spec(0), k_spec(nk), vt_spec(nk),
                  pl.BlockSpec((1, 1, 1, 3, ROWS_Q), lambda b, k, j: (b, k, j, 0, 0)),
                  per_head(strip_own), per_head(strip_win), per_head(strip_cmp),
                  pl.BlockSpec(mm.shape, lambda b, k, j: (0, 0)),
                  pl.BlockSpec(e_tab.shape, lambda b, k, j: (0, 0))],
        out_specs=pl.BlockSpec((1, Q_BLOCK, GROUP * hd), lambda b, k, j: (b, j, k)),
        scratch_shapes=[pltpu.VMEM((1, ROWS_Q), F32), pltpu.VMEM((1, ROWS_Q), F32),
                        pltpu.VMEM((hd, ROWS_Q), F32), score_buf, score_buf, score_buf, score_buf],
        compiler_params=_cparams(3),
        name="nsa_prompt",
    )(q_t, kc, vct, k_hm, vt_hm, k_hm, vt_hm, gates_t, strip_own, strip_win, strip_cmp, mm, e_tab)


def _smp_cmp_kernel(q_ref, kc_ref, vct_ref, bias_ref, mm_ref, oc_ref, idx_ref, *, n_cmp, n_slc, cur):
    nch = kc_ref.shape[2]
    ncol = mm_ref.shape[1]
    coli = lax.broadcasted_iota(jnp.int32, (1, nch), 1)
    maskc = coli < n_cmp
    blk = lax.broadcasted_iota(jnp.int32, (1, ncol), 1)
    k_sel = min(N_SEL, n_slc)
    lane = lax.broadcasted_iota(jnp.int32, (1, LANE), 1)
    nb = q_ref.shape[0]
    mm = mm_ref[...]
    forced = (blk == 0) | (blk == cur) | (blk == cur - 1)
    scores = []
    for b in range(nb):
        for kh in range(N_KV_HEADS):
            qm = q_ref[b, kh * GROUP:(kh + 1) * GROUP].astype(BF16)
            lc = _dot_nt(qm, kc_ref[b, kh]) + bias_ref[kh]
            lcm = jnp.where(maskc, lc, NEG_INF)
            p = jnp.where(maskc, jnp.exp(lcm - jnp.max(lcm, axis=-1, keepdims=True)), 0.0)
            pc = p * (1.0 / jnp.sum(p, axis=-1, keepdims=True))
            oc_ref[b, kh * GROUP:(kh + 1) * GROUP] = _dot_nt(pc.astype(BF16), vct_ref[b, kh])
            imp = pc[0:1]
            for g in range(1, GROUP):
                imp = imp + pc[g:g + 1]
            i1, i2, i3 = _split3(imp)
            score = _dot(i1, mm) + _dot(i2, mm) + _dot(i3, mm)
            sc = jnp.where(forced, FORCE_SCORE, jnp.where(blk <= cur, score, NEG_INF))
            scores.append(jnp.where(blk < n_slc, sc, NOT_A_SLOT))
    sc = jnp.concatenate(scores, axis=0)
    big = jnp.int32(INT_BIG)
    out = jnp.zeros((nb * N_KV_HEADS, LANE), jnp.int32)
    for it in range(k_sel):
        m = jnp.max(sc, axis=-1, keepdims=True)
        first = jnp.min(jnp.where(sc == m, blk, big), axis=-1, keepdims=True)
        out = jnp.where(lane == it, first, out)
        sc = jnp.where(blk == first, PICKED, sc)
    idx_ref[0] = out


def _smp_cmp(q_hm, kc, vct, bias_c, mm, n_cmp, n_slc, cur):
    B = q_hm.shape[0]
    nch = kc.shape[2]
    nb = math.gcd(B, 8)
    o_c, idx = pl.pallas_call(
        functools.partial(_smp_cmp_kernel, n_cmp=n_cmp, n_slc=n_slc, cur=cur),
        out_shape=(jax.ShapeDtypeStruct((B, N_HEADS, HEAD_DIM), F32),
                   jax.ShapeDtypeStruct((B // nb, nb * N_KV_HEADS, LANE), jnp.int32)),
        grid=(B // nb,),
        in_specs=[pl.BlockSpec((nb, N_HEADS, HEAD_DIM), lambda b: (b, 0, 0)),
                  pl.BlockSpec((nb, N_KV_HEADS, nch, HEAD_DIM), lambda b: (b, 0, 0, 0)),
                  pl.BlockSpec((nb, N_KV_HEADS, HEAD_DIM, nch), lambda b: (b, 0, 0, 0)),
                  pl.BlockSpec(bias_c.shape, lambda b: (0, 0, 0)),
                  pl.BlockSpec(mm.shape, lambda b: (0, 0))],
        out_specs=(pl.BlockSpec((nb, N_HEADS, HEAD_DIM), lambda b: (b, 0, 0)),
                   pl.BlockSpec((1, nb * N_KV_HEADS, LANE), lambda b: (b, 0, 0))),
        compiler_params=_cparams(1),
        name="sample_cmp",
    )(q_hm, kc, vct, bias_c, mm)
    return o_c, idx.reshape(B, N_KV_HEADS, LANE)


def _smp_att_kernel(phys_ref, q_ref, pool_ref, bsel_ref, msel_ref, new_ref, win_ref, bwin_ref, bnew_ref,
                    gate_ref, oc_ref, o_ref, buf_ref, sem_ref, *, k_sel):
    b = pl.program_id(0)
    n_blk = N_KV_HEADS * k_sel

    def blk_copy(i):
        return pltpu.make_async_copy(pool_ref.at[phys_ref[b * n_blk + i]], buf_ref.at[i], sem_ref.at[i])

    for i in range(n_blk):
        blk_copy(i).start()
    for i in range(n_blk):
        blk_copy(i).wait()

    nw = win_ref.shape[4]
    for kh in range(N_KV_HEADS):
        rows = slice(kh * GROUP, (kh + 1) * GROUP)
        qm = q_ref[0, rows].astype(BF16)
        kcol = slice(kh * HEAD_DIM, (kh + 1) * HEAD_DIM)
        vcol = slice(KV_WIDTH + kh * HEAD_DIM, KV_WIDTH + (kh + 1) * HEAD_DIM)
        bnew = bnew_ref[kh]

        def branch(kt, vt, bias, mask, knew, vnew, new_ok):
            lg = _dot(qm, kt) + bias
            lg = jnp.where(mask, lg, NEG_INF)
            ln = jnp.sum(qm.astype(F32) * knew.astype(F32), axis=-1, keepdims=True) + bnew
            if new_ok is not None:
                ln = jnp.where(new_ok, ln, NEG_INF)
            mx = jnp.maximum(jnp.max(lg, axis=-1, keepdims=True), ln)
            pe = jnp.where(mask, jnp.exp(lg - mx), 0.0)
            pn = jnp.exp(ln - mx)
            if new_ok is not None:
                pn = jnp.where(new_ok, pn, 0.0)
            den = jnp.sum(pe, axis=-1, keepdims=True) + pn
            any_ok = den > 0.0
            inv = 1.0 / jnp.where(any_ok, den, 1.0)
            o = _dot_nt(pe.astype(BF16), vt) + pn.astype(BF16).astype(F32) * vnew.astype(F32)
            return jnp.where(any_ok, o * inv, 0.0)

        page = buf_ref.shape[4]
        nkeys = k_sel * page
        kt = jnp.concatenate([buf_ref[kh * k_sel + i, 0, kh] for i in range(k_sel)], axis=1).astype(BF16)
        vt = jnp.concatenate([buf_ref[kh * k_sel + i, 1, kh] for i in range(k_sel)], axis=1).astype(BF16)
        knew = new_ref[0, 0:1, kcol].astype(BF16)
        vnew = new_ref[0, 0:1, vcol].astype(BF16)
        msel = msel_ref[0, kh]
        o_s = branch(kt, vt, bsel_ref[0, kh], msel[:, 0:nkeys] > 0.5, knew, vnew, msel[:, nkeys:nkeys + 1] > 0.5)

        kw = win_ref[0, 0, kh].astype(BF16)
        vw = win_ref[0, 1, kh].astype(BF16)
        knw = new_ref[0, 1:2, kcol].astype(BF16)
        vnw = new_ref[0, 1:2, vcol].astype(BF16)
        dist = nw - lax.broadcasted_iota(jnp.int32, (1, nw), 1)
        o_w = branch(kw, vw, bwin_ref[kh], dist < WINDOW, knw, vnw, None)

        g3 = gate_ref[0, rows]
        o_ref[0, rows] = g3[:, 0:1] * oc_ref[0, rows] + g3[:, 1:2] * o_s + g3[:, 2:3] * o_w


def _smp_att(phys, q_hm, pool_blk, bias_sel, mask_sel, new_rows, win_buf, bias_win, bias_new, gates, o_c, k_sel):
    B = q_hm.shape[0]
    n_blk = N_KV_HEADS * k_sel
    full = lambda a: pl.BlockSpec(a.shape, lambda b, ph: (0,) * a.ndim)
    per_b = lambda a: pl.BlockSpec((1,) + a.shape[1:], lambda b, ph: (b,) + (0,) * (a.ndim - 1))
    return pl.pallas_call(
        functools.partial(_smp_att_kernel, k_sel=k_sel),
        out_shape=jax.ShapeDtypeStruct((B, N_HEADS, HEAD_DIM), F32),
        grid_spec=pltpu.PrefetchScalarGridSpec(
            num_scalar_prefetch=1,
            grid=(B,),
            in_specs=[per_b(q_hm), pl.BlockSpec(memory_space=pl.ANY), per_b(bias_sel), per_b(mask_sel),
                      per_b(new_rows), per_b(win_buf), full(bias_win), full(bias_new), per_b(gates), per_b(o_c)],
            out_specs=pl.BlockSpec((1, N_HEADS, HEAD_DIM), lambda b, ph: (b, 0, 0)),
            scratch_shapes=[pltpu.VMEM((n_blk,) + pool_blk.shape[1:], F32),
                            pltpu.SemaphoreType.DMA((n_blk,))]),
        compiler_params=_cparams(1),
        name="sample_att",
    )(phys, q_hm, pool_blk, bias_sel, mask_sel, new_rows, win_buf, bias_win, bias_new, gates, o_c)


def _merge_kernel(x_ref, cn_ref, att_ref, gna_ref, wo_ref, m2_ref, m3_ref, m4_ref, g2_ref, rw_ref, rb_ref,
                  x1_ref, h2_ref, ti_ref, tg_ref, *, n_exp, cw_width):
    att_n = _rms(att_ref[0], gna_ref[...]).astype(BF16)
    mix = _dot(cn_ref[0], wo_ref[0:cw_width, :]) + _dot(att_n, wo_ref[cw_width:, :])
    x1 = x_ref[0] + m2_ref[0] * mix
    x1_ref[0] = x1
    h2 = _rms(x1, g2_ref[...]) * (1.0 + m4_ref[0]) + m3_ref[0]
    h2_ref[0] = h2
    logits = _dot3(h2, rw_ref[...]) + rb_ref[...]
    lane = lax.broadcasted_iota(jnp.int32, (1, LANE), 1)
    sc = jnp.where(lane < n_exp, logits, NOT_A_SLOT)
    big = jnp.int32(INT_BIG)
    ti = jnp.zeros(sc.shape, jnp.int32)
    tv = jnp.zeros(sc.shape, F32)
    v0 = None
    den = None
    for k in range(TOP_K):
        m = jnp.max(sc, axis=-1, keepdims=True)
        first = jnp.min(jnp.where(sc == m, lane, big), axis=-1, keepdims=True)
        if k == 0:
            v0 = m
        e = jnp.exp(m - v0)
        den = e if den is None else den + e
        ti = jnp.where(lane == k, first, ti)
        tv = jnp.where(lane == k, e, tv)
        sc = jnp.where(lane == first, PICKED, sc)
    ti_ref[0] = ti
    tg_ref[0] = tv * (1.0 / den)


def _merge(x, convn, att, gn_att, wo_b, m2, m3, m4, ln2_g, rw_pad, rb_pad, *, seq_mode, tm, n_exp):
    B, T, D = x.shape
    cw_width = convn.shape[2]
    aw = att.shape[2]
    row_blk = lambda w: pl.BlockSpec((1, tm, w), lambda b, t: (b, t, 0))
    mod_spec = pl.BlockSpec((1, 1, D), lambda b, t: (b, 0, 0)) if seq_mode else row_blk(D)
    const = lambda shape: pl.BlockSpec(shape, lambda b, t: (0,) * len(shape))
    return pl.pallas_call(
        functools.partial(_merge_kernel, n_exp=n_exp, cw_width=cw_width),
        out_shape=(jax.ShapeDtypeStruct((B, T, D), F32), jax.ShapeDtypeStruct((B, T, D), F32),
                   jax.ShapeDtypeStruct((B, T, LANE), jnp.int32), jax.ShapeDtypeStruct((B, T, LANE), F32)),
        grid=(B, T // tm),
        in_specs=[row_blk(D), row_blk(cw_width), row_blk(aw), const((1, aw)), const(wo_b.shape),
                  mod_spec, mod_spec, mod_spec, const((1, D)), const(rw_pad.shape), const((1, LANE))],
        out_specs=(row_blk(D), row_blk(D), row_blk(LANE), row_blk(LANE)),
        compiler_params=_cparams(2),
        name="merge_seq" if seq_mode else "merge_rows",
    )(x, convn, att, gn_att.reshape(1, aw), wo_b, m2, m3, m4, ln2_g.reshape(1, D), rw_pad, rb_pad)


def _moe_kernel(be_ref, nu_ref, xs_ref, wgu_ref, bgu_ref, wd_ref, bd_ref, y_ref, wgu_b, wd_b, *, d_ff):
    i = pl.program_id(0)
    prev = be_ref[jnp.maximum(i - 1, 0)]
    changed = jnp.logical_or(i == 0, be_ref[i] != prev)

    @pl.when(jnp.logical_and(changed, i < nu_ref[0]))
    def _():
        wgu_b[...] = wgu_ref[0].astype(BF16)
        wd_b[...] = wd_ref[0].astype(BF16)

    @pl.when(i < nu_ref[0])
    def _():
        gu = _dot(xs_ref[...].astype(BF16), wgu_b[...]) + bgu_ref[0]
        gate_h = jnp.minimum(gu[:, 0:d_ff], SWIGLU_LIMIT)
        up_h = jnp.clip(gu[:, d_ff:], -SWIGLU_LIMIT, SWIGLU_LIMIT)
        act = (up_h + 1.0) * gate_h * _sigmoid(SWIGLU_ALPHA * gate_h)
        y_ref[...] = _dot(act.astype(BF16), wd_b[...]) + bd_ref[0]

    @pl.when(i >= nu_ref[0])
    def _():
        y_ref[...] = jnp.zeros(y_ref.shape, F32)


def _moe_experts(block_e, n_used, xs, w_gu, b_gu, w_down, b_down):
    n_rows, D = xs.shape
    E, _, two_ff = w_gu.shape
    d_ff = two_ff // 2
    n_blocks = n_rows // MOE_ROWS
    return pl.pallas_call(
        functools.partial(_moe_kernel, d_ff=d_ff),
        out_shape=jax.ShapeDtypeStruct((n_rows, D), F32),
        grid_spec=pltpu.PrefetchScalarGridSpec(
            num_scalar_prefetch=2,
            grid=(n_blocks,),
            in_specs=[pl.BlockSpec((MOE_ROWS, D), lambda i, be, nu: (i, 0)),
                      pl.BlockSpec((1, D, two_ff), lambda i, be, nu: (be[i], 0, 0)),
                      pl.BlockSpec((1, 1, two_ff), lambda i, be, nu: (be[i], 0, 0)),
                      pl.BlockSpec((1, d_ff, D), lambda i, be, nu: (be[i], 0, 0)),
                      pl.BlockSpec((1, 1, D), lambda i, be, nu: (be[i], 0, 0))],
            out_specs=pl.BlockSpec((MOE_ROWS, D), lambda i, be, nu: (i, 0)),
            scratch_shapes=[pltpu.VMEM((D, two_ff), BF16), pltpu.VMEM((d_ff, D), BF16)]),
        compiler_params=_cparams(1),
        name="moe_experts",
    )(block_e, n_used, xs, w_gu, b_gu.reshape(E, 1, two_ff), w_down, b_down.reshape(E, 1, D))


def _route(top_idx, n_tok, n_exp):
    n_assign = n_tok * TOP_K
    flat_e = top_idx.reshape(-1)
    experts = jnp.arange(n_exp, dtype=jnp.int32)
    onehot = (flat_e[:, None] == experts[None, :]).astype(jnp.int32)
    csum = jnp.cumsum(onehot, axis=0)
    counts = csum[-1]
    padded = (counts + MOE_ROWS - 1) // MOE_ROWS * MOE_ROWS
    pad_end = jnp.cumsum(padded)
    pad_start = pad_end - padded
    dest = jnp.sum(onehot * (csum - 1 + pad_start[None, :]), axis=1).astype(jnp.int32)
    n_blocks = -(-(n_assign + n_exp * (MOE_ROWS - 1)) // MOE_ROWS)
    n_rows = n_blocks * MOE_ROWS
    blk_row0 = jnp.arange(n_blocks, dtype=jnp.int32) * MOE_ROWS
    block_e = jnp.minimum(jnp.sum((pad_end[None, :] <= blk_row0[:, None]).astype(jnp.int32), axis=1), n_exp - 1)
    n_fill = n_rows - n_assign
    fill_e = jnp.arange(n_fill, dtype=jnp.int32) // (MOE_ROWS - 1)
    fill_i = jnp.arange(n_fill, dtype=jnp.int32) % (MOE_ROWS - 1)
    fill_on = (fill_e < n_exp) & (fill_i < jnp.sum(
        (fill_e[:, None] == experts[None, :]) * (padded - counts)[None, :], axis=1))
    keys = jnp.concatenate([2 * flat_e, jnp.where(fill_on, 2 * fill_e + 1, 2 * n_exp + 1)])
    toks = jnp.concatenate([jnp.arange(n_assign, dtype=jnp.int32) // TOP_K, jnp.full((n_fill,), n_tok, jnp.int32)])
    _, row_tok = lax.sort((keys, toks), num_keys=1, is_stable=True)
    n_used = (pad_end[-1] // MOE_ROWS).astype(jnp.int32).reshape(1)
    return row_tok, dest.reshape(n_tok, TOP_K), block_e.astype(jnp.int32), n_used


def _final_kernel(x1_ref, yg_ref, tg_ref, m5_ref, fg_ref, o_ref):
    tg = tg_ref[0]
    ff = tg[:, 0:1] * yg_ref[0, 0]
    for k in range(1, TOP_K):
        ff = ff + tg[:, k:k + 1] * yg_ref[0, k]
    o_ref[0] = _rms(x1_ref[0] + m5_ref[0] * ff, fg_ref[...])


def _final(x1, yg, tg, m5, final_g, *, seq_mode, tm):
    B, T, D = x1.shape
    row_blk = lambda w: pl.BlockSpec((1, tm, w), lambda b, t: (b, t, 0))
    mod_spec = pl.BlockSpec((1, 1, D), lambda b, t: (b, 0, 0)) if seq_mode else row_blk(D)
    return pl.pallas_call(
        _final_kernel,
        out_shape=jax.ShapeDtypeStruct((B, T, D), F32),
        grid=(B, T // tm),
        in_specs=[row_blk(D), pl.BlockSpec((1, TOP_K, tm, D), lambda b, t: (b, 0, t, 0)), row_blk(LANE),
                  mod_spec, pl.BlockSpec((1, D), lambda b, t: (0, 0))],
        out_specs=row_blk(D),
        compiler_params=_cparams(2),
        name="final_seq" if seq_mode else "final_rows",
    )(x1, yg, tg, m5, final_g.reshape(1, D))


def _row_tile(t):
    for tm in (512, 256, 128, 64, 32, 16, 8):
        if t % tm == 0:
            return tm
    raise ValueError(f"unsupported row count {t}")


def _rows(a, idx):
    return a.at[idx].get(mode="promise_in_bounds")


def kernel(x_prompt, x_sample, cache_kv_cmp, cache_kv_slc, state_kv_win, state_conv, page_table, c_prompt, c_sample,
           rel_bias_table, ln1_g, ln2_g, w_ada, b_ada, w_in, conv_w, cmp_pe, cmp_w1, cmp_w2, gn_conv, gn_att, w_o,
           router_w, router_b, w_gu, b_gu, w_down, b_down, final_g):
    B, T, D = x_prompt.shape
    BS, TS, _ = x_sample.shape
    depth = w_in.shape[0]
    assert depth == 1 and TS == 1
    n_pool, page = cache_kv_cmp.shape[1], cache_kv_cmp.shape[2]
    n_pages = page_table.shape[1]
    past_len = n_pages * page
    n_exp = router_w.shape[2]
    cw_width = conv_w.shape[2]
    kvw2 = 2 * KV_WIDTH
    assert past_len % SLC_BLOCK == 0 and past_len % CMP_STRIDE == 0 and T % page == 0

    in_cols = w_in.shape[2]
    gate0 = 3 * cw_width + N_HEADS * HEAD_DIM + 3 * kvw2
    w_pad = jnp.pad(w_in[0], ((0, 0), (0, gate0 + LANE - in_cols))).astype(BF16)
    wo_b = w_o[0].astype(BF16)
    rw_pad = jnp.pad(router_w[0], ((0, 0), (0, LANE - n_exp)))
    rb_pad = jnp.pad(router_b[0], (0, LANE - n_exp)).reshape(1, LANE)
    w_blk = _cmp_block_weight(cmp_w1[0])

    n_c = B + BS
    n_cp = -(-n_c // 8) * 8
    c_all = jnp.pad(jnp.concatenate([c_prompt, c_sample], axis=0), ((0, n_cp - n_c), (0, 0)))
    mod = _modulation(c_all, w_ada[0], b_ada[0]).reshape(n_cp, 6, D)
    mp = [mod[:B, i].reshape(B, 1, D) for i in range(6)]
    ms = [mod[B:n_c, i].reshape(1, BS, D) for i in range(6)]

    tm = _row_tile(T)
    nq = T // Q_BLOCK
    zeros_prev = jnp.zeros((B, CONV_K - 1, cw_width), F32)
    convn_p, q_p, kvc_p, kvs_p, kvw_p, gate_p, vlast_p, kh_p, vth_p = _inproj(
        x_prompt, mp[0], mp[1], ln1_g[0], w_pad, conv_w[0], gn_conv[0], zeros_prev, zeros_prev, seq_mode=True, tm=tm)
    pt_p = jnp.arange(B * (T // page), dtype=jnp.int32).reshape(B, T // page)
    g_p = math.gcd(T // page, CMP_PAGES)
    ab_p = _cmp_ab(kvc_p.reshape(B * (T // page), page, kvw2), pt_p, w_blk, g_p)
    kc_p, vct_p = _cmp_finish(ab_p, cmp_pe[0], cmp_w1[0], cmp_w2[0])
    gates_p = gate_p[:, :, :3 * N_HEADS].reshape(B, nq, Q_BLOCK, N_KV_HEADS, GROUP, 3)
    gates_p = gates_p.transpose(0, 3, 1, 5, 4, 2).reshape(B, N_KV_HEADS, nq, 3, ROWS_Q)
    att_p = _nsa_prompt(q_p, kh_p, vth_p, kc_p, vct_p, gates_p, rel_bias_table)
    x1_p, h2_p, ti_p, tg_p = _merge(x_prompt, convn_p, att_p, gn_att[0], wo_b, mp[2], mp[3], mp[4], ln2_g[0],
                                    rw_pad, rb_pad, seq_mode=True, tm=tm, n_exp=n_exp)

    xs_rows = x_sample.reshape(1, BS, D)
    prev2 = state_conv[0][:, 0, :].reshape(1, BS, cw_width)
    prev1 = state_conv[0][:, 1, :].reshape(1, BS, cw_width)
    tms = _row_tile(BS)
    convn_s, q_s, kvc_s, kvs_s, kvw_s, gate_s, v_s = _inproj(
        xs_rows, ms[0], ms[1], ln1_g[0], w_pad, conv_w[0], gn_conv[0], prev2, prev1, seq_mode=False, tm=tms)
    n_cmp = (past_len + TS - CMP_LEN) // CMP_STRIDE + 1
    nch_s = past_len // CMP_STRIDE
    assert n_cmp + 1 == nch_s
    ab_s = _cmp_ab_t(cache_kv_cmp[0].transpose(0, 2, 3, 4, 1), page_table, w_blk,
                     math.gcd(n_pages, CMP_PAGES))
    kc_s, vct_s = _cmp_finish(ab_s, cmp_pe[0], cmp_w1[0], cmp_w2[0])
    t_q = past_len
    n_slc = -(-(past_len + TS) // SLC_BLOCK)
    cur = t_q // SLC_BLOCK
    k_sel = min(N_SEL, n_slc)
    tbl = rel_bias_table.reshape(N_BUCKETS, N_KV_HEADS, GROUP)
    pos_c = jnp.arange(nch_s) * CMP_STRIDE + CMP_LEN - 1
    bias_c = tbl[_rel_bucket(t_q - pos_c)].transpose(1, 2, 0)
    ncol = -(-n_slc // LANE) * LANE
    mm_s = _score_matrix(nch_s, ncol)
    q_s_hm = q_s.reshape(N_HEADS, BS, HEAD_DIM).transpose(1, 0, 2).astype(F32)
    o_c_s, idx_pad = _smp_cmp(q_s_hm, kc_s, vct_s, bias_c, mm_s, n_cmp, n_slc, cur)
    idx = idx_pad[:, :, :k_sel]
    blk_per_page = page // SLC_BLOCK
    pg = jnp.minimum(idx // blk_per_page, n_pages - 1)
    phys = jnp.take_along_axis(page_table, pg.reshape(BS, -1), axis=1).astype(jnp.int32)
    pos_s = pg[..., None] * page + jnp.arange(page)
    in_blk = (pos_s // SLC_BLOCK == idx[..., None]) & (pos_s < past_len)
    pos_s = pos_s.reshape(BS, N_KV_HEADS, k_sel * page)
    bucket_hot = (_rel_bucket(t_q - pos_s)[..., None] == jnp.arange(N_BUCKETS)).astype(F32)
    bias_sel = jnp.einsum('bksn,nkg->bkgs', bucket_hot, tbl, precision=lax.Precision.HIGHEST)
    new_sel = jnp.any(idx == cur, axis=-1, keepdims=True)
    mask_sel = jnp.concatenate([in_blk.reshape(BS, N_KV_HEADS, k_sel * page), new_sel], axis=-1)
    mask_sel = mask_sel.astype(F32)[:, :, None, :]
    nw = state_kv_win.shape[2]
    bias_win = tbl[_rel_bucket(nw - jnp.arange(nw))].transpose(1, 2, 0)
    bias_new = tbl[0].reshape(N_KV_HEADS, GROUP, 1)
    new_rows = jnp.stack([kvs_s[0], kvw_s[0]], axis=1)
    gates_s = gate_s[0, :, :3 * N_HEADS].reshape(BS, N_HEADS, 3)
    pool_t = cache_kv_slc[0].transpose(0, 2, 3, 4, 1)
    win_t = state_kv_win[0].transpose(0, 2, 3, 4, 1)
    att_s = _smp_att(phys.reshape(-1), q_s_hm, pool_t, bias_sel, mask_sel, new_rows,
                     win_t, bias_win, bias_new, gates_s, o_c_s, k_sel)
    att_s = att_s.reshape(1, BS, N_HEADS * HEAD_DIM)
    x1_s, h2_s, ti_s, tg_s = _merge(xs_rows, convn_s, att_s, gn_att[0], wo_b, ms[2], ms[3], ms[4], ln2_g[0],
                                    rw_pad, rb_pad, seq_mode=False, tm=tms, n_exp=n_exp)

    n_tok = B * T + BS
    h2_all = jnp.concatenate([h2_p.reshape(B * T, D), h2_s.reshape(BS, D), jnp.zeros((8, D), F32)], axis=0)
    top_idx = jnp.concatenate([ti_p.reshape(B * T, LANE), ti_s.reshape(BS, LANE)], axis=0)[:, :TOP_K]
    row_tok, dest, block_e, n_used = _route(top_idx, n_tok, n_exp)
    xs = _rows(h2_all, row_tok)
    yb = _moe_experts(block_e, n_used, xs, w_gu[0], b_gu[0], w_down[0], b_down[0])
    dest_p = dest[:B * T].reshape(B, T, TOP_K).transpose(0, 2, 1)
    yg_p = _rows(yb, dest_p.reshape(-1)).reshape(B, TOP_K, T, D)
    yg_s = _rows(yb, dest[B * T:].T.reshape(-1)).reshape(1, TOP_K, BS, D)
    y_p = _final(x1_p, yg_p, tg_p, mp[5], final_g, seq_mode=True, tm=tm)
    y_s = _final(x1_s, yg_s, tg_s, ms[5], final_g, seq_mode=False, tm=tms)

    kv_tail = (2, N_KV_HEADS, HEAD_DIM)
    page_shape = (depth, B, T // page, page) + kv_tail
    w_keep = min(WINDOW, T)
    new_win_s = jnp.concatenate([state_kv_win[0][:, TS:], kvw_s.reshape(BS, TS, *kv_tail)], axis=1)
    new_conv_s = jnp.concatenate([state_conv[0][:, TS:], v_s.reshape(BS, TS, cw_width)], axis=1)
    return (y_p, y_s.reshape(BS, TS, D),
            kvc_p.reshape(page_shape), kvc_s.reshape((depth, BS, TS) + kv_tail),
            kvs_p.reshape(page_shape), kvs_s.reshape((depth, BS, TS) + kv_tail),
            kvw_p[:, T - w_keep:].reshape((depth, B, w_keep) + kv_tail), new_win_s[None],
            vlast_p[None], new_conv_s[None])
```

```python
import functools
import math

import numpy as np
import jax
import jax.numpy as jnp
from jax import lax
from jax.experimental import pallas as pl
from jax.experimental.pallas import tpu as pltpu

F32 = jnp.float32
BF16 = jnp.bfloat16

CONV_K = 3
N_HEADS = 8
N_KV_HEADS = 2
GROUP = N_HEADS // N_KV_HEADS
HEAD_DIM = 64
KV_WIDTH = N_KV_HEADS * HEAD_DIM
CMP_LEN = 32
CMP_STRIDE = 16
SLC_BLOCK = 64
RATIO = SLC_BLOCK // CMP_STRIDE
N_SEL = 16
WINDOW = 512
Q_BLOCK = 128
N_BUCKETS = 32
REL_MAX_DIST = 128
TOP_K = 4
SWIGLU_LIMIT = 7.0
SWIGLU_ALPHA = 1.702
EPS = 1e-6
NEG_INF = -1e30
FORCE_SCORE = 1e4

LANE = 128
BF16_ROWS = 16
ROWS_Q = GROUP * Q_BLOCK
KEY_CHUNK = 512
TILES = KEY_CHUNK // Q_BLOCK
MOE_ROWS = 512
CMP_PAGES = 32
MOD_COLS = 1536
VMEM_LIMIT = 48 * 1024 * 1024
PICKED = -3e38
NOT_A_SLOT = -2e38
INT_BIG = 1 << 30


def _cparams(n_axes, vmem_limit=VMEM_LIMIT):
    return pltpu.CompilerParams(dimension_semantics=("arbitrary",) * n_axes, vmem_limit_bytes=vmem_limit)


def _dot(a, b):
    return jnp.dot(a, b, preferred_element_type=F32)


def _dot_nt(a, b):
    return lax.dot_general(a, b, (((1,), (1,)), ((), ())), preferred_element_type=F32)


def _split2(x):
    hi = x.astype(BF16)
    lo = (x - hi.astype(F32)).astype(BF16)
    return hi, lo


def _split3(x):
    a = x.astype(BF16)
    r = x - a.astype(F32)
    b = r.astype(BF16)
    c = (r - b.astype(F32)).astype(BF16)
    return a, b, c


def _dot3(a, b):
    ah, al = _split2(a)
    bh, bl = _split2(b)
    return _dot(ah, bh) + _dot(ah, bl) + _dot(al, bh)


def _sigmoid(x):
    return 1.0 / (1.0 + jnp.exp(-x))


def _rms(x, g):
    return x * lax.rsqrt(jnp.mean(x * x, axis=-1, keepdims=True) + EPS) * g


def _rel_bucket(dist):
    n = jnp.maximum(dist, 0)
    max_exact = N_BUCKETS // 2
    large = max_exact + (jnp.log(jnp.maximum(n, 1).astype(F32) / max_exact)
                         / math.log(REL_MAX_DIST / max_exact) * (N_BUCKETS - max_exact)).astype(jnp.int32)
    return jnp.where(n < max_exact, n, jnp.minimum(large, N_BUCKETS - 1))


def _mod_kernel(c_ref, w_ref, b_ref, o_ref):
    c = c_ref[...]
    o_ref[...] = _dot3(c * _sigmoid(c), w_ref[...]) + b_ref[...]


def _modulation(c, w_ada, b_ada):
    n, d = c.shape
    cols = w_ada.shape[1]
    bn = math.gcd(cols, MOD_COLS)
    return pl.pallas_call(
        _mod_kernel,
        out_shape=jax.ShapeDtypeStruct((n, cols), F32),
        grid=(cols // bn,),
        in_specs=[pl.BlockSpec((n, d), lambda i: (0, 0)),
                  pl.BlockSpec((d, bn), lambda i: (0, i)),
                  pl.BlockSpec((1, bn), lambda i: (0, i))],
        out_specs=pl.BlockSpec((n, bn), lambda i: (0, i)),
        compiler_params=_cparams(1),
        name="modulation",
    )(c, w_ada, b_ada.reshape(1, cols))


def _inproj_kernel(x_ref, m0_ref, m1_ref, g1_ref, w_ref, cw_ref, gnc_ref, pa_ref, pb_ref,
                   convn_ref, q_ref, kvc_ref, kvs_ref, kvw_ref, gate_ref, vlast_ref, *rest, seq_mode, tm, cw_width):
    if seq_mode:
        kh_ref, vth_ref, carry_ref = rest
    x = x_ref[0]
    h = _rms(x, g1_ref[...]) * (1.0 + m1_ref[0]) + m0_ref[0]
    hb = h.astype(BF16)
    c3 = 3 * cw_width
    uc = _dot(hb, w_ref[:, 0:c3])
    b_g = uc[:, 0:cw_width]
    v = uc[:, cw_width:2 * cw_width] * uc[:, 2 * cw_width:c3]
    if seq_mode:
        @pl.when(pl.program_id(1) == 0)
        def _():
            carry_ref[0:2, :] = pa_ref[0]
        c0 = carry_ref[0:1, :]
        c1 = carry_ref[1:2, :]
        row = lax.broadcasted_iota(jnp.int32, (tm, 1), 0)
        vm1 = jnp.where(row == 0, c1, pltpu.roll(v, 1, 0))
        vm2 = jnp.where(row == 0, c0, jnp.where(row == 1, c1, pltpu.roll(v, 2, 0)))
        carry_ref[0:2, :] = v[tm - 2:tm, :]
        vlast_ref[0] = v[tm - 2:tm, :]
    else:
        vm2 = pa_ref[0]
        vm1 = pb_ref[0]
        vlast_ref[0] = v
    cw = cw_ref[...]
    y = cw[0:1, :] * vm2 + cw[1:2, :] * vm1 + cw[2:3, :] * v
    convn_ref[0] = _rms(b_g * y, gnc_ref[...]).astype(BF16)

    aw = N_HEADS * HEAD_DIM
    uq = _dot(hb, w_ref[:, c3:c3 + aw]) * (HEAD_DIM ** -0.5)
    if seq_mode:
        uqt = uq.T
        for k in range(N_KV_HEADS):
            for jj in range(tm // Q_BLOCK):
                q_ref[0, k, jj] = jnp.concatenate(
                    [uqt[(k * GROUP + g) * HEAD_DIM:(k * GROUP + g + 1) * HEAD_DIM, jj * Q_BLOCK:(jj + 1) * Q_BLOCK]
                     for g in range(GROUP)], axis=1).astype(BF16)
    else:
        for hh in range(N_HEADS):
            q_ref[0, hh] = uq[:, hh * HEAD_DIM:(hh + 1) * HEAD_DIM].astype(BF16)
    kv0 = c3 + aw
    kvw3 = 3 * 2 * KV_WIDTH
    ukv = _dot(hb, w_ref[:, kv0:kv0 + kvw3])
    kvc_ref[0] = ukv[:, 0:2 * KV_WIDTH]
    kvs_ref[0] = ukv[:, 2 * KV_WIDTH:4 * KV_WIDTH]
    kvw_ref[0] = ukv[:, 4 * KV_WIDTH:6 * KV_WIDTH]
    if seq_mode:
        for br in range(2):
            c0 = (br + 1) * 2 * KV_WIDTH
            for k in range(N_KV_HEADS):
                kh_ref[0, br * N_KV_HEADS + k] = ukv[:, c0 + k * HEAD_DIM:c0 + (k + 1) * HEAD_DIM].astype(BF16)
            vt = ukv[:, c0 + KV_WIDTH:c0 + 2 * KV_WIDTH].T.astype(BF16)
            for k in range(N_KV_HEADS):
                vth_ref[0, br * N_KV_HEADS + k] = vt[k * HEAD_DIM:(k + 1) * HEAD_DIM, :]
    ug = _dot(hb, w_ref[:, kv0 + kvw3:kv0 + kvw3 + LANE])
    gate_ref[0] = _sigmoid(ug)


def _inproj(x, m0, m1, ln_g, w_pad, conv_w, gn_conv, pa, pb, *, seq_mode, tm):
    B, T, D = x.shape
    cw_width = conv_w.shape[1]
    nt = T // tm
    row_blk = lambda w: pl.BlockSpec((1, tm, w), lambda b, t: (b, t, 0))
    if seq_mode:
        mod_spec = pl.BlockSpec((1, 1, D), lambda b, t: (b, 0, 0))
        prev_spec = pl.BlockSpec((1, 2, cw_width), lambda b, t: (b, 0, 0))
        vlast_shape = jax.ShapeDtypeStruct((B, 2, cw_width), F32)
        vlast_spec = pl.BlockSpec((1, 2, cw_width), lambda b, t: (b, 0, 0))
    else:
        mod_spec = row_blk(D)
        prev_spec = row_blk(cw_width)
        vlast_shape = jax.ShapeDtypeStruct((B, T, cw_width), F32)
        vlast_spec = row_blk(cw_width)
    const = lambda shape: pl.BlockSpec(shape, lambda b, t: (0,) * len(shape))
    kern = functools.partial(_inproj_kernel, seq_mode=seq_mode, tm=tm, cw_width=cw_width)
    if seq_mode:
        assert tm % Q_BLOCK == 0
        q_shape = jax.ShapeDtypeStruct((B, N_KV_HEADS, T // Q_BLOCK, HEAD_DIM, ROWS_Q), BF16)
        q_spec = pl.BlockSpec((1, N_KV_HEADS, tm // Q_BLOCK, HEAD_DIM, ROWS_Q), lambda b, t: (b, 0, t, 0, 0))
    else:
        q_shape = jax.ShapeDtypeStruct((B, N_HEADS, T, HEAD_DIM), BF16)
        q_spec = pl.BlockSpec((1, N_HEADS, tm, HEAD_DIM), lambda b, t: (b, 0, t, 0))
    out_shape = [jax.ShapeDtypeStruct((B, T, cw_width), BF16),
                 q_shape,
                 jax.ShapeDtypeStruct((B, T, 2 * KV_WIDTH), F32),
                 jax.ShapeDtypeStruct((B, T, 2 * KV_WIDTH), F32),
                 jax.ShapeDtypeStruct((B, T, 2 * KV_WIDTH), F32),
                 jax.ShapeDtypeStruct((B, T, LANE), F32),
                 vlast_shape]
    out_specs = [row_blk(cw_width),
                 q_spec,
                 row_blk(2 * KV_WIDTH), row_blk(2 * KV_WIDTH), row_blk(2 * KV_WIDTH),
                 row_blk(LANE), vlast_spec]
    scratch = []
    if seq_mode:
        n_att = 2 * N_KV_HEADS
        out_shape += [jax.ShapeDtypeStruct((B, n_att, T, HEAD_DIM), BF16),
                      jax.ShapeDtypeStruct((B, n_att, HEAD_DIM, T), BF16)]
        out_specs += [pl.BlockSpec((1, n_att, tm, HEAD_DIM), lambda b, t: (b, 0, t, 0)),
                      pl.BlockSpec((1, n_att, HEAD_DIM, tm), lambda b, t: (b, 0, 0, t))]
        scratch = [pltpu.VMEM((8, cw_width), F32)]
    return pl.pallas_call(
        kern,
        out_shape=tuple(out_shape),
        grid=(B, nt),
        in_specs=[row_blk(D), mod_spec, mod_spec, const((1, D)), const(w_pad.shape), const(conv_w.shape),
                  const((1, cw_width)), prev_spec, prev_spec],
        out_specs=tuple(out_specs),
        scratch_shapes=scratch,
        compiler_params=_cparams(2),
        name="inproj_seq" if seq_mode else "inproj_rows",
    )(x, m0, m1, ln_g.reshape(1, D), w_pad, conv_w, gn_conv.reshape(1, cw_width), pa, pb)


def _cmp_ab_kernel(pt_ref, *refs, G):
    pages = refs[:2 * G]
    w_ref, out_ref, x_ref = refs[2 * G], refs[2 * G + 1], refs[2 * G + 2]
    half = KV_WIDTH
    for j in range(G):
        for r in range(CMP_STRIDE):
            for c in range(2):
                x_ref[c, j * 8:(j + 1) * 8, r * half:(r + 1) * half] = (
                    pages[2 * j + c][0, pl.ds(r, 8, stride=CMP_STRIDE), :])
    wcols = w_ref.shape[2]
    for c in range(2):
        out_ref[0, :, c * wcols:(c + 1) * wcols] = _dot(x_ref[c].astype(BF16), w_ref[c])


def _cmp_ab(pool, page_table, w_blk, G):
    P, page, width = pool.shape
    B, n_pages = page_table.shape
    cpp = page // CMP_STRIDE
    assert cpp == 8 and n_pages % G == 0
    kdim = CMP_STRIDE * width // 2

    def pg_spec(j, c):
        return pl.BlockSpec((1, page, width // 2), lambda b, g, pt: (pt[b * n_pages + g * G + j], 0, c))

    return pl.pallas_call(
        functools.partial(_cmp_ab_kernel, G=G),
        out_shape=jax.ShapeDtypeStruct((B, n_pages * cpp, 2 * w_blk.shape[2]), F32),
        grid_spec=pltpu.PrefetchScalarGridSpec(
            num_scalar_prefetch=1,
            grid=(B, n_pages // G),
            in_specs=[pg_spec(j, c) for j in range(G) for c in range(2)]
            + [pl.BlockSpec(w_blk.shape, lambda b, g, pt: (0, 0, 0))],
            out_specs=pl.BlockSpec((1, G * cpp, 2 * w_blk.shape[2]), lambda b, g, pt: (b, g, 0)),
            scratch_shapes=[pltpu.VMEM((2, G * cpp, kdim), F32)]),
        compiler_params=_cparams(2),
        name="cmp_partial",
    )(page_table.reshape(-1).astype(jnp.int32), *([pool] * (2 * G)), w_blk)


def _cmp_ab_t_kernel(pt_ref, *refs, G):
    pages = refs[:G]
    w_ref, out_ref, x_ref, s_ref = refs[G], refs[G + 1], refs[G + 2], refs[G + 3]
    half = KV_WIDTH
    for j in range(G):
        for c in range(2):
            for k in range(N_KV_HEADS):
                s_ref[c, j, :, k * HEAD_DIM:(k + 1) * HEAD_DIM] = pages[j][0, c, k].astype(BF16).T.astype(F32)
    for j in range(G):
        for r in range(CMP_STRIDE):
            for c in range(2):
                x_ref[c, j * 8:(j + 1) * 8, r * half:(r + 1) * half] = (
                    s_ref[c, j, pl.ds(r, 8, stride=CMP_STRIDE), :])
    wcols = w_ref.shape[2]
    for c in range(2):
        out_ref[0, :, c * wcols:(c + 1) * wcols] = _dot(x_ref[c].astype(BF16), w_ref[c])


def _cmp_ab_t(pool_t, page_table, w_blk, G):
    P, _, _, hd, page = pool_t.shape
    B, n_pages = page_table.shape
    cpp = page // CMP_STRIDE
    assert cpp == 8 and n_pages % G == 0
    kdim = CMP_STRIDE * KV_WIDTH

    def pg_spec(j):
        return pl.BlockSpec((1,) + pool_t.shape[1:], lambda b, g, pt: (pt[b * n_pages + g * G + j], 0, 0, 0, 0))

    return pl.pallas_call(
        functools.partial(_cmp_ab_t_kernel, G=G),
        out_shape=jax.ShapeDtypeStruct((B, n_pages * cpp, 2 * w_blk.shape[2]), F32),
        grid_spec=pltpu.PrefetchScalarGridSpec(
            num_scalar_prefetch=1,
            grid=(B, n_pages // G),
            in_specs=[pg_spec(j) for j in range(G)] + [pl.BlockSpec(w_blk.shape, lambda b, g, pt: (0, 0, 0))],
            out_specs=pl.BlockSpec((1, G * cpp, 2 * w_blk.shape[2]), lambda b, g, pt: (b, g, 0)),
            scratch_shapes=[pltpu.VMEM((2, G * cpp, kdim), F32), pltpu.VMEM((2, G, page, KV_WIDTH), F32)]),
        compiler_params=_cparams(2),
        name="cmp_partial_t",
    )(page_table.reshape(-1).astype(jnp.int32), *([pool_t] * G), w_blk)


def _gelu_tanh(x):
    return 0.5 * x * (1.0 + jnp.tanh(math.sqrt(2.0 / math.pi) * (x + 0.044715 * (x * x * x))))


def _cmp_fin_kernel(ab_ref, pe_ref, w1_ref, w2_ref, kc_ref, vct_ref, *, nch):
    for c in range(2):
        pe_t = _dot(pe_ref[c:c + 1, :].astype(BF16), w1_ref[c].astype(BF16))
        w2 = w2_ref[c].astype(BF16)
        for k in range(N_KV_HEADS):
            base = (c * N_KV_HEADS + k) * 2 * HEAD_DIM
            slab = ab_ref[0, :, base:base + 2 * HEAD_DIM]
            nxt = pltpu.roll(slab, nch - 1, 0)
            pre = slab[:, 0:HEAD_DIM] + nxt[:, HEAD_DIM:2 * HEAD_DIM] + pe_t
            blocks = _dot(_gelu_tanh(pre).astype(BF16), w2)
            if c == 0:
                kc_ref[0, k] = blocks.astype(BF16)
            else:
                vct_ref[0, k] = blocks.T.astype(BF16)


def _cmp_finish(ab, cmp_pe, cmp_w1, cmp_w2):
    B, nch, w = ab.shape
    pe = cmp_pe.reshape(2, CMP_LEN * HEAD_DIM)
    return pl.pallas_call(
        functools.partial(_cmp_fin_kernel, nch=nch),
        out_shape=(jax.ShapeDtypeStruct((B, N_KV_HEADS, nch, HEAD_DIM), BF16),
                   jax.ShapeDtypeStruct((B, N_KV_HEADS, HEAD_DIM, nch), BF16)),
        grid=(B,),
        in_specs=[pl.BlockSpec((1, nch, w), lambda b: (b, 0, 0)),
                  pl.BlockSpec(pe.shape, lambda b: (0, 0)),
                  pl.BlockSpec(cmp_w1.shape, lambda b: (0, 0, 0)),
                  pl.BlockSpec(cmp_w2.shape, lambda b: (0, 0, 0))],
        out_specs=(pl.BlockSpec((1, N_KV_HEADS, nch, HEAD_DIM), lambda b: (b, 0, 0, 0)),
                   pl.BlockSpec((1, N_KV_HEADS, HEAD_DIM, nch), lambda b: (b, 0, 0, 0))),
        compiler_params=_cparams(1),
        name="cmp_finish",
    )(ab, pe, cmp_w1, cmp_w2)


def _cmp_block_weight(cmp_w1):
    hid = cmp_w1.shape[2]
    w = cmp_w1.reshape(2, 2, CMP_STRIDE, HEAD_DIM, hid)
    eye = jnp.eye(N_KV_HEADS, dtype=cmp_w1.dtype)
    wb = jnp.einsum('cardh,kj->crkdjah', w, eye)
    return wb.reshape(2, CMP_STRIDE * N_KV_HEADS * HEAD_DIM, N_KV_HEADS * 2 * hid).astype(BF16)


def _score_matrix(nch, n_slc):
    i = np.arange(nch)[:, None]
    j = np.arange(n_slc)[None, :]
    m = 2.0 * ((i // RATIO == j) & (i % RATIO < RATIO - 1)) + 1.0 * (i == RATIO * j + RATIO - 1) \
        + 1.0 * (i == RATIO * j - 1)
    return jnp.asarray(m, dtype=BF16)


def _nsa_kernel(q_ref, kc_ref, vc_ref, ks_ref, vs_ref, kw_ref, vw_ref, gate_ref, sa_ref, sw_ref,
                cs_ref, mm_ref, e_ref, o_ref, m_ref, l_ref, acc_ref, sta_ref, stb_ref, stc_ref, std_ref,
                oc_ref, pen_ref, *, nch, n_slc, parts):
    j = pl.program_id(2)
    q0 = pl.multiple_of(j * Q_BLOCK, Q_BLOCK)
    cq = lax.shift_right_logical(j, TILES.bit_length() - 1)
    jm = jnp.bitwise_and(j, TILES - 1)
    qt = q_ref[0, 0, 0]
    lane_q = jnp.bitwise_and(lax.broadcasted_iota(jnp.int32, (1, ROWS_Q), 1), Q_BLOCK - 1)
    tq = q0 + lane_q
    has_prev = cq >= 1
    k_own = pl.multiple_of(cq * KEY_CHUNK, KEY_CHUNK)
    k_prev = pl.multiple_of(jnp.maximum(cq - 1, 0) * KEY_CHUNK, KEY_CHUNK)
    own0 = pl.multiple_of((2 * TILES - 1 - jm) * Q_BLOCK, Q_BLOCK)
    prev0 = pl.multiple_of(jnp.where(has_prev, TILES - 1 - jm, 2 * TILES) * Q_BLOCK, Q_BLOCK)
    prevw0 = pl.multiple_of(jnp.where(has_prev, TILES - 1 - jm, 2 * TILES - 1) * Q_BLOCK, Q_BLOCK)

    stc_ref[...] = _dot(kw_ref[0, 0, pl.ds(k_own, KEY_CHUNK), :], qt) + sa_ref[0, pl.ds(own0, KEY_CHUNK), :]
    std_ref[...] = _dot(kw_ref[0, 0, pl.ds(k_prev, KEY_CHUNK), :], qt) + sw_ref[0, pl.ds(prevw0, KEY_CHUNK), :]

    per_q = Q_BLOCK // CMP_STRIDE
    nq = pl.num_programs(2)
    c0 = pl.multiple_of((nq - 1 - j) * per_q, per_q)
    anyc = tq >= CMP_LEN - 1
    cur = lax.shift_right_logical(q0 + lax.broadcasted_iota(jnp.int32, (1, Q_BLOCK), 1),
                                  SLC_BLOCK.bit_length() - 1)

    def cmp_and_select(nc, ns):
        lc = _dot(kc_ref[0, 0, 0:nc, :], qt) + cs_ref[0, pl.ds(c0, nc), :]
        p = jnp.exp(lc - jnp.max(lc, axis=0, keepdims=True))
        pc = p * jnp.where(anyc, 1.0 / jnp.sum(p, axis=0, keepdims=True), 0.0)
        oc_ref[...] = _dot(vc_ref[0, 0, :, 0:nc], pc.astype(BF16))
        imp = pc[:, 0:Q_BLOCK]
        for g in range(1, GROUP):
            imp = imp + pc[:, g * Q_BLOCK:(g + 1) * Q_BLOCK]
        mm = mm_ref[0:ns, 0:nc]
        i1, i2, i3 = _split3(imp)
        score = _dot(mm, i1) + _dot(mm, i2) + _dot(mm, i3)
        blk = lax.broadcasted_iota(jnp.int32, (ns, 1), 0)
        forced = (blk == 0) | (blk == cur) | (blk == cur - 1)
        sc = jnp.where(forced, FORCE_SCORE, jnp.where(blk <= cur, score, NEG_INF))
        big = jnp.int32(INT_BIG)
        for _ in range(min(N_SEL, ns)):
            mx = jnp.max(sc, axis=0, keepdims=True)
            first = jnp.min(jnp.where(sc == mx, blk, big), axis=0, keepdims=True)
            sc = jnp.where(blk == first, PICKED, sc)
        pen_ref[0:ns, :] = jnp.where(sc == PICKED, 0.0, NEG_INF).astype(BF16)
        if ns < n_slc:
            pen_ref[ns:n_slc, :] = jnp.full((n_slc - ns, Q_BLOCK), NEG_INF, BF16)

    for part in range(1, parts + 1):
        @pl.when(jnp.logical_and(j * parts >= (part - 1) * nq, j * parts < part * nq))
        def _(part=part):
            cmp_and_select(nch * part // parts, n_slc * part // parts)

    o_c = oc_ref[...]
    qsel = jnp.concatenate([jnp.concatenate([pen_ref[...]] * GROUP, axis=1), qt], axis=0)

    def scores_sel(k0):
        ke = jnp.concatenate([e_ref[pl.ds(k0, KEY_CHUNK), :], ks_ref[0, 0, pl.ds(k0, KEY_CHUNK), :]], axis=1)
        return _dot(ke, qsel)

    def flash(st, v, first):
        mx = jnp.max(st, axis=0, keepdims=True)
        if first:
            pe = jnp.exp(st - mx)
            l_ref[...] = jnp.sum(pe, axis=0, keepdims=True)
            acc_ref[...] = _dot(v, pe.astype(BF16))
            m_ref[...] = mx
        else:
            m_old = m_ref[...]
            m_new = jnp.maximum(m_old, mx)
            a = jnp.exp(m_old - m_new)
            pe = jnp.exp(st - m_new)
            l_ref[...] = a * l_ref[...] + jnp.sum(pe, axis=0, keepdims=True)
            acc_ref[...] = a * acc_ref[...] + _dot(v, pe.astype(BF16))
            m_ref[...] = m_new

    sta_ref[...] = scores_sel(k_own) + sa_ref[0, pl.ds(own0, KEY_CHUNK), :]
    stb_ref[...] = scores_sel(k_prev) + sa_ref[0, pl.ds(prev0, KEY_CHUNK), :]
    flash(stc_ref[...], vw_ref[0, 0, :, pl.ds(k_own, KEY_CHUNK)], True)
    flash(std_ref[...], vw_ref[0, 0, :, pl.ds(k_prev, KEY_CHUNK)], False)
    o_w = acc_ref[...] * (1.0 / l_ref[...])

    flash(sta_ref[...], vs_ref[0, 0, :, pl.ds(k_own, KEY_CHUNK)], True)
    n_far = jnp.maximum(cq - 1, 0)
    last_chunk = ks_ref.shape[2] // KEY_CHUNK - 1

    def chunk_start(c):
        return pl.multiple_of(jnp.minimum(c, last_chunk) * KEY_CHUNK, KEY_CHUNK)

    sta_ref[...] = scores_sel(chunk_start(0))
    flash(stb_ref[...], vs_ref[0, 0, :, pl.ds(k_prev, KEY_CHUNK)], False)

    def far_pair(i, carry):
        c0 = 2 * i
        stb_ref[...] = scores_sel(chunk_start(c0 + 1))
        flash(sta_ref[...], vs_ref[0, 0, :, pl.ds(chunk_start(c0), KEY_CHUNK)], False)
        sta_ref[...] = scores_sel(chunk_start(c0 + 2))
        flash(stb_ref[...], vs_ref[0, 0, :, pl.ds(chunk_start(c0 + 1), KEY_CHUNK)], False)
        return carry

    lax.fori_loop(0, lax.shift_right_logical(n_far, 1), far_pair, 0)

    @pl.when(jnp.bitwise_and(n_far, 1) == 1)
    def _():
        flash(sta_ref[...], vs_ref[0, 0, :, pl.ds(chunk_start(n_far - 1), KEY_CHUNK)], False)

    o_s = acc_ref[...] * (1.0 / l_ref[...])

    g3 = gate_ref[0, 0, 0]
    o = g3[0:1, :] * o_c + g3[1:2, :] * o_s + g3[2:3, :] * o_w
    stacked = jnp.concatenate([o[:, g * Q_BLOCK:(g + 1) * Q_BLOCK] for g in range(GROUP)], axis=0)
    o_ref[0] = stacked.T


def _bias_tables(table, nch, nq):
    tbl = table.reshape(N_BUCKETS, N_KV_HEADS, GROUP)
    far = tbl[N_BUCKETS - 1]
    i = jnp.arange(Q_BLOCK)

    def cols(rel):
        hot = (_rel_bucket(rel)[..., None] == jnp.arange(N_BUCKETS)).astype(F32)
        b = jnp.einsum('kqn,nhg->kqhg', hot, tbl, precision=lax.Precision.HIGHEST) - far
        b = jnp.where((rel >= 0)[:, :, None, None], b, 0.0)
        return b.transpose(2, 0, 3, 1).reshape(N_KV_HEADS, rel.shape[0], ROWS_Q)

    p0 = cols(i[None, :] - i[:, None])
    p1 = cols(Q_BLOCK + i[None, :] - i[:, None])
    per_q = Q_BLOCK // CMP_STRIDE
    rel_c = i[None, :] - CMP_STRIDE * (jnp.arange(nch + per_q * (nq - 1))[:, None] - per_q * (nq - 1)) - (CMP_LEN - 1)
    strip_cmp = jnp.where(jnp.tile(rel_c >= 0, (1, GROUP))[None], cols(rel_c), NEG_INF)
    kk = jnp.arange(Q_BLOCK)[:, None]
    qq = jnp.tile(jnp.arange(Q_BLOCK), GROUP)[None, :]
    bc = lambda a: jnp.broadcast_to(a[None], (N_KV_HEADS, Q_BLOCK, ROWS_Q))
    neg = bc(jnp.full((Q_BLOCK, ROWS_Q), NEG_INF, F32))
    zero = bc(jnp.zeros((Q_BLOCK, ROWS_Q), F32))
    p0_causal = jnp.where((kk <= qq)[None], p0, NEG_INF)
    in_window = bc(jnp.where(kk > qq, 0.0, NEG_INF).astype(F32))
    strip_own = jnp.concatenate([zero] * (2 * TILES - 2) + [p1, p0_causal] + [neg] * TILES, axis=1)
    strip_win = jnp.concatenate([neg] * (TILES - 1) + [in_window] + [zero] * (TILES - 2) + [p1] + [neg] * TILES,
                                axis=1)
    return strip_own, strip_win, strip_cmp


def _nsa_prompt(q_t, k_hm, vt_hm, kc, vct, gates_t, table):
    B, _, T, hd = k_hm.shape
    H = N_HEADS
    nch = kc.shape[2]
    n_slc = T // SLC_BLOCK
    nq = T // Q_BLOCK
    assert T % KEY_CHUNK == 0 and nch == RATIO * n_slc and WINDOW <= KEY_CHUNK
    strip_own, strip_win, strip_cmp = _bias_tables(table, nch, nq)
    e_tab = (jnp.arange(T)[:, None] // SLC_BLOCK == jnp.arange(n_slc)[None, :]).astype(BF16)
    mm = _score_matrix(nch, n_slc).T
    nk = N_KV_HEADS
    k_spec = lambda slab0: pl.BlockSpec((1, 1, T, hd), lambda b, k, j: (b, slab0 + k, 0, 0))
    vt_spec = lambda slab0: pl.BlockSpec((1, 1, hd, T), lambda b, k, j: (b, slab0 + k, 0, 0))
    per_head = lambda a: pl.BlockSpec((1,) + a.shape[1:], lambda b, k, j: (k, 0, 0))
    score_buf = pltpu.VMEM((KEY_CHUNK, ROWS_Q), F32)
    parts = max(p for p in (4, 2, 1) if nq % p == 0 and (nch // p) % LANE == 0
                and (n_slc // p) % BF16_ROWS == 0)
    return pl.pallas_call(
        functools.partial(_nsa_kernel, nch=nch, n_slc=n_slc, parts=parts),
        out_shape=jax.ShapeDtypeStruct((B, T, H * hd), F32),
        grid=(B, nk, nq),
        in_specs=[pl.BlockSpec((1, 1, 1, hd, ROWS_Q), lambda b, k, j: (b, k, j, 0, 0)),
                  pl.BlockSpec((1, 1, nch, hd), lambda b, k, j: (b, k, 0, 0)),
                  pl.BlockSpec((1, 1, hd, nch), lambda b, k, j: (b, k, 0, 0)),
                  k_spec(0), vt_spec(0), k_spec(nk), vt_spec(nk),
                  pl.BlockSpec((1, 1, 1, 3, ROWS_Q), lambda b, k, j: (b, k, j, 0, 0)),
                  per_head(strip_own), per_head(strip_win), per_head(strip_cmp),
                  pl.BlockSpec(mm.shape, lambda b, k, j: (0, 0)),
                  pl.BlockSpec(e_tab.shape, lambda b, k, j: (0, 0))],
        out_specs=pl.BlockSpec((1, Q_BLOCK, GROUP * hd), lambda b, k, j: (b, j, k)),
        scratch_shapes=[pltpu.VMEM((1, ROWS_Q), F32), pltpu.VMEM((1, ROWS_Q), F32),
                        pltpu.VMEM((hd, ROWS_Q), F32), score_buf, score_buf, score_buf, score_buf,
                        pltpu.VMEM((hd, ROWS_Q), F32), pltpu.VMEM((n_slc, Q_BLOCK), BF16)],
        compiler_params=_cparams(3),
        name="nsa_prompt",
    )(q_t, kc, vct, k_hm, vt_hm, k_hm, vt_hm, gates_t, strip_own, strip_win, strip_cmp, mm, e_tab)


def _smp_cmp_kernel(q_ref, kc_ref, vct_ref, bias_ref, mm_ref, oc_ref, idx_ref, *, n_cmp, n_slc, cur):
    nch = kc_ref.shape[2]
    ncol = mm_ref.shape[1]
    coli = lax.broadcasted_iota(jnp.int32, (1, nch), 1)
    maskc = coli < n_cmp
    blk = lax.broadcasted_iota(jnp.int32, (1, ncol), 1)
    k_sel = min(N_SEL, n_slc)
    lane = lax.broadcasted_iota(jnp.int32, (1, LANE), 1)
    nb = q_ref.shape[0]
    mm = mm_ref[...]
    forced = (blk == 0) | (blk == cur) | (blk == cur - 1)
    scores = []
    for b in range(nb):
        for kh in range(N_KV_HEADS):
            qm = q_ref[b, kh * GROUP:(kh + 1) * GROUP].astype(BF16)
            lc = _dot_nt(qm, kc_ref[b, kh]) + bias_ref[kh]
            lcm = jnp.where(maskc, lc, NEG_INF)
            p = jnp.where(maskc, jnp.exp(lcm - jnp.max(lcm, axis=-1, keepdims=True)), 0.0)
            pc = p * (1.0 / jnp.sum(p, axis=-1, keepdims=True))
            oc_ref[b, kh * GROUP:(kh + 1) * GROUP] = _dot_nt(pc.astype(BF16), vct_ref[b, kh])
            imp = pc[0:1]
            for g in range(1, GROUP):
                imp = imp + pc[g:g + 1]
            i1, i2, i3 = _split3(imp)
            score = _dot(i1, mm) + _dot(i2, mm) + _dot(i3, mm)
            sc = jnp.where(forced, FORCE_SCORE, jnp.where(blk <= cur, score, NEG_INF))
            scores.append(jnp.where(blk < n_slc, sc, NOT_A_SLOT))
    sc = jnp.concatenate(scores, axis=0)
    big = jnp.int32(INT_BIG)
    out = jnp.zeros((nb * N_KV_HEADS, LANE), jnp.int32)
    for it in range(k_sel):
        m = jnp.max(sc, axis=-1, keepdims=True)
        first = jnp.min(jnp.where(sc == m, blk, big), axis=-1, keepdims=True)
        out = jnp.where(lane == it, first, out)
        sc = jnp.where(blk == first, PICKED, sc)
    idx_ref[0] = out


def _smp_cmp(q_hm, kc, vct, bias_c, mm, n_cmp, n_slc, cur):
    B = q_hm.shape[0]
    nch = kc.shape[2]
    nb = math.gcd(B, 8)
    o_c, idx = pl.pallas_call(
        functools.partial(_smp_cmp_kernel, n_cmp=n_cmp, n_slc=n_slc, cur=cur),
        out_shape=(jax.ShapeDtypeStruct((B, N_HEADS, HEAD_DIM), F32),
                   jax.ShapeDtypeStruct((B // nb, nb * N_KV_HEADS, LANE), jnp.int32)),
        grid=(B // nb,),
        in_specs=[pl.BlockSpec((nb, N_HEADS, HEAD_DIM), lambda b: (b, 0, 0)),
                  pl.BlockSpec((nb, N_KV_HEADS, nch, HEAD_DIM), lambda b: (b, 0, 0, 0)),
                  pl.BlockSpec((nb, N_KV_HEADS, HEAD_DIM, nch), lambda b: (b, 0, 0, 0)),
                  pl.BlockSpec(bias_c.shape, lambda b: (0, 0, 0)),
                  pl.BlockSpec(mm.shape, lambda b: (0, 0))],
        out_specs=(pl.BlockSpec((nb, N_HEADS, HEAD_DIM), lambda b: (b, 0, 0)),
                   pl.BlockSpec((1, nb * N_KV_HEADS, LANE), lambda b: (b, 0, 0))),
        compiler_params=_cparams(1),
        name="sample_cmp",
    )(q_hm, kc, vct, bias_c, mm)
    return o_c, idx.reshape(B, N_KV_HEADS, LANE)


def _smp_att_kernel(phys_ref, q_ref, pool_ref, bsel_ref, msel_ref, new_ref, win_ref, bwin_ref, bnew_ref,
                    gate_ref, oc_ref, o_ref, buf_ref, sem_ref, *, k_sel):
    b = pl.program_id(0)
    n_blk = N_KV_HEADS * k_sel

    def blk_copy(i):
        return pltpu.make_async_copy(pool_ref.at[phys_ref[b * n_blk + i]], buf_ref.at[i], sem_ref.at[i])

    for i in range(n_blk):
        blk_copy(i).start()
    for i in range(n_blk):
        blk_copy(i).wait()

    nw = win_ref.shape[4]
    for kh in range(N_KV_HEADS):
        rows = slice(kh * GROUP, (kh + 1) * GROUP)
        qm = q_ref[0, rows].astype(BF16)
        kcol = slice(kh * HEAD_DIM, (kh + 1) * HEAD_DIM)
        vcol = slice(KV_WIDTH + kh * HEAD_DIM, KV_WIDTH + (kh + 1) * HEAD_DIM)
        bnew = bnew_ref[kh]

        def branch(kt, vt, bias, mask, knew, vnew, new_ok):
            lg = _dot(qm, kt) + bias
            lg = jnp.where(mask, lg, NEG_INF)
            ln = jnp.sum(qm.astype(F32) * knew.astype(F32), axis=-1, keepdims=True) + bnew
            if new_ok is not None:
                ln = jnp.where(new_ok, ln, NEG_INF)
            mx = jnp.maximum(jnp.max(lg, axis=-1, keepdims=True), ln)
            pe = jnp.where(mask, jnp.exp(lg - mx), 0.0)
            pn = jnp.exp(ln - mx)
            if new_ok is not None:
                pn = jnp.where(new_ok, pn, 0.0)
            den = jnp.sum(pe, axis=-1, keepdims=True) + pn
            any_ok = den > 0.0
            inv = 1.0 / jnp.where(any_ok, den, 1.0)
            o = _dot_nt(pe.astype(BF16), vt) + pn.astype(BF16).astype(F32) * vnew.astype(F32)
            return jnp.where(any_ok, o * inv, 0.0)

        page = buf_ref.shape[4]
        nkeys = k_sel * page
        kt = jnp.concatenate([buf_ref[kh * k_sel + i, 0, kh] for i in range(k_sel)], axis=1).astype(BF16)
        vt = jnp.concatenate([buf_ref[kh * k_sel + i, 1, kh] for i in range(k_sel)], axis=1).astype(BF16)
        knew = new_ref[0, 0:1, kcol].astype(BF16)
        vnew = new_ref[0, 0:1, vcol].astype(BF16)
        msel = msel_ref[0, kh]
        o_s = branch(kt, vt, bsel_ref[0, kh], msel[:, 0:nkeys] > 0.5, knew, vnew, msel[:, nkeys:nkeys + 1] > 0.5)

        kw = win_ref[0, 0, kh].astype(BF16)
        vw = win_ref[0, 1, kh].astype(BF16)
        knw = new_ref[0, 1:2, kcol].astype(BF16)
        vnw = new_ref[0, 1:2, vcol].astype(BF16)
        dist = nw - lax.broadcasted_iota(jnp.int32, (1, nw), 1)
        o_w = branch(kw, vw, bwin_ref[kh], dist < WINDOW, knw, vnw, None)

        g3 = gate_ref[0, rows]
        o_ref[0, rows] = g3[:, 0:1] * oc_ref[0, rows] + g3[:, 1:2] * o_s + g3[:, 2:3] * o_w


def _smp_att(phys, q_hm, pool_blk, bias_sel, mask_sel, new_rows, win_buf, bias_win, bias_new, gates, o_c, k_sel):
    B = q_hm.shape[0]
    n_blk = N_KV_HEADS * k_sel
    full = lambda a: pl.BlockSpec(a.shape, lambda b, ph: (0,) * a.ndim)
    per_b = lambda a: pl.BlockSpec((1,) + a.shape[1:], lambda b, ph: (b,) + (0,) * (a.ndim - 1))
    return pl.pallas_call(
        functools.partial(_smp_att_kernel, k_sel=k_sel),
        out_shape=jax.ShapeDtypeStruct((B, N_HEADS, HEAD_DIM), F32),
        grid_spec=pltpu.PrefetchScalarGridSpec(
            num_scalar_prefetch=1,
            grid=(B,),
            in_specs=[per_b(q_hm), pl.BlockSpec(memory_space=pl.ANY), per_b(bias_sel), per_b(mask_sel),
                      per_b(new_rows), per_b(win_buf), full(bias_win), full(bias_new), per_b(gates), per_b(o_c)],
            out_specs=pl.BlockSpec((1, N_HEADS, HEAD_DIM), lambda b, ph: (b, 0, 0)),
            scratch_shapes=[pltpu.VMEM((n_blk,) + pool_blk.shape[1:], F32),
                            pltpu.SemaphoreType.DMA((n_blk,))]),
        compiler_params=_cparams(1),
        name="sample_att",
    )(phys, q_hm, pool_blk, bias_sel, mask_sel, new_rows, win_buf, bias_win, bias_new, gates, o_c)


def _merge_kernel(x_ref, cn_ref, att_ref, gna_ref, wo_ref, m2_ref, m3_ref, m4_ref, g2_ref, rw_ref, rb_ref,
                  x1_ref, h2_ref, ti_ref, tg_ref, *, n_exp, cw_width):
    att_n = _rms(att_ref[0], gna_ref[...]).astype(BF16)
    mix = _dot(cn_ref[0], wo_ref[0:cw_width, :]) + _dot(att_n, wo_ref[cw_width:, :])
    x1 = x_ref[0] + m2_ref[0] * mix
    x1_ref[0] = x1
    h2 = _rms(x1, g2_ref[...]) * (1.0 + m4_ref[0]) + m3_ref[0]
    h2_ref[0] = h2
    logits = _dot3(h2, rw_ref[...]) + rb_ref[...]
    lane = lax.broadcasted_iota(jnp.int32, (1, LANE), 1)
    sc = jnp.where(lane < n_exp, logits, NOT_A_SLOT)
    big = jnp.int32(INT_BIG)
    ti = jnp.zeros(sc.shape, jnp.int32)
    tv = jnp.zeros(sc.shape, F32)
    v0 = None
    den = None
    for k in range(TOP_K):
        m = jnp.max(sc, axis=-1, keepdims=True)
        first = jnp.min(jnp.where(sc == m, lane, big), axis=-1, keepdims=True)
        if k == 0:
            v0 = m
        e = jnp.exp(m - v0)
        den = e if den is None else den + e
        ti = jnp.where(lane == k, first, ti)
        tv = jnp.where(lane == k, e, tv)
        sc = jnp.where(lane == first, PICKED, sc)
    ti_ref[0] = ti
    tg_ref[0] = tv * (1.0 / den)


def _merge(x, convn, att, gn_att, wo_b, m2, m3, m4, ln2_g, rw_pad, rb_pad, *, seq_mode, tm, n_exp):
    B, T, D = x.shape
    cw_width = convn.shape[2]
    aw = att.shape[2]
    row_blk = lambda w: pl.BlockSpec((1, tm, w), lambda b, t: (b, t, 0))
    mod_spec = pl.BlockSpec((1, 1, D), lambda b, t: (b, 0, 0)) if seq_mode else row_blk(D)
    const = lambda shape: pl.BlockSpec(shape, lambda b, t: (0,) * len(shape))
    return pl.pallas_call(
        functools.partial(_merge_kernel, n_exp=n_exp, cw_width=cw_width),
        out_shape=(jax.ShapeDtypeStruct((B, T, D), F32), jax.ShapeDtypeStruct((B, T, D), F32),
                   jax.ShapeDtypeStruct((B, T, LANE), jnp.int32), jax.ShapeDtypeStruct((B, T, LANE), F32)),
        grid=(B, T // tm),
        in_specs=[row_blk(D), row_blk(cw_width), row_blk(aw), const((1, aw)), const(wo_b.shape),
                  mod_spec, mod_spec, mod_spec, const((1, D)), const(rw_pad.shape), const((1, LANE))],
        out_specs=(row_blk(D), row_blk(D), row_blk(LANE), row_blk(LANE)),
        compiler_params=_cparams(2),
        name="merge_seq" if seq_mode else "merge_rows",
    )(x, convn, att, gn_att.reshape(1, aw), wo_b, m2, m3, m4, ln2_g.reshape(1, D), rw_pad, rb_pad)


def _moe_kernel(be_ref, nu_ref, xs_ref, wgu_ref, bgu_ref, wd_ref, bd_ref, y_ref, wgu_b, wd_b, *, d_ff):
    i = pl.program_id(0)
    prev = be_ref[jnp.maximum(i - 1, 0)]
    changed = jnp.logical_or(i == 0, be_ref[i] != prev)

    @pl.when(jnp.logical_and(changed, i < nu_ref[0]))
    def _():
        wgu_b[...] = wgu_ref[0].astype(BF16)
        wd_b[...] = wd_ref[0].astype(BF16)

    @pl.when(i < nu_ref[0])
    def _():
        gu = _dot(xs_ref[...].astype(BF16), wgu_b[...]) + bgu_ref[0]
        gate_h = jnp.minimum(gu[:, 0:d_ff], SWIGLU_LIMIT)
        up_h = jnp.clip(gu[:, d_ff:], -SWIGLU_LIMIT, SWIGLU_LIMIT)
        act = (up_h + 1.0) * gate_h * _sigmoid(SWIGLU_ALPHA * gate_h)
        y_ref[...] = _dot(act.astype(BF16), wd_b[...]) + bd_ref[0]

    @pl.when(i >= nu_ref[0])
    def _():
        y_ref[...] = jnp.zeros(y_ref.shape, F32)


def _moe_experts(block_e, n_used, xs, w_gu, b_gu, w_down, b_down):
    n_rows, D = xs.shape
    E, _, two_ff = w_gu.shape
    d_ff = two_ff // 2
    n_blocks = n_rows // MOE_ROWS
    return pl.pallas_call(
        functools.partial(_moe_kernel, d_ff=d_ff),
        out_shape=jax.ShapeDtypeStruct((n_rows, D), F32),
        grid_spec=pltpu.PrefetchScalarGridSpec(
            num_scalar_prefetch=2,
            grid=(n_blocks,),
            in_specs=[pl.BlockSpec((MOE_ROWS, D), lambda i, be, nu: (i, 0)),
                      pl.BlockSpec((1, D, two_ff), lambda i, be, nu: (be[i], 0, 0)),
                      pl.BlockSpec((1, 1, two_ff), lambda i, be, nu: (be[i], 0, 0)),
                      pl.BlockSpec((1, d_ff, D), lambda i, be, nu: (be[i], 0, 0)),
                      pl.BlockSpec((1, 1, D), lambda i, be, nu: (be[i], 0, 0))],
            out_specs=pl.BlockSpec((MOE_ROWS, D), lambda i, be, nu: (i, 0)),
            scratch_shapes=[pltpu.VMEM((D, two_ff), BF16), pltpu.VMEM((d_ff, D), BF16)]),
        compiler_params=_cparams(1),
        name="moe_experts",
    )(block_e, n_used, xs, w_gu, b_gu.reshape(E, 1, two_ff), w_down, b_down.reshape(E, 1, D))


def _route(top_idx, n_tok, n_exp):
    n_assign = n_tok * TOP_K
    flat_e = top_idx.reshape(-1)
    experts = jnp.arange(n_exp, dtype=jnp.int32)
    onehot = (flat_e[:, None] == experts[None, :]).astype(jnp.int32)
    csum = jnp.cumsum(onehot, axis=0)
    counts = csum[-1]
    padded = (counts + MOE_ROWS - 1) // MOE_ROWS * MOE_ROWS
    pad_end = jnp.cumsum(padded)
    pad_start = pad_end - padded
    dest = jnp.sum(onehot * (csum - 1 + pad_start[None, :]), axis=1).astype(jnp.int32)
    n_blocks = -(-(n_assign + n_exp * (MOE_ROWS - 1)) // MOE_ROWS)
    n_rows = n_blocks * MOE_ROWS
    blk_row0 = jnp.arange(n_blocks, dtype=jnp.int32) * MOE_ROWS
    block_e = jnp.minimum(jnp.sum((pad_end[None, :] <= blk_row0[:, None]).astype(jnp.int32), axis=1), n_exp - 1)
    n_fill = n_rows - n_assign
    fill_e = jnp.arange(n_fill, dtype=jnp.int32) // (MOE_ROWS - 1)
    fill_i = jnp.arange(n_fill, dtype=jnp.int32) % (MOE_ROWS - 1)
    fill_on = (fill_e < n_exp) & (fill_i < jnp.sum(
        (fill_e[:, None] == experts[None, :]) * (padded - counts)[None, :], axis=1))
    keys = jnp.concatenate([2 * flat_e, jnp.where(fill_on, 2 * fill_e + 1, 2 * n_exp + 1)])
    toks = jnp.concatenate([jnp.arange(n_assign, dtype=jnp.int32) // TOP_K, jnp.full((n_fill,), n_tok, jnp.int32)])
    _, row_tok = lax.sort((keys, toks), num_keys=1, is_stable=True)
    n_used = (pad_end[-1] // MOE_ROWS).astype(jnp.int32).reshape(1)
    return row_tok, dest.reshape(n_tok, TOP_K), block_e.astype(jnp.int32), n_used


def _final_kernel(x1_ref, yg_ref, tg_ref, m5_ref, fg_ref, o_ref):
    tg = tg_ref[0]
    ff = tg[:, 0:1] * yg_ref[0, 0]
    for k in range(1, TOP_K):
        ff = ff + tg[:, k:k + 1] * yg_ref[0, k]
    o_ref[0] = _rms(x1_ref[0] + m5_ref[0] * ff, fg_ref[...])


def _final(x1, yg, tg, m5, final_g, *, seq_mode, tm):
    B, T, D = x1.shape
    row_blk = lambda w: pl.BlockSpec((1, tm, w), lambda b, t: (b, t, 0))
    mod_spec = pl.BlockSpec((1, 1, D), lambda b, t: (b, 0, 0)) if seq_mode else row_blk(D)
    return pl.pallas_call(
        _final_kernel,
        out_shape=jax.ShapeDtypeStruct((B, T, D), F32),
        grid=(B, T // tm),
        in_specs=[row_blk(D), pl.BlockSpec((1, TOP_K, tm, D), lambda b, t: (b, 0, t, 0)), row_blk(LANE),
                  mod_spec, pl.BlockSpec((1, D), lambda b, t: (0, 0))],
        out_specs=row_blk(D),
        compiler_params=_cparams(2),
        name="final_seq" if seq_mode else "final_rows",
    )(x1, yg, tg, m5, final_g.reshape(1, D))


def _row_tile(t):
    for tm in (512, 256, 128, 64, 32, 16, 8):
        if t % tm == 0:
            return tm
    raise ValueError(f"unsupported row count {t}")


def _rows(a, idx):
    return a.at[idx].get(mode="promise_in_bounds")


def kernel(x_prompt, x_sample, cache_kv_cmp, cache_kv_slc, state_kv_win, state_conv, page_table, c_prompt, c_sample,
           rel_bias_table, ln1_g, ln2_g, w_ada, b_ada, w_in, conv_w, cmp_pe, cmp_w1, cmp_w2, gn_conv, gn_att, w_o,
           router_w, router_b, w_gu, b_gu, w_down, b_down, final_g):
    B, T, D = x_prompt.shape
    BS, TS, _ = x_sample.shape
    depth = w_in.shape[0]
    assert depth == 1 and TS == 1
    n_pool, page = cache_kv_cmp.shape[1], cache_kv_cmp.shape[2]
    n_pages = page_table.shape[1]
    past_len = n_pages * page
    n_exp = router_w.shape[2]
    cw_width = conv_w.shape[2]
    kvw2 = 2 * KV_WIDTH
    assert past_len % SLC_BLOCK == 0 and past_len % CMP_STRIDE == 0 and T % page == 0

    in_cols = w_in.shape[2]
    gate0 = 3 * cw_width + N_HEADS * HEAD_DIM + 3 * kvw2
    w_pad = jnp.pad(w_in[0], ((0, 0), (0, gate0 + LANE - in_cols))).astype(BF16)
    wo_b = w_o[0].astype(BF16)
    rw_pad = jnp.pad(router_w[0], ((0, 0), (0, LANE - n_exp)))
    rb_pad = jnp.pad(router_b[0], (0, LANE - n_exp)).reshape(1, LANE)
    w_blk = _cmp_block_weight(cmp_w1[0])

    n_c = B + BS
    n_cp = -(-n_c // 8) * 8
    c_all = jnp.pad(jnp.concatenate([c_prompt, c_sample], axis=0), ((0, n_cp - n_c), (0, 0)))
    mod = _modulation(c_all, w_ada[0], b_ada[0]).reshape(n_cp, 6, D)
    mp = [mod[:B, i].reshape(B, 1, D) for i in range(6)]
    ms = [mod[B:n_c, i].reshape(1, BS, D) for i in range(6)]

    tm = _row_tile(T)
    nq = T // Q_BLOCK
    zeros_prev = jnp.zeros((B, CONV_K - 1, cw_width), F32)
    convn_p, q_p, kvc_p, kvs_p, kvw_p, gate_p, vlast_p, kh_p, vth_p = _inproj(
        x_prompt, mp[0], mp[1], ln1_g[0], w_pad, conv_w[0], gn_conv[0], zeros_prev, zeros_prev, seq_mode=True, tm=tm)
    pt_p = jnp.arange(B * (T // page), dtype=jnp.int32).reshape(B, T // page)
    g_p = math.gcd(T // page, CMP_PAGES)
    ab_p = _cmp_ab(kvc_p.reshape(B * (T // page), page, kvw2), pt_p, w_blk, g_p)
    kc_p, vct_p = _cmp_finish(ab_p, cmp_pe[0], cmp_w1[0], cmp_w2[0])
    gates_p = gate_p[:, :, :3 * N_HEADS].reshape(B, nq, Q_BLOCK, N_KV_HEADS, GROUP, 3)
    gates_p = gates_p.transpose(0, 3, 1, 5, 4, 2).reshape(B, N_KV_HEADS, nq, 3, ROWS_Q)
    att_p = _nsa_prompt(q_p, kh_p, vth_p, kc_p, vct_p, gates_p, rel_bias_table)
    x1_p, h2_p, ti_p, tg_p = _merge(x_prompt, convn_p, att_p, gn_att[0], wo_b, mp[2], mp[3], mp[4], ln2_g[0],
                                    rw_pad, rb_pad, seq_mode=True, tm=tm, n_exp=n_exp)

    xs_rows = x_sample.reshape(1, BS, D)
    prev2 = state_conv[0][:, 0, :].reshape(1, BS, cw_width)
    prev1 = state_conv[0][:, 1, :].reshape(1, BS, cw_width)
    tms = _row_tile(BS)
    convn_s, q_s, kvc_s, kvs_s, kvw_s, gate_s, v_s = _inproj(
        xs_rows, ms[0], ms[1], ln1_g[0], w_pad, conv_w[0], gn_conv[0], prev2, prev1, seq_mode=False, tm=tms)
    n_cmp = (past_len + TS - CMP_LEN) // CMP_STRIDE + 1
    nch_s = past_len // CMP_STRIDE
    assert n_cmp + 1 == nch_s
    ab_s = _cmp_ab_t(cache_kv_cmp[0].transpose(0, 2, 3, 4, 1), page_table, w_blk,
                     math.gcd(n_pages, CMP_PAGES))
    kc_s, vct_s = _cmp_finish(ab_s, cmp_pe[0], cmp_w1[0], cmp_w2[0])
    t_q = past_len
    n_slc = -(-(past_len + TS) // SLC_BLOCK)
    cur = t_q // SLC_BLOCK
    k_sel = min(N_SEL, n_slc)
    tbl = rel_bias_table.reshape(N_BUCKETS, N_KV_HEADS, GROUP)
    pos_c = jnp.arange(nch_s) * CMP_STRIDE + CMP_LEN - 1
    bias_c = tbl[_rel_bucket(t_q - pos_c)].transpose(1, 2, 0)
    ncol = -(-n_slc // LANE) * LANE
    mm_s = _score_matrix(nch_s, ncol)
    q_s_hm = q_s.reshape(N_HEADS, BS, HEAD_DIM).transpose(1, 0, 2).astype(F32)
    o_c_s, idx_pad = _smp_cmp(q_s_hm, kc_s, vct_s, bias_c, mm_s, n_cmp, n_slc, cur)
    idx = idx_pad[:, :, :k_sel]
    blk_per_page = page // SLC_BLOCK
    pg = jnp.minimum(idx // blk_per_page, n_pages - 1)
    phys = jnp.take_along_axis(page_table, pg.reshape(BS, -1), axis=1).astype(jnp.int32)
    pos_s = pg[..., None] * page + jnp.arange(page)
    in_blk = (pos_s // SLC_BLOCK == idx[..., None]) & (pos_s < past_len)
    pos_s = pos_s.reshape(BS, N_KV_HEADS, k_sel * page)
    bucket_hot = (_rel_bucket(t_q - pos_s)[..., None] == jnp.arange(N_BUCKETS)).astype(F32)
    bias_sel = jnp.einsum('bksn,nkg->bkgs', bucket_hot, tbl, precision=lax.Precision.HIGHEST)
    new_sel = jnp.any(idx == cur, axis=-1, keepdims=True)
    mask_sel = jnp.concatenate([in_blk.reshape(BS, N_KV_HEADS, k_sel * page), new_sel], axis=-1)
    mask_sel = mask_sel.astype(F32)[:, :, None, :]
    nw = state_kv_win.shape[2]
    bias_win = tbl[_rel_bucket(nw - jnp.arange(nw))].transpose(1, 2, 0)
    bias_new = tbl[0].reshape(N_KV_HEADS, GROUP, 1)
    new_rows = jnp.stack([kvs_s[0], kvw_s[0]], axis=1)
    gates_s = gate_s[0, :, :3 * N_HEADS].reshape(BS, N_HEADS, 3)
    pool_t = cache_kv_slc[0].transpose(0, 2, 3, 4, 1)
    win_t = state_kv_win[0].transpose(0, 2, 3, 4, 1)
    att_s = _smp_att(phys.reshape(-1), q_s_hm, pool_t, bias_sel, mask_sel, new_rows,
                     win_t, bias_win, bias_new, gates_s, o_c_s, k_sel)
    att_s = att_s.reshape(1, BS, N_HEADS * HEAD_DIM)
    x1_s, h2_s, ti_s, tg_s = _merge(xs_rows, convn_s, att_s, gn_att[0], wo_b, ms[2], ms[3], ms[4], ln2_g[0],
                                    rw_pad, rb_pad, seq_mode=False, tm=tms, n_exp=n_exp)

    n_tok = B * T + BS
    h2_all = jnp.concatenate([h2_p.reshape(B * T, D), h2_s.reshape(BS, D), jnp.zeros((8, D), F32)], axis=0)
    top_idx = jnp.concatenate([ti_p.reshape(B * T, LANE), ti_s.reshape(BS, LANE)], axis=0)[:, :TOP_K]
    row_tok, dest, block_e, n_used = _route(top_idx, n_tok, n_exp)
    xs = _rows(h2_all, row_tok)
    yb = _moe_experts(block_e, n_used, xs, w_gu[0], b_gu[0], w_down[0], b_down[0])
    dest_p = dest[:B * T].reshape(B, T, TOP_K).transpose(0, 2, 1)
    yg_p = _rows(yb, dest_p.reshape(-1)).reshape(B, TOP_K, T, D)
    yg_s = _rows(yb, dest[B * T:].T.reshape(-1)).reshape(1, TOP_K, BS, D)
    y_p = _final(x1_p, yg_p, tg_p, mp[5], final_g, seq_mode=True, tm=tm)
    y_s = _final(x1_s, yg_s, tg_s, ms[5], final_g, seq_mode=False, tm=tms)

    kv_tail = (2, N_KV_HEADS, HEAD_DIM)
    page_shape = (depth, B, T // page, page) + kv_tail
    w_keep = min(WINDOW, T)
    new_win_s = jnp.concatenate([state_kv_win[0][:, TS:], kvw_s.reshape(BS, TS, *kv_tail)], axis=1)
    new_conv_s = jnp.concatenate([state_conv[0][:, TS:], v_s.reshape(BS, TS, cw_width)], axis=1)
    return (y_p, y_s.reshape(BS, TS, D),
            kvc_p.reshape(page_shape), kvc_s.reshape((depth, BS, TS) + kv_tail),
            kvs_p.reshape(page_shape), kvs_s.reshape((depth, BS, TS) + kv_tail),
            kvw_p[:, T - w_keep:].reshape((depth, B, w_keep) + kv_tail), new_win_s[None],
            vlast_p[None], new_conv_s[None])
```

```python
import functools
import math

import numpy as np
import jax
import jax.numpy as jnp
from jax import lax
from jax.experimental import pallas as pl
from jax.experimental.pallas import tpu as pltpu

F32 = jnp.float32
BF16 = jnp.bfloat16

CONV_K = 3
N_HEADS = 8
N_KV_HEADS = 2
GROUP = N_HEADS // N_KV_HEADS
HEAD_DIM = 64
KV_WIDTH = N_KV_HEADS * HEAD_DIM
CMP_LEN = 32
CMP_STRIDE = 16
SLC_BLOCK = 64
RATIO = SLC_BLOCK // CMP_STRIDE
N_SEL = 16
WINDOW = 512
Q_BLOCK = 128
N_BUCKETS = 32
REL_MAX_DIST = 128
TOP_K = 4
SWIGLU_LIMIT = 7.0
SWIGLU_ALPHA = 1.702
EPS = 1e-6
NEG_INF = -1e30
FORCE_SCORE = 1e4

LANE = 128
BF16_ROWS = 16
ROWS_Q = GROUP * Q_BLOCK
KEY_CHUNK = 512
TILES = KEY_CHUNK // Q_BLOCK
FAR_UNROLL = 4
MOE_ROWS = 512
CMP_PAGES = 32
MOD_COLS = 1536
VMEM_LIMIT = 48 * 1024 * 1024
PICKED = -3e38
NOT_A_SLOT = -2e38
INT_BIG = 1 << 30


def _cparams(n_axes, vmem_limit=VMEM_LIMIT):
    return pltpu.CompilerParams(dimension_semantics=("arbitrary",) * n_axes, vmem_limit_bytes=vmem_limit)


def _dot(a, b):
    return jnp.dot(a, b, preferred_element_type=F32)


def _dot_nt(a, b):
    return lax.dot_general(a, b, (((1,), (1,)), ((), ())), preferred_element_type=F32)


def _split2(x):
    hi = x.astype(BF16)
    lo = (x - hi.astype(F32)).astype(BF16)
    return hi, lo


def _split3(x):
    a = x.astype(BF16)
    r = x - a.astype(F32)
    b = r.astype(BF16)
    c = (r - b.astype(F32)).astype(BF16)
    return a, b, c


def _dot3(a, b):
    ah, al = _split2(a)
    bh, bl = _split2(b)
    return _dot(ah, bh) + _dot(ah, bl) + _dot(al, bh)


def _sigmoid(x):
    return 1.0 / (1.0 + jnp.exp(-x))


def _rms(x, g):
    return x * lax.rsqrt(jnp.mean(x * x, axis=-1, keepdims=True) + EPS) * g


def _rel_bucket(dist):
    n = jnp.maximum(dist, 0)
    max_exact = N_BUCKETS // 2
    large = max_exact + (jnp.log(jnp.maximum(n, 1).astype(F32) / max_exact)
                         / math.log(REL_MAX_DIST / max_exact) * (N_BUCKETS - max_exact)).astype(jnp.int32)
    return jnp.where(n < max_exact, n, jnp.minimum(large, N_BUCKETS - 1))


def _mod_kernel(c_ref, w_ref, b_ref, o_ref):
    c = c_ref[...]
    o_ref[...] = _dot3(c * _sigmoid(c), w_ref[...]) + b_ref[...]


def _modulation(c, w_ada, b_ada):
    n, d = c.shape
    cols = w_ada.shape[1]
    bn = math.gcd(cols, MOD_COLS)
    return pl.pallas_call(
        _mod_kernel,
        out_shape=jax.ShapeDtypeStruct((n, cols), F32),
        grid=(cols // bn,),
        in_specs=[pl.BlockSpec((n, d), lambda i: (0, 0)),
                  pl.BlockSpec((d, bn), lambda i: (0, i)),
                  pl.BlockSpec((1, bn), lambda i: (0, i))],
        out_specs=pl.BlockSpec((n, bn), lambda i: (0, i)),
        compiler_params=_cparams(1),
        name="modulation",
    )(c, w_ada, b_ada.reshape(1, cols))


def _inproj_kernel(x_ref, m0_ref, m1_ref, g1_ref, w_ref, cw_ref, gnc_ref, pa_ref, pb_ref,
                   convn_ref, q_ref, kvc_ref, kvs_ref, kvw_ref, gate_ref, vlast_ref, *rest, seq_mode, tm, cw_width):
    if seq_mode:
        kh_ref, vth_ref, carry_ref = rest
    x = x_ref[0]
    h = _rms(x, g1_ref[...]) * (1.0 + m1_ref[0]) + m0_ref[0]
    hb = h.astype(BF16)
    c3 = 3 * cw_width
    uc = _dot(hb, w_ref[:, 0:c3])
    b_g = uc[:, 0:cw_width]
    v = uc[:, cw_width:2 * cw_width] * uc[:, 2 * cw_width:c3]
    if seq_mode:
        @pl.when(pl.program_id(1) == 0)
        def _():
            carry_ref[0:2, :] = pa_ref[0]
        c0 = carry_ref[0:1, :]
        c1 = carry_ref[1:2, :]
        row = lax.broadcasted_iota(jnp.int32, (tm, 1), 0)
        vm1 = jnp.where(row == 0, c1, pltpu.roll(v, 1, 0))
        vm2 = jnp.where(row == 0, c0, jnp.where(row == 1, c1, pltpu.roll(v, 2, 0)))
        carry_ref[0:2, :] = v[tm - 2:tm, :]
        vlast_ref[0] = v[tm - 2:tm, :]
    else:
        vm2 = pa_ref[0]
        vm1 = pb_ref[0]
        vlast_ref[0] = v
    cw = cw_ref[...]
    y = cw[0:1, :] * vm2 + cw[1:2, :] * vm1 + cw[2:3, :] * v
    convn_ref[0] = _rms(b_g * y, gnc_ref[...]).astype(BF16)

    aw = N_HEADS * HEAD_DIM
    uq = _dot(hb, w_ref[:, c3:c3 + aw]) * (HEAD_DIM ** -0.5)
    if seq_mode:
        uqt = uq.T
        for k in range(N_KV_HEADS):
            for jj in range(tm // Q_BLOCK):
                q_ref[0, k, jj] = jnp.concatenate(
                    [uqt[(k * GROUP + g) * HEAD_DIM:(k * GROUP + g + 1) * HEAD_DIM, jj * Q_BLOCK:(jj + 1) * Q_BLOCK]
                     for g in range(GROUP)], axis=1).astype(BF16)
    else:
        for hh in range(N_HEADS):
            q_ref[0, hh] = uq[:, hh * HEAD_DIM:(hh + 1) * HEAD_DIM].astype(BF16)
    kv0 = c3 + aw
    kvw3 = 3 * 2 * KV_WIDTH
    ukv = _dot(hb, w_ref[:, kv0:kv0 + kvw3])
    kvc_ref[0] = ukv[:, 0:2 * KV_WIDTH]
    kvs_ref[0] = ukv[:, 2 * KV_WIDTH:4 * KV_WIDTH]
    kvw_ref[0] = ukv[:, 4 * KV_WIDTH:6 * KV_WIDTH]
    if seq_mode:
        for br in range(2):
            c0 = (br + 1) * 2 * KV_WIDTH
            for k in range(N_KV_HEADS):
                kh_ref[0, br * N_KV_HEADS + k] = ukv[:, c0 + k * HEAD_DIM:c0 + (k + 1) * HEAD_DIM].astype(BF16)
            vt = ukv[:, c0 + KV_WIDTH:c0 + 2 * KV_WIDTH].T.astype(BF16)
            for k in range(N_KV_HEADS):
                vth_ref[0, br * N_KV_HEADS + k] = vt[k * HEAD_DIM:(k + 1) * HEAD_DIM, :]
    ug = _dot(hb, w_ref[:, kv0 + kvw3:kv0 + kvw3 + LANE])
    gate_ref[0] = _sigmoid(ug)


def _inproj(x, m0, m1, ln_g, w_pad, conv_w, gn_conv, pa, pb, *, seq_mode, tm):
    B, T, D = x.shape
    cw_width = conv_w.shape[1]
    nt = T // tm
    row_blk = lambda w: pl.BlockSpec((1, tm, w), lambda b, t: (b, t, 0))
    if seq_mode:
        mod_spec = pl.BlockSpec((1, 1, D), lambda b, t: (b, 0, 0))
        prev_spec = pl.BlockSpec((1, 2, cw_width), lambda b, t: (b, 0, 0))
        vlast_shape = jax.ShapeDtypeStruct((B, 2, cw_width), F32)
        vlast_spec = pl.BlockSpec((1, 2, cw_width), lambda b, t: (b, 0, 0))
    else:
        mod_spec = row_blk(D)
        prev_spec = row_blk(cw_width)
        vlast_shape = jax.ShapeDtypeStruct((B, T, cw_width), F32)
        vlast_spec = row_blk(cw_width)
    const = lambda shape: pl.BlockSpec(shape, lambda b, t: (0,) * len(shape))
    kern = functools.partial(_inproj_kernel, seq_mode=seq_mode, tm=tm, cw_width=cw_width)
    if seq_mode:
        assert tm % Q_BLOCK == 0
        q_shape = jax.ShapeDtypeStruct((B, N_KV_HEADS, T // Q_BLOCK, HEAD_DIM, ROWS_Q), BF16)
        q_spec = pl.BlockSpec((1, N_KV_HEADS, tm // Q_BLOCK, HEAD_DIM, ROWS_Q), lambda b, t: (b, 0, t, 0, 0))
    else:
        q_shape = jax.ShapeDtypeStruct((B, N_HEADS, T, HEAD_DIM), BF16)
        q_spec = pl.BlockSpec((1, N_HEADS, tm, HEAD_DIM), lambda b, t: (b, 0, t, 0))
    out_shape = [jax.ShapeDtypeStruct((B, T, cw_width), BF16),
                 q_shape,
                 jax.ShapeDtypeStruct((B, T, 2 * KV_WIDTH), F32),
                 jax.ShapeDtypeStruct((B, T, 2 * KV_WIDTH), F32),
                 jax.ShapeDtypeStruct((B, T, 2 * KV_WIDTH), F32),
                 jax.ShapeDtypeStruct((B, T, LANE), F32),
                 vlast_shape]
    out_specs = [row_blk(cw_width),
                 q_spec,
                 row_blk(2 * KV_WIDTH), row_blk(2 * KV_WIDTH), row_blk(2 * KV_WIDTH),
                 row_blk(LANE), vlast_spec]
    scratch = []
    if seq_mode:
        n_att = 2 * N_KV_HEADS
        out_shape += [jax.ShapeDtypeStruct((B, n_att, T, HEAD_DIM), BF16),
                      jax.ShapeDtypeStruct((B, n_att, HEAD_DIM, T), BF16)]
        out_specs += [pl.BlockSpec((1, n_att, tm, HEAD_DIM), lambda b, t: (b, 0, t, 0)),
                      pl.BlockSpec((1, n_att, HEAD_DIM, tm), lambda b, t: (b, 0, 0, t))]
        scratch = [pltpu.VMEM((8, cw_width), F32)]
    return pl.pallas_call(
        kern,
        out_shape=tuple(out_shape),
        grid=(B, nt),
        in_specs=[row_blk(D), mod_spec, mod_spec, const((1, D)), const(w_pad.shape), const(conv_w.shape),
                  const((1, cw_width)), prev_spec, prev_spec],
        out_specs=tuple(out_specs),
        scratch_shapes=scratch,
        compiler_params=_cparams(2),
        name="inproj_seq" if seq_mode else "inproj_rows",
    )(x, m0, m1, ln_g.reshape(1, D), w_pad, conv_w, gn_conv.reshape(1, cw_width), pa, pb)


def _cmp_ab_kernel(pt_ref, *refs, G):
    pages = refs[:2 * G]
    w_ref, out_ref, x_ref = refs[2 * G], refs[2 * G + 1], refs[2 * G + 2]
    half = KV_WIDTH
    for j in range(G):
        for r in range(CMP_STRIDE):
            for c in range(2):
                x_ref[c, j * 8:(j + 1) * 8, r * half:(r + 1) * half] = (
                    pages[2 * j + c][0, pl.ds(r, 8, stride=CMP_STRIDE), :])
    wcols = w_ref.shape[2]
    for c in range(2):
        out_ref[0, :, c * wcols:(c + 1) * wcols] = _dot(x_ref[c].astype(BF16), w_ref[c])


def _cmp_ab(pool, page_table, w_blk, G):
    P, page, width = pool.shape
    B, n_pages = page_table.shape
    cpp = page // CMP_STRIDE
    assert cpp == 8 and n_pages % G == 0
    kdim = CMP_STRIDE * width // 2

    def pg_spec(j, c):
        return pl.BlockSpec((1, page, width // 2), lambda b, g, pt: (pt[b * n_pages + g * G + j], 0, c))

    return pl.pallas_call(
        functools.partial(_cmp_ab_kernel, G=G),
        out_shape=jax.ShapeDtypeStruct((B, n_pages * cpp, 2 * w_blk.shape[2]), F32),
        grid_spec=pltpu.PrefetchScalarGridSpec(
            num_scalar_prefetch=1,
            grid=(B, n_pages // G),
            in_specs=[pg_spec(j, c) for j in range(G) for c in range(2)]
            + [pl.BlockSpec(w_blk.shape, lambda b, g, pt: (0, 0, 0))],
            out_specs=pl.BlockSpec((1, G * cpp, 2 * w_blk.shape[2]), lambda b, g, pt: (b, g, 0)),
            scratch_shapes=[pltpu.VMEM((2, G * cpp, kdim), F32)]),
        compiler_params=_cparams(2),
        name="cmp_partial",
    )(page_table.reshape(-1).astype(jnp.int32), *([pool] * (2 * G)), w_blk)


def _cmp_ab_t_kernel(pt_ref, *refs, G):
    pages = refs[:G]
    w_ref, out_ref, x_ref, s_ref = refs[G], refs[G + 1], refs[G + 2], refs[G + 3]
    half = KV_WIDTH
    for j in range(G):
        for c in range(2):
            for k in range(N_KV_HEADS):
                s_ref[c, j, :, k * HEAD_DIM:(k + 1) * HEAD_DIM] = pages[j][0, c, k].astype(BF16).T.astype(F32)
    for j in range(G):
        for r in range(CMP_STRIDE):
            for c in range(2):
                x_ref[c, j * 8:(j + 1) * 8, r * half:(r + 1) * half] = (
                    s_ref[c, j, pl.ds(r, 8, stride=CMP_STRIDE), :])
    wcols = w_ref.shape[2]
    for c in range(2):
        out_ref[0, :, c * wcols:(c + 1) * wcols] = _dot(x_ref[c].astype(BF16), w_ref[c])


def _cmp_ab_t(pool_t, page_table, w_blk, G):
    P, _, _, hd, page = pool_t.shape
    B, n_pages = page_table.shape
    cpp = page // CMP_STRIDE
    assert cpp == 8 and n_pages % G == 0
    kdim = CMP_STRIDE * KV_WIDTH

    def pg_spec(j):
        return pl.BlockSpec((1,) + pool_t.shape[1:], lambda b, g, pt: (pt[b * n_pages + g * G + j], 0, 0, 0, 0))

    return pl.pallas_call(
        functools.partial(_cmp_ab_t_kernel, G=G),
        out_shape=jax.ShapeDtypeStruct((B, n_pages * cpp, 2 * w_blk.shape[2]), F32),
        grid_spec=pltpu.PrefetchScalarGridSpec(
            num_scalar_prefetch=1,
            grid=(B, n_pages // G),
            in_specs=[pg_spec(j) for j in range(G)] + [pl.BlockSpec(w_blk.shape, lambda b, g, pt: (0, 0, 0))],
            out_specs=pl.BlockSpec((1, G * cpp, 2 * w_blk.shape[2]), lambda b, g, pt: (b, g, 0)),
            scratch_shapes=[pltpu.VMEM((2, G * cpp, kdim), F32), pltpu.VMEM((2, G, page, KV_WIDTH), F32)]),
        compiler_params=_cparams(2),
        name="cmp_partial_t",
    )(page_table.reshape(-1).astype(jnp.int32), *([pool_t] * G), w_blk)


def _gelu_tanh(x):
    return 0.5 * x * (1.0 + jnp.tanh(math.sqrt(2.0 / math.pi) * (x + 0.044715 * (x * x * x))))


def _cmp_fin_kernel(ab_ref, pe_ref, w1_ref, w2_ref, kc_ref, vct_ref, *, nch):
    for c in range(2):
        pe_t = _dot(pe_ref[c:c + 1, :].astype(BF16), w1_ref[c].astype(BF16))
        w2 = w2_ref[c].astype(BF16)
        for k in range(N_KV_HEADS):
            base = (c * N_KV_HEADS + k) * 2 * HEAD_DIM
            slab = ab_ref[0, :, base:base + 2 * HEAD_DIM]
            nxt = pltpu.roll(slab, nch - 1, 0)
            pre = slab[:, 0:HEAD_DIM] + nxt[:, HEAD_DIM:2 * HEAD_DIM] + pe_t
            blocks = _dot(_gelu_tanh(pre).astype(BF16), w2)
            if c == 0:
                kc_ref[0, k] = blocks.astype(BF16)
            else:
                vct_ref[0, k] = blocks.T.astype(BF16)


def _cmp_finish(ab, cmp_pe, cmp_w1, cmp_w2):
    B, nch, w = ab.shape
    pe = cmp_pe.reshape(2, CMP_LEN * HEAD_DIM)
    return pl.pallas_call(
        functools.partial(_cmp_fin_kernel, nch=nch),
        out_shape=(jax.ShapeDtypeStruct((B, N_KV_HEADS, nch, HEAD_DIM), BF16),
                   jax.ShapeDtypeStruct((B, N_KV_HEADS, HEAD_DIM, nch), BF16)),
        grid=(B,),
        in_specs=[pl.BlockSpec((1, nch, w), lambda b: (b, 0, 0)),
                  pl.BlockSpec(pe.shape, lambda b: (0, 0)),
                  pl.BlockSpec(cmp_w1.shape, lambda b: (0, 0, 0)),
                  pl.BlockSpec(cmp_w2.shape, lambda b: (0, 0, 0))],
        out_specs=(pl.BlockSpec((1, N_KV_HEADS, nch, HEAD_DIM), lambda b: (b, 0, 0, 0)),
                   pl.BlockSpec((1, N_KV_HEADS, HEAD_DIM, nch), lambda b: (b, 0, 0, 0))),
        compiler_params=_cparams(1),
        name="cmp_finish",
    )(ab, pe, cmp_w1, cmp_w2)


def _cmp_block_weight(cmp_w1):
    hid = cmp_w1.shape[2]
    w = cmp_w1.reshape(2, 2, CMP_STRIDE, HEAD_DIM, hid)
    eye = jnp.eye(N_KV_HEADS, dtype=cmp_w1.dtype)
    wb = jnp.einsum('cardh,kj->crkdjah', w, eye)
    return wb.reshape(2, CMP_STRIDE * N_KV_HEADS * HEAD_DIM, N_KV_HEADS * 2 * hid).astype(BF16)


def _score_matrix(nch, n_slc):
    i = np.arange(nch)[:, None]
    j = np.arange(n_slc)[None, :]
    m = 2.0 * ((i // RATIO == j) & (i % RATIO < RATIO - 1)) + 1.0 * (i == RATIO * j + RATIO - 1) \
        + 1.0 * (i == RATIO * j - 1)
    return jnp.asarray(m, dtype=BF16)


def _nsa_kernel(q_ref, kc_ref, vc_ref, ks_ref, vs_ref, kw_ref, vw_ref, gate_ref, sa_ref, sw_ref,
                cs_ref, mm_ref, e_ref, o_ref, m_ref, l_ref, acc_ref, sta_ref, stb_ref, stc_ref, std_ref,
                oc_ref, pen_ref, *, nch, n_slc, parts):
    j = pl.program_id(2)
    q0 = pl.multiple_of(j * Q_BLOCK, Q_BLOCK)
    cq = lax.shift_right_logical(j, TILES.bit_length() - 1)
    jm = jnp.bitwise_and(j, TILES - 1)
    qt = q_ref[0, 0, 0]
    lane_q = jnp.bitwise_and(lax.broadcasted_iota(jnp.int32, (1, ROWS_Q), 1), Q_BLOCK - 1)
    tq = q0 + lane_q
    has_prev = cq >= 1
    k_own = pl.multiple_of(cq * KEY_CHUNK, KEY_CHUNK)
    k_prev = pl.multiple_of(jnp.maximum(cq - 1, 0) * KEY_CHUNK, KEY_CHUNK)
    own0 = pl.multiple_of((2 * TILES - 1 - jm) * Q_BLOCK, Q_BLOCK)
    prev0 = pl.multiple_of(jnp.where(has_prev, TILES - 1 - jm, 2 * TILES) * Q_BLOCK, Q_BLOCK)
    prevw0 = pl.multiple_of(jnp.where(has_prev, TILES - 1 - jm, 2 * TILES - 1) * Q_BLOCK, Q_BLOCK)

    stc_ref[...] = _dot(kw_ref[0, 0, pl.ds(k_own, KEY_CHUNK), :], qt) + sa_ref[0, pl.ds(own0, KEY_CHUNK), :]
    std_ref[...] = _dot(kw_ref[0, 0, pl.ds(k_prev, KEY_CHUNK), :], qt) + sw_ref[0, pl.ds(prevw0, KEY_CHUNK), :]

    per_q = Q_BLOCK // CMP_STRIDE
    nq = pl.num_programs(2)
    c0 = pl.multiple_of((nq - 1 - j) * per_q, per_q)
    anyc = tq >= CMP_LEN - 1
    cur = lax.shift_right_logical(q0 + lax.broadcasted_iota(jnp.int32, (1, Q_BLOCK), 1),
                                  SLC_BLOCK.bit_length() - 1)

    def cmp_and_select(nc, ns):
        lc = _dot(kc_ref[0, 0, 0:nc, :], qt) + cs_ref[0, pl.ds(c0, nc), :]
        p = jnp.exp(lc - jnp.max(lc, axis=0, keepdims=True))
        pc = p * jnp.where(anyc, 1.0 / jnp.sum(p, axis=0, keepdims=True), 0.0)
        oc_ref[...] = _dot(vc_ref[0, 0, :, 0:nc], pc.astype(BF16))
        imp = pc[:, 0:Q_BLOCK]
        for g in range(1, GROUP):
            imp = imp + pc[:, g * Q_BLOCK:(g + 1) * Q_BLOCK]
        mm = mm_ref[0:ns, 0:nc]
        i1, i2, i3 = _split3(imp)
        score = _dot(mm, i1) + _dot(mm, i2) + _dot(mm, i3)
        blk = lax.broadcasted_iota(jnp.int32, (ns, 1), 0)
        forced = (blk == 0) | (blk == cur) | (blk == cur - 1)
        sc = jnp.where(forced, FORCE_SCORE, jnp.where(blk <= cur, score, NEG_INF))
        big = jnp.int32(INT_BIG)
        for _ in range(min(N_SEL, ns)):
            mx = jnp.max(sc, axis=0, keepdims=True)
            first = jnp.min(jnp.where(sc == mx, blk, big), axis=0, keepdims=True)
            sc = jnp.where(blk == first, PICKED, sc)
        pen_ref[0:ns, :] = jnp.where(sc == PICKED, 0.0, NEG_INF).astype(BF16)
        if ns < n_slc:
            pen_ref[ns:n_slc, :] = jnp.full((n_slc - ns, Q_BLOCK), NEG_INF, BF16)

    for part in range(1, parts + 1):
        @pl.when(jnp.logical_and(j * parts >= (part - 1) * nq, j * parts < part * nq))
        def _(part=part):
            cmp_and_select(nch * part // parts, n_slc * part // parts)

    o_c = oc_ref[...]
    qsel = jnp.concatenate([jnp.concatenate([pen_ref[...]] * GROUP, axis=1), qt], axis=0)

    def scores_sel(k0):
        ke = jnp.concatenate([e_ref[pl.ds(k0, KEY_CHUNK), :], ks_ref[0, 0, pl.ds(k0, KEY_CHUNK), :]], axis=1)
        return _dot(ke, qsel)

    def flash(st, v, first):
        mx = jnp.max(st, axis=0, keepdims=True)
        if first:
            pe = jnp.exp(st - mx)
            l_ref[...] = jnp.sum(pe, axis=0, keepdims=True)
            acc_ref[...] = _dot(v, pe.astype(BF16))
            m_ref[...] = mx
        else:
            m_old = m_ref[...]
            m_new = jnp.maximum(m_old, mx)
            a = jnp.exp(m_old - m_new)
            pe = jnp.exp(st - m_new)
            l_ref[...] = a * l_ref[...] + jnp.sum(pe, axis=0, keepdims=True)
            acc_ref[...] = a * acc_ref[...] + _dot(v, pe.astype(BF16))
            m_ref[...] = m_new

    sta_ref[...] = scores_sel(k_own) + sa_ref[0, pl.ds(own0, KEY_CHUNK), :]
    stb_ref[...] = scores_sel(k_prev) + sa_ref[0, pl.ds(prev0, KEY_CHUNK), :]
    flash(stc_ref[...], vw_ref[0, 0, :, pl.ds(k_own, KEY_CHUNK)], True)
    flash(std_ref[...], vw_ref[0, 0, :, pl.ds(k_prev, KEY_CHUNK)], False)
    o_w = acc_ref[...] * (1.0 / l_ref[...])

    flash(sta_ref[...], vs_ref[0, 0, :, pl.ds(k_own, KEY_CHUNK)], True)
    n_far = jnp.maximum(cq - 1, 0)
    last_chunk = ks_ref.shape[2] // KEY_CHUNK - 1

    def chunk_start(c):
        return pl.multiple_of(jnp.minimum(c, last_chunk) * KEY_CHUNK, KEY_CHUNK)

    sta_ref[...] = scores_sel(chunk_start(0))
    flash(stb_ref[...], vs_ref[0, 0, :, pl.ds(k_prev, KEY_CHUNK)], False)

    def far_steps(c0, n):
        bufs = (sta_ref, stb_ref)
        for t in range(n):
            cur_buf, nxt_buf = bufs[t % 2], bufs[(t + 1) % 2]
            nxt_buf[...] = scores_sel(chunk_start(c0 + t + 1))
            flash(cur_buf[...], vs_ref[0, 0, :, pl.ds(chunk_start(c0 + t), KEY_CHUNK)], False)

    def far_group(i, carry):
        far_steps(FAR_UNROLL * i, FAR_UNROLL)
        return carry

    n_groups = lax.shift_right_logical(n_far, FAR_UNROLL.bit_length() - 1)
    lax.fori_loop(0, n_groups, far_group, 0)
    rest0 = n_groups * FAR_UNROLL

    @pl.when(jnp.bitwise_and(n_far, 2) == 2)
    def _():
        far_steps(rest0, 2)

    @pl.when(jnp.bitwise_and(n_far, 1) == 1)
    def _():
        flash(sta_ref[...], vs_ref[0, 0, :, pl.ds(chunk_start(n_far - 1), KEY_CHUNK)], False)

    o_s = acc_ref[...] * (1.0 / l_ref[...])

    g3 = gate_ref[0, 0, 0]
    o = g3[0:1, :] * o_c + g3[1:2, :] * o_s + g3[2:3, :] * o_w
    stacked = jnp.concatenate([o[:, g * Q_BLOCK:(g + 1) * Q_BLOCK] for g in range(GROUP)], axis=0)
    o_ref[0] = stacked.T


def _bias_tables(table, nch, nq):
    tbl = table.reshape(N_BUCKETS, N_KV_HEADS, GROUP)
    far = tbl[N_BUCKETS - 1]
    i = jnp.arange(Q_BLOCK)

    def cols(rel):
        hot = (_rel_bucket(rel)[..., None] == jnp.arange(N_BUCKETS)).astype(F32)
        b = jnp.einsum('kqn,nhg->kqhg', hot, tbl, precision=lax.Precision.HIGHEST) - far
        b = jnp.where((rel >= 0)[:, :, None, None], b, 0.0)
        return b.transpose(2, 0, 3, 1).reshape(N_KV_HEADS, rel.shape[0], ROWS_Q)

    p0 = cols(i[None, :] - i[:, None])
    p1 = cols(Q_BLOCK + i[None, :] - i[:, None])
    per_q = Q_BLOCK // CMP_STRIDE
    rel_c = i[None, :] - CMP_STRIDE * (jnp.arange(nch + per_q * (nq - 1))[:, None] - per_q * (nq - 1)) - (CMP_LEN - 1)
    strip_cmp = jnp.where(jnp.tile(rel_c >= 0, (1, GROUP))[None], cols(rel_c), NEG_INF)
    kk = jnp.arange(Q_BLOCK)[:, None]
    qq = jnp.tile(jnp.arange(Q_BLOCK), GROUP)[None, :]
    bc = lambda a: jnp.broadcast_to(a[None], (N_KV_HEADS, Q_BLOCK, ROWS_Q))
    neg = bc(jnp.full((Q_BLOCK, ROWS_Q), NEG_INF, F32))
    zero = bc(jnp.zeros((Q_BLOCK, ROWS_Q), F32))
    p0_causal = jnp.where((kk <= qq)[None], p0, NEG_INF)
    in_window = bc(jnp.where(kk > qq, 0.0, NEG_INF).astype(F32))
    strip_own = jnp.concatenate([zero] * (2 * TILES - 2) + [p1, p0_causal] + [neg] * TILES, axis=1)
    strip_win = jnp.concatenate([neg] * (TILES - 1) + [in_window] + [zero] * (TILES - 2) + [p1] + [neg] * TILES,
                                axis=1)
    return strip_own, strip_win, strip_cmp


def _nsa_prompt(q_t, k_hm, vt_hm, kc, vct, gates_t, table):
    B, _, T, hd = k_hm.shape
    H = N_HEADS
    nch = kc.shape[2]
    n_slc = T // SLC_BLOCK
    nq = T // Q_BLOCK
    assert T % KEY_CHUNK == 0 and nch == RATIO * n_slc and WINDOW <= KEY_CHUNK
    strip_own, strip_win, strip_cmp = _bias_tables(table, nch, nq)
    e_tab = (jnp.arange(T)[:, None] // SLC_BLOCK == jnp.arange(n_slc)[None, :]).astype(BF16)
    mm = _score_matrix(nch, n_slc).T
    nk = N_KV_HEADS
    k_spec = lambda slab0: pl.BlockSpec((1, 1, T, hd), lambda b, k, j: (b, slab0 + k, 0, 0))
    vt_spec = lambda slab0: pl.BlockSpec((1, 1, hd, T), lambda b, k, j: (b, slab0 + k, 0, 0))
    per_head = lambda a: pl.BlockSpec((1,) + a.shape[1:], lambda b, k, j: (k, 0, 0))
    score_buf = pltpu.VMEM((KEY_CHUNK, ROWS_Q), F32)
    parts = max(p for p in (4, 2, 1) if nq % p == 0 and (nch // p) % LANE == 0
                and (n_slc // p) % BF16_ROWS == 0)
    return pl.pallas_call(
        functools.partial(_nsa_kernel, nch=nch, n_slc=n_slc, parts=parts),
        out_shape=jax.ShapeDtypeStruct((B, T, H * hd), F32),
        grid=(B, nk, nq),
        in_specs=[pl.BlockSpec((1, 1, 1, hd, ROWS_Q), lambda b, k, j: (b, k, j, 0, 0)),
                  pl.BlockSpec((1, 1, nch, hd), lambda b, k, j: (b, k, 0, 0)),
                  pl.BlockSpec((1, 1, hd, nch), lambda b, k, j: (b, k, 0, 0)),
                  k_spec(0), vt_spec(0), k_spec(nk), vt_spec(nk),
                  pl.BlockSpec((1, 1, 1, 3, ROWS_Q), lambda b, k, j: (b, k, j, 0, 0)),
                  per_head(strip_own), per_head(strip_win), per_head(strip_cmp),
                  pl.BlockSpec(mm.shape, lambda b, k, j: (0, 0)),
                  pl.BlockSpec(e_tab.shape, lambda b, k, j: (0, 0))],
        out_specs=pl.BlockSpec((1, Q_BLOCK, GROUP * hd), lambda b, k, j: (b, j, k)),
        scratch_shapes=[pltpu.VMEM((1, ROWS_Q), F32), pltpu.VMEM((1, ROWS_Q), F32),
                        pltpu.VMEM((hd, ROWS_Q), F32), score_buf, score_buf, score_buf, score_buf,
                        pltpu.VMEM((hd, ROWS_Q), F32), pltpu.VMEM((n_slc, Q_BLOCK), BF16)],
        compiler_params=_cparams(3),
        name="nsa_prompt",
    )(q_t, kc, vct, k_hm, vt_hm, k_hm, vt_hm, gates_t, strip_own, strip_win, strip_cmp, mm, e_tab)


def _smp_cmp_kernel(q_ref, kc_ref, vct_ref, bias_ref, mm_ref, oc_ref, idx_ref, *, n_cmp, n_slc, cur):
    nch = kc_ref.shape[2]
    ncol = mm_ref.shape[1]
    coli = lax.broadcasted_iota(jnp.int32, (1, nch), 1)
    maskc = coli < n_cmp
    blk = lax.broadcasted_iota(jnp.int32, (1, ncol), 1)
    k_sel = min(N_SEL, n_slc)
    lane = lax.broadcasted_iota(jnp.int32, (1, LANE), 1)
    nb = q_ref.shape[0]
    mm = mm_ref[...]
    forced = (blk == 0) | (blk == cur) | (blk == cur - 1)
    scores = []
    for b in range(nb):
        for kh in range(N_KV_HEADS):
            qm = q_ref[b, kh * GROUP:(kh + 1) * GROUP].astype(BF16)
            lc = _dot_nt(qm, kc_ref[b, kh]) + bias_ref[kh]
            lcm = jnp.where(maskc, lc, NEG_INF)
            p = jnp.where(maskc, jnp.exp(lcm - jnp.max(lcm, axis=-1, keepdims=True)), 0.0)
            pc = p * (1.0 / jnp.sum(p, axis=-1, keepdims=True))
            oc_ref[b, kh * GROUP:(kh + 1) * GROUP] = _dot_nt(pc.astype(BF16), vct_ref[b, kh])
            imp = pc[0:1]
            for g in range(1, GROUP):
                imp = imp + pc[g:g + 1]
            i1, i2, i3 = _split3(imp)
            score = _dot(i1, mm) + _dot(i2, mm) + _dot(i3, mm)
            sc = jnp.where(forced, FORCE_SCORE, jnp.where(blk <= cur, score, NEG_INF))
            scores.append(jnp.where(blk < n_slc, sc, NOT_A_SLOT))
    sc = jnp.concatenate(scores, axis=0)
    big = jnp.int32(INT_BIG)
    out = jnp.zeros((nb * N_KV_HEADS, LANE), jnp.int32)
    for it in range(k_sel):
        m = jnp.max(sc, axis=-1, keepdims=True)
        first = jnp.min(jnp.where(sc == m, blk, big), axis=-1, keepdims=True)
        out = jnp.where(lane == it, first, out)
        sc = jnp.where(blk == first, PICKED, sc)
    idx_ref[0] = out


def _smp_cmp(q_hm, kc, vct, bias_c, mm, n_cmp, n_slc, cur):
    B = q_hm.shape[0]
    nch = kc.shape[2]
    nb = math.gcd(B, 8)
    o_c, idx = pl.pallas_call(
        functools.partial(_smp_cmp_kernel, n_cmp=n_cmp, n_slc=n_slc, cur=cur),
        out_shape=(jax.ShapeDtypeStruct((B, N_HEADS, HEAD_DIM), F32),
                   jax.ShapeDtypeStruct((B // nb, nb * N_KV_HEADS, LANE), jnp.int32)),
        grid=(B // nb,),
        in_specs=[pl.BlockSpec((nb, N_HEADS, HEAD_DIM), lambda b: (b, 0, 0)),
                  pl.BlockSpec((nb, N_KV_HEADS, nch, HEAD_DIM), lambda b: (b, 0, 0, 0)),
                  pl.BlockSpec((nb, N_KV_HEADS, HEAD_DIM, nch), lambda b: (b, 0, 0, 0)),
                  pl.BlockSpec(bias_c.shape, lambda b: (0, 0, 0)),
                  pl.BlockSpec(mm.shape, lambda b: (0, 0))],
        out_specs=(pl.BlockSpec((nb, N_HEADS, HEAD_DIM), lambda b: (b, 0, 0)),
                   pl.BlockSpec((1, nb * N_KV_HEADS, LANE), lambda b: (b, 0, 0))),
        compiler_params=_cparams(1),
        name="sample_cmp",
    )(q_hm, kc, vct, bias_c, mm)
    return o_c, idx.reshape(B, N_KV_HEADS, LANE)


def _smp_att_kernel(phys_ref, q_ref, pool_ref, bsel_ref, msel_ref, new_ref, win_ref, bwin_ref, bnew_ref,
                    gate_ref, oc_ref, o_ref, buf_ref, sem_ref, *, k_sel):
    b = pl.program_id(0)
    n_blk = N_KV_HEADS * k_sel

    def blk_copy(i):
        return pltpu.make_async_copy(pool_ref.at[phys_ref[b * n_blk + i]], buf_ref.at[i], sem_ref.at[i])

    for i in range(n_blk):
        blk_copy(i).start()
    for i in range(n_blk):
        blk_copy(i).wait()

    nw = win_ref.shape[4]
    for kh in range(N_KV_HEADS):
        rows = slice(kh * GROUP, (kh + 1) * GROUP)
        qm = q_ref[0, rows].astype(BF16)
        kcol = slice(kh * HEAD_DIM, (kh + 1) * HEAD_DIM)
        vcol = slice(KV_WIDTH + kh * HEAD_DIM, KV_WIDTH + (kh + 1) * HEAD_DIM)
        bnew = bnew_ref[kh]

        def branch(kt, vt, bias, mask, knew, vnew, new_ok):
            lg = _dot(qm, kt) + bias
            lg = jnp.where(mask, lg, NEG_INF)
            ln = jnp.sum(qm.astype(F32) * knew.astype(F32), axis=-1, keepdims=True) + bnew
            if new_ok is not None:
                ln = jnp.where(new_ok, ln, NEG_INF)
            mx = jnp.maximum(jnp.max(lg, axis=-1, keepdims=True), ln)
            pe = jnp.where(mask, jnp.exp(lg - mx), 0.0)
            pn = jnp.exp(ln - mx)
            if new_ok is not None:
                pn = jnp.where(new_ok, pn, 0.0)
            den = jnp.sum(pe, axis=-1, keepdims=True) + pn
            any_ok = den > 0.0
            inv = 1.0 / jnp.where(any_ok, den, 1.0)
            o = _dot_nt(pe.astype(BF16), vt) + pn.astype(BF16).astype(F32) * vnew.astype(F32)
            return jnp.where(any_ok, o * inv, 0.0)

        page = buf_ref.shape[4]
        nkeys = k_sel * page
        kt = jnp.concatenate([buf_ref[kh * k_sel + i, 0, kh] for i in range(k_sel)], axis=1).astype(BF16)
        vt = jnp.concatenate([buf_ref[kh * k_sel + i, 1, kh] for i in range(k_sel)], axis=1).astype(BF16)
        knew = new_ref[0, 0:1, kcol].astype(BF16)
        vnew = new_ref[0, 0:1, vcol].astype(BF16)
        msel = msel_ref[0, kh]
        o_s = branch(kt, vt, bsel_ref[0, kh], msel[:, 0:nkeys] > 0.5, knew, vnew, msel[:, nkeys:nkeys + 1] > 0.5)

        kw = win_ref[0, 0, kh].astype(BF16)
        vw = win_ref[0, 1, kh].astype(BF16)
        knw = new_ref[0, 1:2, kcol].astype(BF16)
        vnw = new_ref[0, 1:2, vcol].astype(BF16)
        dist = nw - lax.broadcasted_iota(jnp.int32, (1, nw), 1)
        o_w = branch(kw, vw, bwin_ref[kh], dist < WINDOW, knw, vnw, None)

        g3 = gate_ref[0, rows]
        o_ref[0, rows] = g3[:, 0:1] * oc_ref[0, rows] + g3[:, 1:2] * o_s + g3[:, 2:3] * o_w


def _smp_att(phys, q_hm, pool_blk, bias_sel, mask_sel, new_rows, win_buf, bias_win, bias_new, gates, o_c, k_sel):
    B = q_hm.shape[0]
    n_blk = N_KV_HEADS * k_sel
    full = lambda a: pl.BlockSpec(a.shape, lambda b, ph: (0,) * a.ndim)
    per_b = lambda a: pl.BlockSpec((1,) + a.shape[1:], lambda b, ph: (b,) + (0,) * (a.ndim - 1))
    return pl.pallas_call(
        functools.partial(_smp_att_kernel, k_sel=k_sel),
        out_shape=jax.ShapeDtypeStruct((B, N_HEADS, HEAD_DIM), F32),
        grid_spec=pltpu.PrefetchScalarGridSpec(
            num_scalar_prefetch=1,
            grid=(B,),
            in_specs=[per_b(q_hm), pl.BlockSpec(memory_space=pl.ANY), per_b(bias_sel), per_b(mask_sel),
                      per_b(new_rows), per_b(win_buf), full(bias_win), full(bias_new), per_b(gates), per_b(o_c)],
            out_specs=pl.BlockSpec((1, N_HEADS, HEAD_DIM), lambda b, ph: (b, 0, 0)),
            scratch_shapes=[pltpu.VMEM((n_blk,) + pool_blk.shape[1:], F32),
                            pltpu.SemaphoreType.DMA((n_blk,))]),
        compiler_params=_cparams(1),
        name="sample_att",
    )(phys, q_hm, pool_blk, bias_sel, mask_sel, new_rows, win_buf, bias_win, bias_new, gates, o_c)


def _merge_kernel(x_ref, cn_ref, att_ref, gna_ref, wo_ref, m2_ref, m3_ref, m4_ref, g2_ref, rw_ref, rb_ref,
                  x1_ref, h2_ref, ti_ref, tg_ref, *, n_exp, cw_width):
    att_n = _rms(att_ref[0], gna_ref[...]).astype(BF16)
    mix = _dot(cn_ref[0], wo_ref[0:cw_width, :]) + _dot(att_n, wo_ref[cw_width:, :])
    x1 = x_ref[0] + m2_ref[0] * mix
    x1_ref[0] = x1
    h2 = _rms(x1, g2_ref[...]) * (1.0 + m4_ref[0]) + m3_ref[0]
    h2_ref[0] = h2
    logits = _dot3(h2, rw_ref[...]) + rb_ref[...]
    lane = lax.broadcasted_iota(jnp.int32, (1, LANE), 1)
    sc = jnp.where(lane < n_exp, logits, NOT_A_SLOT)
    big = jnp.int32(INT_BIG)
    ti = jnp.zeros(sc.shape, jnp.int32)
    tv = jnp.zeros(sc.shape, F32)
    v0 = None
    den = None
    for k in range(TOP_K):
        m = jnp.max(sc, axis=-1, keepdims=True)
        first = jnp.min(jnp.where(sc == m, lane, big), axis=-1, keepdims=True)
        if k == 0:
            v0 = m
        e = jnp.exp(m - v0)
        den = e if den is None else den + e
        ti = jnp.where(lane == k, first, ti)
        tv = jnp.where(lane == k, e, tv)
        sc = jnp.where(lane == first, PICKED, sc)
    ti_ref[0] = ti
    tg_ref[0] = tv * (1.0 / den)


def _merge(x, convn, att, gn_att, wo_b, m2, m3, m4, ln2_g, rw_pad, rb_pad, *, seq_mode, tm, n_exp):
    B, T, D = x.shape
    cw_width = convn.shape[2]
    aw = att.shape[2]
    row_blk = lambda w: pl.BlockSpec((1, tm, w), lambda b, t: (b, t, 0))
    mod_spec = pl.BlockSpec((1, 1, D), lambda b, t: (b, 0, 0)) if seq_mode else row_blk(D)
    const = lambda shape: pl.BlockSpec(shape, lambda b, t: (0,) * len(shape))
    return pl.pallas_call(
        functools.partial(_merge_kernel, n_exp=n_exp, cw_width=cw_width),
        out_shape=(jax.ShapeDtypeStruct((B, T, D), F32), jax.ShapeDtypeStruct((B, T, D), F32),
                   jax.ShapeDtypeStruct((B, T, LANE), jnp.int32), jax.ShapeDtypeStruct((B, T, LANE), F32)),
        grid=(B, T // tm),
        in_specs=[row_blk(D), row_blk(cw_width), row_blk(aw), const((1, aw)), const(wo_b.shape),
                  mod_spec, mod_spec, mod_spec, const((1, D)), const(rw_pad.shape), const((1, LANE))],
        out_specs=(row_blk(D), row_blk(D), row_blk(LANE), row_blk(LANE)),
        compiler_params=_cparams(2),
        name="merge_seq" if seq_mode else "merge_rows",
    )(x, convn, att, gn_att.reshape(1, aw), wo_b, m2, m3, m4, ln2_g.reshape(1, D), rw_pad, rb_pad)


def _moe_kernel(be_ref, nu_ref, xs_ref, wgu_ref, bgu_ref, wd_ref, bd_ref, y_ref, wgu_b, wd_b, *, d_ff):
    i = pl.program_id(0)
    prev = be_ref[jnp.maximum(i - 1, 0)]
    changed = jnp.logical_or(i == 0, be_ref[i] != prev)

    @pl.when(jnp.logical_and(changed, i < nu_ref[0]))
    def _():
        wgu_b[...] = wgu_ref[0].astype(BF16)
        wd_b[...] = wd_ref[0].astype(BF16)

    @pl.when(i < nu_ref[0])
    def _():
        gu = _dot(xs_ref[...].astype(BF16), wgu_b[...]) + bgu_ref[0]
        gate_h = jnp.minimum(gu[:, 0:d_ff], SWIGLU_LIMIT)
        up_h = jnp.clip(gu[:, d_ff:], -SWIGLU_LIMIT, SWIGLU_LIMIT)
        act = (up_h + 1.0) * gate_h * _sigmoid(SWIGLU_ALPHA * gate_h)
        y_ref[...] = _dot(act.astype(BF16), wd_b[...]) + bd_ref[0]

    @pl.when(i >= nu_ref[0])
    def _():
        y_ref[...] = jnp.zeros(y_ref.shape, F32)


def _moe_experts(block_e, n_used, xs, w_gu, b_gu, w_down, b_down):
    n_rows, D = xs.shape
    E, _, two_ff = w_gu.shape
    d_ff = two_ff // 2
    n_blocks = n_rows // MOE_ROWS
    return pl.pallas_call(
        functools.partial(_moe_kernel, d_ff=d_ff),
        out_shape=jax.ShapeDtypeStruct((n_rows, D), F32),
        grid_spec=pltpu.PrefetchScalarGridSpec(
            num_scalar_prefetch=2,
            grid=(n_blocks,),
            in_specs=[pl.BlockSpec((MOE_ROWS, D), lambda i, be, nu: (i, 0)),
                      pl.BlockSpec((1, D, two_ff), lambda i, be, nu: (be[i], 0, 0)),
                      pl.BlockSpec((1, 1, two_ff), lambda i, be, nu: (be[i], 0, 0)),
                      pl.BlockSpec((1, d_ff, D), lambda i, be, nu: (be[i], 0, 0)),
                      pl.BlockSpec((1, 1, D), lambda i, be, nu: (be[i], 0, 0))],
            out_specs=pl.BlockSpec((MOE_ROWS, D), lambda i, be, nu: (i, 0)),
            scratch_shapes=[pltpu.VMEM((D, two_ff), BF16), pltpu.VMEM((d_ff, D), BF16)]),
        compiler_params=_cparams(1),
        name="moe_experts",
    )(block_e, n_used, xs, w_gu, b_gu.reshape(E, 1, two_ff), w_down, b_down.reshape(E, 1, D))


def _route(top_idx, n_tok, n_exp):
    n_assign = n_tok * TOP_K
    flat_e = top_idx.reshape(-1)
    experts = jnp.arange(n_exp, dtype=jnp.int32)
    onehot = (flat_e[:, None] == experts[None, :]).astype(jnp.int32)
    csum = jnp.cumsum(onehot, axis=0)
    counts = csum[-1]
    padded = (counts + MOE_ROWS - 1) // MOE_ROWS * MOE_ROWS
    pad_end = jnp.cumsum(padded)
    pad_start = pad_end - padded
    dest = jnp.sum(onehot * (csum - 1 + pad_start[None, :]), axis=1).astype(jnp.int32)
    n_blocks = -(-(n_assign + n_exp * (MOE_ROWS - 1)) // MOE_ROWS)
    n_rows = n_blocks * MOE_ROWS
    blk_row0 = jnp.arange(n_blocks, dtype=jnp.int32) * MOE_ROWS
    block_e = jnp.minimum(jnp.sum((pad_end[None, :] <= blk_row0[:, None]).astype(jnp.int32), axis=1), n_exp - 1)
    n_fill = n_rows - n_assign
    fill_e = jnp.arange(n_fill, dtype=jnp.int32) // (MOE_ROWS - 1)
    fill_i = jnp.arange(n_fill, dtype=jnp.int32) % (MOE_ROWS - 1)
    fill_on = (fill_e < n_exp) & (fill_i < jnp.sum(
        (fill_e[:, None] == experts[None, :]) * (padded - counts)[None, :], axis=1))
    keys = jnp.concatenate([2 * flat_e, jnp.where(fill_on, 2 * fill_e + 1, 2 * n_exp + 1)])
    toks = jnp.concatenate([jnp.arange(n_assign, dtype=jnp.int32) // TOP_K, jnp.full((n_fill,), n_tok, jnp.int32)])
    _, row_tok = lax.sort((keys, toks), num_keys=1, is_stable=True)
    n_used = (pad_end[-1] // MOE_ROWS).astype(jnp.int32).reshape(1)
    return row_tok, dest.reshape(n_tok, TOP_K), block_e.astype(jnp.int32), n_used


def _final_kernel(x1_ref, yg_ref, tg_ref, m5_ref, fg_ref, o_ref):
    tg = tg_ref[0]
    ff = tg[:, 0:1] * yg_ref[0, 0]
    for k in range(1, TOP_K):
        ff = ff + tg[:, k:k + 1] * yg_ref[0, k]
    o_ref[0] = _rms(x1_ref[0] + m5_ref[0] * ff, fg_ref[...])


def _final(x1, yg, tg, m5, final_g, *, seq_mode, tm):
    B, T, D = x1.shape
    row_blk = lambda w: pl.BlockSpec((1, tm, w), lambda b, t: (b, t, 0))
    mod_spec = pl.BlockSpec((1, 1, D), lambda b, t: (b, 0, 0)) if seq_mode else row_blk(D)
    return pl.pallas_call(
        _final_kernel,
        out_shape=jax.ShapeDtypeStruct((B, T, D), F32),
        grid=(B, T // tm),
        in_specs=[row_blk(D), pl.BlockSpec((1, TOP_K, tm, D), lambda b, t: (b, 0, t, 0)), row_blk(LANE),
                  mod_spec, pl.BlockSpec((1, D), lambda b, t: (0, 0))],
        out_specs=row_blk(D),
        compiler_params=_cparams(2),
        name="final_seq" if seq_mode else "final_rows",
    )(x1, yg, tg, m5, final_g.reshape(1, D))


def _row_tile(t):
    for tm in (512, 256, 128, 64, 32, 16, 8):
        if t % tm == 0:
            return tm
    raise ValueError(f"unsupported row count {t}")


def _rows(a, idx):
    return a.at[idx].get(mode="promise_in_bounds")


def kernel(x_prompt, x_sample, cache_kv_cmp, cache_kv_slc, state_kv_win, state_conv, page_table, c_prompt, c_sample,
           rel_bias_table, ln1_g, ln2_g, w_ada, b_ada, w_in, conv_w, cmp_pe, cmp_w1, cmp_w2, gn_conv, gn_att, w_o,
           router_w, router_b, w_gu, b_gu, w_down, b_down, final_g):
    B, T, D = x_prompt.shape
    BS, TS, _ = x_sample.shape
    depth = w_in.shape[0]
    assert depth == 1 and TS == 1
    n_pool, page = cache_kv_cmp.shape[1], cache_kv_cmp.shape[2]
    n_pages = page_table.shape[1]
    past_len = n_pages * page
    n_exp = router_w.shape[2]
    cw_width = conv_w.shape[2]
    kvw2 = 2 * KV_WIDTH
    assert past_len % SLC_BLOCK == 0 and past_len % CMP_STRIDE == 0 and T % page == 0

    in_cols = w_in.shape[2]
    gate0 = 3 * cw_width + N_HEADS * HEAD_DIM + 3 * kvw2
    w_pad = jnp.pad(w_in[0], ((0, 0), (0, gate0 + LANE - in_cols))).astype(BF16)
    wo_b = w_o[0].astype(BF16)
    rw_pad = jnp.pad(router_w[0], ((0, 0), (0, LANE - n_exp)))
    rb_pad = jnp.pad(router_b[0], (0, LANE - n_exp)).reshape(1, LANE)
    w_blk = _cmp_block_weight(cmp_w1[0])

    n_c = B + BS
    n_cp = -(-n_c // 8) * 8
    c_all = jnp.pad(jnp.concatenate([c_prompt, c_sample], axis=0), ((0, n_cp - n_c), (0, 0)))
    mod = _modulation(c_all, w_ada[0], b_ada[0]).reshape(n_cp, 6, D)
    mp = [mod[:B, i].reshape(B, 1, D) for i in range(6)]
    ms = [mod[B:n_c, i].reshape(1, BS, D) for i in range(6)]

    tm = _row_tile(T)
    nq = T // Q_BLOCK
    zeros_prev = jnp.zeros((B, CONV_K - 1, cw_width), F32)
    convn_p, q_p, kvc_p, kvs_p, kvw_p, gate_p, vlast_p, kh_p, vth_p = _inproj(
        x_prompt, mp[0], mp[1], ln1_g[0], w_pad, conv_w[0], gn_conv[0], zeros_prev, zeros_prev, seq_mode=True, tm=tm)
    pt_p = jnp.arange(B * (T // page), dtype=jnp.int32).reshape(B, T // page)
    g_p = math.gcd(T // page, CMP_PAGES)
    ab_p = _cmp_ab(kvc_p.reshape(B * (T // page), page, kvw2), pt_p, w_blk, g_p)
    kc_p, vct_p = _cmp_finish(ab_p, cmp_pe[0], cmp_w1[0], cmp_w2[0])
    gates_p = gate_p[:, :, :3 * N_HEADS].reshape(B, nq, Q_BLOCK, N_KV_HEADS, GROUP, 3)
    gates_p = gates_p.transpose(0, 3, 1, 5, 4, 2).reshape(B, N_KV_HEADS, nq, 3, ROWS_Q)
    att_p = _nsa_prompt(q_p, kh_p, vth_p, kc_p, vct_p, gates_p, rel_bias_table)
    x1_p, h2_p, ti_p, tg_p = _merge(x_prompt, convn_p, att_p, gn_att[0], wo_b, mp[2], mp[3], mp[4], ln2_g[0],
                                    rw_pad, rb_pad, seq_mode=True, tm=tm, n_exp=n_exp)

    xs_rows = x_sample.reshape(1, BS, D)
    prev2 = state_conv[0][:, 0, :].reshape(1, BS, cw_width)
    prev1 = state_conv[0][:, 1, :].reshape(1, BS, cw_width)
    tms = _row_tile(BS)
    convn_s, q_s, kvc_s, kvs_s, kvw_s, gate_s, v_s = _inproj(
        xs_rows, ms[0], ms[1], ln1_g[0], w_pad, conv_w[0], gn_conv[0], prev2, prev1, seq_mode=False, tm=tms)
    n_cmp = (past_len + TS - CMP_LEN) // CMP_STRIDE + 1
    nch_s = past_len // CMP_STRIDE
    assert n_cmp + 1 == nch_s
    ab_s = _cmp_ab_t(cache_kv_cmp[0].transpose(0, 2, 3, 4, 1), page_table, w_blk,
                     math.gcd(n_pages, CMP_PAGES))
    kc_s, vct_s = _cmp_finish(ab_s, cmp_pe[0], cmp_w1[0], cmp_w2[0])
    t_q = past_len
    n_slc = -(-(past_len + TS) // SLC_BLOCK)
    cur = t_q // SLC_BLOCK
    k_sel = min(N_SEL, n_slc)
    tbl = rel_bias_table.reshape(N_BUCKETS, N_KV_HEADS, GROUP)
    pos_c = jnp.arange(nch_s) * CMP_STRIDE + CMP_LEN - 1
    bias_c = tbl[_rel_bucket(t_q - pos_c)].transpose(1, 2, 0)
    ncol = -(-n_slc // LANE) * LANE
    mm_s = _score_matrix(nch_s, ncol)
    q_s_hm = q_s.reshape(N_HEADS, BS, HEAD_DIM).transpose(1, 0, 2).astype(F32)
    o_c_s, idx_pad = _smp_cmp(q_s_hm, kc_s, vct_s, bias_c, mm_s, n_cmp, n_slc, cur)
    idx = idx_pad[:, :, :k_sel]
    blk_per_page = page // SLC_BLOCK
    pg = jnp.minimum(idx // blk_per_page, n_pages - 1)
    phys = jnp.take_along_axis(page_table, pg.reshape(BS, -1), axis=1).astype(jnp.int32)
    pos_s = pg[..., None] * page + jnp.arange(page)
    in_blk = (pos_s // SLC_BLOCK == idx[..., None]) & (pos_s < past_len)
    pos_s = pos_s.reshape(BS, N_KV_HEADS, k_sel * page)
    bucket_hot = (_rel_bucket(t_q - pos_s)[..., None] == jnp.arange(N_BUCKETS)).astype(F32)
    bias_sel = jnp.einsum('bksn,nkg->bkgs', bucket_hot, tbl, precision=lax.Precision.HIGHEST)
    new_sel = jnp.any(idx == cur, axis=-1, keepdims=True)
    mask_sel = jnp.concatenate([in_blk.reshape(BS, N_KV_HEADS, k_sel * page), new_sel], axis=-1)
    mask_sel = mask_sel.astype(F32)[:, :, None, :]
    nw = state_kv_win.shape[2]
    bias_win = tbl[_rel_bucket(nw - jnp.arange(nw))].transpose(1, 2, 0)
    bias_new = tbl[0].reshape(N_KV_HEADS, GROUP, 1)
    new_rows = jnp.stack([kvs_s[0], kvw_s[0]], axis=1)
    gates_s = gate_s[0, :, :3 * N_HEADS].reshape(BS, N_HEADS, 3)
    pool_t = cache_kv_slc[0].transpose(0, 2, 3, 4, 1)
    win_t = state_kv_win[0].transpose(0, 2, 3, 4, 1)
    att_s = _smp_att(phys.reshape(-1), q_s_hm, pool_t, bias_sel, mask_sel, new_rows,
                     win_t, bias_win, bias_new, gates_s, o_c_s, k_sel)
    att_s = att_s.reshape(1, BS, N_HEADS * HEAD_DIM)
    x1_s, h2_s, ti_s, tg_s = _merge(xs_rows, convn_s, att_s, gn_att[0], wo_b, ms[2], ms[3], ms[4], ln2_g[0],
                                    rw_pad, rb_pad, seq_mode=False, tm=tms, n_exp=n_exp)

    n_tok = B * T + BS
    h2_all = jnp.concatenate([h2_p.reshape(B * T, D), h2_s.reshape(BS, D), jnp.zeros((8, D), F32)], axis=0)
    top_idx = jnp.concatenate([ti_p.reshape(B * T, LANE), ti_s.reshape(BS, LANE)], axis=0)[:, :TOP_K]
    row_tok, dest, block_e, n_used = _route(top_idx, n_tok, n_exp)
    xs = _rows(h2_all, row_tok)
    yb = _moe_experts(block_e, n_used, xs, w_gu[0], b_gu[0], w_down[0], b_down[0])
    dest_p = dest[:B * T].reshape(B, T, TOP_K).transpose(0, 2, 1)
    yg_p = _rows(yb, dest_p.reshape(-1)).reshape(B, TOP_K, T, D)
    yg_s = _rows(yb, dest[B * T:].T.reshape(-1)).reshape(1, TOP_K, BS, D)
    y_p = _final(x1_p, yg_p, tg_p, mp[5], final_g, seq_mode=True, tm=tm)
    y_s = _final(x1_s, yg_s, tg_s, ms[5], final_g, seq_mode=False, tm=tms)

    kv_tail = (2, N_KV_HEADS, HEAD_DIM)
    page_shape = (depth, B, T // page, page) + kv_tail
    w_keep = min(WINDOW, T)
    new_win_s = jnp.concatenate([state_kv_win[0][:, TS:], kvw_s.reshape(BS, TS, *kv_tail)], axis=1)
    new_conv_s = jnp.concatenate([state_conv[0][:, TS:], v_s.reshape(BS, TS, cw_width)], axis=1)
    return (y_p, y_s.reshape(BS, TS, D),
            kvc_p.reshape(page_shape), kvc_s.reshape((depth, BS, TS) + kv_tail),
            kvs_p.reshape(page_shape), kvs_s.reshape((depth, BS, TS) + kv_tail),
            kvw_p[:, T - w_keep:].reshape((depth, B, w_keep) + kv_tail), new_win_s[None],
            vlast_p[None], new_conv_s[None])
```

```python
import functools
import math

import numpy as np
import jax
import jax.numpy as jnp
from jax import lax
from jax.experimental import pallas as pl
from jax.experimental.pallas import tpu as pltpu

F32 = jnp.float32
BF16 = jnp.bfloat16

CONV_K = 3
N_HEADS = 8
N_KV_HEADS = 2
GROUP = N_HEADS // N_KV_HEADS
HEAD_DIM = 64
KV_WIDTH = N_KV_HEADS * HEAD_DIM
CMP_LEN = 32
CMP_STRIDE = 16
SLC_BLOCK = 64
RATIO = SLC_BLOCK // CMP_STRIDE
N_SEL = 16
WINDOW = 512
Q_BLOCK = 128
N_BUCKETS = 32
REL_MAX_DIST = 128
TOP_K = 4
SWIGLU_LIMIT = 7.0
SWIGLU_ALPHA = 1.702
EPS = 1e-6
NEG_INF = -1e30
FORCE_SCORE = 1e4

LANE = 128
BF16_ROWS = 16
ROWS_Q = GROUP * Q_BLOCK
KEY_CHUNK = 512
TILES = KEY_CHUNK // Q_BLOCK
FAR_UNROLL = 4
MOE_ROWS = 512
CMP_PAGES = 64
LOG2E = math.log2(math.e)
MOD_COLS = 1536
VMEM_LIMIT = 48 * 1024 * 1024
PICKED = -3e38
NOT_A_SLOT = -2e38
INT_BIG = 1 << 30


def _cparams(n_axes, vmem_limit=VMEM_LIMIT):
    return pltpu.CompilerParams(dimension_semantics=("arbitrary",) * n_axes, vmem_limit_bytes=vmem_limit)


def _dot(a, b):
    return jnp.dot(a, b, preferred_element_type=F32)


def _dot_nt(a, b):
    return lax.dot_general(a, b, (((1,), (1,)), ((), ())), preferred_element_type=F32)


def _split2(x):
    hi = x.astype(BF16)
    lo = (x - hi.astype(F32)).astype(BF16)
    return hi, lo


def _split3(x):
    a = x.astype(BF16)
    r = x - a.astype(F32)
    b = r.astype(BF16)
    c = (r - b.astype(F32)).astype(BF16)
    return a, b, c


def _dot3(a, b):
    ah, al = _split2(a)
    bh, bl = _split2(b)
    return _dot(ah, bh) + _dot(ah, bl) + _dot(al, bh)


def _sigmoid(x):
    return 1.0 / (1.0 + jnp.exp(-x))


def _rms(x, g):
    return x * lax.rsqrt(jnp.mean(x * x, axis=-1, keepdims=True) + EPS) * g


def _rel_bucket(dist):
    n = jnp.maximum(dist, 0)
    max_exact = N_BUCKETS // 2
    large = max_exact + (jnp.log(jnp.maximum(n, 1).astype(F32) / max_exact)
                         / math.log(REL_MAX_DIST / max_exact) * (N_BUCKETS - max_exact)).astype(jnp.int32)
    return jnp.where(n < max_exact, n, jnp.minimum(large, N_BUCKETS - 1))


def _mod_kernel(c_ref, w_ref, b_ref, o_ref):
    c = c_ref[...]
    o_ref[...] = _dot3(c * _sigmoid(c), w_ref[...]) + b_ref[...]


def _modulation(c, w_ada, b_ada):
    n, d = c.shape
    cols = w_ada.shape[1]
    bn = math.gcd(cols, MOD_COLS)
    return pl.pallas_call(
        _mod_kernel,
        out_shape=jax.ShapeDtypeStruct((n, cols), F32),
        grid=(cols // bn,),
        in_specs=[pl.BlockSpec((n, d), lambda i: (0, 0)),
                  pl.BlockSpec((d, bn), lambda i: (0, i)),
                  pl.BlockSpec((1, bn), lambda i: (0, i))],
        out_specs=pl.BlockSpec((n, bn), lambda i: (0, i)),
        compiler_params=_cparams(1),
        name="modulation",
    )(c, w_ada, b_ada.reshape(1, cols))


def _inproj_kernel(x_ref, m0_ref, m1_ref, g1_ref, w_ref, cw_ref, gnc_ref, pa_ref, pb_ref,
                   convn_ref, q_ref, kvc_ref, kvs_ref, kvw_ref, gate_ref, vlast_ref, *rest, seq_mode, tm, cw_width):
    if seq_mode:
        kh_ref, vth_ref, carry_ref = rest
    x = x_ref[0]
    h = _rms(x, g1_ref[...]) * (1.0 + m1_ref[0]) + m0_ref[0]
    hb = h.astype(BF16)
    c3 = 3 * cw_width
    uc = _dot(hb, w_ref[:, 0:c3])
    b_g = uc[:, 0:cw_width]
    v = uc[:, cw_width:2 * cw_width] * uc[:, 2 * cw_width:c3]
    if seq_mode:
        @pl.when(pl.program_id(1) == 0)
        def _():
            carry_ref[0:2, :] = pa_ref[0]
        c0 = carry_ref[0:1, :]
        c1 = carry_ref[1:2, :]
        row = lax.broadcasted_iota(jnp.int32, (tm, 1), 0)
        vm1 = jnp.where(row == 0, c1, pltpu.roll(v, 1, 0))
        vm2 = jnp.where(row == 0, c0, jnp.where(row == 1, c1, pltpu.roll(v, 2, 0)))
        carry_ref[0:2, :] = v[tm - 2:tm, :]
        vlast_ref[0] = v[tm - 2:tm, :]
    else:
        vm2 = pa_ref[0]
        vm1 = pb_ref[0]
        vlast_ref[0] = v
    cw = cw_ref[...]
    y = cw[0:1, :] * vm2 + cw[1:2, :] * vm1 + cw[2:3, :] * v
    convn_ref[0] = _rms(b_g * y, gnc_ref[...]).astype(BF16)

    aw = N_HEADS * HEAD_DIM
    uq = _dot(hb, w_ref[:, c3:c3 + aw]) * (HEAD_DIM ** -0.5 * (LOG2E if seq_mode else 1.0))
    if seq_mode:
        uqt = uq.T
        for k in range(N_KV_HEADS):
            for jj in range(tm // Q_BLOCK):
                q_ref[0, k, jj] = jnp.concatenate(
                    [uqt[(k * GROUP + g) * HEAD_DIM:(k * GROUP + g + 1) * HEAD_DIM, jj * Q_BLOCK:(jj + 1) * Q_BLOCK]
                     for g in range(GROUP)], axis=1).astype(BF16)
    else:
        for hh in range(N_HEADS):
            q_ref[0, hh] = uq[:, hh * HEAD_DIM:(hh + 1) * HEAD_DIM].astype(BF16)
    kv0 = c3 + aw
    kvw3 = 3 * 2 * KV_WIDTH
    ukv = _dot(hb, w_ref[:, kv0:kv0 + kvw3])
    kvc_ref[0] = ukv[:, 0:2 * KV_WIDTH]
    kvs_ref[0] = ukv[:, 2 * KV_WIDTH:4 * KV_WIDTH]
    kvw_ref[0] = ukv[:, 4 * KV_WIDTH:6 * KV_WIDTH]
    if seq_mode:
        for br in range(2):
            c0 = (br + 1) * 2 * KV_WIDTH
            for k in range(N_KV_HEADS):
                kh_ref[0, br * N_KV_HEADS + k] = ukv[:, c0 + k * HEAD_DIM:c0 + (k + 1) * HEAD_DIM].astype(BF16)
            vt = ukv[:, c0 + KV_WIDTH:c0 + 2 * KV_WIDTH].T.astype(BF16)
            for k in range(N_KV_HEADS):
                vth_ref[0, br * N_KV_HEADS + k] = vt[k * HEAD_DIM:(k + 1) * HEAD_DIM, :]
    ug = _dot(hb, w_ref[:, kv0 + kvw3:kv0 + kvw3 + LANE])
    gate_ref[0] = _sigmoid(ug)


def _inproj(x, m0, m1, ln_g, w_pad, conv_w, gn_conv, pa, pb, *, seq_mode, tm):
    B, T, D = x.shape
    cw_width = conv_w.shape[1]
    nt = T // tm
    row_blk = lambda w: pl.BlockSpec((1, tm, w), lambda b, t: (b, t, 0))
    if seq_mode:
        mod_spec = pl.BlockSpec((1, 1, D), lambda b, t: (b, 0, 0))
        prev_spec = pl.BlockSpec((1, 2, cw_width), lambda b, t: (b, 0, 0))
        vlast_shape = jax.ShapeDtypeStruct((B, 2, cw_width), F32)
        vlast_spec = pl.BlockSpec((1, 2, cw_width), lambda b, t: (b, 0, 0))
    else:
        mod_spec = row_blk(D)
        prev_spec = row_blk(cw_width)
        vlast_shape = jax.ShapeDtypeStruct((B, T, cw_width), F32)
        vlast_spec = row_blk(cw_width)
    const = lambda shape: pl.BlockSpec(shape, lambda b, t: (0,) * len(shape))
    kern = functools.partial(_inproj_kernel, seq_mode=seq_mode, tm=tm, cw_width=cw_width)
    if seq_mode:
        assert tm % Q_BLOCK == 0
        q_shape = jax.ShapeDtypeStruct((B, N_KV_HEADS, T // Q_BLOCK, HEAD_DIM, ROWS_Q), BF16)
        q_spec = pl.BlockSpec((1, N_KV_HEADS, tm // Q_BLOCK, HEAD_DIM, ROWS_Q), lambda b, t: (b, 0, t, 0, 0))
    else:
        q_shape = jax.ShapeDtypeStruct((B, N_HEADS, T, HEAD_DIM), BF16)
        q_spec = pl.BlockSpec((1, N_HEADS, tm, HEAD_DIM), lambda b, t: (b, 0, t, 0))
    out_shape = [jax.ShapeDtypeStruct((B, T, cw_width), BF16),
                 q_shape,
                 jax.ShapeDtypeStruct((B, T, 2 * KV_WIDTH), F32),
                 jax.ShapeDtypeStruct((B, T, 2 * KV_WIDTH), F32),
                 jax.ShapeDtypeStruct((B, T, 2 * KV_WIDTH), F32),
                 jax.ShapeDtypeStruct((B, T, LANE), F32),
                 vlast_shape]
    out_specs = [row_blk(cw_width),
                 q_spec,
                 row_blk(2 * KV_WIDTH), row_blk(2 * KV_WIDTH), row_blk(2 * KV_WIDTH),
                 row_blk(LANE), vlast_spec]
    scratch = []
    if seq_mode:
        n_att = 2 * N_KV_HEADS
        out_shape += [jax.ShapeDtypeStruct((B, n_att, T, HEAD_DIM), BF16),
                      jax.ShapeDtypeStruct((B, n_att, HEAD_DIM, T), BF16)]
        out_specs += [pl.BlockSpec((1, n_att, tm, HEAD_DIM), lambda b, t: (b, 0, t, 0)),
                      pl.BlockSpec((1, n_att, HEAD_DIM, tm), lambda b, t: (b, 0, 0, t))]
        scratch = [pltpu.VMEM((8, cw_width), F32)]
    return pl.pallas_call(
        kern,
        out_shape=tuple(out_shape),
        grid=(B, nt),
        in_specs=[row_blk(D), mod_spec, mod_spec, const((1, D)), const(w_pad.shape), const(conv_w.shape),
                  const((1, cw_width)), prev_spec, prev_spec],
        out_specs=tuple(out_specs),
        scratch_shapes=scratch,
        compiler_params=_cparams(2),
        name="inproj_seq" if seq_mode else "inproj_rows",
    )(x, m0, m1, ln_g.reshape(1, D), w_pad, conv_w, gn_conv.reshape(1, cw_width), pa, pb)


def _cmp_ab_kernel(pt_ref, *refs, G):
    pages = refs[:2 * G]
    w_ref, out_ref, x_ref = refs[2 * G], refs[2 * G + 1], refs[2 * G + 2]
    half = KV_WIDTH
    for j in range(G):
        for r in range(CMP_STRIDE):
            for c in range(2):
                x_ref[c, j * 8:(j + 1) * 8, r * half:(r + 1) * half] = (
                    pages[2 * j + c][0, pl.ds(r, 8, stride=CMP_STRIDE), :])
    wcols = w_ref.shape[2]
    for c in range(2):
        out_ref[0, :, c * wcols:(c + 1) * wcols] = _dot(x_ref[c].astype(BF16), w_ref[c])


def _cmp_ab(pool, page_table, w_blk, G):
    P, page, width = pool.shape
    B, n_pages = page_table.shape
    cpp = page // CMP_STRIDE
    assert cpp == 8 and n_pages % G == 0
    kdim = CMP_STRIDE * width // 2

    def pg_spec(j, c):
        return pl.BlockSpec((1, page, width // 2), lambda b, g, pt: (pt[b * n_pages + g * G + j], 0, c))

    return pl.pallas_call(
        functools.partial(_cmp_ab_kernel, G=G),
        out_shape=jax.ShapeDtypeStruct((B, n_pages * cpp, 2 * w_blk.shape[2]), F32),
        grid_spec=pltpu.PrefetchScalarGridSpec(
            num_scalar_prefetch=1,
            grid=(B, n_pages // G),
            in_specs=[pg_spec(j, c) for j in range(G) for c in range(2)]
            + [pl.BlockSpec(w_blk.shape, lambda b, g, pt: (0, 0, 0))],
            out_specs=pl.BlockSpec((1, G * cpp, 2 * w_blk.shape[2]), lambda b, g, pt: (b, g, 0)),
            scratch_shapes=[pltpu.VMEM((2, G * cpp, kdim), F32)]),
        compiler_params=_cparams(2),
        name="cmp_partial",
    )(page_table.reshape(-1).astype(jnp.int32), *([pool] * (2 * G)), w_blk)


def _cmp_ab_t_kernel(pt_ref, *refs, G):
    pages = refs[:G]
    w_ref, out_ref, x_ref, s_ref = refs[G], refs[G + 1], refs[G + 2], refs[G + 3]
    half = KV_WIDTH
    for j in range(G):
        for c in range(2):
            for k in range(N_KV_HEADS):
                s_ref[c, j, :, k * HEAD_DIM:(k + 1) * HEAD_DIM] = pages[j][0, c, k].astype(BF16).T.astype(F32)
    for j in range(G):
        for r in range(CMP_STRIDE):
            for c in range(2):
                x_ref[c, j * 8:(j + 1) * 8, r * half:(r + 1) * half] = (
                    s_ref[c, j, pl.ds(r, 8, stride=CMP_STRIDE), :])
    wcols = w_ref.shape[2]
    for c in range(2):
        out_ref[0, :, c * wcols:(c + 1) * wcols] = _dot(x_ref[c].astype(BF16), w_ref[c])


def _cmp_ab_t(pool_t, page_table, w_blk, G):
    P, _, _, hd, page = pool_t.shape
    B, n_pages = page_table.shape
    cpp = page // CMP_STRIDE
    assert cpp == 8 and n_pages % G == 0
    kdim = CMP_STRIDE * KV_WIDTH

    def pg_spec(j):
        return pl.BlockSpec((1,) + pool_t.shape[1:], lambda b, g, pt: (pt[b * n_pages + g * G + j], 0, 0, 0, 0))

    return pl.pallas_call(
        functools.partial(_cmp_ab_t_kernel, G=G),
        out_shape=jax.ShapeDtypeStruct((B, n_pages * cpp, 2 * w_blk.shape[2]), F32),
        grid_spec=pltpu.PrefetchScalarGridSpec(
            num_scalar_prefetch=1,
            grid=(B, n_pages // G),
            in_specs=[pg_spec(j) for j in range(G)] + [pl.BlockSpec(w_blk.shape, lambda b, g, pt: (0, 0, 0))],
            out_specs=pl.BlockSpec((1, G * cpp, 2 * w_blk.shape[2]), lambda b, g, pt: (b, g, 0)),
            scratch_shapes=[pltpu.VMEM((2, G * cpp, kdim), F32), pltpu.VMEM((2, G, page, KV_WIDTH), F32)]),
        compiler_params=_cparams(2),
        name="cmp_partial_t",
    )(page_table.reshape(-1).astype(jnp.int32), *([pool_t] * G), w_blk)


def _gelu_tanh(x):
    return 0.5 * x * (1.0 + jnp.tanh(math.sqrt(2.0 / math.pi) * (x + 0.044715 * (x * x * x))))


def _cmp_fin_kernel(ab_ref, pe_ref, w1_ref, w2_ref, kc_ref, vct_ref, *, nch):
    for c in range(2):
        pe_t = _dot(pe_ref[c:c + 1, :].astype(BF16), w1_ref[c].astype(BF16))
        w2 = w2_ref[c].astype(BF16)
        for k in range(N_KV_HEADS):
            base = (c * N_KV_HEADS + k) * 2 * HEAD_DIM
            slab = ab_ref[0, :, base:base + 2 * HEAD_DIM]
            nxt = pltpu.roll(slab, nch - 1, 0)
            pre = slab[:, 0:HEAD_DIM] + nxt[:, HEAD_DIM:2 * HEAD_DIM] + pe_t
            blocks = _dot(_gelu_tanh(pre).astype(BF16), w2)
            if c == 0:
                kc_ref[0, k] = blocks.astype(BF16)
            else:
                vct_ref[0, k] = blocks.T.astype(BF16)


def _cmp_finish(ab, cmp_pe, cmp_w1, cmp_w2):
    B, nch, w = ab.shape
    pe = cmp_pe.reshape(2, CMP_LEN * HEAD_DIM)
    return pl.pallas_call(
        functools.partial(_cmp_fin_kernel, nch=nch),
        out_shape=(jax.ShapeDtypeStruct((B, N_KV_HEADS, nch, HEAD_DIM), BF16),
                   jax.ShapeDtypeStruct((B, N_KV_HEADS, HEAD_DIM, nch), BF16)),
        grid=(B,),
        in_specs=[pl.BlockSpec((1, nch, w), lambda b: (b, 0, 0)),
                  pl.BlockSpec(pe.shape, lambda b: (0, 0)),
                  pl.BlockSpec(cmp_w1.shape, lambda b: (0, 0, 0)),
                  pl.BlockSpec(cmp_w2.shape, lambda b: (0, 0, 0))],
        out_specs=(pl.BlockSpec((1, N_KV_HEADS, nch, HEAD_DIM), lambda b: (b, 0, 0, 0)),
                   pl.BlockSpec((1, N_KV_HEADS, HEAD_DIM, nch), lambda b: (b, 0, 0, 0))),
        compiler_params=_cparams(1),
        name="cmp_finish",
    )(ab, pe, cmp_w1, cmp_w2)


def _cmp_block_weight(cmp_w1):
    hid = cmp_w1.shape[2]
    w = cmp_w1.reshape(2, 2, CMP_STRIDE, HEAD_DIM, hid)
    eye = jnp.eye(N_KV_HEADS, dtype=cmp_w1.dtype)
    wb = jnp.einsum('cardh,kj->crkdjah', w, eye)
    return wb.reshape(2, CMP_STRIDE * N_KV_HEADS * HEAD_DIM, N_KV_HEADS * 2 * hid).astype(BF16)


def _score_matrix(nch, n_slc):
    i = np.arange(nch)[:, None]
    j = np.arange(n_slc)[None, :]
    m = 2.0 * ((i // RATIO == j) & (i % RATIO < RATIO - 1)) + 1.0 * (i == RATIO * j + RATIO - 1) \
        + 1.0 * (i == RATIO * j - 1)
    return jnp.asarray(m, dtype=BF16)


def _nsa_kernel(q_ref, kc_ref, vc_ref, ks_ref, vs_ref, kw_ref, vw_ref, gate_ref, sa_ref, sw_ref,
                cs_ref, mm_ref, e_ref, o_ref, m_ref, l_ref, acc_ref, sta_ref, stb_ref, stc_ref, std_ref,
                oc_ref, pen_ref, *, nch, n_slc, parts):
    j = pl.program_id(2)
    q0 = pl.multiple_of(j * Q_BLOCK, Q_BLOCK)
    cq = lax.shift_right_logical(j, TILES.bit_length() - 1)
    jm = jnp.bitwise_and(j, TILES - 1)
    qt = q_ref[0, 0, 0]
    lane_q = jnp.bitwise_and(lax.broadcasted_iota(jnp.int32, (1, ROWS_Q), 1), Q_BLOCK - 1)
    tq = q0 + lane_q
    has_prev = cq >= 1
    k_own = pl.multiple_of(cq * KEY_CHUNK, KEY_CHUNK)
    k_prev = pl.multiple_of(jnp.maximum(cq - 1, 0) * KEY_CHUNK, KEY_CHUNK)
    own0 = pl.multiple_of((2 * TILES - 1 - jm) * Q_BLOCK, Q_BLOCK)
    prev0 = pl.multiple_of(jnp.where(has_prev, TILES - 1 - jm, 2 * TILES) * Q_BLOCK, Q_BLOCK)
    prevw0 = pl.multiple_of(jnp.where(has_prev, TILES - 1 - jm, 2 * TILES - 1) * Q_BLOCK, Q_BLOCK)

    stc_ref[...] = _dot(kw_ref[0, 0, pl.ds(k_own, KEY_CHUNK), :], qt) + sa_ref[0, pl.ds(own0, KEY_CHUNK), :]
    std_ref[...] = _dot(kw_ref[0, 0, pl.ds(k_prev, KEY_CHUNK), :], qt) + sw_ref[0, pl.ds(prevw0, KEY_CHUNK), :]

    per_q = Q_BLOCK // CMP_STRIDE
    nq = pl.num_programs(2)
    c0 = pl.multiple_of((nq - 1 - j) * per_q, per_q)
    anyc = tq >= CMP_LEN - 1
    cur = lax.shift_right_logical(q0 + lax.broadcasted_iota(jnp.int32, (1, Q_BLOCK), 1),
                                  SLC_BLOCK.bit_length() - 1)

    def cmp_and_select(nc, ns):
        lc = _dot(kc_ref[0, 0, 0:nc, :], qt) + cs_ref[0, pl.ds(c0, nc), :]
        p = jnp.exp2(lc - jnp.max(lc, axis=0, keepdims=True))
        pc = p * jnp.where(anyc, 1.0 / jnp.sum(p, axis=0, keepdims=True), 0.0)
        oc_ref[...] = _dot(vc_ref[0, 0, :, 0:nc], pc.astype(BF16))
        imp = pc[:, 0:Q_BLOCK]
        for g in range(1, GROUP):
            imp = imp + pc[:, g * Q_BLOCK:(g + 1) * Q_BLOCK]
        mm = mm_ref[0:ns, 0:nc]
        i1, i2, i3 = _split3(imp)
        score = _dot(mm, i1) + _dot(mm, i2) + _dot(mm, i3)
        blk = lax.broadcasted_iota(jnp.int32, (ns, 1), 0)
        forced = (blk == 0) | (blk == cur) | (blk == cur - 1)
        sc = jnp.where(forced, FORCE_SCORE, jnp.where(blk <= cur, score, NEG_INF))
        big = jnp.int32(INT_BIG)
        for _ in range(min(N_SEL, ns)):
            mx = jnp.max(sc, axis=0, keepdims=True)
            first = jnp.min(jnp.where(sc == mx, blk, big), axis=0, keepdims=True)
            sc = jnp.where(blk == first, PICKED, sc)
        pen_ref[0:ns, :] = jnp.where(sc == PICKED, 0.0, NEG_INF).astype(BF16)
        if ns < n_slc:
            pen_ref[ns:n_slc, :] = jnp.full((n_slc - ns, Q_BLOCK), NEG_INF, BF16)

    for part in range(1, parts + 1):
        @pl.when(jnp.logical_and(j * parts >= (part - 1) * nq, j * parts < part * nq))
        def _(part=part):
            cmp_and_select(nch * part // parts, n_slc * part // parts)

    o_c = oc_ref[...]
    qsel = jnp.concatenate([jnp.concatenate([pen_ref[...]] * GROUP, axis=1), qt], axis=0)

    def scores_sel(k0):
        ke = jnp.concatenate([e_ref[pl.ds(k0, KEY_CHUNK), :], ks_ref[0, 0, pl.ds(k0, KEY_CHUNK), :]], axis=1)
        return _dot(ke, qsel)

    def flash(st, v, first):
        mx = jnp.max(st, axis=0, keepdims=True)
        if first:
            pe = jnp.exp2(st - mx)
            l_ref[...] = jnp.sum(pe, axis=0, keepdims=True)
            acc_ref[...] = _dot(v, pe.astype(BF16))
            m_ref[...] = mx
        else:
            m_old = m_ref[...]
            m_new = jnp.maximum(m_old, mx)
            a = jnp.exp2(m_old - m_new)
            pe = jnp.exp2(st - m_new)
            l_ref[...] = a * l_ref[...] + jnp.sum(pe, axis=0, keepdims=True)
            acc_ref[...] = a * acc_ref[...] + _dot(v, pe.astype(BF16))
            m_ref[...] = m_new

    sta_ref[...] = scores_sel(k_own) + sa_ref[0, pl.ds(own0, KEY_CHUNK), :]
    stb_ref[...] = scores_sel(k_prev) + sa_ref[0, pl.ds(prev0, KEY_CHUNK), :]
    flash(stc_ref[...], vw_ref[0, 0, :, pl.ds(k_own, KEY_CHUNK)], True)
    flash(std_ref[...], vw_ref[0, 0, :, pl.ds(k_prev, KEY_CHUNK)], False)
    o_w = acc_ref[...] * (1.0 / l_ref[...])

    flash(sta_ref[...], vs_ref[0, 0, :, pl.ds(k_own, KEY_CHUNK)], True)
    n_far = jnp.maximum(cq - 1, 0)
    last_chunk = ks_ref.shape[2] // KEY_CHUNK - 1

    def chunk_start(c):
        return pl.multiple_of(jnp.minimum(c, last_chunk) * KEY_CHUNK, KEY_CHUNK)

    sta_ref[...] = scores_sel(chunk_start(0))
    flash(stb_ref[...], vs_ref[0, 0, :, pl.ds(k_prev, KEY_CHUNK)], False)

    def far_steps(c0, n):
        bufs = (sta_ref, stb_ref)
        for t in range(n):
            cur_buf, nxt_buf = bufs[t % 2], bufs[(t + 1) % 2]
            nxt_buf[...] = scores_sel(chunk_start(c0 + t + 1))
            flash(cur_buf[...], vs_ref[0, 0, :, pl.ds(chunk_start(c0 + t), KEY_CHUNK)], False)

    def far_group(i, carry):
        far_steps(FAR_UNROLL * i, FAR_UNROLL)
        return carry

    n_groups = lax.shift_right_logical(n_far, FAR_UNROLL.bit_length() - 1)
    lax.fori_loop(0, n_groups, far_group, 0)
    rest0 = n_groups * FAR_UNROLL

    @pl.when(jnp.bitwise_and(n_far, 2) == 2)
    def _():
        far_steps(rest0, 2)

    @pl.when(jnp.bitwise_and(n_far, 1) == 1)
    def _():
        flash(sta_ref[...], vs_ref[0, 0, :, pl.ds(chunk_start(n_far - 1), KEY_CHUNK)], False)

    o_s = acc_ref[...] * (1.0 / l_ref[...])

    g3 = gate_ref[0, 0, 0]
    o = g3[0:1, :] * o_c + g3[1:2, :] * o_s + g3[2:3, :] * o_w
    stacked = jnp.concatenate([o[:, g * Q_BLOCK:(g + 1) * Q_BLOCK] for g in range(GROUP)], axis=0)
    o_ref[0] = stacked.T


def _bias_tables(table, nch, nq):
    tbl = table.reshape(N_BUCKETS, N_KV_HEADS, GROUP)
    far = tbl[N_BUCKETS - 1]
    i = jnp.arange(Q_BLOCK)

    def cols(rel):
        hot = (_rel_bucket(rel)[..., None] == jnp.arange(N_BUCKETS)).astype(F32)
        b = (jnp.einsum('kqn,nhg->kqhg', hot, tbl, precision=lax.Precision.HIGHEST) - far) * LOG2E
        b = jnp.where((rel >= 0)[:, :, None, None], b, 0.0)
        return b.transpose(2, 0, 3, 1).reshape(N_KV_HEADS, rel.shape[0], ROWS_Q)

    p0 = cols(i[None, :] - i[:, None])
    p1 = cols(Q_BLOCK + i[None, :] - i[:, None])
    per_q = Q_BLOCK // CMP_STRIDE
    rel_c = i[None, :] - CMP_STRIDE * (jnp.arange(nch + per_q * (nq - 1))[:, None] - per_q * (nq - 1)) - (CMP_LEN - 1)
    strip_cmp = jnp.where(jnp.tile(rel_c >= 0, (1, GROUP))[None], cols(rel_c), NEG_INF)
    kk = jnp.arange(Q_BLOCK)[:, None]
    qq = jnp.tile(jnp.arange(Q_BLOCK), GROUP)[None, :]
    bc = lambda a: jnp.broadcast_to(a[None], (N_KV_HEADS, Q_BLOCK, ROWS_Q))
    neg = bc(jnp.full((Q_BLOCK, ROWS_Q), NEG_INF, F32))
    zero = bc(jnp.zeros((Q_BLOCK, ROWS_Q), F32))
    p0_causal = jnp.where((kk <= qq)[None], p0, NEG_INF)
    in_window = bc(jnp.where(kk > qq, 0.0, NEG_INF).astype(F32))
    strip_own = jnp.concatenate([zero] * (2 * TILES - 2) + [p1, p0_causal] + [neg] * TILES, axis=1)
    strip_win = jnp.concatenate([neg] * (TILES - 1) + [in_window] + [zero] * (TILES - 2) + [p1] + [neg] * TILES,
                                axis=1)
    return strip_own, strip_win, strip_cmp


def _nsa_prompt(q_t, k_hm, vt_hm, kc, vct, gates_t, table):
    B, _, T, hd = k_hm.shape
    H = N_HEADS
    nch = kc.shape[2]
    n_slc = T // SLC_BLOCK
    nq = T // Q_BLOCK
    assert T % KEY_CHUNK == 0 and nch == RATIO * n_slc and WINDOW <= KEY_CHUNK
    strip_own, strip_win, strip_cmp = _bias_tables(table, nch, nq)
    e_tab = (jnp.arange(T)[:, None] // SLC_BLOCK == jnp.arange(n_slc)[None, :]).astype(BF16)
    mm = _score_matrix(nch, n_slc).T
    nk = N_KV_HEADS
    k_spec = lambda slab0: pl.BlockSpec((1, 1, T, hd), lambda b, k, j: (b, slab0 + k, 0, 0))
    vt_spec = lambda slab0: pl.BlockSpec((1, 1, hd, T), lambda b, k, j: (b, slab0 + k, 0, 0))
    per_head = lambda a: pl.BlockSpec((1,) + a.shape[1:], lambda b, k, j: (k, 0, 0))
    score_buf = pltpu.VMEM((KEY_CHUNK, ROWS_Q), F32)
    parts = max(p for p in (4, 2, 1) if nq % p == 0 and (nch // p) % LANE == 0
                and (n_slc // p) % BF16_ROWS == 0)
    return pl.pallas_call(
        functools.partial(_nsa_kernel, nch=nch, n_slc=n_slc, parts=parts),
        out_shape=jax.ShapeDtypeStruct((B, T, H * hd), F32),
        grid=(B, nk, nq),
        in_specs=[pl.BlockSpec((1, 1, 1, hd, ROWS_Q), lambda b, k, j: (b, k, j, 0, 0)),
                  pl.BlockSpec((1, 1, nch, hd), lambda b, k, j: (b, k, 0, 0)),
                  pl.BlockSpec((1, 1, hd, nch), lambda b, k, j: (b, k, 0, 0)),
                  k_spec(0), vt_spec(0), k_spec(nk), vt_spec(nk),
                  pl.BlockSpec((1, 1, 1, 3, ROWS_Q), lambda b, k, j: (b, k, j, 0, 0)),
                  per_head(strip_own), per_head(strip_win), per_head(strip_cmp),
                  pl.BlockSpec(mm.shape, lambda b, k, j: (0, 0)),
                  pl.BlockSpec(e_tab.shape, lambda b, k, j: (0, 0))],
        out_specs=pl.BlockSpec((1, Q_BLOCK, GROUP * hd), lambda b, k, j: (b, j, k)),
        scratch_shapes=[pltpu.VMEM((1, ROWS_Q), F32), pltpu.VMEM((1, ROWS_Q), F32),
                        pltpu.VMEM((hd, ROWS_Q), F32), score_buf, score_buf, score_buf, score_buf,
                        pltpu.VMEM((hd, ROWS_Q), F32), pltpu.VMEM((n_slc, Q_BLOCK), BF16)],
        compiler_params=_cparams(3),
        name="nsa_prompt",
    )(q_t, kc, vct, k_hm, vt_hm, k_hm, vt_hm, gates_t, strip_own, strip_win, strip_cmp, mm, e_tab)


def _smp_cmp_kernel(q_ref, kc_ref, vct_ref, bias_ref, mm_ref, oc_ref, idx_ref, *, n_cmp, n_slc, cur):
    nch = kc_ref.shape[2]
    ncol = mm_ref.shape[1]
    coli = lax.broadcasted_iota(jnp.int32, (1, nch), 1)
    maskc = coli < n_cmp
    blk = lax.broadcasted_iota(jnp.int32, (1, ncol), 1)
    k_sel = min(N_SEL, n_slc)
    lane = lax.broadcasted_iota(jnp.int32, (1, LANE), 1)
    nb = q_ref.shape[0]
    mm = mm_ref[...]
    forced = (blk == 0) | (blk == cur) | (blk == cur - 1)
    scores = []
    for b in range(nb):
        for kh in range(N_KV_HEADS):
            qm = q_ref[b, kh * GROUP:(kh + 1) * GROUP].astype(BF16)
            lc = _dot_nt(qm, kc_ref[b, kh]) + bias_ref[kh]
            lcm = jnp.where(maskc, lc, NEG_INF)
            p = jnp.where(maskc, jnp.exp(lcm - jnp.max(lcm, axis=-1, keepdims=True)), 0.0)
            pc = p * (1.0 / jnp.sum(p, axis=-1, keepdims=True))
            oc_ref[b, kh * GROUP:(kh + 1) * GROUP] = _dot_nt(pc.astype(BF16), vct_ref[b, kh])
            imp = pc[0:1]
            for g in range(1, GROUP):
                imp = imp + pc[g:g + 1]
            i1, i2, i3 = _split3(imp)
            score = _dot(i1, mm) + _dot(i2, mm) + _dot(i3, mm)
            sc = jnp.where(forced, FORCE_SCORE, jnp.where(blk <= cur, score, NEG_INF))
            scores.append(jnp.where(blk < n_slc, sc, NOT_A_SLOT))
    sc = jnp.concatenate(scores, axis=0)
    big = jnp.int32(INT_BIG)
    out = jnp.zeros((nb * N_KV_HEADS, LANE), jnp.int32)
    for it in range(k_sel):
        m = jnp.max(sc, axis=-1, keepdims=True)
        first = jnp.min(jnp.where(sc == m, blk, big), axis=-1, keepdims=True)
        out = jnp.where(lane == it, first, out)
        sc = jnp.where(blk == first, PICKED, sc)
    idx_ref[0] = out


def _smp_cmp(q_hm, kc, vct, bias_c, mm, n_cmp, n_slc, cur):
    B = q_hm.shape[0]
    nch = kc.shape[2]
    nb = math.gcd(B, 8)
    o_c, idx = pl.pallas_call(
        functools.partial(_smp_cmp_kernel, n_cmp=n_cmp, n_slc=n_slc, cur=cur),
        out_shape=(jax.ShapeDtypeStruct((B, N_HEADS, HEAD_DIM), F32),
                   jax.ShapeDtypeStruct((B // nb, nb * N_KV_HEADS, LANE), jnp.int32)),
        grid=(B // nb,),
        in_specs=[pl.BlockSpec((nb, N_HEADS, HEAD_DIM), lambda b: (b, 0, 0)),
                  pl.BlockSpec((nb, N_KV_HEADS, nch, HEAD_DIM), lambda b: (b, 0, 0, 0)),
                  pl.BlockSpec((nb, N_KV_HEADS, HEAD_DIM, nch), lambda b: (b, 0, 0, 0)),
                  pl.BlockSpec(bias_c.shape, lambda b: (0, 0, 0)),
                  pl.BlockSpec(mm.shape, lambda b: (0, 0))],
        out_specs=(pl.BlockSpec((nb, N_HEADS, HEAD_DIM), lambda b: (b, 0, 0)),
                   pl.BlockSpec((1, nb * N_KV_HEADS, LANE), lambda b: (b, 0, 0))),
        compiler_params=_cparams(1),
        name="sample_cmp",
    )(q_hm, kc, vct, bias_c, mm)
    return o_c, idx.reshape(B, N_KV_HEADS, LANE)


def _smp_att_kernel(phys_ref, q_ref, pool_ref, bsel_ref, msel_ref, new_ref, win_ref, bwin_ref, bnew_ref,
                    gate_ref, oc_ref, o_ref, buf_ref, sem_ref, *, k_sel):
    b = pl.program_id(0)
    n_blk = N_KV_HEADS * k_sel

    def blk_copy(i):
        return pltpu.make_async_copy(pool_ref.at[phys_ref[b * n_blk + i]], buf_ref.at[i], sem_ref.at[i])

    for i in range(n_blk):
        blk_copy(i).start()
    for i in range(n_blk):
        blk_copy(i).wait()

    nw = win_ref.shape[4]
    for kh in range(N_KV_HEADS):
        rows = slice(kh * GROUP, (kh + 1) * GROUP)
        qm = q_ref[0, rows].astype(BF16)
        kcol = slice(kh * HEAD_DIM, (kh + 1) * HEAD_DIM)
        vcol = slice(KV_WIDTH + kh * HEAD_DIM, KV_WIDTH + (kh + 1) * HEAD_DIM)
        bnew = bnew_ref[kh]

        def branch(kt, vt, bias, mask, knew, vnew, new_ok):
            lg = _dot(qm, kt) + bias
            lg = jnp.where(mask, lg, NEG_INF)
            ln = jnp.sum(qm.astype(F32) * knew.astype(F32), axis=-1, keepdims=True) + bnew
            if new_ok is not None:
                ln = jnp.where(new_ok, ln, NEG_INF)
            mx = jnp.maximum(jnp.max(lg, axis=-1, keepdims=True), ln)
            pe = jnp.where(mask, jnp.exp(lg - mx), 0.0)
            pn = jnp.exp(ln - mx)
            if new_ok is not None:
                pn = jnp.where(new_ok, pn, 0.0)
            den = jnp.sum(pe, axis=-1, keepdims=True) + pn
            any_ok = den > 0.0
            inv = 1.0 / jnp.where(any_ok, den, 1.0)
            o = _dot_nt(pe.astype(BF16), vt) + pn.astype(BF16).astype(F32) * vnew.astype(F32)
            return jnp.where(any_ok, o * inv, 0.0)

        page = buf_ref.shape[4]
        nkeys = k_sel * page
        kt = jnp.concatenate([buf_ref[kh * k_sel + i, 0, kh] for i in range(k_sel)], axis=1).astype(BF16)
        vt = jnp.concatenate([buf_ref[kh * k_sel + i, 1, kh] for i in range(k_sel)], axis=1).astype(BF16)
        knew = new_ref[0, 0:1, kcol].astype(BF16)
        vnew = new_ref[0, 0:1, vcol].astype(BF16)
        msel = msel_ref[0, kh]
        o_s = branch(kt, vt, bsel_ref[0, kh], msel[:, 0:nkeys] > 0.5, knew, vnew, msel[:, nkeys:nkeys + 1] > 0.5)

        kw = win_ref[0, 0, kh].astype(BF16)
        vw = win_ref[0, 1, kh].astype(BF16)
        knw = new_ref[0, 1:2, kcol].astype(BF16)
        vnw = new_ref[0, 1:2, vcol].astype(BF16)
        dist = nw - lax.broadcasted_iota(jnp.int32, (1, nw), 1)
        o_w = branch(kw, vw, bwin_ref[kh], dist < WINDOW, knw, vnw, None)

        g3 = gate_ref[0, rows]
        o_ref[0, rows] = g3[:, 0:1] * oc_ref[0, rows] + g3[:, 1:2] * o_s + g3[:, 2:3] * o_w


def _smp_att(phys, q_hm, pool_blk, bias_sel, mask_sel, new_rows, win_buf, bias_win, bias_new, gates, o_c, k_sel):
    B = q_hm.shape[0]
    n_blk = N_KV_HEADS * k_sel
    full = lambda a: pl.BlockSpec(a.shape, lambda b, ph: (0,) * a.ndim)
    per_b = lambda a: pl.BlockSpec((1,) + a.shape[1:], lambda b, ph: (b,) + (0,) * (a.ndim - 1))
    return pl.pallas_call(
        functools.partial(_smp_att_kernel, k_sel=k_sel),
        out_shape=jax.ShapeDtypeStruct((B, N_HEADS, HEAD_DIM), F32),
        grid_spec=pltpu.PrefetchScalarGridSpec(
            num_scalar_prefetch=1,
            grid=(B,),
            in_specs=[per_b(q_hm), pl.BlockSpec(memory_space=pl.ANY), per_b(bias_sel), per_b(mask_sel),
                      per_b(new_rows), per_b(win_buf), full(bias_win), full(bias_new), per_b(gates), per_b(o_c)],
            out_specs=pl.BlockSpec((1, N_HEADS, HEAD_DIM), lambda b, ph: (b, 0, 0)),
            scratch_shapes=[pltpu.VMEM((n_blk,) + pool_blk.shape[1:], F32),
                            pltpu.SemaphoreType.DMA((n_blk,))]),
        compiler_params=_cparams(1),
        name="sample_att",
    )(phys, q_hm, pool_blk, bias_sel, mask_sel, new_rows, win_buf, bias_win, bias_new, gates, o_c)


def _merge_kernel(x_ref, cn_ref, att_ref, gna_ref, wo_ref, m2_ref, m3_ref, m4_ref, g2_ref, rw_ref, rb_ref,
                  x1_ref, h2_ref, ti_ref, tg_ref, *, n_exp, cw_width):
    att_n = _rms(att_ref[0], gna_ref[...]).astype(BF16)
    mix = _dot(cn_ref[0], wo_ref[0:cw_width, :]) + _dot(att_n, wo_ref[cw_width:, :])
    x1 = x_ref[0] + m2_ref[0] * mix
    x1_ref[0] = x1
    h2 = _rms(x1, g2_ref[...]) * (1.0 + m4_ref[0]) + m3_ref[0]
    h2_ref[0] = h2
    logits = _dot3(h2, rw_ref[...]) + rb_ref[...]
    lane = lax.broadcasted_iota(jnp.int32, (1, LANE), 1)
    sc = jnp.where(lane < n_exp, logits, NOT_A_SLOT)
    big = jnp.int32(INT_BIG)
    ti = jnp.zeros(sc.shape, jnp.int32)
    tv = jnp.zeros(sc.shape, F32)
    v0 = None
    den = None
    for k in range(TOP_K):
        m = jnp.max(sc, axis=-1, keepdims=True)
        first = jnp.min(jnp.where(sc == m, lane, big), axis=-1, keepdims=True)
        if k == 0:
            v0 = m
        e = jnp.exp(m - v0)
        den = e if den is None else den + e
        ti = jnp.where(lane == k, first, ti)
        tv = jnp.where(lane == k, e, tv)
        sc = jnp.where(lane == first, PICKED, sc)
    ti_ref[0] = ti
    tg_ref[0] = tv * (1.0 / den)


def _merge(x, convn, att, gn_att, wo_b, m2, m3, m4, ln2_g, rw_pad, rb_pad, *, seq_mode, tm, n_exp):
    B, T, D = x.shape
    cw_width = convn.shape[2]
    aw = att.shape[2]
    row_blk = lambda w: pl.BlockSpec((1, tm, w), lambda b, t: (b, t, 0))
    mod_spec = pl.BlockSpec((1, 1, D), lambda b, t: (b, 0, 0)) if seq_mode else row_blk(D)
    const = lambda shape: pl.BlockSpec(shape, lambda b, t: (0,) * len(shape))
    return pl.pallas_call(
        functools.partial(_merge_kernel, n_exp=n_exp, cw_width=cw_width),
        out_shape=(jax.ShapeDtypeStruct((B, T, D), F32), jax.ShapeDtypeStruct((B, T, D), F32),
                   jax.ShapeDtypeStruct((B, T, LANE), jnp.int32), jax.ShapeDtypeStruct((B, T, LANE), F32)),
        grid=(B, T // tm),
        in_specs=[row_blk(D), row_blk(cw_width), row_blk(aw), const((1, aw)), const(wo_b.shape),
                  mod_spec, mod_spec, mod_spec, const((1, D)), const(rw_pad.shape), const((1, LANE))],
        out_specs=(row_blk(D), row_blk(D), row_blk(LANE), row_blk(LANE)),
        compiler_params=_cparams(2),
        name="merge_seq" if seq_mode else "merge_rows",
    )(x, convn, att, gn_att.reshape(1, aw), wo_b, m2, m3, m4, ln2_g.reshape(1, D), rw_pad, rb_pad)


def _moe_kernel(be_ref, nu_ref, xs_ref, wgu_ref, bgu_ref, wd_ref, bd_ref, y_ref, wgu_b, wd_b, *, d_ff):
    i = pl.program_id(0)
    prev = be_ref[jnp.maximum(i - 1, 0)]
    changed = jnp.logical_or(i == 0, be_ref[i] != prev)

    @pl.when(jnp.logical_and(changed, i < nu_ref[0]))
    def _():
        wgu_b[...] = wgu_ref[0].astype(BF16)
        wd_b[...] = wd_ref[0].astype(BF16)

    @pl.when(i < nu_ref[0])
    def _():
        gu = _dot(xs_ref[...].astype(BF16), wgu_b[...]) + bgu_ref[0]
        gate_h = jnp.minimum(gu[:, 0:d_ff], SWIGLU_LIMIT)
        up_h = jnp.clip(gu[:, d_ff:], -SWIGLU_LIMIT, SWIGLU_LIMIT)
        act = (up_h + 1.0) * gate_h * _sigmoid(SWIGLU_ALPHA * gate_h)
        y_ref[...] = _dot(act.astype(BF16), wd_b[...]) + bd_ref[0]

    @pl.when(i >= nu_ref[0])
    def _():
        y_ref[...] = jnp.zeros(y_ref.shape, F32)


def _moe_experts(block_e, n_used, xs, w_gu, b_gu, w_down, b_down):
    n_rows, D = xs.shape
    E, _, two_ff = w_gu.shape
    d_ff = two_ff // 2
    n_blocks = n_rows // MOE_ROWS
    return pl.pallas_call(
        functools.partial(_moe_kernel, d_ff=d_ff),
        out_shape=jax.ShapeDtypeStruct((n_rows, D), F32),
        grid_spec=pltpu.PrefetchScalarGridSpec(
            num_scalar_prefetch=2,
            grid=(n_blocks,),
            in_specs=[pl.BlockSpec((MOE_ROWS, D), lambda i, be, nu: (i, 0)),
                      pl.BlockSpec((1, D, two_ff), lambda i, be, nu: (be[i], 0, 0)),
                      pl.BlockSpec((1, 1, two_ff), lambda i, be, nu: (be[i], 0, 0)),
                      pl.BlockSpec((1, d_ff, D), lambda i, be, nu: (be[i], 0, 0)),
                      pl.BlockSpec((1, 1, D), lambda i, be, nu: (be[i], 0, 0))],
            out_specs=pl.BlockSpec((MOE_ROWS, D), lambda i, be, nu: (i, 0)),
            scratch_shapes=[pltpu.VMEM((D, two_ff), BF16), pltpu.VMEM((d_ff, D), BF16)]),
        compiler_params=_cparams(1),
        name="moe_experts",
    )(block_e, n_used, xs, w_gu, b_gu.reshape(E, 1, two_ff), w_down, b_down.reshape(E, 1, D))


def _route(top_idx, n_tok, n_exp):
    n_assign = n_tok * TOP_K
    flat_e = top_idx.reshape(-1)
    experts = jnp.arange(n_exp, dtype=jnp.int32)
    onehot = (flat_e[:, None] == experts[None, :]).astype(jnp.int32)
    csum = jnp.cumsum(onehot, axis=0)
    counts = csum[-1]
    padded = (counts + MOE_ROWS - 1) // MOE_ROWS * MOE_ROWS
    pad_end = jnp.cumsum(padded)
    pad_start = pad_end - padded
    dest = jnp.sum(onehot * (csum - 1 + pad_start[None, :]), axis=1).astype(jnp.int32)
    n_blocks = -(-(n_assign + n_exp * (MOE_ROWS - 1)) // MOE_ROWS)
    n_rows = n_blocks * MOE_ROWS
    blk_row0 = jnp.arange(n_blocks, dtype=jnp.int32) * MOE_ROWS
    block_e = jnp.minimum(jnp.sum((pad_end[None, :] <= blk_row0[:, None]).astype(jnp.int32), axis=1), n_exp - 1)
    n_fill = n_rows - n_assign
    fill_e = jnp.arange(n_fill, dtype=jnp.int32) // (MOE_ROWS - 1)
    fill_i = jnp.arange(n_fill, dtype=jnp.int32) % (MOE_ROWS - 1)
    fill_on = (fill_e < n_exp) & (fill_i < jnp.sum(
        (fill_e[:, None] == experts[None, :]) * (padded - counts)[None, :], axis=1))
    keys = jnp.concatenate([2 * flat_e, jnp.where(fill_on, 2 * fill_e + 1, 2 * n_exp + 1)])
    toks = jnp.concatenate([jnp.arange(n_assign, dtype=jnp.int32) // TOP_K, jnp.full((n_fill,), n_tok, jnp.int32)])
    _, row_tok = lax.sort((keys, toks), num_keys=1, is_stable=True)
    n_used = (pad_end[-1] // MOE_ROWS).astype(jnp.int32).reshape(1)
    return row_tok, dest.reshape(n_tok, TOP_K), block_e.astype(jnp.int32), n_used


def _final_kernel(x1_ref, yg_ref, tg_ref, m5_ref, fg_ref, o_ref):
    tg = tg_ref[0]
    ff = tg[:, 0:1] * yg_ref[0, 0]
    for k in range(1, TOP_K):
        ff = ff + tg[:, k:k + 1] * yg_ref[0, k]
    o_ref[0] = _rms(x1_ref[0] + m5_ref[0] * ff, fg_ref[...])


def _final(x1, yg, tg, m5, final_g, *, seq_mode, tm):
    B, T, D = x1.shape
    row_blk = lambda w: pl.BlockSpec((1, tm, w), lambda b, t: (b, t, 0))
    mod_spec = pl.BlockSpec((1, 1, D), lambda b, t: (b, 0, 0)) if seq_mode else row_blk(D)
    return pl.pallas_call(
        _final_kernel,
        out_shape=jax.ShapeDtypeStruct((B, T, D), F32),
        grid=(B, T // tm),
        in_specs=[row_blk(D), pl.BlockSpec((1, TOP_K, tm, D), lambda b, t: (b, 0, t, 0)), row_blk(LANE),
                  mod_spec, pl.BlockSpec((1, D), lambda b, t: (0, 0))],
        out_specs=row_blk(D),
        compiler_params=_cparams(2),
        name="final_seq" if seq_mode else "final_rows",
    )(x1, yg, tg, m5, final_g.reshape(1, D))


def _row_tile(t):
    for tm in (512, 256, 128, 64, 32, 16, 8):
        if t % tm == 0:
            return tm
    raise ValueError(f"unsupported row count {t}")


def _rows(a, idx):
    return a.at[idx].get(mode="promise_in_bounds")


def kernel(x_prompt, x_sample, cache_kv_cmp, cache_kv_slc, state_kv_win, state_conv, page_table, c_prompt, c_sample,
           rel_bias_table, ln1_g, ln2_g, w_ada, b_ada, w_in, conv_w, cmp_pe, cmp_w1, cmp_w2, gn_conv, gn_att, w_o,
           router_w, router_b, w_gu, b_gu, w_down, b_down, final_g):
    B, T, D = x_prompt.shape
    BS, TS, _ = x_sample.shape
    depth = w_in.shape[0]
    assert depth == 1 and TS == 1
    n_pool, page = cache_kv_cmp.shape[1], cache_kv_cmp.shape[2]
    n_pages = page_table.shape[1]
    past_len = n_pages * page
    n_exp = router_w.shape[2]
    cw_width = conv_w.shape[2]
    kvw2 = 2 * KV_WIDTH
    assert past_len % SLC_BLOCK == 0 and past_len % CMP_STRIDE == 0 and T % page == 0

    in_cols = w_in.shape[2]
    gate0 = 3 * cw_width + N_HEADS * HEAD_DIM + 3 * kvw2
    w_pad = jnp.pad(w_in[0], ((0, 0), (0, gate0 + LANE - in_cols))).astype(BF16)
    wo_b = w_o[0].astype(BF16)
    rw_pad = jnp.pad(router_w[0], ((0, 0), (0, LANE - n_exp)))
    rb_pad = jnp.pad(router_b[0], (0, LANE - n_exp)).reshape(1, LANE)
    w_blk = _cmp_block_weight(cmp_w1[0])

    n_c = B + BS
    n_cp = -(-n_c // 8) * 8
    c_all = jnp.pad(jnp.concatenate([c_prompt, c_sample], axis=0), ((0, n_cp - n_c), (0, 0)))
    mod = _modulation(c_all, w_ada[0], b_ada[0]).reshape(n_cp, 6, D)
    mp = [mod[:B, i].reshape(B, 1, D) for i in range(6)]
    ms = [mod[B:n_c, i].reshape(1, BS, D) for i in range(6)]

    tm = _row_tile(T)
    nq = T // Q_BLOCK
    zeros_prev = jnp.zeros((B, CONV_K - 1, cw_width), F32)
    convn_p, q_p, kvc_p, kvs_p, kvw_p, gate_p, vlast_p, kh_p, vth_p = _inproj(
        x_prompt, mp[0], mp[1], ln1_g[0], w_pad, conv_w[0], gn_conv[0], zeros_prev, zeros_prev, seq_mode=True, tm=tm)
    pt_p = jnp.arange(B * (T // page), dtype=jnp.int32).reshape(B, T // page)
    g_p = math.gcd(T // page, CMP_PAGES)
    ab_p = _cmp_ab(kvc_p.reshape(B * (T // page), page, kvw2), pt_p, w_blk, g_p)
    kc_p, vct_p = _cmp_finish(ab_p, cmp_pe[0], cmp_w1[0], cmp_w2[0])
    gates_p = gate_p[:, :, :3 * N_HEADS].reshape(B, nq, Q_BLOCK, N_KV_HEADS, GROUP, 3)
    gates_p = gates_p.transpose(0, 3, 1, 5, 4, 2).reshape(B, N_KV_HEADS, nq, 3, ROWS_Q)
    att_p = _nsa_prompt(q_p, kh_p, vth_p, kc_p, vct_p, gates_p, rel_bias_table)
    x1_p, h2_p, ti_p, tg_p = _merge(x_prompt, convn_p, att_p, gn_att[0], wo_b, mp[2], mp[3], mp[4], ln2_g[0],
                                    rw_pad, rb_pad, seq_mode=True, tm=tm, n_exp=n_exp)

    xs_rows = x_sample.reshape(1, BS, D)
    prev2 = state_conv[0][:, 0, :].reshape(1, BS, cw_width)
    prev1 = state_conv[0][:, 1, :].reshape(1, BS, cw_width)
    tms = _row_tile(BS)
    convn_s, q_s, kvc_s, kvs_s, kvw_s, gate_s, v_s = _inproj(
        xs_rows, ms[0], ms[1], ln1_g[0], w_pad, conv_w[0], gn_conv[0], prev2, prev1, seq_mode=False, tm=tms)
    n_cmp = (past_len + TS - CMP_LEN) // CMP_STRIDE + 1
    nch_s = past_len // CMP_STRIDE
    assert n_cmp + 1 == nch_s
    ab_s = _cmp_ab_t(cache_kv_cmp[0].transpose(0, 2, 3, 4, 1), page_table, w_blk,
                     math.gcd(n_pages, CMP_PAGES))
    kc_s, vct_s = _cmp_finish(ab_s, cmp_pe[0], cmp_w1[0], cmp_w2[0])
    t_q = past_len
    n_slc = -(-(past_len + TS) // SLC_BLOCK)
    cur = t_q // SLC_BLOCK
    k_sel = min(N_SEL, n_slc)
    tbl = rel_bias_table.reshape(N_BUCKETS, N_KV_HEADS, GROUP)
    pos_c = jnp.arange(nch_s) * CMP_STRIDE + CMP_LEN - 1
    bias_c = tbl[_rel_bucket(t_q - pos_c)].transpose(1, 2, 0)
    ncol = -(-n_slc // LANE) * LANE
    mm_s = _score_matrix(nch_s, ncol)
    q_s_hm = q_s.reshape(N_HEADS, BS, HEAD_DIM).transpose(1, 0, 2).astype(F32)
    o_c_s, idx_pad = _smp_cmp(q_s_hm, kc_s, vct_s, bias_c, mm_s, n_cmp, n_slc, cur)
    idx = idx_pad[:, :, :k_sel]
    blk_per_page = page // SLC_BLOCK
    pg = jnp.minimum(idx // blk_per_page, n_pages - 1)
    phys = jnp.take_along_axis(page_table, pg.reshape(BS, -1), axis=1).astype(jnp.int32)
    pos_s = pg[..., None] * page + jnp.arange(page)
    in_blk = (pos_s // SLC_BLOCK == idx[..., None]) & (pos_s < past_len)
    pos_s = pos_s.reshape(BS, N_KV_HEADS, k_sel * page)
    bucket_hot = (_rel_bucket(t_q - pos_s)[..., None] == jnp.arange(N_BUCKETS)).astype(F32)
    bias_sel = jnp.einsum('bksn,nkg->bkgs', bucket_hot, tbl, precision=lax.Precision.HIGHEST)
    new_sel = jnp.any(idx == cur, axis=-1, keepdims=True)
    mask_sel = jnp.concatenate([in_blk.reshape(BS, N_KV_HEADS, k_sel * page), new_sel], axis=-1)
    mask_sel = mask_sel.astype(F32)[:, :, None, :]
    nw = state_kv_win.shape[2]
    bias_win = tbl[_rel_bucket(nw - jnp.arange(nw))].transpose(1, 2, 0)
    bias_new = tbl[0].reshape(N_KV_HEADS, GROUP, 1)
    new_rows = jnp.stack([kvs_s[0], kvw_s[0]], axis=1)
    gates_s = gate_s[0, :, :3 * N_HEADS].reshape(BS, N_HEADS, 3)
    pool_t = cache_kv_slc[0].transpose(0, 2, 3, 4, 1)
    win_t = state_kv_win[0].transpose(0, 2, 3, 4, 1)
    att_s = _smp_att(phys.reshape(-1), q_s_hm, pool_t, bias_sel, mask_sel, new_rows,
                     win_t, bias_win, bias_new, gates_s, o_c_s, k_sel)
    att_s = att_s.reshape(1, BS, N_HEADS * HEAD_DIM)
    x1_s, h2_s, ti_s, tg_s = _merge(xs_rows, convn_s, att_s, gn_att[0], wo_b, ms[2], ms[3], ms[4], ln2_g[0],
                                    rw_pad, rb_pad, seq_mode=False, tm=tms, n_exp=n_exp)

    n_tok = B * T + BS
    h2_all = jnp.concatenate([h2_p.reshape(B * T, D), h2_s.reshape(BS, D), jnp.zeros((8, D), F32)], axis=0)
    top_idx = jnp.concatenate([ti_p.reshape(B * T, LANE), ti_s.reshape(BS, LANE)], axis=0)[:, :TOP_K]
    row_tok, dest, block_e, n_used = _route(top_idx, n_tok, n_exp)
    xs = _rows(h2_all, row_tok)
    yb = _moe_experts(block_e, n_used, xs, w_gu[0], b_gu[0], w_down[0], b_down[0])
    dest_p = dest[:B * T].reshape(B, T, TOP_K).transpose(0, 2, 1)
    yg_p = _rows(yb, dest_p.reshape(-1)).reshape(B, TOP_K, T, D)
    yg_s = _rows(yb, dest[B * T:].T.reshape(-1)).reshape(1, TOP_K, BS, D)
    y_p = _final(x1_p, yg_p, tg_p, mp[5], final_g, seq_mode=True, tm=tm)
    y_s = _final(x1_s, yg_s, tg_s, ms[5], final_g, seq_mode=False, tm=tms)

    kv_tail = (2, N_KV_HEADS, HEAD_DIM)
    page_shape = (depth, B, T // page, page) + kv_tail
    w_keep = min(WINDOW, T)
    new_win_s = jnp.concatenate([state_kv_win[0][:, TS:], kvw_s.reshape(BS, TS, *kv_tail)], axis=1)
    new_conv_s = jnp.concatenate([state_conv[0][:, TS:], v_s.reshape(BS, TS, cw_width)], axis=1)
    return (y_p, y_s.reshape(BS, TS, D),
            kvc_p.reshape(page_shape), kvc_s.reshape((depth, BS, TS) + kv_tail),
            kvs_p.reshape(page_shape), kvs_s.reshape((depth, BS, TS) + kv_tail),
            kvw_p[:, T - w_keep:].reshape((depth, B, w_keep) + kv_tail), new_win_s[None],
            vlast_p[None], new_conv_s[None])
```

```python
import functools
import math

import numpy as np
import jax
import jax.numpy as jnp
from jax import lax
from jax.experimental import pallas as pl
from jax.experimental.pallas import tpu as pltpu

F32 = jnp.float32
BF16 = jnp.bfloat16

CONV_K = 3
N_HEADS = 8
N_KV_HEADS = 2
GROUP = N_HEADS // N_KV_HEADS
HEAD_DIM = 64
KV_WIDTH = N_KV_HEADS * HEAD_DIM
CMP_LEN = 32
CMP_STRIDE = 16
SLC_BLOCK = 64
RATIO = SLC_BLOCK // CMP_STRIDE
N_SEL = 16
WINDOW = 512
Q_BLOCK = 128
N_BUCKETS = 32
REL_MAX_DIST = 128
TOP_K = 4
SWIGLU_LIMIT = 7.0
SWIGLU_ALPHA = 1.702
EPS = 1e-6
NEG_INF = -1e30
FORCE_SCORE = 1e4

LANE = 128
BF16_ROWS = 16
ROWS_Q = GROUP * Q_BLOCK
KEY_CHUNK = 512
TILES = KEY_CHUNK // Q_BLOCK
FAR_UNROLL = 4
MOE_ROWS = 512
CMP_PAGES = 64
SMP_BATCH = 4
LOG2E = math.log2(math.e)
MOD_COLS = 1536
VMEM_LIMIT = 48 * 1024 * 1024
PICKED = -3e38
NOT_A_SLOT = -2e38
INT_BIG = 1 << 30


def _cparams(n_axes, vmem_limit=VMEM_LIMIT):
    return pltpu.CompilerParams(dimension_semantics=("arbitrary",) * n_axes, vmem_limit_bytes=vmem_limit)


def _dot(a, b):
    return jnp.dot(a, b, preferred_element_type=F32)


def _dot_nt(a, b):
    return lax.dot_general(a, b, (((1,), (1,)), ((), ())), preferred_element_type=F32)


def _split2(x):
    hi = x.astype(BF16)
    lo = (x - hi.astype(F32)).astype(BF16)
    return hi, lo


def _split3(x):
    a = x.astype(BF16)
    r = x - a.astype(F32)
    b = r.astype(BF16)
    c = (r - b.astype(F32)).astype(BF16)
    return a, b, c


def _dot3(a, b):
    ah, al = _split2(a)
    bh, bl = _split2(b)
    return _dot(ah, bh) + _dot(ah, bl) + _dot(al, bh)


def _sigmoid(x):
    return 1.0 / (1.0 + jnp.exp(-x))


def _rms(x, g):
    return x * lax.rsqrt(jnp.mean(x * x, axis=-1, keepdims=True) + EPS) * g


def _rel_bucket(dist):
    n = jnp.maximum(dist, 0)
    max_exact = N_BUCKETS // 2
    large = max_exact + (jnp.log(jnp.maximum(n, 1).astype(F32) / max_exact)
                         / math.log(REL_MAX_DIST / max_exact) * (N_BUCKETS - max_exact)).astype(jnp.int32)
    return jnp.where(n < max_exact, n, jnp.minimum(large, N_BUCKETS - 1))


def _mod_kernel(c_ref, w_ref, b_ref, o_ref):
    c = c_ref[...]
    o_ref[...] = _dot3(c * _sigmoid(c), w_ref[...]) + b_ref[...]


def _modulation(c, w_ada, b_ada):
    n, d = c.shape
    cols = w_ada.shape[1]
    bn = math.gcd(cols, MOD_COLS)
    return pl.pallas_call(
        _mod_kernel,
        out_shape=jax.ShapeDtypeStruct((n, cols), F32),
        grid=(cols // bn,),
        in_specs=[pl.BlockSpec((n, d), lambda i: (0, 0)),
                  pl.BlockSpec((d, bn), lambda i: (0, i)),
                  pl.BlockSpec((1, bn), lambda i: (0, i))],
        out_specs=pl.BlockSpec((n, bn), lambda i: (0, i)),
        compiler_params=_cparams(1),
        name="modulation",
    )(c, w_ada, b_ada.reshape(1, cols))


def _inproj_kernel(x_ref, m0_ref, m1_ref, g1_ref, w_ref, cw_ref, gnc_ref, pa_ref, pb_ref,
                   convn_ref, q_ref, kvc_ref, kvs_ref, kvw_ref, gate_ref, vlast_ref, *rest, seq_mode, tm, cw_width):
    if seq_mode:
        kh_ref, vth_ref, carry_ref = rest
    x = x_ref[0]
    h = _rms(x, g1_ref[...]) * (1.0 + m1_ref[0]) + m0_ref[0]
    hb = h.astype(BF16)
    c3 = 3 * cw_width
    uc = _dot(hb, w_ref[:, 0:c3])
    b_g = uc[:, 0:cw_width]
    v = uc[:, cw_width:2 * cw_width] * uc[:, 2 * cw_width:c3]
    if seq_mode:
        @pl.when(pl.program_id(1) == 0)
        def _():
            carry_ref[0:2, :] = pa_ref[0]
        c0 = carry_ref[0:1, :]
        c1 = carry_ref[1:2, :]
        row = lax.broadcasted_iota(jnp.int32, (tm, 1), 0)
        vm1 = jnp.where(row == 0, c1, pltpu.roll(v, 1, 0))
        vm2 = jnp.where(row == 0, c0, jnp.where(row == 1, c1, pltpu.roll(v, 2, 0)))
        carry_ref[0:2, :] = v[tm - 2:tm, :]
        vlast_ref[0] = v[tm - 2:tm, :]
    else:
        vm2 = pa_ref[0]
        vm1 = pb_ref[0]
        vlast_ref[0] = v
    cw = cw_ref[...]
    y = cw[0:1, :] * vm2 + cw[1:2, :] * vm1 + cw[2:3, :] * v
    convn_ref[0] = _rms(b_g * y, gnc_ref[...]).astype(BF16)

    aw = N_HEADS * HEAD_DIM
    uq = _dot(hb, w_ref[:, c3:c3 + aw]) * (HEAD_DIM ** -0.5 * (LOG2E if seq_mode else 1.0))
    if seq_mode:
        uqt = uq.T
        for k in range(N_KV_HEADS):
            for jj in range(tm // Q_BLOCK):
                q_ref[0, k, jj] = jnp.concatenate(
                    [uqt[(k * GROUP + g) * HEAD_DIM:(k * GROUP + g + 1) * HEAD_DIM, jj * Q_BLOCK:(jj + 1) * Q_BLOCK]
                     for g in range(GROUP)], axis=1).astype(BF16)
    else:
        for hh in range(N_HEADS):
            q_ref[0, hh] = uq[:, hh * HEAD_DIM:(hh + 1) * HEAD_DIM].astype(BF16)
    kv0 = c3 + aw
    kvw3 = 3 * 2 * KV_WIDTH
    ukv = _dot(hb, w_ref[:, kv0:kv0 + kvw3])
    kvc_ref[0] = ukv[:, 0:2 * KV_WIDTH]
    kvs_ref[0] = ukv[:, 2 * KV_WIDTH:4 * KV_WIDTH]
    kvw_ref[0] = ukv[:, 4 * KV_WIDTH:6 * KV_WIDTH]
    if seq_mode:
        for br in range(2):
            c0 = (br + 1) * 2 * KV_WIDTH
            for k in range(N_KV_HEADS):
                kh_ref[0, br * N_KV_HEADS + k] = ukv[:, c0 + k * HEAD_DIM:c0 + (k + 1) * HEAD_DIM].astype(BF16)
            vt = ukv[:, c0 + KV_WIDTH:c0 + 2 * KV_WIDTH].T.astype(BF16)
            for k in range(N_KV_HEADS):
                vth_ref[0, br * N_KV_HEADS + k] = vt[k * HEAD_DIM:(k + 1) * HEAD_DIM, :]
    ug = _dot(hb, w_ref[:, kv0 + kvw3:kv0 + kvw3 + LANE])
    gate_ref[0] = _sigmoid(ug)


def _inproj(x, m0, m1, ln_g, w_pad, conv_w, gn_conv, pa, pb, *, seq_mode, tm):
    B, T, D = x.shape
    cw_width = conv_w.shape[1]
    nt = T // tm
    row_blk = lambda w: pl.BlockSpec((1, tm, w), lambda b, t: (b, t, 0))
    if seq_mode:
        mod_spec = pl.BlockSpec((1, 1, D), lambda b, t: (b, 0, 0))
        prev_spec = pl.BlockSpec((1, 2, cw_width), lambda b, t: (b, 0, 0))
        vlast_shape = jax.ShapeDtypeStruct((B, 2, cw_width), F32)
        vlast_spec = pl.BlockSpec((1, 2, cw_width), lambda b, t: (b, 0, 0))
    else:
        mod_spec = row_blk(D)
        prev_spec = row_blk(cw_width)
        vlast_shape = jax.ShapeDtypeStruct((B, T, cw_width), F32)
        vlast_spec = row_blk(cw_width)
    const = lambda shape: pl.BlockSpec(shape, lambda b, t: (0,) * len(shape))
    kern = functools.partial(_inproj_kernel, seq_mode=seq_mode, tm=tm, cw_width=cw_width)
    if seq_mode:
        assert tm % Q_BLOCK == 0
        q_shape = jax.ShapeDtypeStruct((B, N_KV_HEADS, T // Q_BLOCK, HEAD_DIM, ROWS_Q), BF16)
        q_spec = pl.BlockSpec((1, N_KV_HEADS, tm // Q_BLOCK, HEAD_DIM, ROWS_Q), lambda b, t: (b, 0, t, 0, 0))
    else:
        q_shape = jax.ShapeDtypeStruct((B, N_HEADS, T, HEAD_DIM), BF16)
        q_spec = pl.BlockSpec((1, N_HEADS, tm, HEAD_DIM), lambda b, t: (b, 0, t, 0))
    out_shape = [jax.ShapeDtypeStruct((B, T, cw_width), BF16),
                 q_shape,
                 jax.ShapeDtypeStruct((B, T, 2 * KV_WIDTH), F32),
                 jax.ShapeDtypeStruct((B, T, 2 * KV_WIDTH), F32),
                 jax.ShapeDtypeStruct((B, T, 2 * KV_WIDTH), F32),
                 jax.ShapeDtypeStruct((B, T, LANE), F32),
                 vlast_shape]
    out_specs = [row_blk(cw_width),
                 q_spec,
                 row_blk(2 * KV_WIDTH), row_blk(2 * KV_WIDTH), row_blk(2 * KV_WIDTH),
                 row_blk(LANE), vlast_spec]
    scratch = []
    if seq_mode:
        n_att = 2 * N_KV_HEADS
        out_shape += [jax.ShapeDtypeStruct((B, n_att, T, HEAD_DIM), BF16),
                      jax.ShapeDtypeStruct((B, n_att, HEAD_DIM, T), BF16)]
        out_specs += [pl.BlockSpec((1, n_att, tm, HEAD_DIM), lambda b, t: (b, 0, t, 0)),
                      pl.BlockSpec((1, n_att, HEAD_DIM, tm), lambda b, t: (b, 0, 0, t))]
        scratch = [pltpu.VMEM((8, cw_width), F32)]
    return pl.pallas_call(
        kern,
        out_shape=tuple(out_shape),
        grid=(B, nt),
        in_specs=[row_blk(D), mod_spec, mod_spec, const((1, D)), const(w_pad.shape), const(conv_w.shape),
                  const((1, cw_width)), prev_spec, prev_spec],
        out_specs=tuple(out_specs),
        scratch_shapes=scratch,
        compiler_params=_cparams(2),
        name="inproj_seq" if seq_mode else "inproj_rows",
    )(x, m0, m1, ln_g.reshape(1, D), w_pad, conv_w, gn_conv.reshape(1, cw_width), pa, pb)


def _cmp_ab_kernel(pt_ref, *refs, G):
    pages = refs[:2 * G]
    w_ref, out_ref, x_ref = refs[2 * G], refs[2 * G + 1], refs[2 * G + 2]
    half = KV_WIDTH
    for j in range(G):
        for r in range(CMP_STRIDE):
            for c in range(2):
                x_ref[c, j * 8:(j + 1) * 8, r * half:(r + 1) * half] = (
                    pages[2 * j + c][0, pl.ds(r, 8, stride=CMP_STRIDE), :])
    wcols = w_ref.shape[2]
    for c in range(2):
        out_ref[0, :, c * wcols:(c + 1) * wcols] = _dot(x_ref[c].astype(BF16), w_ref[c])


def _cmp_ab(pool, page_table, w_blk, G):
    P, page, width = pool.shape
    B, n_pages = page_table.shape
    cpp = page // CMP_STRIDE
    assert cpp == 8 and n_pages % G == 0
    kdim = CMP_STRIDE * width // 2

    def pg_spec(j, c):
        return pl.BlockSpec((1, page, width // 2), lambda b, g, pt: (pt[b * n_pages + g * G + j], 0, c))

    return pl.pallas_call(
        functools.partial(_cmp_ab_kernel, G=G),
        out_shape=jax.ShapeDtypeStruct((B, n_pages * cpp, 2 * w_blk.shape[2]), F32),
        grid_spec=pltpu.PrefetchScalarGridSpec(
            num_scalar_prefetch=1,
            grid=(B, n_pages // G),
            in_specs=[pg_spec(j, c) for j in range(G) for c in range(2)]
            + [pl.BlockSpec(w_blk.shape, lambda b, g, pt: (0, 0, 0))],
            out_specs=pl.BlockSpec((1, G * cpp, 2 * w_blk.shape[2]), lambda b, g, pt: (b, g, 0)),
            scratch_shapes=[pltpu.VMEM((2, G * cpp, kdim), F32)]),
        compiler_params=_cparams(2),
        name="cmp_partial",
    )(page_table.reshape(-1).astype(jnp.int32), *([pool] * (2 * G)), w_blk)


def _cmp_ab_t_kernel(pt_ref, *refs, G):
    pages = refs[:G]
    w_ref, out_ref, x_ref, s_ref = refs[G], refs[G + 1], refs[G + 2], refs[G + 3]
    half = KV_WIDTH
    for j in range(G):
        for c in range(2):
            for k in range(N_KV_HEADS):
                s_ref[c, j, :, k * HEAD_DIM:(k + 1) * HEAD_DIM] = pages[j][0, c, k].astype(BF16).T.astype(F32)
    for j in range(G):
        for r in range(CMP_STRIDE):
            for c in range(2):
                x_ref[c, j * 8:(j + 1) * 8, r * half:(r + 1) * half] = (
                    s_ref[c, j, pl.ds(r, 8, stride=CMP_STRIDE), :])
    wcols = w_ref.shape[2]
    for c in range(2):
        out_ref[0, :, c * wcols:(c + 1) * wcols] = _dot(x_ref[c].astype(BF16), w_ref[c])


def _cmp_ab_t(pool_t, page_table, w_blk, G):
    P, _, _, hd, page = pool_t.shape
    B, n_pages = page_table.shape
    cpp = page // CMP_STRIDE
    assert cpp == 8 and n_pages % G == 0
    kdim = CMP_STRIDE * KV_WIDTH

    def pg_spec(j):
        return pl.BlockSpec((1,) + pool_t.shape[1:], lambda b, g, pt: (pt[b * n_pages + g * G + j], 0, 0, 0, 0))

    return pl.pallas_call(
        functools.partial(_cmp_ab_t_kernel, G=G),
        out_shape=jax.ShapeDtypeStruct((B, n_pages * cpp, 2 * w_blk.shape[2]), F32),
        grid_spec=pltpu.PrefetchScalarGridSpec(
            num_scalar_prefetch=1,
            grid=(B, n_pages // G),
            in_specs=[pg_spec(j) for j in range(G)] + [pl.BlockSpec(w_blk.shape, lambda b, g, pt: (0, 0, 0))],
            out_specs=pl.BlockSpec((1, G * cpp, 2 * w_blk.shape[2]), lambda b, g, pt: (b, g, 0)),
            scratch_shapes=[pltpu.VMEM((2, G * cpp, kdim), F32), pltpu.VMEM((2, G, page, KV_WIDTH), F32)]),
        compiler_params=_cparams(2),
        name="cmp_partial_t",
    )(page_table.reshape(-1).astype(jnp.int32), *([pool_t] * G), w_blk)


def _gelu_tanh(x):
    return 0.5 * x * (1.0 + jnp.tanh(math.sqrt(2.0 / math.pi) * (x + 0.044715 * (x * x * x))))


def _cmp_fin_kernel(ab_ref, pe_ref, w1_ref, w2_ref, kc_ref, vct_ref, *, nch):
    for c in range(2):
        pe_t = _dot(pe_ref[c:c + 1, :].astype(BF16), w1_ref[c].astype(BF16))
        w2 = w2_ref[c].astype(BF16)
        for k in range(N_KV_HEADS):
            base = (c * N_KV_HEADS + k) * 2 * HEAD_DIM
            slab = ab_ref[0, :, base:base + 2 * HEAD_DIM]
            nxt = pltpu.roll(slab, nch - 1, 0)
            pre = slab[:, 0:HEAD_DIM] + nxt[:, HEAD_DIM:2 * HEAD_DIM] + pe_t
            blocks = _dot(_gelu_tanh(pre).astype(BF16), w2)
            if c == 0:
                kc_ref[0, k] = blocks.astype(BF16)
            else:
                vct_ref[0, k] = blocks.T.astype(BF16)


def _cmp_finish(ab, cmp_pe, cmp_w1, cmp_w2):
    B, nch, w = ab.shape
    pe = cmp_pe.reshape(2, CMP_LEN * HEAD_DIM)
    return pl.pallas_call(
        functools.partial(_cmp_fin_kernel, nch=nch),
        out_shape=(jax.ShapeDtypeStruct((B, N_KV_HEADS, nch, HEAD_DIM), BF16),
                   jax.ShapeDtypeStruct((B, N_KV_HEADS, HEAD_DIM, nch), BF16)),
        grid=(B,),
        in_specs=[pl.BlockSpec((1, nch, w), lambda b: (b, 0, 0)),
                  pl.BlockSpec(pe.shape, lambda b: (0, 0)),
                  pl.BlockSpec(cmp_w1.shape, lambda b: (0, 0, 0)),
                  pl.BlockSpec(cmp_w2.shape, lambda b: (0, 0, 0))],
        out_specs=(pl.BlockSpec((1, N_KV_HEADS, nch, HEAD_DIM), lambda b: (b, 0, 0, 0)),
                   pl.BlockSpec((1, N_KV_HEADS, HEAD_DIM, nch), lambda b: (b, 0, 0, 0))),
        compiler_params=_cparams(1),
        name="cmp_finish",
    )(ab, pe, cmp_w1, cmp_w2)


def _cmp_block_weight(cmp_w1):
    hid = cmp_w1.shape[2]
    w = cmp_w1.reshape(2, 2, CMP_STRIDE, HEAD_DIM, hid)
    eye = jnp.eye(N_KV_HEADS, dtype=cmp_w1.dtype)
    wb = jnp.einsum('cardh,kj->crkdjah', w, eye)
    return wb.reshape(2, CMP_STRIDE * N_KV_HEADS * HEAD_DIM, N_KV_HEADS * 2 * hid).astype(BF16)


def _score_matrix(nch, n_slc):
    i = np.arange(nch)[:, None]
    j = np.arange(n_slc)[None, :]
    m = 2.0 * ((i // RATIO == j) & (i % RATIO < RATIO - 1)) + 1.0 * (i == RATIO * j + RATIO - 1) \
        + 1.0 * (i == RATIO * j - 1)
    return jnp.asarray(m, dtype=BF16)


def _nsa_kernel(q_ref, kc_ref, vc_ref, ks_ref, vs_ref, kw_ref, vw_ref, gate_ref, sa_ref, sw_ref,
                cs_ref, mm_ref, e_ref, o_ref, m_ref, l_ref, acc_ref, sta_ref, stb_ref, stc_ref, std_ref,
                oc_ref, pen_ref, *, nch, n_slc, parts):
    j = pl.program_id(2)
    q0 = pl.multiple_of(j * Q_BLOCK, Q_BLOCK)
    cq = lax.shift_right_logical(j, TILES.bit_length() - 1)
    jm = jnp.bitwise_and(j, TILES - 1)
    qt = q_ref[0, 0, 0]
    lane_q = jnp.bitwise_and(lax.broadcasted_iota(jnp.int32, (1, ROWS_Q), 1), Q_BLOCK - 1)
    tq = q0 + lane_q
    has_prev = cq >= 1
    k_own = pl.multiple_of(cq * KEY_CHUNK, KEY_CHUNK)
    k_prev = pl.multiple_of(jnp.maximum(cq - 1, 0) * KEY_CHUNK, KEY_CHUNK)
    own0 = pl.multiple_of((2 * TILES - 1 - jm) * Q_BLOCK, Q_BLOCK)
    prev0 = pl.multiple_of(jnp.where(has_prev, TILES - 1 - jm, 2 * TILES) * Q_BLOCK, Q_BLOCK)
    prevw0 = pl.multiple_of(jnp.where(has_prev, TILES - 1 - jm, 2 * TILES - 1) * Q_BLOCK, Q_BLOCK)

    stc_ref[...] = _dot(kw_ref[0, 0, pl.ds(k_own, KEY_CHUNK), :], qt) + sa_ref[0, pl.ds(own0, KEY_CHUNK), :]
    std_ref[...] = _dot(kw_ref[0, 0, pl.ds(k_prev, KEY_CHUNK), :], qt) + sw_ref[0, pl.ds(prevw0, KEY_CHUNK), :]

    per_q = Q_BLOCK // CMP_STRIDE
    nq = pl.num_programs(2)
    c0 = pl.multiple_of((nq - 1 - j) * per_q, per_q)
    anyc = tq >= CMP_LEN - 1
    cur = lax.shift_right_logical(q0 + lax.broadcasted_iota(jnp.int32, (1, Q_BLOCK), 1),
                                  SLC_BLOCK.bit_length() - 1)

    def cmp_and_select(nc, ns):
        lc = _dot(kc_ref[0, 0, 0:nc, :], qt) + cs_ref[0, pl.ds(c0, nc), :]
        p = jnp.exp2(lc - jnp.max(lc, axis=0, keepdims=True))
        pc = p * jnp.where(anyc, 1.0 / jnp.sum(p, axis=0, keepdims=True), 0.0)
        oc_ref[...] = _dot(vc_ref[0, 0, :, 0:nc], pc.astype(BF16))
        imp = pc[:, 0:Q_BLOCK]
        for g in range(1, GROUP):
            imp = imp + pc[:, g * Q_BLOCK:(g + 1) * Q_BLOCK]
        mm = mm_ref[0:ns, 0:nc]
        i1, i2, i3 = _split3(imp)
        score = _dot(mm, i1) + _dot(mm, i2) + _dot(mm, i3)
        blk = lax.broadcasted_iota(jnp.int32, (ns, 1), 0)
        forced = (blk == 0) | (blk == cur) | (blk == cur - 1)
        sc = jnp.where(forced, FORCE_SCORE, jnp.where(blk <= cur, score, NEG_INF))
        big = jnp.int32(INT_BIG)
        for _ in range(min(N_SEL, ns)):
            mx = jnp.max(sc, axis=0, keepdims=True)
            first = jnp.min(jnp.where(sc == mx, blk, big), axis=0, keepdims=True)
            sc = jnp.where(blk == first, PICKED, sc)
        pen_ref[0:ns, :] = jnp.where(sc == PICKED, 0.0, NEG_INF).astype(BF16)
        if ns < n_slc:
            pen_ref[ns:n_slc, :] = jnp.full((n_slc - ns, Q_BLOCK), NEG_INF, BF16)

    for part in range(1, parts + 1):
        @pl.when(jnp.logical_and(j * parts >= (part - 1) * nq, j * parts < part * nq))
        def _(part=part):
            cmp_and_select(nch * part // parts, n_slc * part // parts)

    o_c = oc_ref[...]
    qsel = jnp.concatenate([jnp.concatenate([pen_ref[...]] * GROUP, axis=1), qt], axis=0)

    def scores_sel(k0):
        ke = jnp.concatenate([e_ref[pl.ds(k0, KEY_CHUNK), :], ks_ref[0, 0, pl.ds(k0, KEY_CHUNK), :]], axis=1)
        return _dot(ke, qsel)

    def flash(st, v, first):
        mx = jnp.max(st, axis=0, keepdims=True)
        if first:
            pe = jnp.exp2(st - mx)
            l_ref[...] = jnp.sum(pe, axis=0, keepdims=True)
            acc_ref[...] = _dot(v, pe.astype(BF16))
            m_ref[...] = mx
        else:
            m_old = m_ref[...]
            m_new = jnp.maximum(m_old, mx)
            a = jnp.exp2(m_old - m_new)
            pe = jnp.exp2(st - m_new)
            l_ref[...] = a * l_ref[...] + jnp.sum(pe, axis=0, keepdims=True)
            acc_ref[...] = a * acc_ref[...] + _dot(v, pe.astype(BF16))
            m_ref[...] = m_new

    sta_ref[...] = scores_sel(k_own) + sa_ref[0, pl.ds(own0, KEY_CHUNK), :]
    stb_ref[...] = scores_sel(k_prev) + sa_ref[0, pl.ds(prev0, KEY_CHUNK), :]
    flash(stc_ref[...], vw_ref[0, 0, :, pl.ds(k_own, KEY_CHUNK)], True)
    flash(std_ref[...], vw_ref[0, 0, :, pl.ds(k_prev, KEY_CHUNK)], False)
    o_w = acc_ref[...] * (1.0 / l_ref[...])

    flash(sta_ref[...], vs_ref[0, 0, :, pl.ds(k_own, KEY_CHUNK)], True)
    n_far = jnp.maximum(cq - 1, 0)
    last_chunk = ks_ref.shape[2] // KEY_CHUNK - 1

    def chunk_start(c):
        return pl.multiple_of(jnp.minimum(c, last_chunk) * KEY_CHUNK, KEY_CHUNK)

    sta_ref[...] = scores_sel(chunk_start(0))
    flash(stb_ref[...], vs_ref[0, 0, :, pl.ds(k_prev, KEY_CHUNK)], False)

    def far_steps(c0, n):
        bufs = (sta_ref, stb_ref)
        for t in range(n):
            cur_buf, nxt_buf = bufs[t % 2], bufs[(t + 1) % 2]
            nxt_buf[...] = scores_sel(chunk_start(c0 + t + 1))
            flash(cur_buf[...], vs_ref[0, 0, :, pl.ds(chunk_start(c0 + t), KEY_CHUNK)], False)

    def far_group(i, carry):
        far_steps(FAR_UNROLL * i, FAR_UNROLL)
        return carry

    n_groups = lax.shift_right_logical(n_far, FAR_UNROLL.bit_length() - 1)
    lax.fori_loop(0, n_groups, far_group, 0)
    rest0 = n_groups * FAR_UNROLL

    @pl.when(jnp.bitwise_and(n_far, 2) == 2)
    def _():
        far_steps(rest0, 2)

    @pl.when(jnp.bitwise_and(n_far, 1) == 1)
    def _():
        flash(sta_ref[...], vs_ref[0, 0, :, pl.ds(chunk_start(n_far - 1), KEY_CHUNK)], False)

    o_s = acc_ref[...] * (1.0 / l_ref[...])

    g3 = gate_ref[0, 0, 0]
    o = g3[0:1, :] * o_c + g3[1:2, :] * o_s + g3[2:3, :] * o_w
    stacked = jnp.concatenate([o[:, g * Q_BLOCK:(g + 1) * Q_BLOCK] for g in range(GROUP)], axis=0)
    o_ref[0] = stacked.T


def _bias_tables(table, nch, nq):
    tbl = table.reshape(N_BUCKETS, N_KV_HEADS, GROUP)
    far = tbl[N_BUCKETS - 1]
    i = jnp.arange(Q_BLOCK)

    def cols(rel):
        hot = (_rel_bucket(rel)[..., None] == jnp.arange(N_BUCKETS)).astype(F32)
        b = (jnp.einsum('kqn,nhg->kqhg', hot, tbl, precision=lax.Precision.HIGHEST) - far) * LOG2E
        b = jnp.where((rel >= 0)[:, :, None, None], b, 0.0)
        return b.transpose(2, 0, 3, 1).reshape(N_KV_HEADS, rel.shape[0], ROWS_Q)

    p0 = cols(i[None, :] - i[:, None])
    p1 = cols(Q_BLOCK + i[None, :] - i[:, None])
    per_q = Q_BLOCK // CMP_STRIDE
    rel_c = i[None, :] - CMP_STRIDE * (jnp.arange(nch + per_q * (nq - 1))[:, None] - per_q * (nq - 1)) - (CMP_LEN - 1)
    strip_cmp = jnp.where(jnp.tile(rel_c >= 0, (1, GROUP))[None], cols(rel_c), NEG_INF)
    kk = jnp.arange(Q_BLOCK)[:, None]
    qq = jnp.tile(jnp.arange(Q_BLOCK), GROUP)[None, :]
    bc = lambda a: jnp.broadcast_to(a[None], (N_KV_HEADS, Q_BLOCK, ROWS_Q))
    neg = bc(jnp.full((Q_BLOCK, ROWS_Q), NEG_INF, F32))
    zero = bc(jnp.zeros((Q_BLOCK, ROWS_Q), F32))
    p0_causal = jnp.where((kk <= qq)[None], p0, NEG_INF)
    in_window = bc(jnp.where(kk > qq, 0.0, NEG_INF).astype(F32))
    strip_own = jnp.concatenate([zero] * (2 * TILES - 2) + [p1, p0_causal] + [neg] * TILES, axis=1)
    strip_win = jnp.concatenate([neg] * (TILES - 1) + [in_window] + [zero] * (TILES - 2) + [p1] + [neg] * TILES,
                                axis=1)
    return strip_own, strip_win, strip_cmp


def _nsa_prompt(q_t, k_hm, vt_hm, kc, vct, gates_t, table):
    B, _, T, hd = k_hm.shape
    H = N_HEADS
    nch = kc.shape[2]
    n_slc = T // SLC_BLOCK
    nq = T // Q_BLOCK
    assert T % KEY_CHUNK == 0 and nch == RATIO * n_slc and WINDOW <= KEY_CHUNK
    strip_own, strip_win, strip_cmp = _bias_tables(table, nch, nq)
    e_tab = (jnp.arange(T)[:, None] // SLC_BLOCK == jnp.arange(n_slc)[None, :]).astype(BF16)
    mm = _score_matrix(nch, n_slc).T
    nk = N_KV_HEADS
    k_spec = lambda slab0: pl.BlockSpec((1, 1, T, hd), lambda b, k, j: (b, slab0 + k, 0, 0))
    vt_spec = lambda slab0: pl.BlockSpec((1, 1, hd, T), lambda b, k, j: (b, slab0 + k, 0, 0))
    per_head = lambda a: pl.BlockSpec((1,) + a.shape[1:], lambda b, k, j: (k, 0, 0))
    score_buf = pltpu.VMEM((KEY_CHUNK, ROWS_Q), F32)
    parts = max(p for p in (4, 2, 1) if nq % p == 0 and (nch // p) % LANE == 0
                and (n_slc // p) % BF16_ROWS == 0)
    return pl.pallas_call(
        functools.partial(_nsa_kernel, nch=nch, n_slc=n_slc, parts=parts),
        out_shape=jax.ShapeDtypeStruct((B, T, H * hd), F32),
        grid=(B, nk, nq),
        in_specs=[pl.BlockSpec((1, 1, 1, hd, ROWS_Q), lambda b, k, j: (b, k, j, 0, 0)),
                  pl.BlockSpec((1, 1, nch, hd), lambda b, k, j: (b, k, 0, 0)),
                  pl.BlockSpec((1, 1, hd, nch), lambda b, k, j: (b, k, 0, 0)),
                  k_spec(0), vt_spec(0), k_spec(nk), vt_spec(nk),
                  pl.BlockSpec((1, 1, 1, 3, ROWS_Q), lambda b, k, j: (b, k, j, 0, 0)),
                  per_head(strip_own), per_head(strip_win), per_head(strip_cmp),
                  pl.BlockSpec(mm.shape, lambda b, k, j: (0, 0)),
                  pl.BlockSpec(e_tab.shape, lambda b, k, j: (0, 0))],
        out_specs=pl.BlockSpec((1, Q_BLOCK, GROUP * hd), lambda b, k, j: (b, j, k)),
        scratch_shapes=[pltpu.VMEM((1, ROWS_Q), F32), pltpu.VMEM((1, ROWS_Q), F32),
                        pltpu.VMEM((hd, ROWS_Q), F32), score_buf, score_buf, score_buf, score_buf,
                        pltpu.VMEM((hd, ROWS_Q), F32), pltpu.VMEM((n_slc, Q_BLOCK), BF16)],
        compiler_params=_cparams(3),
        name="nsa_prompt",
    )(q_t, kc, vct, k_hm, vt_hm, k_hm, vt_hm, gates_t, strip_own, strip_win, strip_cmp, mm, e_tab)


def _smp_cmp_kernel(q_ref, kc_ref, vct_ref, bias_ref, mm_ref, oc_ref, idx_ref, *, n_cmp, n_slc, cur):
    nch = kc_ref.shape[2]
    ncol = mm_ref.shape[1]
    coli = lax.broadcasted_iota(jnp.int32, (1, nch), 1)
    maskc = coli < n_cmp
    blk = lax.broadcasted_iota(jnp.int32, (1, ncol), 1)
    k_sel = min(N_SEL, n_slc)
    lane = lax.broadcasted_iota(jnp.int32, (1, LANE), 1)
    nb = q_ref.shape[0]
    mm = mm_ref[...]
    forced = (blk == 0) | (blk == cur) | (blk == cur - 1)
    scores = []
    for b in range(nb):
        for kh in range(N_KV_HEADS):
            qm = q_ref[b, kh * GROUP:(kh + 1) * GROUP].astype(BF16)
            lc = _dot_nt(qm, kc_ref[b, kh]) + bias_ref[kh]
            lcm = jnp.where(maskc, lc, NEG_INF)
            p = jnp.where(maskc, jnp.exp(lcm - jnp.max(lcm, axis=-1, keepdims=True)), 0.0)
            pc = p * (1.0 / jnp.sum(p, axis=-1, keepdims=True))
            oc_ref[b, kh * GROUP:(kh + 1) * GROUP] = _dot_nt(pc.astype(BF16), vct_ref[b, kh])
            imp = pc[0:1]
            for g in range(1, GROUP):
                imp = imp + pc[g:g + 1]
            i1, i2, i3 = _split3(imp)
            score = _dot(i1, mm) + _dot(i2, mm) + _dot(i3, mm)
            sc = jnp.where(forced, FORCE_SCORE, jnp.where(blk <= cur, score, NEG_INF))
            scores.append(jnp.where(blk < n_slc, sc, NOT_A_SLOT))
    sc = jnp.concatenate(scores, axis=0)
    big = jnp.int32(INT_BIG)
    out = jnp.zeros((nb * N_KV_HEADS, LANE), jnp.int32)
    for it in range(k_sel):
        m = jnp.max(sc, axis=-1, keepdims=True)
        first = jnp.min(jnp.where(sc == m, blk, big), axis=-1, keepdims=True)
        out = jnp.where(lane == it, first, out)
        sc = jnp.where(blk == first, PICKED, sc)
    idx_ref[0] = out


def _smp_cmp(q_hm, kc, vct, bias_c, mm, n_cmp, n_slc, cur):
    B = q_hm.shape[0]
    nch = kc.shape[2]
    nb = math.gcd(B, 8)
    o_c, idx = pl.pallas_call(
        functools.partial(_smp_cmp_kernel, n_cmp=n_cmp, n_slc=n_slc, cur=cur),
        out_shape=(jax.ShapeDtypeStruct((B, N_HEADS, HEAD_DIM), F32),
                   jax.ShapeDtypeStruct((B // nb, nb * N_KV_HEADS, LANE), jnp.int32)),
        grid=(B // nb,),
        in_specs=[pl.BlockSpec((nb, N_HEADS, HEAD_DIM), lambda b: (b, 0, 0)),
                  pl.BlockSpec((nb, N_KV_HEADS, nch, HEAD_DIM), lambda b: (b, 0, 0, 0)),
                  pl.BlockSpec((nb, N_KV_HEADS, HEAD_DIM, nch), lambda b: (b, 0, 0, 0)),
                  pl.BlockSpec(bias_c.shape, lambda b: (0, 0, 0)),
                  pl.BlockSpec(mm.shape, lambda b: (0, 0))],
        out_specs=(pl.BlockSpec((nb, N_HEADS, HEAD_DIM), lambda b: (b, 0, 0)),
                   pl.BlockSpec((1, nb * N_KV_HEADS, LANE), lambda b: (b, 0, 0))),
        compiler_params=_cparams(1),
        name="sample_cmp",
    )(q_hm, kc, vct, bias_c, mm)
    return o_c, idx.reshape(B, N_KV_HEADS, LANE)


def _smp_att_kernel(phys_ref, q_ref, pool_ref, bsel_ref, msel_ref, new_ref, win_ref, bwin_ref, bnew_ref,
                    gate_ref, oc_ref, o_ref, buf_ref, sem_ref, *, k_sel):
    nb = q_ref.shape[0]
    n_blk = N_KV_HEADS * k_sel
    n_dma = nb * n_blk
    base = pl.program_id(0) * n_dma

    def blk_copy(i):
        return pltpu.make_async_copy(pool_ref.at[phys_ref[base + i]], buf_ref.at[i], sem_ref.at[i])

    for i in range(n_dma):
        blk_copy(i).start()
    for i in range(n_dma):
        blk_copy(i).wait()

    nw = win_ref.shape[4]
    for bl, kh in [(bl, kh) for bl in range(nb) for kh in range(N_KV_HEADS)]:
        buf0 = bl * n_blk + kh * k_sel
        rows = slice(kh * GROUP, (kh + 1) * GROUP)
        qm = q_ref[bl, rows].astype(BF16)
        kcol = slice(kh * HEAD_DIM, (kh + 1) * HEAD_DIM)
        vcol = slice(KV_WIDTH + kh * HEAD_DIM, KV_WIDTH + (kh + 1) * HEAD_DIM)
        bnew = bnew_ref[kh]

        def branch(kt, vt, bias, mask, knew, vnew, new_ok):
            lg = _dot(qm, kt) + bias
            lg = jnp.where(mask, lg, NEG_INF)
            ln = jnp.sum(qm.astype(F32) * knew.astype(F32), axis=-1, keepdims=True) + bnew
            if new_ok is not None:
                ln = jnp.where(new_ok, ln, NEG_INF)
            mx = jnp.maximum(jnp.max(lg, axis=-1, keepdims=True), ln)
            pe = jnp.where(mask, jnp.exp(lg - mx), 0.0)
            pn = jnp.exp(ln - mx)
            if new_ok is not None:
                pn = jnp.where(new_ok, pn, 0.0)
            den = jnp.sum(pe, axis=-1, keepdims=True) + pn
            any_ok = den > 0.0
            inv = 1.0 / jnp.where(any_ok, den, 1.0)
            o = _dot_nt(pe.astype(BF16), vt) + pn.astype(BF16).astype(F32) * vnew.astype(F32)
            return jnp.where(any_ok, o * inv, 0.0)

        page = buf_ref.shape[4]
        nkeys = k_sel * page
        kt = jnp.concatenate([buf_ref[buf0 + i, 0, kh] for i in range(k_sel)], axis=1).astype(BF16)
        vt = jnp.concatenate([buf_ref[buf0 + i, 1, kh] for i in range(k_sel)], axis=1).astype(BF16)
        knew = new_ref[bl, 0:1, kcol].astype(BF16)
        vnew = new_ref[bl, 0:1, vcol].astype(BF16)
        msel = msel_ref[bl, kh]
        o_s = branch(kt, vt, bsel_ref[bl, kh], msel[:, 0:nkeys] > 0.5, knew, vnew, msel[:, nkeys:nkeys + 1] > 0.5)

        kw = win_ref[bl, 0, kh].astype(BF16)
        vw = win_ref[bl, 1, kh].astype(BF16)
        knw = new_ref[bl, 1:2, kcol].astype(BF16)
        vnw = new_ref[bl, 1:2, vcol].astype(BF16)
        dist = nw - lax.broadcasted_iota(jnp.int32, (1, nw), 1)
        o_w = branch(kw, vw, bwin_ref[kh], dist < WINDOW, knw, vnw, None)

        g3 = gate_ref[bl, rows]
        o_ref[bl, rows] = g3[:, 0:1] * oc_ref[bl, rows] + g3[:, 1:2] * o_s + g3[:, 2:3] * o_w


def _smp_att(phys, q_hm, pool_blk, bias_sel, mask_sel, new_rows, win_buf, bias_win, bias_new, gates, o_c, k_sel):
    B = q_hm.shape[0]
    nb = math.gcd(B, SMP_BATCH)
    n_dma = nb * N_KV_HEADS * k_sel
    full = lambda a: pl.BlockSpec(a.shape, lambda b, ph: (0,) * a.ndim)
    per_b = lambda a: pl.BlockSpec((nb,) + a.shape[1:], lambda b, ph: (b,) + (0,) * (a.ndim - 1))
    return pl.pallas_call(
        functools.partial(_smp_att_kernel, k_sel=k_sel),
        out_shape=jax.ShapeDtypeStruct((B, N_HEADS, HEAD_DIM), F32),
        grid_spec=pltpu.PrefetchScalarGridSpec(
            num_scalar_prefetch=1,
            grid=(B // nb,),
            in_specs=[per_b(q_hm), pl.BlockSpec(memory_space=pl.ANY), per_b(bias_sel), per_b(mask_sel),
                      per_b(new_rows), per_b(win_buf), full(bias_win), full(bias_new), per_b(gates), per_b(o_c)],
            out_specs=pl.BlockSpec((nb, N_HEADS, HEAD_DIM), lambda b, ph: (b, 0, 0)),
            scratch_shapes=[pltpu.VMEM((n_dma,) + pool_blk.shape[1:], F32),
                            pltpu.SemaphoreType.DMA((n_dma,))]),
        compiler_params=_cparams(1),
        name="sample_att",
    )(phys, q_hm, pool_blk, bias_sel, mask_sel, new_rows, win_buf, bias_win, bias_new, gates, o_c)


def _merge_kernel(x_ref, cn_ref, att_ref, gna_ref, wo_ref, m2_ref, m3_ref, m4_ref, g2_ref, rw_ref, rb_ref,
                  x1_ref, h2_ref, ti_ref, tg_ref, *, n_exp, cw_width):
    att_n = _rms(att_ref[0], gna_ref[...]).astype(BF16)
    mix = _dot(cn_ref[0], wo_ref[0:cw_width, :]) + _dot(att_n, wo_ref[cw_width:, :])
    x1 = x_ref[0] + m2_ref[0] * mix
    x1_ref[0] = x1
    h2 = _rms(x1, g2_ref[...]) * (1.0 + m4_ref[0]) + m3_ref[0]
    h2_ref[0] = h2
    logits = _dot3(h2, rw_ref[...]) + rb_ref[...]
    lane = lax.broadcasted_iota(jnp.int32, (1, LANE), 1)
    sc = jnp.where(lane < n_exp, logits, NOT_A_SLOT)
    big = jnp.int32(INT_BIG)
    ti = jnp.zeros(sc.shape, jnp.int32)
    tv = jnp.zeros(sc.shape, F32)
    v0 = None
    den = None
    for k in range(TOP_K):
        m = jnp.max(sc, axis=-1, keepdims=True)
        first = jnp.min(jnp.where(sc == m, lane, big), axis=-1, keepdims=True)
        if k == 0:
            v0 = m
        e = jnp.exp(m - v0)
        den = e if den is None else den + e
        ti = jnp.where(lane == k, first, ti)
        tv = jnp.where(lane == k, e, tv)
        sc = jnp.where(lane == first, PICKED, sc)
    ti_ref[0] = ti
    tg_ref[0] = tv * (1.0 / den)


def _merge(x, convn, att, gn_att, wo_b, m2, m3, m4, ln2_g, rw_pad, rb_pad, *, seq_mode, tm, n_exp):
    B, T, D = x.shape
    cw_width = convn.shape[2]
    aw = att.shape[2]
    row_blk = lambda w: pl.BlockSpec((1, tm, w), lambda b, t: (b, t, 0))
    mod_spec = pl.BlockSpec((1, 1, D), lambda b, t: (b, 0, 0)) if seq_mode else row_blk(D)
    const = lambda shape: pl.BlockSpec(shape, lambda b, t: (0,) * len(shape))
    return pl.pallas_call(
        functools.partial(_merge_kernel, n_exp=n_exp, cw_width=cw_width),
        out_shape=(jax.ShapeDtypeStruct((B, T, D), F32), jax.ShapeDtypeStruct((B, T, D), F32),
                   jax.ShapeDtypeStruct((B, T, LANE), jnp.int32), jax.ShapeDtypeStruct((B, T, LANE), F32)),
        grid=(B, T // tm),
        in_specs=[row_blk(D), row_blk(cw_width), row_blk(aw), const((1, aw)), const(wo_b.shape),
                  mod_spec, mod_spec, mod_spec, const((1, D)), const(rw_pad.shape), const((1, LANE))],
        out_specs=(row_blk(D), row_blk(D), row_blk(LANE), row_blk(LANE)),
        compiler_params=_cparams(2),
        name="merge_seq" if seq_mode else "merge_rows",
    )(x, convn, att, gn_att.reshape(1, aw), wo_b, m2, m3, m4, ln2_g.reshape(1, D), rw_pad, rb_pad)


def _moe_kernel(be_ref, nu_ref, xs_ref, wgu_ref, bgu_ref, wd_ref, bd_ref, y_ref, wgu_b, wd_b, *, d_ff):
    i = pl.program_id(0)
    prev = be_ref[jnp.maximum(i - 1, 0)]
    changed = jnp.logical_or(i == 0, be_ref[i] != prev)

    @pl.when(jnp.logical_and(changed, i < nu_ref[0]))
    def _():
        wgu_b[...] = wgu_ref[0].astype(BF16)
        wd_b[...] = wd_ref[0].astype(BF16)

    @pl.when(i < nu_ref[0])
    def _():
        gu = _dot(xs_ref[...].astype(BF16), wgu_b[...]) + bgu_ref[0]
        gate_h = jnp.minimum(gu[:, 0:d_ff], SWIGLU_LIMIT)
        up_h = jnp.clip(gu[:, d_ff:], -SWIGLU_LIMIT, SWIGLU_LIMIT)
        act = (up_h + 1.0) * gate_h * _sigmoid(SWIGLU_ALPHA * gate_h)
        y_ref[...] = _dot(act.astype(BF16), wd_b[...]) + bd_ref[0]

    @pl.when(i >= nu_ref[0])
    def _():
        y_ref[...] = jnp.zeros(y_ref.shape, F32)


def _moe_experts(block_e, n_used, xs, w_gu, b_gu, w_down, b_down):
    n_rows, D = xs.shape
    E, _, two_ff = w_gu.shape
    d_ff = two_ff // 2
    n_blocks = n_rows // MOE_ROWS
    return pl.pallas_call(
        functools.partial(_moe_kernel, d_ff=d_ff),
        out_shape=jax.ShapeDtypeStruct((n_rows, D), F32),
        grid_spec=pltpu.PrefetchScalarGridSpec(
            num_scalar_prefetch=2,
            grid=(n_blocks,),
            in_specs=[pl.BlockSpec((MOE_ROWS, D), lambda i, be, nu: (i, 0)),
                      pl.BlockSpec((1, D, two_ff), lambda i, be, nu: (be[i], 0, 0)),
                      pl.BlockSpec((1, 1, two_ff), lambda i, be, nu: (be[i], 0, 0)),
                      pl.BlockSpec((1, d_ff, D), lambda i, be, nu: (be[i], 0, 0)),
                      pl.BlockSpec((1, 1, D), lambda i, be, nu: (be[i], 0, 0))],
            out_specs=pl.BlockSpec((MOE_ROWS, D), lambda i, be, nu: (i, 0)),
            scratch_shapes=[pltpu.VMEM((D, two_ff), BF16), pltpu.VMEM((d_ff, D), BF16)]),
        compiler_params=_cparams(1),
        name="moe_experts",
    )(block_e, n_used, xs, w_gu, b_gu.reshape(E, 1, two_ff), w_down, b_down.reshape(E, 1, D))


def _route(top_idx, n_tok, n_exp):
    n_assign = n_tok * TOP_K
    flat_e = top_idx.reshape(-1)
    experts = jnp.arange(n_exp, dtype=jnp.int32)
    onehot = (flat_e[:, None] == experts[None, :]).astype(jnp.int32)
    csum = jnp.cumsum(onehot, axis=0)
    counts = csum[-1]
    padded = (counts + MOE_ROWS - 1) // MOE_ROWS * MOE_ROWS
    pad_end = jnp.cumsum(padded)
    pad_start = pad_end - padded
    dest = jnp.sum(onehot * (csum - 1 + pad_start[None, :]), axis=1).astype(jnp.int32)
    n_blocks = -(-(n_assign + n_exp * (MOE_ROWS - 1)) // MOE_ROWS)
    n_rows = n_blocks * MOE_ROWS
    blk_row0 = jnp.arange(n_blocks, dtype=jnp.int32) * MOE_ROWS
    block_e = jnp.minimum(jnp.sum((pad_end[None, :] <= blk_row0[:, None]).astype(jnp.int32), axis=1), n_exp - 1)
    n_fill = n_rows - n_assign
    fill_e = jnp.arange(n_fill, dtype=jnp.int32) // (MOE_ROWS - 1)
    fill_i = jnp.arange(n_fill, dtype=jnp.int32) % (MOE_ROWS - 1)
    fill_on = (fill_e < n_exp) & (fill_i < jnp.sum(
        (fill_e[:, None] == experts[None, :]) * (padded - counts)[None, :], axis=1))
    keys = jnp.concatenate([2 * flat_e, jnp.where(fill_on, 2 * fill_e + 1, 2 * n_exp + 1)])
    toks = jnp.concatenate([jnp.arange(n_assign, dtype=jnp.int32) // TOP_K, jnp.full((n_fill,), n_tok, jnp.int32)])
    _, row_tok = lax.sort((keys, toks), num_keys=1, is_stable=True)
    n_used = (pad_end[-1] // MOE_ROWS).astype(jnp.int32).reshape(1)
    return row_tok, dest.reshape(n_tok, TOP_K), block_e.astype(jnp.int32), n_used


def _final_kernel(x1_ref, yg_ref, tg_ref, m5_ref, fg_ref, o_ref):
    tg = tg_ref[0]
    ff = tg[:, 0:1] * yg_ref[0, 0]
    for k in range(1, TOP_K):
        ff = ff + tg[:, k:k + 1] * yg_ref[0, k]
    o_ref[0] = _rms(x1_ref[0] + m5_ref[0] * ff, fg_ref[...])


def _final(x1, yg, tg, m5, final_g, *, seq_mode, tm):
    B, T, D = x1.shape
    row_blk = lambda w: pl.BlockSpec((1, tm, w), lambda b, t: (b, t, 0))
    mod_spec = pl.BlockSpec((1, 1, D), lambda b, t: (b, 0, 0)) if seq_mode else row_blk(D)
    return pl.pallas_call(
        _final_kernel,
        out_shape=jax.ShapeDtypeStruct((B, T, D), F32),
        grid=(B, T // tm),
        in_specs=[row_blk(D), pl.BlockSpec((1, TOP_K, tm, D), lambda b, t: (b, 0, t, 0)), row_blk(LANE),
                  mod_spec, pl.BlockSpec((1, D), lambda b, t: (0, 0))],
        out_specs=row_blk(D),
        compiler_params=_cparams(2),
        name="final_seq" if seq_mode else "final_rows",
    )(x1, yg, tg, m5, final_g.reshape(1, D))


def _row_tile(t):
    for tm in (512, 256, 128, 64, 32, 16, 8):
        if t % tm == 0:
            return tm
    raise ValueError(f"unsupported row count {t}")


def _rows(a, idx):
    return a.at[idx].get(mode="promise_in_bounds")


def kernel(x_prompt, x_sample, cache_kv_cmp, cache_kv_slc, state_kv_win, state_conv, page_table, c_prompt, c_sample,
           rel_bias_table, ln1_g, ln2_g, w_ada, b_ada, w_in, conv_w, cmp_pe, cmp_w1, cmp_w2, gn_conv, gn_att, w_o,
           router_w, router_b, w_gu, b_gu, w_down, b_down, final_g):
    B, T, D = x_prompt.shape
    BS, TS, _ = x_sample.shape
    depth = w_in.shape[0]
    assert depth == 1 and TS == 1
    n_pool, page = cache_kv_cmp.shape[1], cache_kv_cmp.shape[2]
    n_pages = page_table.shape[1]
    past_len = n_pages * page
    n_exp = router_w.shape[2]
    cw_width = conv_w.shape[2]
    kvw2 = 2 * KV_WIDTH
    assert past_len % SLC_BLOCK == 0 and past_len % CMP_STRIDE == 0 and T % page == 0

    in_cols = w_in.shape[2]
    gate0 = 3 * cw_width + N_HEADS * HEAD_DIM + 3 * kvw2
    w_pad = jnp.pad(w_in[0], ((0, 0), (0, gate0 + LANE - in_cols))).astype(BF16)
    wo_b = w_o[0].astype(BF16)
    rw_pad = jnp.pad(router_w[0], ((0, 0), (0, LANE - n_exp)))
    rb_pad = jnp.pad(router_b[0], (0, LANE - n_exp)).reshape(1, LANE)
    w_blk = _cmp_block_weight(cmp_w1[0])

    n_c = B + BS
    n_cp = -(-n_c // 8) * 8
    c_all = jnp.pad(jnp.concatenate([c_prompt, c_sample], axis=0), ((0, n_cp - n_c), (0, 0)))
    mod = _modulation(c_all, w_ada[0], b_ada[0]).reshape(n_cp, 6, D)
    mp = [mod[:B, i].reshape(B, 1, D) for i in range(6)]
    ms = [mod[B:n_c, i].reshape(1, BS, D) for i in range(6)]

    tm = _row_tile(T)
    nq = T // Q_BLOCK
    zeros_prev = jnp.zeros((B, CONV_K - 1, cw_width), F32)
    convn_p, q_p, kvc_p, kvs_p, kvw_p, gate_p, vlast_p, kh_p, vth_p = _inproj(
        x_prompt, mp[0], mp[1], ln1_g[0], w_pad, conv_w[0], gn_conv[0], zeros_prev, zeros_prev, seq_mode=True, tm=tm)
    pt_p = jnp.arange(B * (T // page), dtype=jnp.int32).reshape(B, T // page)
    g_p = math.gcd(T // page, CMP_PAGES)
    ab_p = _cmp_ab(kvc_p.reshape(B * (T // page), page, kvw2), pt_p, w_blk, g_p)
    kc_p, vct_p = _cmp_finish(ab_p, cmp_pe[0], cmp_w1[0], cmp_w2[0])
    gates_p = gate_p[:, :, :3 * N_HEADS].reshape(B, nq, Q_BLOCK, N_KV_HEADS, GROUP, 3)
    gates_p = gates_p.transpose(0, 3, 1, 5, 4, 2).reshape(B, N_KV_HEADS, nq, 3, ROWS_Q)
    att_p = _nsa_prompt(q_p, kh_p, vth_p, kc_p, vct_p, gates_p, rel_bias_table)
    x1_p, h2_p, ti_p, tg_p = _merge(x_prompt, convn_p, att_p, gn_att[0], wo_b, mp[2], mp[3], mp[4], ln2_g[0],
                                    rw_pad, rb_pad, seq_mode=True, tm=tm, n_exp=n_exp)

    xs_rows = x_sample.reshape(1, BS, D)
    prev2 = state_conv[0][:, 0, :].reshape(1, BS, cw_width)
    prev1 = state_conv[0][:, 1, :].reshape(1, BS, cw_width)
    tms = _row_tile(BS)
    convn_s, q_s, kvc_s, kvs_s, kvw_s, gate_s, v_s = _inproj(
        xs_rows, ms[0], ms[1], ln1_g[0], w_pad, conv_w[0], gn_conv[0], prev2, prev1, seq_mode=False, tm=tms)
    n_cmp = (past_len + TS - CMP_LEN) // CMP_STRIDE + 1
    nch_s = past_len // CMP_STRIDE
    assert n_cmp + 1 == nch_s
    ab_s = _cmp_ab_t(cache_kv_cmp[0].transpose(0, 2, 3, 4, 1), page_table, w_blk,
                     math.gcd(n_pages, CMP_PAGES))
    kc_s, vct_s = _cmp_finish(ab_s, cmp_pe[0], cmp_w1[0], cmp_w2[0])
    t_q = past_len
    n_slc = -(-(past_len + TS) // SLC_BLOCK)
    cur = t_q // SLC_BLOCK
    k_sel = min(N_SEL, n_slc)
    tbl = rel_bias_table.reshape(N_BUCKETS, N_KV_HEADS, GROUP)
    pos_c = jnp.arange(nch_s) * CMP_STRIDE + CMP_LEN - 1
    bias_c = tbl[_rel_bucket(t_q - pos_c)].transpose(1, 2, 0)
    ncol = -(-n_slc // LANE) * LANE
    mm_s = _score_matrix(nch_s, ncol)
    q_s_hm = q_s.reshape(N_HEADS, BS, HEAD_DIM).transpose(1, 0, 2).astype(F32)
    o_c_s, idx_pad = _smp_cmp(q_s_hm, kc_s, vct_s, bias_c, mm_s, n_cmp, n_slc, cur)
    idx = idx_pad[:, :, :k_sel]
    blk_per_page = page // SLC_BLOCK
    pg = jnp.minimum(idx // blk_per_page, n_pages - 1)
    phys = jnp.take_along_axis(page_table, pg.reshape(BS, -1), axis=1).astype(jnp.int32)
    pos_s = pg[..., None] * page + jnp.arange(page)
    in_blk = (pos_s // SLC_BLOCK == idx[..., None]) & (pos_s < past_len)
    pos_s = pos_s.reshape(BS, N_KV_HEADS, k_sel * page)
    bucket_hot = (_rel_bucket(t_q - pos_s)[..., None] == jnp.arange(N_BUCKETS)).astype(F32)
    bias_sel = jnp.einsum('bksn,nkg->bkgs', bucket_hot, tbl, precision=lax.Precision.HIGHEST)
    new_sel = jnp.any(idx == cur, axis=-1, keepdims=True)
    mask_sel = jnp.concatenate([in_blk.reshape(BS, N_KV_HEADS, k_sel * page), new_sel], axis=-1)
    mask_sel = mask_sel.astype(F32)[:, :, None, :]
    nw = state_kv_win.shape[2]
    bias_win = tbl[_rel_bucket(nw - jnp.arange(nw))].transpose(1, 2, 0)
    bias_new = tbl[0].reshape(N_KV_HEADS, GROUP, 1)
    new_rows = jnp.stack([kvs_s[0], kvw_s[0]], axis=1)
    gates_s = gate_s[0, :, :3 * N_HEADS].reshape(BS, N_HEADS, 3)
    pool_t = cache_kv_slc[0].transpose(0, 2, 3, 4, 1)
    win_t = state_kv_win[0].transpose(0, 2, 3, 4, 1)
    att_s = _smp_att(phys.reshape(-1), q_s_hm, pool_t, bias_sel, mask_sel, new_rows,
                     win_t, bias_win, bias_new, gates_s, o_c_s, k_sel)
    att_s = att_s.reshape(1, BS, N_HEADS * HEAD_DIM)
    x1_s, h2_s, ti_s, tg_s = _merge(xs_rows, convn_s, att_s, gn_att[0], wo_b, ms[2], ms[3], ms[4], ln2_g[0],
                                    rw_pad, rb_pad, seq_mode=False, tm=tms, n_exp=n_exp)

    n_tok = B * T + BS
    h2_all = jnp.concatenate([h2_p.reshape(B * T, D), h2_s.reshape(BS, D), jnp.zeros((8, D), F32)], axis=0)
    top_idx = jnp.concatenate([ti_p.reshape(B * T, LANE), ti_s.reshape(BS, LANE)], axis=0)[:, :TOP_K]
    row_tok, dest, block_e, n_used = _route(top_idx, n_tok, n_exp)
    xs = _rows(h2_all, row_tok)
    yb = _moe_experts(block_e, n_used, xs, w_gu[0], b_gu[0], w_down[0], b_down[0])
    dest_p = dest[:B * T].reshape(B, T, TOP_K).transpose(0, 2, 1)
    yg_p = _rows(yb, dest_p.reshape(-1)).reshape(B, TOP_K, T, D)
    yg_s = _rows(yb, dest[B * T:].T.reshape(-1)).reshape(1, TOP_K, BS, D)
    y_p = _final(x1_p, yg_p, tg_p, mp[5], final_g, seq_mode=True, tm=tm)
    y_s = _final(x1_s, yg_s, tg_s, ms[5], final_g, seq_mode=False, tm=tms)

    kv_tail = (2, N_KV_HEADS, HEAD_DIM)
    page_shape = (depth, B, T // page, page) + kv_tail
    w_keep = min(WINDOW, T)
    new_win_s = jnp.concatenate([state_kv_win[0][:, TS:], kvw_s.reshape(BS, TS, *kv_tail)], axis=1)
    new_conv_s = jnp.concatenate([state_conv[0][:, TS:], v_s.reshape(BS, TS, cw_width)], axis=1)
    return (y_p, y_s.reshape(BS, TS, D),
            kvc_p.reshape(page_shape), kvc_s.reshape((depth, BS, TS) + kv_tail),
            kvs_p.reshape(page_shape), kvs_s.reshape((depth, BS, TS) + kv_tail),
            kvw_p[:, T - w_keep:].reshape((depth, B, w_keep) + kv_tail), new_win_s[None],
            vlast_p[None], new_conv_s[None])
```

```python
import functools
import math

import numpy as np
import jax
import jax.numpy as jnp
from jax import lax
from jax.experimental import pallas as pl
from jax.experimental.pallas import tpu as pltpu

F32 = jnp.float32
BF16 = jnp.bfloat16

CONV_K = 3
N_HEADS = 8
N_KV_HEADS = 2
GROUP = N_HEADS // N_KV_HEADS
HEAD_DIM = 64
KV_WIDTH = N_KV_HEADS * HEAD_DIM
CMP_LEN = 32
CMP_STRIDE = 16
SLC_BLOCK = 64
RATIO = SLC_BLOCK // CMP_STRIDE
N_SEL = 16
WINDOW = 512
Q_BLOCK = 128
N_BUCKETS = 32
REL_MAX_DIST = 128
TOP_K = 4
SWIGLU_LIMIT = 7.0
SWIGLU_ALPHA = 1.702
EPS = 1e-6
NEG_INF = -1e30
FORCE_SCORE = 1e4

LANE = 128
BF16_ROWS = 16
ROWS_Q = GROUP * Q_BLOCK
KEY_CHUNK = 512
TILES = KEY_CHUNK // Q_BLOCK
FAR_UNROLL = 4
MOE_ROWS = 512
CMP_PAGES = 64
SMP_BATCH = 4
LOG2E = math.log2(math.e)
MOD_COLS = 1536
VMEM_LIMIT = 48 * 1024 * 1024
PICKED = -3e38
NOT_A_SLOT = -2e38
INT_BIG = 1 << 30


def _cparams(n_axes, vmem_limit=VMEM_LIMIT):
    return pltpu.CompilerParams(dimension_semantics=("arbitrary",) * n_axes, vmem_limit_bytes=vmem_limit)


def _dot(a, b):
    return jnp.dot(a, b, preferred_element_type=F32)


def _dot_nt(a, b):
    return lax.dot_general(a, b, (((1,), (1,)), ((), ())), preferred_element_type=F32)


def _split2(x):
    hi = x.astype(BF16)
    lo = (x - hi.astype(F32)).astype(BF16)
    return hi, lo


def _split3(x):
    a = x.astype(BF16)
    r = x - a.astype(F32)
    b = r.astype(BF16)
    c = (r - b.astype(F32)).astype(BF16)
    return a, b, c


def _dot3(a, b):
    ah, al = _split2(a)
    bh, bl = _split2(b)
    return _dot(ah, bh) + _dot(ah, bl) + _dot(al, bh)


def _sigmoid(x):
    return 1.0 / (1.0 + jnp.exp(-x))


def _rms(x, g):
    return x * lax.rsqrt(jnp.mean(x * x, axis=-1, keepdims=True) + EPS) * g


def _rel_bucket(dist):
    n = jnp.maximum(dist, 0)
    max_exact = N_BUCKETS // 2
    large = max_exact + (jnp.log(jnp.maximum(n, 1).astype(F32) / max_exact)
                         / math.log(REL_MAX_DIST / max_exact) * (N_BUCKETS - max_exact)).astype(jnp.int32)
    return jnp.where(n < max_exact, n, jnp.minimum(large, N_BUCKETS - 1))


def _mod_kernel(c_ref, w_ref, b_ref, o_ref):
    c = c_ref[...]
    o_ref[...] = _dot3(c * _sigmoid(c), w_ref[...]) + b_ref[...]


def _modulation(c, w_ada, b_ada):
    n, d = c.shape
    cols = w_ada.shape[1]
    bn = math.gcd(cols, MOD_COLS)
    return pl.pallas_call(
        _mod_kernel,
        out_shape=jax.ShapeDtypeStruct((n, cols), F32),
        grid=(cols // bn,),
        in_specs=[pl.BlockSpec((n, d), lambda i: (0, 0)),
                  pl.BlockSpec((d, bn), lambda i: (0, i)),
                  pl.BlockSpec((1, bn), lambda i: (0, i))],
        out_specs=pl.BlockSpec((n, bn), lambda i: (0, i)),
        compiler_params=_cparams(1),
        name="modulation",
    )(c, w_ada, b_ada.reshape(1, cols))


def _inproj_kernel(x_ref, m0_ref, m1_ref, g1_ref, w_ref, cw_ref, gnc_ref, pa_ref, pb_ref,
                   convn_ref, q_ref, kvc_ref, kvs_ref, kvw_ref, gate_ref, vlast_ref, *rest, seq_mode, tm, cw_width):
    if seq_mode:
        kh_ref, vth_ref, carry_ref = rest
    x = x_ref[0]
    h = _rms(x, g1_ref[...]) * (1.0 + m1_ref[0]) + m0_ref[0]
    hb = h.astype(BF16)
    c3 = 3 * cw_width
    uc = _dot(hb, w_ref[:, 0:c3])
    b_g = uc[:, 0:cw_width]
    v = uc[:, cw_width:2 * cw_width] * uc[:, 2 * cw_width:c3]
    if seq_mode:
        @pl.when(pl.program_id(1) == 0)
        def _():
            carry_ref[0:2, :] = pa_ref[0]
        c0 = carry_ref[0:1, :]
        c1 = carry_ref[1:2, :]
        row = lax.broadcasted_iota(jnp.int32, (tm, 1), 0)
        vm1 = jnp.where(row == 0, c1, pltpu.roll(v, 1, 0))
        vm2 = jnp.where(row == 0, c0, jnp.where(row == 1, c1, pltpu.roll(v, 2, 0)))
        carry_ref[0:2, :] = v[tm - 2:tm, :]
        vlast_ref[0] = v[tm - 2:tm, :]
    else:
        vm2 = pa_ref[0]
        vm1 = pb_ref[0]
        vlast_ref[0] = v
    cw = cw_ref[...]
    y = cw[0:1, :] * vm2 + cw[1:2, :] * vm1 + cw[2:3, :] * v
    convn_ref[0] = _rms(b_g * y, gnc_ref[...]).astype(BF16)

    aw = N_HEADS * HEAD_DIM
    uq = _dot(hb, w_ref[:, c3:c3 + aw]) * (HEAD_DIM ** -0.5 * (LOG2E if seq_mode else 1.0))
    if seq_mode:
        uqt = uq.T
        for k in range(N_KV_HEADS):
            for jj in range(tm // Q_BLOCK):
                q_ref[0, k, jj] = jnp.concatenate(
                    [uqt[(k * GROUP + g) * HEAD_DIM:(k * GROUP + g + 1) * HEAD_DIM, jj * Q_BLOCK:(jj + 1) * Q_BLOCK]
                     for g in range(GROUP)], axis=1).astype(BF16)
    else:
        for hh in range(N_HEADS):
            q_ref[0, hh] = uq[:, hh * HEAD_DIM:(hh + 1) * HEAD_DIM].astype(BF16)
    kv0 = c3 + aw
    kvw3 = 3 * 2 * KV_WIDTH
    ukv = _dot(hb, w_ref[:, kv0:kv0 + kvw3])
    kvc_ref[0] = ukv[:, 0:2 * KV_WIDTH]
    kvs_ref[0] = ukv[:, 2 * KV_WIDTH:4 * KV_WIDTH]
    kvw_ref[0] = ukv[:, 4 * KV_WIDTH:6 * KV_WIDTH]
    if seq_mode:
        for br in range(2):
            c0 = (br + 1) * 2 * KV_WIDTH
            for k in range(N_KV_HEADS):
                kh_ref[0, br * N_KV_HEADS + k] = ukv[:, c0 + k * HEAD_DIM:c0 + (k + 1) * HEAD_DIM].astype(BF16)
            vt = ukv[:, c0 + KV_WIDTH:c0 + 2 * KV_WIDTH].T.astype(BF16)
            for k in range(N_KV_HEADS):
                vth_ref[0, br * N_KV_HEADS + k] = vt[k * HEAD_DIM:(k + 1) * HEAD_DIM, :]
    ug = _dot(hb, w_ref[:, kv0 + kvw3:kv0 + kvw3 + LANE])
    gate_ref[0] = _sigmoid(ug)


def _inproj(x, m0, m1, ln_g, w_pad, conv_w, gn_conv, pa, pb, *, seq_mode, tm):
    B, T, D = x.shape
    cw_width = conv_w.shape[1]
    nt = T // tm
    row_blk = lambda w: pl.BlockSpec((1, tm, w), lambda b, t: (b, t, 0))
    if seq_mode:
        mod_spec = pl.BlockSpec((1, 1, D), lambda b, t: (b, 0, 0))
        prev_spec = pl.BlockSpec((1, 2, cw_width), lambda b, t: (b, 0, 0))
        vlast_shape = jax.ShapeDtypeStruct((B, 2, cw_width), F32)
        vlast_spec = pl.BlockSpec((1, 2, cw_width), lambda b, t: (b, 0, 0))
    else:
        mod_spec = row_blk(D)
        prev_spec = row_blk(cw_width)
        vlast_shape = jax.ShapeDtypeStruct((B, T, cw_width), F32)
        vlast_spec = row_blk(cw_width)
    const = lambda shape: pl.BlockSpec(shape, lambda b, t: (0,) * len(shape))
    kern = functools.partial(_inproj_kernel, seq_mode=seq_mode, tm=tm, cw_width=cw_width)
    if seq_mode:
        assert tm % Q_BLOCK == 0
        q_shape = jax.ShapeDtypeStruct((B, N_KV_HEADS, T // Q_BLOCK, HEAD_DIM, ROWS_Q), BF16)
        q_spec = pl.BlockSpec((1, N_KV_HEADS, tm // Q_BLOCK, HEAD_DIM, ROWS_Q), lambda b, t: (b, 0, t, 0, 0))
    else:
        q_shape = jax.ShapeDtypeStruct((B, N_HEADS, T, HEAD_DIM), BF16)
        q_spec = pl.BlockSpec((1, N_HEADS, tm, HEAD_DIM), lambda b, t: (b, 0, t, 0))
    out_shape = [jax.ShapeDtypeStruct((B, T, cw_width), BF16),
                 q_shape,
                 jax.ShapeDtypeStruct((B, T, 2 * KV_WIDTH), F32),
                 jax.ShapeDtypeStruct((B, T, 2 * KV_WIDTH), F32),
                 jax.ShapeDtypeStruct((B, T, 2 * KV_WIDTH), F32),
                 jax.ShapeDtypeStruct((B, T, LANE), F32),
                 vlast_shape]
    out_specs = [row_blk(cw_width),
                 q_spec,
                 row_blk(2 * KV_WIDTH), row_blk(2 * KV_WIDTH), row_blk(2 * KV_WIDTH),
                 row_blk(LANE), vlast_spec]
    scratch = []
    if seq_mode:
        n_att = 2 * N_KV_HEADS
        out_shape += [jax.ShapeDtypeStruct((B, n_att, T, HEAD_DIM), BF16),
                      jax.ShapeDtypeStruct((B, n_att, HEAD_DIM, T), BF16)]
        out_specs += [pl.BlockSpec((1, n_att, tm, HEAD_DIM), lambda b, t: (b, 0, t, 0)),
                      pl.BlockSpec((1, n_att, HEAD_DIM, tm), lambda b, t: (b, 0, 0, t))]
        scratch = [pltpu.VMEM((8, cw_width), F32)]
    return pl.pallas_call(
        kern,
        out_shape=tuple(out_shape),
        grid=(B, nt),
        in_specs=[row_blk(D), mod_spec, mod_spec, const((1, D)), const(w_pad.shape), const(conv_w.shape),
                  const((1, cw_width)), prev_spec, prev_spec],
        out_specs=tuple(out_specs),
        scratch_shapes=scratch,
        compiler_params=_cparams(2),
        name="inproj_seq" if seq_mode else "inproj_rows",
    )(x, m0, m1, ln_g.reshape(1, D), w_pad, conv_w, gn_conv.reshape(1, cw_width), pa, pb)


def _cmp_ab_kernel(pt_ref, *refs, G):
    pages = refs[:2 * G]
    w_ref, out_ref, x_ref = refs[2 * G], refs[2 * G + 1], refs[2 * G + 2]
    half = KV_WIDTH
    for j in range(G):
        for r in range(CMP_STRIDE):
            for c in range(2):
                x_ref[c, j * 8:(j + 1) * 8, r * half:(r + 1) * half] = (
                    pages[2 * j + c][0, pl.ds(r, 8, stride=CMP_STRIDE), :])
    wcols = w_ref.shape[2]
    for c in range(2):
        out_ref[0, :, c * wcols:(c + 1) * wcols] = _dot(x_ref[c].astype(BF16), w_ref[c])


def _cmp_ab(pool, page_table, w_blk, G):
    P, page, width = pool.shape
    B, n_pages = page_table.shape
    cpp = page // CMP_STRIDE
    assert cpp == 8 and n_pages % G == 0
    kdim = CMP_STRIDE * width // 2

    def pg_spec(j, c):
        return pl.BlockSpec((1, page, width // 2), lambda b, g, pt: (pt[b * n_pages + g * G + j], 0, c))

    return pl.pallas_call(
        functools.partial(_cmp_ab_kernel, G=G),
        out_shape=jax.ShapeDtypeStruct((B, n_pages * cpp, 2 * w_blk.shape[2]), F32),
        grid_spec=pltpu.PrefetchScalarGridSpec(
            num_scalar_prefetch=1,
            grid=(B, n_pages // G),
            in_specs=[pg_spec(j, c) for j in range(G) for c in range(2)]
            + [pl.BlockSpec(w_blk.shape, lambda b, g, pt: (0, 0, 0))],
            out_specs=pl.BlockSpec((1, G * cpp, 2 * w_blk.shape[2]), lambda b, g, pt: (b, g, 0)),
            scratch_shapes=[pltpu.VMEM((2, G * cpp, kdim), F32)]),
        compiler_params=_cparams(2),
        name="cmp_partial",
    )(page_table.reshape(-1).astype(jnp.int32), *([pool] * (2 * G)), w_blk)


def _cmp_ab_t_kernel(pt_ref, *refs, G):
    pages = refs[:G]
    w_ref, out_ref, x_ref, s_ref = refs[G], refs[G + 1], refs[G + 2], refs[G + 3]
    half = KV_WIDTH
    for j in range(G):
        for c in range(2):
            for k in range(N_KV_HEADS):
                s_ref[c, j, :, k * HEAD_DIM:(k + 1) * HEAD_DIM] = pages[j][0, c, k].astype(BF16).T.astype(F32)
    for j in range(G):
        for r in range(CMP_STRIDE):
            for c in range(2):
                x_ref[c, j * 8:(j + 1) * 8, r * half:(r + 1) * half] = (
                    s_ref[c, j, pl.ds(r, 8, stride=CMP_STRIDE), :])
    wcols = w_ref.shape[2]
    for c in range(2):
        out_ref[0, :, c * wcols:(c + 1) * wcols] = _dot(x_ref[c].astype(BF16), w_ref[c])


def _cmp_ab_t(pool_t, page_table, w_blk, G):
    P, _, _, hd, page = pool_t.shape
    B, n_pages = page_table.shape
    cpp = page // CMP_STRIDE
    assert cpp == 8 and n_pages % G == 0
    kdim = CMP_STRIDE * KV_WIDTH

    def pg_spec(j):
        return pl.BlockSpec((1,) + pool_t.shape[1:], lambda b, g, pt: (pt[b * n_pages + g * G + j], 0, 0, 0, 0))

    return pl.pallas_call(
        functools.partial(_cmp_ab_t_kernel, G=G),
        out_shape=jax.ShapeDtypeStruct((B, n_pages * cpp, 2 * w_blk.shape[2]), F32),
        grid_spec=pltpu.PrefetchScalarGridSpec(
            num_scalar_prefetch=1,
            grid=(B, n_pages // G),
            in_specs=[pg_spec(j) for j in range(G)] + [pl.BlockSpec(w_blk.shape, lambda b, g, pt: (0, 0, 0))],
            out_specs=pl.BlockSpec((1, G * cpp, 2 * w_blk.shape[2]), lambda b, g, pt: (b, g, 0)),
            scratch_shapes=[pltpu.VMEM((2, G * cpp, kdim), F32), pltpu.VMEM((2, G, page, KV_WIDTH), F32)]),
        compiler_params=_cparams(2),
        name="cmp_partial_t",
    )(page_table.reshape(-1).astype(jnp.int32), *([pool_t] * G), w_blk)


def _gelu_tanh(x):
    return 0.5 * x * (1.0 + jnp.tanh(math.sqrt(2.0 / math.pi) * (x + 0.044715 * (x * x * x))))


def _cmp_fin_kernel(ab_ref, pe_ref, w1_ref, w2_ref, kc_ref, vct_ref, *, nch):
    for c in range(2):
        pe_t = _dot(pe_ref[c:c + 1, :].astype(BF16), w1_ref[c].astype(BF16))
        w2 = w2_ref[c].astype(BF16)
        for k in range(N_KV_HEADS):
            base = (c * N_KV_HEADS + k) * 2 * HEAD_DIM
            slab = ab_ref[0, :, base:base + 2 * HEAD_DIM]
            nxt = pltpu.roll(slab, nch - 1, 0)
            pre = slab[:, 0:HEAD_DIM] + nxt[:, HEAD_DIM:2 * HEAD_DIM] + pe_t
            blocks = _dot(_gelu_tanh(pre).astype(BF16), w2)
            if c == 0:
                kc_ref[0, k] = blocks.astype(BF16)
            else:
                vct_ref[0, k] = blocks.T.astype(BF16)


def _cmp_finish(ab, cmp_pe, cmp_w1, cmp_w2):
    B, nch, w = ab.shape
    pe = cmp_pe.reshape(2, CMP_LEN * HEAD_DIM)
    return pl.pallas_call(
        functools.partial(_cmp_fin_kernel, nch=nch),
        out_shape=(jax.ShapeDtypeStruct((B, N_KV_HEADS, nch, HEAD_DIM), BF16),
                   jax.ShapeDtypeStruct((B, N_KV_HEADS, HEAD_DIM, nch), BF16)),
        grid=(B,),
        in_specs=[pl.BlockSpec((1, nch, w), lambda b: (b, 0, 0)),
                  pl.BlockSpec(pe.shape, lambda b: (0, 0)),
                  pl.BlockSpec(cmp_w1.shape, lambda b: (0, 0, 0)),
                  pl.BlockSpec(cmp_w2.shape, lambda b: (0, 0, 0))],
        out_specs=(pl.BlockSpec((1, N_KV_HEADS, nch, HEAD_DIM), lambda b: (b, 0, 0, 0)),
                   pl.BlockSpec((1, N_KV_HEADS, HEAD_DIM, nch), lambda b: (b, 0, 0, 0))),
        compiler_params=_cparams(1),
        name="cmp_finish",
    )(ab, pe, cmp_w1, cmp_w2)


def _cmp_block_weight(cmp_w1):
    hid = cmp_w1.shape[2]
    w = cmp_w1.reshape(2, 2, CMP_STRIDE, HEAD_DIM, hid)
    eye = jnp.eye(N_KV_HEADS, dtype=cmp_w1.dtype)
    wb = jnp.einsum('cardh,kj->crkdjah', w, eye)
    return wb.reshape(2, CMP_STRIDE * N_KV_HEADS * HEAD_DIM, N_KV_HEADS * 2 * hid).astype(BF16)


def _score_matrix(nch, n_slc):
    i = np.arange(nch)[:, None]
    j = np.arange(n_slc)[None, :]
    m = 2.0 * ((i // RATIO == j) & (i % RATIO < RATIO - 1)) + 1.0 * (i == RATIO * j + RATIO - 1) \
        + 1.0 * (i == RATIO * j - 1)
    return jnp.asarray(m, dtype=BF16)


def _nsa_kernel(q_ref, kc_ref, vc_ref, ks_ref, vs_ref, kw_ref, vw_ref, gate_ref, sa_ref, sw_ref,
                cs_ref, mm_ref, e_ref, o_ref, m_ref, l_ref, acc_ref, sta_ref, stb_ref, stc_ref, std_ref,
                oc_ref, pen_ref, *, nch, n_slc, parts):
    j = pl.program_id(2)
    q0 = pl.multiple_of(j * Q_BLOCK, Q_BLOCK)
    cq = lax.shift_right_logical(j, TILES.bit_length() - 1)
    jm = jnp.bitwise_and(j, TILES - 1)
    qt = q_ref[0, 0, 0]
    lane_q = jnp.bitwise_and(lax.broadcasted_iota(jnp.int32, (1, ROWS_Q), 1), Q_BLOCK - 1)
    tq = q0 + lane_q
    has_prev = cq >= 1
    k_own = pl.multiple_of(cq * KEY_CHUNK, KEY_CHUNK)
    k_prev = pl.multiple_of(jnp.maximum(cq - 1, 0) * KEY_CHUNK, KEY_CHUNK)
    own0 = pl.multiple_of((2 * TILES - 1 - jm) * Q_BLOCK, Q_BLOCK)
    prev0 = pl.multiple_of(jnp.where(has_prev, TILES - 1 - jm, 2 * TILES) * Q_BLOCK, Q_BLOCK)
    prevw0 = pl.multiple_of(jnp.where(has_prev, TILES - 1 - jm, 2 * TILES - 1) * Q_BLOCK, Q_BLOCK)

    stc_ref[...] = _dot(kw_ref[0, 0, pl.ds(k_own, KEY_CHUNK), :], qt) + sa_ref[0, pl.ds(own0, KEY_CHUNK), :]
    std_ref[...] = _dot(kw_ref[0, 0, pl.ds(k_prev, KEY_CHUNK), :], qt) + sw_ref[0, pl.ds(prevw0, KEY_CHUNK), :]

    per_q = Q_BLOCK // CMP_STRIDE
    nq = pl.num_programs(2)
    c0 = pl.multiple_of((nq - 1 - j) * per_q, per_q)
    anyc = tq >= CMP_LEN - 1
    cur = lax.shift_right_logical(q0 + lax.broadcasted_iota(jnp.int32, (1, Q_BLOCK), 1),
                                  SLC_BLOCK.bit_length() - 1)

    def cmp_and_select(nc, ns):
        lc = _dot(kc_ref[0, 0, 0:nc, :], qt) + cs_ref[0, pl.ds(c0, nc), :]
        p = jnp.exp2(lc - jnp.max(lc, axis=0, keepdims=True))
        pc = p * jnp.where(anyc, 1.0 / jnp.sum(p, axis=0, keepdims=True), 0.0)
        oc_ref[...] = _dot(vc_ref[0, 0, :, 0:nc], pc.astype(BF16))
        imp = pc[:, 0:Q_BLOCK]
        for g in range(1, GROUP):
            imp = imp + pc[:, g * Q_BLOCK:(g + 1) * Q_BLOCK]
        mm = mm_ref[0:ns, 0:nc]
        i1, i2, i3 = _split3(imp)
        score = _dot(mm, i1) + _dot(mm, i2) + _dot(mm, i3)
        blk = lax.broadcasted_iota(jnp.int32, (ns, 1), 0)
        forced = (blk == 0) | (blk == cur) | (blk == cur - 1)
        sc = jnp.where(forced, FORCE_SCORE, jnp.where(blk <= cur, score, NEG_INF))
        big = jnp.int32(INT_BIG)
        for _ in range(min(N_SEL, ns)):
            mx = jnp.max(sc, axis=0, keepdims=True)
            first = jnp.min(jnp.where(sc == mx, blk, big), axis=0, keepdims=True)
            sc = jnp.where(blk == first, PICKED, sc)
        pen_ref[0:ns, :] = jnp.where(sc == PICKED, 0.0, NEG_INF).astype(BF16)
        if ns < n_slc:
            pen_ref[ns:n_slc, :] = jnp.full((n_slc - ns, Q_BLOCK), NEG_INF, BF16)

    for part in range(1, parts + 1):
        @pl.when(jnp.logical_and(j * parts >= (part - 1) * nq, j * parts < part * nq))
        def _(part=part):
            cmp_and_select(nch * part // parts, n_slc * part // parts)

    o_c = oc_ref[...]
    qsel = jnp.concatenate([jnp.concatenate([pen_ref[...]] * GROUP, axis=1), qt], axis=0)

    def scores_sel(k0):
        ke = jnp.concatenate([e_ref[pl.ds(k0, KEY_CHUNK), :], ks_ref[0, 0, pl.ds(k0, KEY_CHUNK), :]], axis=1)
        return _dot(ke, qsel)

    def flash(st, v, first):
        mx = jnp.max(st, axis=0, keepdims=True)
        if first:
            pe = jnp.exp2(st - mx)
            l_ref[...] = jnp.sum(pe, axis=0, keepdims=True)
            acc_ref[...] = _dot(v, pe.astype(BF16))
            m_ref[...] = mx
        else:
            m_old = m_ref[...]
            m_new = jnp.maximum(m_old, mx)
            a = jnp.exp2(m_old - m_new)
            pe = jnp.exp2(st - m_new)
            l_ref[...] = a * l_ref[...] + jnp.sum(pe, axis=0, keepdims=True)
            acc_ref[...] = a * acc_ref[...] + _dot(v, pe.astype(BF16))
            m_ref[...] = m_new

    sta_ref[...] = scores_sel(k_own) + sa_ref[0, pl.ds(own0, KEY_CHUNK), :]
    stb_ref[...] = scores_sel(k_prev) + sa_ref[0, pl.ds(prev0, KEY_CHUNK), :]
    flash(stc_ref[...], vw_ref[0, 0, :, pl.ds(k_own, KEY_CHUNK)], True)
    flash(std_ref[...], vw_ref[0, 0, :, pl.ds(k_prev, KEY_CHUNK)], False)
    o_w = acc_ref[...] * (1.0 / l_ref[...])

    flash(sta_ref[...], vs_ref[0, 0, :, pl.ds(k_own, KEY_CHUNK)], True)
    n_far = jnp.maximum(cq - 1, 0)
    last_chunk = ks_ref.shape[2] // KEY_CHUNK - 1

    def chunk_start(c):
        return pl.multiple_of(jnp.minimum(c, last_chunk) * KEY_CHUNK, KEY_CHUNK)

    sta_ref[...] = scores_sel(chunk_start(0))
    flash(stb_ref[...], vs_ref[0, 0, :, pl.ds(k_prev, KEY_CHUNK)], False)

    def far_steps(c0, n):
        bufs = (sta_ref, stb_ref)
        for t in range(n):
            cur_buf, nxt_buf = bufs[t % 2], bufs[(t + 1) % 2]
            nxt_buf[...] = scores_sel(chunk_start(c0 + t + 1))
            flash(cur_buf[...], vs_ref[0, 0, :, pl.ds(chunk_start(c0 + t), KEY_CHUNK)], False)

    def far_group(i, carry):
        far_steps(FAR_UNROLL * i, FAR_UNROLL)
        return carry

    n_groups = lax.shift_right_logical(n_far, FAR_UNROLL.bit_length() - 1)
    lax.fori_loop(0, n_groups, far_group, 0)
    rest0 = n_groups * FAR_UNROLL

    @pl.when(jnp.bitwise_and(n_far, 2) == 2)
    def _():
        far_steps(rest0, 2)

    @pl.when(jnp.bitwise_and(n_far, 1) == 1)
    def _():
        flash(sta_ref[...], vs_ref[0, 0, :, pl.ds(chunk_start(n_far - 1), KEY_CHUNK)], False)

    o_s = acc_ref[...] * (1.0 / l_ref[...])

    g3 = gate_ref[0, 0, 0]
    o = g3[0:1, :] * o_c + g3[1:2, :] * o_s + g3[2:3, :] * o_w
    stacked = jnp.concatenate([o[:, g * Q_BLOCK:(g + 1) * Q_BLOCK] for g in range(GROUP)], axis=0)
    o_ref[0] = stacked.T


def _bias_tables(table, nch, nq):
    tbl = table.reshape(N_BUCKETS, N_KV_HEADS, GROUP)
    far = tbl[N_BUCKETS - 1]
    i = jnp.arange(Q_BLOCK)

    def cols(rel):
        hot = (_rel_bucket(rel)[..., None] == jnp.arange(N_BUCKETS)).astype(F32)
        b = (jnp.einsum('kqn,nhg->kqhg', hot, tbl, precision=lax.Precision.HIGHEST) - far) * LOG2E
        b = jnp.where((rel >= 0)[:, :, None, None], b, 0.0)
        return b.transpose(2, 0, 3, 1).reshape(N_KV_HEADS, rel.shape[0], ROWS_Q)

    p0 = cols(i[None, :] - i[:, None])
    p1 = cols(Q_BLOCK + i[None, :] - i[:, None])
    per_q = Q_BLOCK // CMP_STRIDE
    rel_c = i[None, :] - CMP_STRIDE * (jnp.arange(nch + per_q * (nq - 1))[:, None] - per_q * (nq - 1)) - (CMP_LEN - 1)
    strip_cmp = jnp.where(jnp.tile(rel_c >= 0, (1, GROUP))[None], cols(rel_c), NEG_INF)
    kk = jnp.arange(Q_BLOCK)[:, None]
    qq = jnp.tile(jnp.arange(Q_BLOCK), GROUP)[None, :]
    bc = lambda a: jnp.broadcast_to(a[None], (N_KV_HEADS, Q_BLOCK, ROWS_Q))
    neg = bc(jnp.full((Q_BLOCK, ROWS_Q), NEG_INF, F32))
    zero = bc(jnp.zeros((Q_BLOCK, ROWS_Q), F32))
    p0_causal = jnp.where((kk <= qq)[None], p0, NEG_INF)
    in_window = bc(jnp.where(kk > qq, 0.0, NEG_INF).astype(F32))
    strip_own = jnp.concatenate([zero] * (2 * TILES - 2) + [p1, p0_causal] + [neg] * TILES, axis=1)
    strip_win = jnp.concatenate([neg] * (TILES - 1) + [in_window] + [zero] * (TILES - 2) + [p1] + [neg] * TILES,
                                axis=1)
    return strip_own, strip_win, strip_cmp


def _nsa_prompt(q_t, k_hm, vt_hm, kc, vct, gates_t, table):
    B, _, T, hd = k_hm.shape
    H = N_HEADS
    nch = kc.shape[2]
    n_slc = T // SLC_BLOCK
    nq = T // Q_BLOCK
    assert T % KEY_CHUNK == 0 and nch == RATIO * n_slc and WINDOW <= KEY_CHUNK
    strip_own, strip_win, strip_cmp = _bias_tables(table, nch, nq)
    e_tab = (jnp.arange(T)[:, None] // SLC_BLOCK == jnp.arange(n_slc)[None, :]).astype(BF16)
    mm = _score_matrix(nch, n_slc).T
    nk = N_KV_HEADS
    k_spec = lambda slab0: pl.BlockSpec((1, 1, T, hd), lambda b, k, j: (b, slab0 + k, 0, 0))
    vt_spec = lambda slab0: pl.BlockSpec((1, 1, hd, T), lambda b, k, j: (b, slab0 + k, 0, 0))
    per_head = lambda a: pl.BlockSpec((1,) + a.shape[1:], lambda b, k, j: (k, 0, 0))
    score_buf = pltpu.VMEM((KEY_CHUNK, ROWS_Q), F32)
    parts = max(p for p in (4, 2, 1) if nq % p == 0 and (nch // p) % LANE == 0
                and (n_slc // p) % BF16_ROWS == 0)
    return pl.pallas_call(
        functools.partial(_nsa_kernel, nch=nch, n_slc=n_slc, parts=parts),
        out_shape=jax.ShapeDtypeStruct((B, T, H * hd), F32),
        grid=(B, nk, nq),
        in_specs=[pl.BlockSpec((1, 1, 1, hd, ROWS_Q), lambda b, k, j: (b, k, j, 0, 0)),
                  pl.BlockSpec((1, 1, nch, hd), lambda b, k, j: (b, k, 0, 0)),
                  pl.BlockSpec((1, 1, hd, nch), lambda b, k, j: (b, k, 0, 0)),
                  k_spec(0), vt_spec(0), k_spec(nk), vt_spec(nk),
                  pl.BlockSpec((1, 1, 1, 3, ROWS_Q), lambda b, k, j: (b, k, j, 0, 0)),
                  per_head(strip_own), per_head(strip_win), per_head(strip_cmp),
                  pl.BlockSpec(mm.shape, lambda b, k, j: (0, 0)),
                  pl.BlockSpec(e_tab.shape, lambda b, k, j: (0, 0))],
        out_specs=pl.BlockSpec((1, Q_BLOCK, GROUP * hd), lambda b, k, j: (b, j, k)),
        scratch_shapes=[pltpu.VMEM((1, ROWS_Q), F32), pltpu.VMEM((1, ROWS_Q), F32),
                        pltpu.VMEM((hd, ROWS_Q), F32), score_buf, score_buf, score_buf, score_buf,
                        pltpu.VMEM((hd, ROWS_Q), F32), pltpu.VMEM((n_slc, Q_BLOCK), BF16)],
        compiler_params=_cparams(3),
        name="nsa_prompt",
    )(q_t, kc, vct, k_hm, vt_hm, k_hm, vt_hm, gates_t, strip_own, strip_win, strip_cmp, mm, e_tab)


def _smp_cmp_kernel(q_ref, kc_ref, vct_ref, bias_ref, mm_ref, oc_ref, idx_ref, *, n_cmp, n_slc, cur):
    nch = kc_ref.shape[2]
    ncol = mm_ref.shape[1]
    coli = lax.broadcasted_iota(jnp.int32, (1, nch), 1)
    maskc = coli < n_cmp
    blk = lax.broadcasted_iota(jnp.int32, (1, ncol), 1)
    k_sel = min(N_SEL, n_slc)
    lane = lax.broadcasted_iota(jnp.int32, (1, LANE), 1)
    nb = q_ref.shape[0]
    mm = mm_ref[...]
    forced = (blk == 0) | (blk == cur) | (blk == cur - 1)
    scores = []
    for b in range(nb):
        for kh in range(N_KV_HEADS):
            qm = q_ref[b, kh * GROUP:(kh + 1) * GROUP].astype(BF16)
            lc = _dot_nt(qm, kc_ref[b, kh]) + bias_ref[kh]
            lcm = jnp.where(maskc, lc, NEG_INF)
            p = jnp.where(maskc, jnp.exp(lcm - jnp.max(lcm, axis=-1, keepdims=True)), 0.0)
            pc = p * (1.0 / jnp.sum(p, axis=-1, keepdims=True))
            oc_ref[b, kh * GROUP:(kh + 1) * GROUP] = _dot_nt(pc.astype(BF16), vct_ref[b, kh])
            imp = pc[0:1]
            for g in range(1, GROUP):
                imp = imp + pc[g:g + 1]
            i1, i2, i3 = _split3(imp)
            score = _dot(i1, mm) + _dot(i2, mm) + _dot(i3, mm)
            sc = jnp.where(forced, FORCE_SCORE, jnp.where(blk <= cur, score, NEG_INF))
            scores.append(jnp.where(blk < n_slc, sc, NOT_A_SLOT))
    sc = jnp.concatenate(scores, axis=0)
    big = jnp.int32(INT_BIG)
    out = jnp.zeros((nb * N_KV_HEADS, LANE), jnp.int32)
    for it in range(k_sel):
        m = jnp.max(sc, axis=-1, keepdims=True)
        first = jnp.min(jnp.where(sc == m, blk, big), axis=-1, keepdims=True)
        out = jnp.where(lane == it, first, out)
        sc = jnp.where(blk == first, PICKED, sc)
    idx_ref[0] = out


def _smp_cmp(q_hm, kc, vct, bias_c, mm, n_cmp, n_slc, cur):
    B = q_hm.shape[0]
    nch = kc.shape[2]
    nb = math.gcd(B, 8)
    o_c, idx = pl.pallas_call(
        functools.partial(_smp_cmp_kernel, n_cmp=n_cmp, n_slc=n_slc, cur=cur),
        out_shape=(jax.ShapeDtypeStruct((B, N_HEADS, HEAD_DIM), F32),
                   jax.ShapeDtypeStruct((B // nb, nb * N_KV_HEADS, LANE), jnp.int32)),
        grid=(B // nb,),
        in_specs=[pl.BlockSpec((nb, N_HEADS, HEAD_DIM), lambda b: (b, 0, 0)),
                  pl.BlockSpec((nb, N_KV_HEADS, nch, HEAD_DIM), lambda b: (b, 0, 0, 0)),
                  pl.BlockSpec((nb, N_KV_HEADS, HEAD_DIM, nch), lambda b: (b, 0, 0, 0)),
                  pl.BlockSpec(bias_c.shape, lambda b: (0, 0, 0)),
                  pl.BlockSpec(mm.shape, lambda b: (0, 0))],
        out_specs=(pl.BlockSpec((nb, N_HEADS, HEAD_DIM), lambda b: (b, 0, 0)),
                   pl.BlockSpec((1, nb * N_KV_HEADS, LANE), lambda b: (b, 0, 0))),
        compiler_params=_cparams(1),
        name="sample_cmp",
    )(q_hm, kc, vct, bias_c, mm)
    return o_c, idx.reshape(B, N_KV_HEADS, LANE)


def _smp_att_kernel(phys_ref, q_ref, pool_ref, bsel_ref, msel_ref, new_ref, win_ref, bwin_ref, bnew_ref,
                    gate_ref, oc_ref, o_ref, buf_ref, sem_ref, *, k_sel):
    nb = q_ref.shape[0]
    n_blk = N_KV_HEADS * k_sel
    n_dma = nb * n_blk
    base = pl.program_id(0) * n_dma

    def blk_copy(i):
        return pltpu.make_async_copy(pool_ref.at[phys_ref[base + i]], buf_ref.at[i], sem_ref.at[i])

    for i in range(n_dma):
        blk_copy(i).start()
    for i in range(n_dma):
        blk_copy(i).wait()

    nw = win_ref.shape[4]
    for bl, kh in [(bl, kh) for bl in range(nb) for kh in range(N_KV_HEADS)]:
        buf0 = bl * n_blk + kh * k_sel
        rows = slice(kh * GROUP, (kh + 1) * GROUP)
        qm = q_ref[bl, rows].astype(BF16)
        kcol = slice(kh * HEAD_DIM, (kh + 1) * HEAD_DIM)
        vcol = slice(KV_WIDTH + kh * HEAD_DIM, KV_WIDTH + (kh + 1) * HEAD_DIM)
        bnew = bnew_ref[kh]

        def branch(kt, vt, bias, mask, knew, vnew, new_ok):
            lg = _dot(qm, kt) + bias
            lg = jnp.where(mask, lg, NEG_INF)
            ln = jnp.sum(qm.astype(F32) * knew.astype(F32), axis=-1, keepdims=True) + bnew
            if new_ok is not None:
                ln = jnp.where(new_ok, ln, NEG_INF)
            mx = jnp.maximum(jnp.max(lg, axis=-1, keepdims=True), ln)
            pe = jnp.where(mask, jnp.exp(lg - mx), 0.0)
            pn = jnp.exp(ln - mx)
            if new_ok is not None:
                pn = jnp.where(new_ok, pn, 0.0)
            den = jnp.sum(pe, axis=-1, keepdims=True) + pn
            any_ok = den > 0.0
            inv = 1.0 / jnp.where(any_ok, den, 1.0)
            o = _dot_nt(pe.astype(BF16), vt) + pn.astype(BF16).astype(F32) * vnew.astype(F32)
            return jnp.where(any_ok, o * inv, 0.0)

        page = buf_ref.shape[4]
        nkeys = k_sel * page
        kt = jnp.concatenate([buf_ref[buf0 + i, 0, kh] for i in range(k_sel)], axis=1).astype(BF16)
        vt = jnp.concatenate([buf_ref[buf0 + i, 1, kh] for i in range(k_sel)], axis=1).astype(BF16)
        knew = new_ref[bl, 0:1, kcol].astype(BF16)
        vnew = new_ref[bl, 0:1, vcol].astype(BF16)
        msel = msel_ref[bl, kh]
        o_s = branch(kt, vt, bsel_ref[bl, kh], msel[:, 0:nkeys] > 0.5, knew, vnew, msel[:, nkeys:nkeys + 1] > 0.5)

        kw = win_ref[bl, 0, kh].astype(BF16)
        vw = win_ref[bl, 1, kh].astype(BF16)
        knw = new_ref[bl, 1:2, kcol].astype(BF16)
        vnw = new_ref[bl, 1:2, vcol].astype(BF16)
        dist = nw - lax.broadcasted_iota(jnp.int32, (1, nw), 1)
        o_w = branch(kw, vw, bwin_ref[kh], dist < WINDOW, knw, vnw, None)

        g3 = gate_ref[bl, rows]
        o_ref[bl, rows] = g3[:, 0:1] * oc_ref[bl, rows] + g3[:, 1:2] * o_s + g3[:, 2:3] * o_w


def _smp_att(phys, q_hm, pool_blk, bias_sel, mask_sel, new_rows, win_buf, bias_win, bias_new, gates, o_c, k_sel):
    B = q_hm.shape[0]
    nb = math.gcd(B, SMP_BATCH)
    n_dma = nb * N_KV_HEADS * k_sel
    full = lambda a: pl.BlockSpec(a.shape, lambda b, ph: (0,) * a.ndim)
    per_b = lambda a: pl.BlockSpec((nb,) + a.shape[1:], lambda b, ph: (b,) + (0,) * (a.ndim - 1))
    return pl.pallas_call(
        functools.partial(_smp_att_kernel, k_sel=k_sel),
        out_shape=jax.ShapeDtypeStruct((B, N_HEADS, HEAD_DIM), F32),
        grid_spec=pltpu.PrefetchScalarGridSpec(
            num_scalar_prefetch=1,
            grid=(B // nb,),
            in_specs=[per_b(q_hm), pl.BlockSpec(memory_space=pl.ANY), per_b(bias_sel), per_b(mask_sel),
                      per_b(new_rows), per_b(win_buf), full(bias_win), full(bias_new), per_b(gates), per_b(o_c)],
            out_specs=pl.BlockSpec((nb, N_HEADS, HEAD_DIM), lambda b, ph: (b, 0, 0)),
            scratch_shapes=[pltpu.VMEM((n_dma,) + pool_blk.shape[1:], F32),
                            pltpu.SemaphoreType.DMA((n_dma,))]),
        compiler_params=_cparams(1),
        name="sample_att",
    )(phys, q_hm, pool_blk, bias_sel, mask_sel, new_rows, win_buf, bias_win, bias_new, gates, o_c)


def _merge_kernel(x_ref, cn_ref, att_ref, gna_ref, wo_ref, m2_ref, m3_ref, m4_ref, g2_ref, rw_ref, rb_ref,
                  x1_ref, h2_ref, ti_ref, tg_ref, *, n_exp, cw_width):
    att_n = _rms(att_ref[0], gna_ref[...]).astype(BF16)
    mix = _dot(cn_ref[0], wo_ref[0:cw_width, :]) + _dot(att_n, wo_ref[cw_width:, :])
    x1 = x_ref[0] + m2_ref[0] * mix
    x1_ref[0] = x1
    h2 = _rms(x1, g2_ref[...]) * (1.0 + m4_ref[0]) + m3_ref[0]
    h2_ref[0] = h2
    logits = _dot3(h2, rw_ref[...]) + rb_ref[...]
    lane = lax.broadcasted_iota(jnp.int32, (1, LANE), 1)
    sc = jnp.where(lane < n_exp, logits, NOT_A_SLOT)
    big = jnp.int32(INT_BIG)
    ti = jnp.zeros(sc.shape, jnp.int32)
    tv = jnp.zeros(sc.shape, F32)
    v0 = None
    den = None
    for k in range(TOP_K):
        m = jnp.max(sc, axis=-1, keepdims=True)
        first = jnp.min(jnp.where(sc == m, lane, big), axis=-1, keepdims=True)
        if k == 0:
            v0 = m
        e = jnp.exp(m - v0)
        den = e if den is None else den + e
        ti = jnp.where(lane == k, first, ti)
        tv = jnp.where(lane == k, e, tv)
        sc = jnp.where(lane == first, PICKED, sc)
    ti_ref[0] = ti
    tg_ref[0] = tv * (1.0 / den)


def _merge(x, convn, att, gn_att, wo_b, m2, m3, m4, ln2_g, rw_pad, rb_pad, *, seq_mode, tm, n_exp):
    B, T, D = x.shape
    cw_width = convn.shape[2]
    aw = att.shape[2]
    row_blk = lambda w: pl.BlockSpec((1, tm, w), lambda b, t: (b, t, 0))
    mod_spec = pl.BlockSpec((1, 1, D), lambda b, t: (b, 0, 0)) if seq_mode else row_blk(D)
    const = lambda shape: pl.BlockSpec(shape, lambda b, t: (0,) * len(shape))
    return pl.pallas_call(
        functools.partial(_merge_kernel, n_exp=n_exp, cw_width=cw_width),
        out_shape=(jax.ShapeDtypeStruct((B, T, D), F32), jax.ShapeDtypeStruct((B, T, D), F32),
                   jax.ShapeDtypeStruct((B, T, LANE), jnp.int32), jax.ShapeDtypeStruct((B, T, LANE), F32)),
        grid=(B, T // tm),
        in_specs=[row_blk(D), row_blk(cw_width), row_blk(aw), const((1, aw)), const(wo_b.shape),
                  mod_spec, mod_spec, mod_spec, const((1, D)), const(rw_pad.shape), const((1, LANE))],
        out_specs=(row_blk(D), row_blk(D), row_blk(LANE), row_blk(LANE)),
        compiler_params=_cparams(2),
        name="merge_seq" if seq_mode else "merge_rows",
    )(x, convn, att, gn_att.reshape(1, aw), wo_b, m2, m3, m4, ln2_g.reshape(1, D), rw_pad, rb_pad)


def _moe_kernel(be_ref, nu_ref, xs_ref, wgu_ref, bgu_ref, wd_ref, bd_ref, y_ref, wgu_b, wd_b, *, d_ff):
    i = pl.program_id(0)
    prev = be_ref[jnp.maximum(i - 1, 0)]
    changed = jnp.logical_or(i == 0, be_ref[i] != prev)

    @pl.when(jnp.logical_and(changed, i < nu_ref[0]))
    def _():
        wgu_b[...] = wgu_ref[0].astype(BF16)
        wd_b[...] = wd_ref[0].astype(BF16)

    @pl.when(i < nu_ref[0])
    def _():
        gu = _dot(xs_ref[...].astype(BF16), wgu_b[...]) + bgu_ref[0]
        gate_h = jnp.minimum(gu[:, 0:d_ff], SWIGLU_LIMIT)
        up_h = jnp.clip(gu[:, d_ff:], -SWIGLU_LIMIT, SWIGLU_LIMIT)
        act = (up_h + 1.0) * gate_h * _sigmoid(SWIGLU_ALPHA * gate_h)
        y_ref[...] = _dot(act.astype(BF16), wd_b[...]) + bd_ref[0]

    @pl.when(i >= nu_ref[0])
    def _():
        y_ref[...] = jnp.zeros(y_ref.shape, F32)


def _moe_experts(block_e, n_used, xs, w_gu, b_gu, w_down, b_down):
    n_rows, D = xs.shape
    E, _, two_ff = w_gu.shape
    d_ff = two_ff // 2
    n_blocks = n_rows // MOE_ROWS
    return pl.pallas_call(
        functools.partial(_moe_kernel, d_ff=d_ff),
        out_shape=jax.ShapeDtypeStruct((n_rows, D), F32),
        grid_spec=pltpu.PrefetchScalarGridSpec(
            num_scalar_prefetch=2,
            grid=(n_blocks,),
            in_specs=[pl.BlockSpec((MOE_ROWS, D), lambda i, be, nu: (i, 0)),
                      pl.BlockSpec((1, D, two_ff), lambda i, be, nu: (be[i], 0, 0)),
                      pl.BlockSpec((1, 1, two_ff), lambda i, be, nu: (be[i], 0, 0)),
                      pl.BlockSpec((1, d_ff, D), lambda i, be, nu: (be[i], 0, 0)),
                      pl.BlockSpec((1, 1, D), lambda i, be, nu: (be[i], 0, 0))],
            out_specs=pl.BlockSpec((MOE_ROWS, D), lambda i, be, nu: (i, 0)),
            scratch_shapes=[pltpu.VMEM((D, two_ff), BF16), pltpu.VMEM((d_ff, D), BF16)]),
        compiler_params=_cparams(1),
        name="moe_experts",
    )(block_e, n_used, xs, w_gu, b_gu.reshape(E, 1, two_ff), w_down, b_down.reshape(E, 1, D))


def _route(top_idx, n_tok, n_exp):
    n_assign = n_tok * TOP_K
    flat_e = top_idx.reshape(-1)
    experts = jnp.arange(n_exp, dtype=jnp.int32)
    onehot = (flat_e[:, None] == experts[None, :]).astype(jnp.int32)
    csum = jnp.cumsum(onehot, axis=0)
    counts = csum[-1]
    padded = (counts + MOE_ROWS - 1) // MOE_ROWS * MOE_ROWS
    pad_end = jnp.cumsum(padded)
    pad_start = pad_end - padded
    dest = jnp.sum(onehot * (csum - 1 + pad_start[None, :]), axis=1).astype(jnp.int32)
    n_blocks = -(-(n_assign + n_exp * (MOE_ROWS - 1)) // MOE_ROWS)
    n_rows = n_blocks * MOE_ROWS
    blk_row0 = jnp.arange(n_blocks, dtype=jnp.int32) * MOE_ROWS
    block_e = jnp.minimum(jnp.sum((pad_end[None, :] <= blk_row0[:, None]).astype(jnp.int32), axis=1), n_exp - 1)
    n_fill = n_rows - n_assign
    fill_e = jnp.arange(n_fill, dtype=jnp.int32) // (MOE_ROWS - 1)
    fill_i = jnp.arange(n_fill, dtype=jnp.int32) % (MOE_ROWS - 1)
    fill_on = (fill_e < n_exp) & (fill_i < jnp.sum(
        (fill_e[:, None] == experts[None, :]) * (padded - counts)[None, :], axis=1))
    keys = jnp.concatenate([2 * flat_e, jnp.where(fill_on, 2 * fill_e + 1, 2 * n_exp + 1)])
    pos_bits = (n_rows - 1).bit_length()
    assert (2 * n_exp + 2) << pos_bits < 2 ** 31
    packed = jnp.left_shift(keys, pos_bits) | jnp.arange(n_rows, dtype=jnp.int32)
    pos = jnp.bitwise_and(lax.sort(packed), (1 << pos_bits) - 1)
    row_tok = jnp.where(pos < n_assign, pos // TOP_K, n_tok).astype(jnp.int32)
    n_used = (pad_end[-1] // MOE_ROWS).astype(jnp.int32).reshape(1)
    return row_tok, dest.reshape(n_tok, TOP_K), block_e.astype(jnp.int32), n_used


def _final_kernel(x1_ref, yg_ref, tg_ref, m5_ref, fg_ref, o_ref):
    tg = tg_ref[0]
    ff = tg[:, 0:1] * yg_ref[0, 0]
    for k in range(1, TOP_K):
        ff = ff + tg[:, k:k + 1] * yg_ref[0, k]
    o_ref[0] = _rms(x1_ref[0] + m5_ref[0] * ff, fg_ref[...])


def _final(x1, yg, tg, m5, final_g, *, seq_mode, tm):
    B, T, D = x1.shape
    row_blk = lambda w: pl.BlockSpec((1, tm, w), lambda b, t: (b, t, 0))
    mod_spec = pl.BlockSpec((1, 1, D), lambda b, t: (b, 0, 0)) if seq_mode else row_blk(D)
    return pl.pallas_call(
        _final_kernel,
        out_shape=jax.ShapeDtypeStruct((B, T, D), F32),
        grid=(B, T // tm),
        in_specs=[row_blk(D), pl.BlockSpec((1, TOP_K, tm, D), lambda b, t: (b, 0, t, 0)), row_blk(LANE),
                  mod_spec, pl.BlockSpec((1, D), lambda b, t: (0, 0))],
        out_specs=row_blk(D),
        compiler_params=_cparams(2),
        name="final_seq" if seq_mode else "final_rows",
    )(x1, yg, tg, m5, final_g.reshape(1, D))


def _row_tile(t):
    for tm in (512, 256, 128, 64, 32, 16, 8):
        if t % tm == 0:
            return tm
    raise ValueError(f"unsupported row count {t}")


def _rows(a, idx):
    return a.at[idx].get(mode="promise_in_bounds")


def kernel(x_prompt, x_sample, cache_kv_cmp, cache_kv_slc, state_kv_win, state_conv, page_table, c_prompt, c_sample,
           rel_bias_table, ln1_g, ln2_g, w_ada, b_ada, w_in, conv_w, cmp_pe, cmp_w1, cmp_w2, gn_conv, gn_att, w_o,
           router_w, router_b, w_gu, b_gu, w_down, b_down, final_g):
    B, T, D = x_prompt.shape
    BS, TS, _ = x_sample.shape
    depth = w_in.shape[0]
    assert depth == 1 and TS == 1
    n_pool, page = cache_kv_cmp.shape[1], cache_kv_cmp.shape[2]
    n_pages = page_table.shape[1]
    past_len = n_pages * page
    n_exp = router_w.shape[2]
    cw_width = conv_w.shape[2]
    kvw2 = 2 * KV_WIDTH
    assert past_len % SLC_BLOCK == 0 and past_len % CMP_STRIDE == 0 and T % page == 0

    in_cols = w_in.shape[2]
    gate0 = 3 * cw_width + N_HEADS * HEAD_DIM + 3 * kvw2
    w_pad = jnp.pad(w_in[0], ((0, 0), (0, gate0 + LANE - in_cols))).astype(BF16)
    wo_b = w_o[0].astype(BF16)
    rw_pad = jnp.pad(router_w[0], ((0, 0), (0, LANE - n_exp)))
    rb_pad = jnp.pad(router_b[0], (0, LANE - n_exp)).reshape(1, LANE)
    w_blk = _cmp_block_weight(cmp_w1[0])

    n_c = B + BS
    n_cp = -(-n_c // 8) * 8
    c_all = jnp.pad(jnp.concatenate([c_prompt, c_sample], axis=0), ((0, n_cp - n_c), (0, 0)))
    mod = _modulation(c_all, w_ada[0], b_ada[0]).reshape(n_cp, 6, D)
    mp = [mod[:B, i].reshape(B, 1, D) for i in range(6)]
    ms = [mod[B:n_c, i].reshape(1, BS, D) for i in range(6)]

    tm = _row_tile(T)
    nq = T // Q_BLOCK
    zeros_prev = jnp.zeros((B, CONV_K - 1, cw_width), F32)
    convn_p, q_p, kvc_p, kvs_p, kvw_p, gate_p, vlast_p, kh_p, vth_p = _inproj(
        x_prompt, mp[0], mp[1], ln1_g[0], w_pad, conv_w[0], gn_conv[0], zeros_prev, zeros_prev, seq_mode=True, tm=tm)
    pt_p = jnp.arange(B * (T // page), dtype=jnp.int32).reshape(B, T // page)
    g_p = math.gcd(T // page, CMP_PAGES)
    ab_p = _cmp_ab(kvc_p.reshape(B * (T // page), page, kvw2), pt_p, w_blk, g_p)
    kc_p, vct_p = _cmp_finish(ab_p, cmp_pe[0], cmp_w1[0], cmp_w2[0])
    gates_p = gate_p[:, :, :3 * N_HEADS].reshape(B, nq, Q_BLOCK, N_KV_HEADS, GROUP, 3)
    gates_p = gates_p.transpose(0, 3, 1, 5, 4, 2).reshape(B, N_KV_HEADS, nq, 3, ROWS_Q)
    att_p = _nsa_prompt(q_p, kh_p, vth_p, kc_p, vct_p, gates_p, rel_bias_table)
    x1_p, h2_p, ti_p, tg_p = _merge(x_prompt, convn_p, att_p, gn_att[0], wo_b, mp[2], mp[3], mp[4], ln2_g[0],
                                    rw_pad, rb_pad, seq_mode=True, tm=tm, n_exp=n_exp)

    xs_rows = x_sample.reshape(1, BS, D)
    prev2 = state_conv[0][:, 0, :].reshape(1, BS, cw_width)
    prev1 = state_conv[0][:, 1, :].reshape(1, BS, cw_width)
    tms = _row_tile(BS)
    convn_s, q_s, kvc_s, kvs_s, kvw_s, gate_s, v_s = _inproj(
        xs_rows, ms[0], ms[1], ln1_g[0], w_pad, conv_w[0], gn_conv[0], prev2, prev1, seq_mode=False, tm=tms)
    n_cmp = (past_len + TS - CMP_LEN) // CMP_STRIDE + 1
    nch_s = past_len // CMP_STRIDE
    assert n_cmp + 1 == nch_s
    ab_s = _cmp_ab_t(cache_kv_cmp[0].transpose(0, 2, 3, 4, 1), page_table, w_blk,
                     math.gcd(n_pages, CMP_PAGES))
    kc_s, vct_s = _cmp_finish(ab_s, cmp_pe[0], cmp_w1[0], cmp_w2[0])
    t_q = past_len
    n_slc = -(-(past_len + TS) // SLC_BLOCK)
    cur = t_q // SLC_BLOCK
    k_sel = min(N_SEL, n_slc)
    tbl = rel_bias_table.reshape(N_BUCKETS, N_KV_HEADS, GROUP)
    pos_c = jnp.arange(nch_s) * CMP_STRIDE + CMP_LEN - 1
    bias_c = tbl[_rel_bucket(t_q - pos_c)].transpose(1, 2, 0)
    ncol = -(-n_slc // LANE) * LANE
    mm_s = _score_matrix(nch_s, ncol)
    q_s_hm = q_s.reshape(N_HEADS, BS, HEAD_DIM).transpose(1, 0, 2).astype(F32)
    o_c_s, idx_pad = _smp_cmp(q_s_hm, kc_s, vct_s, bias_c, mm_s, n_cmp, n_slc, cur)
    idx = idx_pad[:, :, :k_sel]
    blk_per_page = page // SLC_BLOCK
    pg = jnp.minimum(idx // blk_per_page, n_pages - 1)
    phys = jnp.take_along_axis(page_table, pg.reshape(BS, -1), axis=1).astype(jnp.int32)
    pos_s = pg[..., None] * page + jnp.arange(page)
    in_blk = (pos_s // SLC_BLOCK == idx[..., None]) & (pos_s < past_len)
    pos_s = pos_s.reshape(BS, N_KV_HEADS, k_sel * page)
    bucket_hot = (_rel_bucket(t_q - pos_s)[..., None] == jnp.arange(N_BUCKETS)).astype(F32)
    bias_sel = jnp.einsum('bksn,nkg->bkgs', bucket_hot, tbl, precision=lax.Precision.HIGHEST)
    new_sel = jnp.any(idx == cur, axis=-1, keepdims=True)
    mask_sel = jnp.concatenate([in_blk.reshape(BS, N_KV_HEADS, k_sel * page), new_sel], axis=-1)
    mask_sel = mask_sel.astype(F32)[:, :, None, :]
    nw = state_kv_win.shape[2]
    bias_win = tbl[_rel_bucket(nw - jnp.arange(nw))].transpose(1, 2, 0)
    bias_new = tbl[0].reshape(N_KV_HEADS, GROUP, 1)
    new_rows = jnp.stack([kvs_s[0], kvw_s[0]], axis=1)
    gates_s = gate_s[0, :, :3 * N_HEADS].reshape(BS, N_HEADS, 3)
    pool_t = cache_kv_slc[0].transpose(0, 2, 3, 4, 1)
    win_t = state_kv_win[0].transpose(0, 2, 3, 4, 1)
    att_s = _smp_att(phys.reshape(-1), q_s_hm, pool_t, bias_sel, mask_sel, new_rows,
                     win_t, bias_win, bias_new, gates_s, o_c_s, k_sel)
    att_s = att_s.reshape(1, BS, N_HEADS * HEAD_DIM)
    x1_s, h2_s, ti_s, tg_s = _merge(xs_rows, convn_s, att_s, gn_att[0], wo_b, ms[2], ms[3], ms[4], ln2_g[0],
                                    rw_pad, rb_pad, seq_mode=False, tm=tms, n_exp=n_exp)

    n_tok = B * T + BS
    h2_all = jnp.concatenate([h2_p.reshape(B * T, D), h2_s.reshape(BS, D), jnp.zeros((8, D), F32)], axis=0)
    top_idx = jnp.concatenate([ti_p.reshape(B * T, LANE), ti_s.reshape(BS, LANE)], axis=0)[:, :TOP_K]
    row_tok, dest, block_e, n_used = _route(top_idx, n_tok, n_exp)
    xs = _rows(h2_all, row_tok)
    yb = _moe_experts(block_e, n_used, xs, w_gu[0], b_gu[0], w_down[0], b_down[0])
    dest_p = dest[:B * T].reshape(B, T, TOP_K).transpose(0, 2, 1)
    yg_p = _rows(yb, dest_p.reshape(-1)).reshape(B, TOP_K, T, D)
    yg_s = _rows(yb, dest[B * T:].T.reshape(-1)).reshape(1, TOP_K, BS, D)
    y_p = _final(x1_p, yg_p, tg_p, mp[5], final_g, seq_mode=True, tm=tm)
    y_s = _final(x1_s, yg_s, tg_s, ms[5], final_g, seq_mode=False, tm=tms)

    kv_tail = (2, N_KV_HEADS, HEAD_DIM)
    page_shape = (depth, B, T // page, page) + kv_tail
    w_keep = min(WINDOW, T)
    new_win_s = jnp.concatenate([state_kv_win[0][:, TS:], kvw_s.reshape(BS, TS, *kv_tail)], axis=1)
    new_conv_s = jnp.concatenate([state_conv[0][:, TS:], v_s.reshape(BS, TS, cw_width)], axis=1)
    return (y_p, y_s.reshape(BS, TS, D),
            kvc_p.reshape(page_shape), kvc_s.reshape((depth, BS, TS) + kv_tail),
            kvs_p.reshape(page_shape), kvs_s.reshape((depth, BS, TS) + kv_tail),
            kvw_p[:, T - w_keep:].reshape((depth, B, w_keep) + kv_tail), new_win_s[None],
            vlast_p[None], new_conv_s[None])
```
